```python
import jax, jax.numpy as jnp
from jax import lax
import numpy as np

D_MODEL = 2048
BATCH = 8
SEQ = 4096
DEPTH = 2

N_A_LAYERS = DEPTH // 2
N_B_LAYERS = DEPTH - N_A_LAYERS
EPS = 1e-6
GLA_HEADS = 4
GLA_DK = D_MODEL // 2 // GLA_HEADS
GLA_DV = D_MODEL // GLA_HEADS
GLA_GATE_RANK = 16
GLA_GATE_TAU = 16.0
GLA_CHUNK = 64
GLA_QK_W = GLA_HEADS * GLA_DK
GLA_V_W = GLA_HEADS * GLA_DV
GLA_IN_W = 2 * GLA_QK_W + 2 * GLA_V_W + GLA_GATE_RANK
SB_HEADS = 16
SB_HEAD_DIM = D_MODEL // SB_HEADS
SB_W = SB_HEADS * SB_HEAD_DIM
SB_BLOCK = 128
D_FF = -(-8 * D_MODEL // (3 * 256)) * 256

kernel_name = "yoco_gla_stick_breaking_hybrid"


def rmsnorm(x, w):
    xf = x.astype(jnp.float32)
    xf = xf * lax.rsqrt(jnp.mean(xf * xf, axis=-1, keepdims=True) + EPS)
    return xf.astype(x.dtype) * w


def split_heads(t, n_heads):
    b, s, _ = t.shape
    return t.reshape(b, s, n_heads, -1).transpose(0, 2, 1, 3)


def swiglu(h, w_gate_up, w_down):
    gate, up = jnp.split(h @ w_gate_up, 2, axis=-1)
    return (jax.nn.silu(gate) * up) @ w_down


def gla_chunked(q, k, v, g):
    out_dtype = v.dtype
    q, k, v, g = (t.astype(jnp.float32) for t in (q, k, v, g))
    B, H, S, DK = q.shape
    DV = v.shape[-1]
    C = GLA_CHUNK
    NC = S // C

    def to_chunks(t):
        return t.reshape(B, H, NC, C, t.shape[-1]).transpose(2, 0, 1, 3, 4)

    causal = jnp.tril(jnp.ones((C, C), dtype=bool))[:, :, None]

    def step(state, inp):
        qi, ki, vi, gi = inp
        b = jnp.cumsum(gi, axis=-2)
        o_inter = jnp.einsum('bhck,bhkv->bhcv', qi * jnp.exp(b), state)
        rel = b[..., :, None, :] - b[..., None, :, :]
        decay = jnp.where(causal, jnp.exp(jnp.minimum(rel, 0.0)), 0.0)
        scores = jnp.einsum('bhik,bhjk,bhijk->bhij', qi, ki, decay)
        o_intra = jnp.einsum('bhij,bhjv->bhiv', scores, vi)
        b_last = b[..., -1:, :]
        k_dec = ki * jnp.exp(b_last - b)
        new_state = state * jnp.exp(b_last)[..., 0, :, None] + jnp.einsum('bhck,bhcv->bhkv', k_dec, vi)
        return new_state, o_inter + o_intra

    state0 = jnp.zeros((B, H, DK, DV), jnp.float32)
    _, o = lax.scan(step, state0, (to_chunks(q), to_chunks(k), to_chunks(v), to_chunks(g)))
    return o.transpose(1, 2, 0, 3, 4).reshape(B, H, S, DV).astype(out_dtype)


def gla_mixer(h, w_in, w_gate_up, b_gate, gnorm_w, w_out):
    B, S, _ = h.shape
    proj = h @ w_in
    q, k, v, r, gl = jnp.split(
        proj, [GLA_QK_W, 2 * GLA_QK_W, 2 * GLA_QK_W + GLA_V_W, 2 * GLA_QK_W + 2 * GLA_V_W], axis=-1)
    log_alpha = jax.nn.log_sigmoid((gl @ w_gate_up + b_gate).astype(jnp.float32)) / GLA_GATE_TAU
    o = gla_chunked(split_heads(q, GLA_HEADS) * (GLA_DK ** -0.5), split_heads(k, GLA_HEADS),
                    split_heads(v, GLA_HEADS), split_heads(log_alpha, GLA_HEADS))
    o = rmsnorm(o.transpose(0, 2, 1, 3), gnorm_w)
    o = o * jax.nn.silu(r).reshape(B, S, GLA_HEADS, GLA_DV)
    return o.reshape(B, S, GLA_V_W) @ w_out


def stick_breaking_attention(q, k, v):
    B, H, S, hd = q.shape
    NB = S // SB_BLOCK
    qb = q.reshape(B, H, NB, SB_BLOCK, hd).transpose(2, 0, 1, 3, 4)
    kpos = jnp.arange(S)

    def block(args):
        qi, i = args
        z = jnp.einsum('bhqd,bhkd->bhqk', qi, k).astype(jnp.float32) * (hd ** -0.5)
        qpos = i * SB_BLOCK + jnp.arange(SB_BLOCK)
        mask = kpos[None, :] < qpos[:, None]
        log_fail = jnp.where(mask, jax.nn.log_sigmoid(-z), 0.0)
        after = lax.cumsum(log_fail, axis=log_fail.ndim - 1, reverse=True) - log_fail
        a = jnp.where(mask, jnp.exp(jax.nn.log_sigmoid(z) + after), 0.0)
        return jnp.einsum('bhqk,bhkd->bhqd', a.astype(v.dtype), v)

    o = lax.map(block, (qb, jnp.arange(NB)))
    return o.transpose(1, 2, 0, 3, 4).reshape(B, H, S, hd)


def _fwd_setup_inputs(seed: int = 0) -> dict:
    key = jax.random.key(seed)
    ks = jax.random.split(key, 20)
    f32 = jnp.float32

    def nrm(k, shape, fan_in):
        return jax.random.normal(k, shape, f32) * (fan_in ** -0.5)

    def gain(k, shape):
        return 1.0 + 0.01 * jax.random.normal(k, shape, f32)

    return {
        "x": jax.random.normal(ks[0], (BATCH, SEQ, D_MODEL), f32),
        "attn_norm_w": gain(ks[1], (DEPTH, D_MODEL)),
        "ffn_norm_w": gain(ks[2], (DEPTH, D_MODEL)),
        "gla_w_in": nrm(ks[3], (N_A_LAYERS, D_MODEL, GLA_IN_W), D_MODEL),
        "gla_w_gate_up": nrm(ks[4], (N_A_LAYERS, GLA_GATE_RANK, GLA_QK_W), GLA_GATE_RANK),
        "gla_b_gate": 0.1 * jax.random.normal(ks[5], (N_A_LAYERS, GLA_QK_W), f32),
        "gla_gnorm_w": gain(ks[6], (N_A_LAYERS, GLA_DV)),
        "gla_w_out": nrm(ks[7], (N_A_LAYERS, GLA_V_W, D_MODEL), GLA_V_W),
        "kv_norm_w": gain(ks[8], (D_MODEL,)),
        "sb_w_kv": nrm(ks[9], (D_MODEL, 2 * SB_W), D_MODEL),
        "sb_w_q": nrm(ks[10], (N_B_LAYERS, D_MODEL, SB_W), D_MODEL),
        "sb_w_out": nrm(ks[11], (N_B_LAYERS, SB_W, D_MODEL), SB_W),
        "ffn_w_gate_up": nrm(ks[12], (DEPTH, D_MODEL, 2 * D_FF), D_MODEL),
        "ffn_w_down": nrm(ks[13], (DEPTH, D_FF, D_MODEL), D_FF),
        "final_norm_w": gain(ks[14], (D_MODEL,)),
    }


def _fwd_reference(x, attn_norm_w, ffn_norm_w, gla_w_in, gla_w_gate_up, gla_b_gate, gla_gnorm_w, gla_w_out,
              kv_norm_w, sb_w_kv, sb_w_q, sb_w_out, ffn_w_gate_up, ffn_w_down, final_norm_w):
    h = x
    k_shared = None
    v_shared = None
    for layer in range(DEPTH):
        a = rmsnorm(h, attn_norm_w[layer])
        if layer < N_A_LAYERS:
            i = layer
            h = h + gla_mixer(a, gla_w_in[i], gla_w_gate_up[i], gla_b_gate[i], gla_gnorm_w[i], gla_w_out[i])
        else:
            j = layer - N_A_LAYERS
            q = split_heads(a @ sb_w_q[j], SB_HEADS)
            o = stick_breaking_attention(q, k_shared, v_shared)
            o = o.transpose(0, 2, 1, 3).reshape(h.shape[0], h.shape[1], SB_W)
            h = h + o @ sb_w_out[j]
        h = h + swiglu(rmsnorm(h, ffn_norm_w[layer]), ffn_w_gate_up[layer], ffn_w_down[layer])
        if layer == N_A_LAYERS - 1:
            kv = rmsnorm(h, kv_norm_w) @ sb_w_kv
            k_c, v_c = jnp.split(kv, 2, axis=-1)
            k_shared = split_heads(k_c, SB_HEADS)
            v_shared = split_heads(v_c, SB_HEADS)
    return rmsnorm(h, final_norm_w)


import jax as _jax
import jax.numpy as _jnp

TWIN_FORMAT = 'train_step'
FWD_PARAMS = ['x', 'attn_norm_w', 'ffn_norm_w', 'gla_w_in', 'gla_w_gate_up', 'gla_b_gate', 'gla_gnorm_w', 'gla_w_out', 'kv_norm_w', 'sb_w_kv', 'sb_w_q', 'sb_w_out', 'ffn_w_gate_up', 'ffn_w_down', 'final_norm_w']
TWIN_WEIGHTS = ['attn_norm_w', 'ffn_norm_w', 'gla_w_in', 'gla_w_gate_up', 'gla_b_gate', 'gla_gnorm_w', 'gla_w_out', 'kv_norm_w', 'sb_w_kv', 'sb_w_q', 'sb_w_out', 'ffn_w_gate_up', 'ffn_w_down', 'final_norm_w']
TWIN_DIFF_INPUT = 'x'
TWIN_INPUTS = ['x', 'attn_norm_w', 'ffn_norm_w', 'gla_w_in', 'gla_w_gate_up', 'gla_b_gate', 'gla_gnorm_w', 'gla_w_out', 'kv_norm_w', 'sb_w_kv', 'sb_w_q', 'sb_w_out', 'ffn_w_gate_up', 'ffn_w_down', 'final_norm_w', 'loss_target', 'm_attn_norm_w', 'm_ffn_norm_w', 'm_gla_w_in', 'm_gla_w_gate_up', 'm_gla_b_gate', 'm_gla_gnorm_w', 'm_gla_w_out', 'm_kv_norm_w', 'm_sb_w_kv', 'm_sb_w_q', 'm_sb_w_out', 'm_ffn_w_gate_up', 'm_ffn_w_down', 'm_final_norm_w', 'v_attn_norm_w', 'v_ffn_norm_w', 'v_gla_w_in', 'v_gla_w_gate_up', 'v_gla_b_gate', 'v_gla_gnorm_w', 'v_gla_w_out', 'v_kv_norm_w', 'v_sb_w_kv', 'v_sb_w_q', 'v_sb_w_out', 'v_ffn_w_gate_up', 'v_ffn_w_down', 'v_final_norm_w']
TWIN_OUTPUTS = ['loss', 'grad_x', 'grad_attn_norm_w', 'grad_ffn_norm_w', 'grad_gla_w_in', 'grad_gla_w_gate_up', 'grad_gla_b_gate', 'grad_gla_gnorm_w', 'grad_gla_w_out', 'grad_kv_norm_w', 'grad_sb_w_kv', 'grad_sb_w_q', 'grad_sb_w_out', 'grad_ffn_w_gate_up', 'grad_ffn_w_down', 'grad_final_norm_w', 'delta_attn_norm_w', 'delta_ffn_norm_w', 'delta_gla_w_in', 'delta_gla_w_gate_up', 'delta_gla_b_gate', 'delta_gla_gnorm_w', 'delta_gla_w_out', 'delta_kv_norm_w', 'delta_sb_w_kv', 'delta_sb_w_q', 'delta_sb_w_out', 'delta_ffn_w_gate_up', 'delta_ffn_w_down', 'delta_final_norm_w', 'new_m_attn_norm_w', 'new_m_ffn_norm_w', 'new_m_gla_w_in', 'new_m_gla_w_gate_up', 'new_m_gla_b_gate', 'new_m_gla_gnorm_w', 'new_m_gla_w_out', 'new_m_kv_norm_w', 'new_m_sb_w_kv', 'new_m_sb_w_q', 'new_m_sb_w_out', 'new_m_ffn_w_gate_up', 'new_m_ffn_w_down', 'new_m_final_norm_w', 'new_v_attn_norm_w', 'new_v_ffn_norm_w', 'new_v_gla_w_in', 'new_v_gla_w_gate_up', 'new_v_gla_b_gate', 'new_v_gla_gnorm_w', 'new_v_gla_w_out', 'new_v_kv_norm_w', 'new_v_sb_w_kv', 'new_v_sb_w_q', 'new_v_sb_w_out', 'new_v_ffn_w_gate_up', 'new_v_ffn_w_down', 'new_v_final_norm_w']
TWIN_LEAF_KINDS = {'loss': 'loss', 'grad_x': 'grad_x', 'grad_attn_norm_w': 'grad_w', 'grad_ffn_norm_w': 'grad_w', 'grad_gla_w_in': 'grad_w', 'grad_gla_w_gate_up': 'grad_w', 'grad_gla_b_gate': 'grad_w', 'grad_gla_gnorm_w': 'grad_w', 'grad_gla_w_out': 'grad_w', 'grad_kv_norm_w': 'grad_w', 'grad_sb_w_kv': 'grad_w', 'grad_sb_w_q': 'grad_w', 'grad_sb_w_out': 'grad_w', 'grad_ffn_w_gate_up': 'grad_w', 'grad_ffn_w_down': 'grad_w', 'grad_final_norm_w': 'grad_w', 'delta_attn_norm_w': 'delta_w', 'delta_ffn_norm_w': 'delta_w', 'delta_gla_w_in': 'delta_w', 'delta_gla_w_gate_up': 'delta_w', 'delta_gla_b_gate': 'delta_w', 'delta_gla_gnorm_w': 'delta_w', 'delta_gla_w_out': 'delta_w', 'delta_kv_norm_w': 'delta_w', 'delta_sb_w_kv': 'delta_w', 'delta_sb_w_q': 'delta_w', 'delta_sb_w_out': 'delta_w', 'delta_ffn_w_gate_up': 'delta_w', 'delta_ffn_w_down': 'delta_w', 'delta_final_norm_w': 'delta_w', 'new_m_attn_norm_w': 'new_m', 'new_m_ffn_norm_w': 'new_m', 'new_m_gla_w_in': 'new_m', 'new_m_gla_w_gate_up': 'new_m', 'new_m_gla_b_gate': 'new_m', 'new_m_gla_gnorm_w': 'new_m', 'new_m_gla_w_out': 'new_m', 'new_m_kv_norm_w': 'new_m', 'new_m_sb_w_kv': 'new_m', 'new_m_sb_w_q': 'new_m', 'new_m_sb_w_out': 'new_m', 'new_m_ffn_w_gate_up': 'new_m', 'new_m_ffn_w_down': 'new_m', 'new_m_final_norm_w': 'new_m', 'new_v_attn_norm_w': 'new_v', 'new_v_ffn_norm_w': 'new_v', 'new_v_gla_w_in': 'new_v', 'new_v_gla_w_gate_up': 'new_v', 'new_v_gla_b_gate': 'new_v', 'new_v_gla_gnorm_w': 'new_v', 'new_v_gla_w_out': 'new_v', 'new_v_kv_norm_w': 'new_v', 'new_v_sb_w_kv': 'new_v', 'new_v_sb_w_q': 'new_v', 'new_v_sb_w_out': 'new_v', 'new_v_ffn_w_gate_up': 'new_v', 'new_v_ffn_w_down': 'new_v', 'new_v_final_norm_w': 'new_v'}


def _forward(args):
    return _fwd_reference(*[args[k] for k in FWD_PARAMS])


def _output_shape():
    def fwd():
        inp = _fwd_setup_inputs(0)
        return _fwd_reference(*[inp[k] for k in FWD_PARAMS])
    out = _jax.eval_shape(fwd)
    return out.shape, out.dtype

N_MICROBATCH = 1
ADAM_LR = 0.001
ADAM_B1 = 0.9
ADAM_B2 = 0.999
ADAM_EPS = 1e-08
ADAM_WD = 0.01
ADAM_STEP = 10
PER_EXAMPLE_BATCH_AXIS = {'x': 0, 'loss_target': 0}
SHARED_INPUTS = []
_WEIGHT_DTYPES = {'attn_norm_w': _jnp.float32, 'ffn_norm_w': _jnp.float32, 'gla_w_in': _jnp.float32, 'gla_w_gate_up': _jnp.float32, 'gla_b_gate': _jnp.float32, 'gla_gnorm_w': _jnp.float32, 'gla_w_out': _jnp.float32, 'kv_norm_w': _jnp.float32, 'sb_w_kv': _jnp.float32, 'sb_w_q': _jnp.float32, 'sb_w_out': _jnp.float32, 'ffn_w_gate_up': _jnp.float32, 'ffn_w_down': _jnp.float32, 'final_norm_w': _jnp.float32}
MOMENT_SCALE = {'attn_norm_w': 8.351406e-02, 'ffn_norm_w': 5.765343e-02, 'gla_w_in': 6.617554e-02, 'gla_w_gate_up': 9.107735e-03, 'gla_b_gate': 3.697967e-02, 'gla_gnorm_w': 1.167693e-01, 'gla_w_out': 5.600270e-02, 'kv_norm_w': 4.533089e-02, 'sb_w_kv': 3.271343e-02, 'sb_w_q': 1.948768e-02, 'sb_w_out': 4.203063e-02, 'ffn_w_gate_up': 2.433814e-02, 'ffn_w_down': 3.971606e-02, 'final_norm_w': 1.598008e+01}


def _to_microbatches(a, axis):
    t = _jnp.moveaxis(a, axis, 0)
    t = t.reshape((N_MICROBATCH, t.shape[0] // N_MICROBATCH) + t.shape[1:])
    return _jnp.moveaxis(t, 1, axis + 1)


def setup_inputs(seed: int = 0) -> dict:
    inp = _fwd_setup_inputs(seed)
    key = _jax.random.fold_in(_jax.random.key(seed), 7919)
    shape, _ = _output_shape()
    out = dict(inp)
    out["loss_target"] = _jax.random.normal(_jax.random.fold_in(key, 0), shape, _jnp.float32)
    for i, name in enumerate(TWIN_WEIGHTS):
        w = inp[name].astype(_jnp.float32)
        if MOMENT_SCALE is None:
            s = _jnp.sqrt(_jnp.mean(_jnp.square(w)) + 1e-30)
        else:
            s = MOMENT_SCALE[name]
        km, kv = _jax.random.split(_jax.random.fold_in(key, i + 1))
        out[name] = w
        out["m_" + name] = s * _jax.random.normal(km, w.shape, _jnp.float32)
        out["v_" + name] = (s * s) * _jax.random.uniform(kv, w.shape, _jnp.float32, 0.5, 1.5)
    if N_MICROBATCH > 1:
        for name, axis in PER_EXAMPLE_BATCH_AXIS.items():
            out[name] = _to_microbatches(out[name], axis)
    return {'x': out['x'], 'attn_norm_w': out['attn_norm_w'], 'ffn_norm_w': out['ffn_norm_w'], 'gla_w_in': out['gla_w_in'], 'gla_w_gate_up': out['gla_w_gate_up'], 'gla_b_gate': out['gla_b_gate'], 'gla_gnorm_w': out['gla_gnorm_w'], 'gla_w_out': out['gla_w_out'], 'kv_norm_w': out['kv_norm_w'], 'sb_w_kv': out['sb_w_kv'], 'sb_w_q': out['sb_w_q'], 'sb_w_out': out['sb_w_out'], 'ffn_w_gate_up': out['ffn_w_gate_up'], 'ffn_w_down': out['ffn_w_down'], 'final_norm_w': out['final_norm_w'], 'loss_target': out['loss_target'], 'm_attn_norm_w': out['m_attn_norm_w'], 'm_ffn_norm_w': out['m_ffn_norm_w'], 'm_gla_w_in': out['m_gla_w_in'], 'm_gla_w_gate_up': out['m_gla_w_gate_up'], 'm_gla_b_gate': out['m_gla_b_gate'], 'm_gla_gnorm_w': out['m_gla_gnorm_w'], 'm_gla_w_out': out['m_gla_w_out'], 'm_kv_norm_w': out['m_kv_norm_w'], 'm_sb_w_kv': out['m_sb_w_kv'], 'm_sb_w_q': out['m_sb_w_q'], 'm_sb_w_out': out['m_sb_w_out'], 'm_ffn_w_gate_up': out['m_ffn_w_gate_up'], 'm_ffn_w_down': out['m_ffn_w_down'], 'm_final_norm_w': out['m_final_norm_w'], 'v_attn_norm_w': out['v_attn_norm_w'], 'v_ffn_norm_w': out['v_ffn_norm_w'], 'v_gla_w_in': out['v_gla_w_in'], 'v_gla_w_gate_up': out['v_gla_w_gate_up'], 'v_gla_b_gate': out['v_gla_b_gate'], 'v_gla_gnorm_w': out['v_gla_gnorm_w'], 'v_gla_w_out': out['v_gla_w_out'], 'v_kv_norm_w': out['v_kv_norm_w'], 'v_sb_w_kv': out['v_sb_w_kv'], 'v_sb_w_q': out['v_sb_w_q'], 'v_sb_w_out': out['v_sb_w_out'], 'v_ffn_w_gate_up': out['v_ffn_w_gate_up'], 'v_ffn_w_down': out['v_ffn_w_down'], 'v_final_norm_w': out['v_final_norm_w']}


def _loss(weights, diff, rest, loss_target):
    with _jax.named_scope("forward"):
        args = {**rest, TWIN_DIFF_INPUT: diff, **{k: w.astype(_WEIGHT_DTYPES[k]) for k, w in weights.items()}}
        y = _forward(args)
    with _jax.named_scope("loss_head"):
        err = _jnp.square(y.astype(_jnp.float32) - loss_target)
        return 0.5 * _jnp.sum(_jnp.mean(err, axis=-1)) if err.ndim else 0.5 * err


def _adamw(w, g, m, v):
    m = ADAM_B1 * m + (1.0 - ADAM_B1) * g
    v = ADAM_B2 * v + (1.0 - ADAM_B2) * _jnp.square(g)
    m_hat = m / (1.0 - ADAM_B1 ** ADAM_STEP)
    v_hat = v / (1.0 - ADAM_B2 ** ADAM_STEP)
    delta = -ADAM_LR * (m_hat / (_jnp.sqrt(v_hat) + ADAM_EPS) + ADAM_WD * w)
    return delta, m, v


def reference(x, attn_norm_w, ffn_norm_w, gla_w_in, gla_w_gate_up, gla_b_gate, gla_gnorm_w, gla_w_out, kv_norm_w, sb_w_kv, sb_w_q, sb_w_out, ffn_w_gate_up, ffn_w_down, final_norm_w, loss_target, m_attn_norm_w, m_ffn_norm_w, m_gla_w_in, m_gla_w_gate_up, m_gla_b_gate, m_gla_gnorm_w, m_gla_w_out, m_kv_norm_w, m_sb_w_kv, m_sb_w_q, m_sb_w_out, m_ffn_w_gate_up, m_ffn_w_down, m_final_norm_w, v_attn_norm_w, v_ffn_norm_w, v_gla_w_in, v_gla_w_gate_up, v_gla_b_gate, v_gla_gnorm_w, v_gla_w_out, v_kv_norm_w, v_sb_w_kv, v_sb_w_q, v_sb_w_out, v_ffn_w_gate_up, v_ffn_w_down, v_final_norm_w):
    given = dict(x=x, attn_norm_w=attn_norm_w, ffn_norm_w=ffn_norm_w, gla_w_in=gla_w_in, gla_w_gate_up=gla_w_gate_up, gla_b_gate=gla_b_gate, gla_gnorm_w=gla_gnorm_w, gla_w_out=gla_w_out, kv_norm_w=kv_norm_w, sb_w_kv=sb_w_kv, sb_w_q=sb_w_q, sb_w_out=sb_w_out, ffn_w_gate_up=ffn_w_gate_up, ffn_w_down=ffn_w_down, final_norm_w=final_norm_w, loss_target=loss_target, m_attn_norm_w=m_attn_norm_w, m_ffn_norm_w=m_ffn_norm_w, m_gla_w_in=m_gla_w_in, m_gla_w_gate_up=m_gla_w_gate_up, m_gla_b_gate=m_gla_b_gate, m_gla_gnorm_w=m_gla_gnorm_w, m_gla_w_out=m_gla_w_out, m_kv_norm_w=m_kv_norm_w, m_sb_w_kv=m_sb_w_kv, m_sb_w_q=m_sb_w_q, m_sb_w_out=m_sb_w_out, m_ffn_w_gate_up=m_ffn_w_gate_up, m_ffn_w_down=m_ffn_w_down, m_final_norm_w=m_final_norm_w, v_attn_norm_w=v_attn_norm_w, v_ffn_norm_w=v_ffn_norm_w, v_gla_w_in=v_gla_w_in, v_gla_w_gate_up=v_gla_w_gate_up, v_gla_b_gate=v_gla_b_gate, v_gla_gnorm_w=v_gla_gnorm_w, v_gla_w_out=v_gla_w_out, v_kv_norm_w=v_kv_norm_w, v_sb_w_kv=v_sb_w_kv, v_sb_w_q=v_sb_w_q, v_sb_w_out=v_sb_w_out, v_ffn_w_gate_up=v_ffn_w_gate_up, v_ffn_w_down=v_ffn_w_down, v_final_norm_w=v_final_norm_w)
    weights = {n: given[n] for n in TWIN_WEIGHTS}
    shared = {n: given[n] for n in SHARED_INPUTS}
    per_example = {n: given[n] for n in ['x']}
    grad_fn = _jax.value_and_grad(_loss, argnums=(0, 1))

    def one_microbatch(ex, loss_target):
        ex = dict(ex)
        diff = ex.pop(TWIN_DIFF_INPUT)
        return grad_fn(weights, diff, {**shared, **ex}, loss_target)

    if N_MICROBATCH == 1:
        loss, (grad_w, grad_x) = one_microbatch(per_example, given["loss_target"])
    else:
        def body(carry, xs):
            loss_sum, grad_sum = carry
            l_k, (gw_k, gx_k) = one_microbatch(xs[0], xs[1])
            with _jax.named_scope("update"):
                return (loss_sum + l_k, _jax.tree.map(_jnp.add, grad_sum, gw_k)), gx_k

        init = (_jnp.zeros((), _jnp.float32), _jax.tree.map(_jnp.zeros_like, weights))
        (loss, grad_w), grad_x = _jax.lax.scan(body, init, (per_example, given["loss_target"]))
    with _jax.named_scope("update"):
        delta_w, new_m, new_v = {}, {}, {}
        for n in TWIN_WEIGHTS:
            delta_w[n], new_m[n], new_v[n] = _adamw(weights[n], grad_w[n], given["m_" + n], given["v_" + n])
    return (loss, grad_x, *[grad_w[n] for n in TWIN_WEIGHTS], *[delta_w[n] for n in TWIN_WEIGHTS],
            *[new_m[n] for n in TWIN_WEIGHTS], *[new_v[n] for n in TWIN_WEIGHTS])
```

```python
import functools

import jax
import jax.numpy as jnp
from jax import lax
from jax.experimental import pallas as pl
from jax.experimental.pallas import tpu as pltpu

F32 = jnp.float32
BF16 = jnp.bfloat16
SDS = jax.ShapeDtypeStruct
MESH = pl.DeviceIdType.MESH

EPS = 1e-6
GLA_HEADS = 4
GLA_GATE_RANK = 16
GLA_GATE_TAU = 16.0
GLA_CHUNK = 128
SB_HEADS = 16
SB_TQ = 256
SB_TK = 128
ADAM_LR = 0.001
ADAM_B1 = 0.9
ADAM_B2 = 0.999
ADAM_EPS = 1e-08
ADAM_WD = 0.01
ADAM_STEP = 10

LANES = 128
SUBLANES = 8
N_CHIPS = 4
N_DEV = 8
PACK_W = 1024
PACK_ROW_UNIT = 32
VMEM_LIMIT = 56 * 1024 * 1024


def _tile(dim, target, unit=LANES):
    if dim <= target:
        return dim
    t = (target // unit) * unit
    while t >= unit:
        if dim % t == 0:
            return t
        t -= unit
    raise ValueError(f"no tile for {dim}")


def _cp(*sem):
    return pltpu.CompilerParams(dimension_semantics=sem, vmem_limit_bytes=VMEM_LIMIT)


def _sigmoid(x):
    return 1.0 / (1.0 + jnp.exp(-x))


def _dot(a, b, ca=1, cb=0):
    return lax.dot_general(a, b, (((ca,), (cb,)), ((), ())), preferred_element_type=F32)


def _split_dot(tri, x):
    hi = x.astype(BF16)
    lo = (x - hi.astype(F32)).astype(BF16)
    return _dot(tri, hi) + _dot(tri, lo)


def _hilo(x):
    hi = x.astype(BF16)
    return hi, (x - hi.astype(F32)).astype(BF16)


def _dot3(a, b, ca=1, cb=0):
    return _dot(a[0], b[0], ca, cb) + _dot(a[0], b[1], ca, cb) + _dot(a[1], b[0], ca, cb)


def _tri(n, kind):
    r = lax.broadcasted_iota(jnp.int32, (n, n), 0)
    c = lax.broadcasted_iota(jnp.int32, (n, n), 1)
    m = {"le": c <= r, "ge": c >= r, "lt": c < r, "gt": c > r}[kind]
    return jnp.where(m, 1.0, 0.0).astype(BF16)


def matmul(a, b, *, name, ta=False, tb=False, out_dtype=F32, residual=None, tm=1024, tn=1024, tk=512):
    m, k = (a.shape[1], a.shape[0]) if ta else a.shape
    n, kb = (b.shape[0], b.shape[1]) if tb else (b.shape[1], b.shape[0])
    assert k == kb, (a.shape, b.shape, ta, tb)
    tm, tn, tk = _tile(m, tm), _tile(n, tn), _tile(k, tk)
    nk = k // tk
    ca, cb = (0 if ta else 1), (1 if tb else 0)

    def body(*refs):
        if residual is None:
            a_ref, b_ref, o_ref, acc = refs
        else:
            a_ref, b_ref, r_ref, o_ref, acc = refs
        kk = pl.program_id(2)

        @pl.when(kk == 0)
        def _():
            acc[...] = jnp.zeros_like(acc)

        acc[...] += _dot(a_ref[...].astype(BF16), b_ref[...].astype(BF16), ca, cb)

        @pl.when(kk == nk - 1)
        def _():
            r = acc[...]
            if residual is not None:
                r = r + r_ref[...]
            o_ref[...] = r.astype(out_dtype)

    a_spec = pl.BlockSpec((tk, tm), lambda i, j, kk: (kk, i)) if ta else pl.BlockSpec((tm, tk), lambda i, j, kk: (i, kk))
    b_spec = pl.BlockSpec((tn, tk), lambda i, j, kk: (j, kk)) if tb else pl.BlockSpec((tk, tn), lambda i, j, kk: (kk, j))
    o_spec = pl.BlockSpec((tm, tn), lambda i, j, kk: (i, j))
    in_specs, args = [a_spec, b_spec], [a, b]
    if residual is not None:
        in_specs.append(o_spec)
        args.append(residual)
    return pl.pallas_call(
        body, name=name, grid=(m // tm, n // tn, nk), in_specs=in_specs, out_specs=o_spec,
        out_shape=SDS((m, n), out_dtype), scratch_shapes=[pltpu.VMEM((tm, tn), F32)],
        compiler_params=_cp("parallel", "parallel", "arbitrary"))(*args)


def rmsnorm_fwd(x, w, *, name, tr=256):
    s, d = x.shape

    def body(x_ref, w_ref, o_ref):
        xf = x_ref[...]
        r = lax.rsqrt(jnp.mean(xf * xf, axis=-1, keepdims=True) + EPS)
        o_ref[...] = (xf * r * w_ref[...]).astype(BF16)

    row = pl.BlockSpec((tr, d), lambda i: (i, 0))
    return pl.pallas_call(
        body, name=name, grid=(s // tr,), in_specs=[row, pl.BlockSpec((1, d), lambda i: (0, 0))], out_specs=row,
        out_shape=SDS((s, d), BF16), compiler_params=_cp("parallel"))(x, w.reshape(1, d))


def rmsnorm_bwd(dy, x, w, dres, *, name, tr=256):
    s, d = x.shape

    def body(dy_ref, x_ref, w_ref, dres_ref, dx_ref, dw_ref):
        i = pl.program_id(0)
        xf = x_ref[...]
        r = lax.rsqrt(jnp.mean(xf * xf, axis=-1, keepdims=True) + EPS)
        xh = xf * r
        dyf = dy_ref[...].astype(F32)
        dxh = dyf * w_ref[...]
        dx_ref[...] = dres_ref[...] + r * (dxh - xh * jnp.mean(dxh * xh, axis=-1, keepdims=True))
        part = jnp.sum(dyf * xh, axis=0, keepdims=True)

        @pl.when(i == 0)
        def _():
            dw_ref[...] = part

        @pl.when(i > 0)
        def _():
            dw_ref[...] += part

    row = pl.BlockSpec((tr, d), lambda i: (i, 0))
    vec = pl.BlockSpec((1, d), lambda i: (0, 0))
    return pl.pallas_call(
        body, name=name, grid=(s // tr,), in_specs=[row, row, vec, row], out_specs=[row, vec],
        out_shape=[SDS((s, d), F32), SDS((1, d), F32)], compiler_params=_cp("arbitrary"))(dy, x, w.reshape(1, d), dres)


def final_loss(h, w, target, *, name, tr=256):
    s, d = h.shape

    def body(h_ref, w_ref, t_ref, dh_ref, dw_ref, loss_ref):
        i = pl.program_id(0)
        xf = h_ref[...]
        r = lax.rsqrt(jnp.mean(xf * xf, axis=-1, keepdims=True) + EPS)
        xh = xf * r
        err = xh * w_ref[...] - t_ref[...]
        lpart = 0.5 * jnp.sum(jnp.sum(err * err, axis=-1, keepdims=True) * (1.0 / d), axis=0, keepdims=True)
        dy = err * (1.0 / d)
        dxh = dy * w_ref[...]
        dh_ref[...] = r * (dxh - xh * jnp.mean(dxh * xh, axis=-1, keepdims=True))
        part = jnp.sum(dy * xh, axis=0, keepdims=True)
        lrow = jnp.broadcast_to(lpart, (1, LANES))

        @pl.when(i == 0)
        def _():
            dw_ref[...] = part
            loss_ref[...] = lrow

        @pl.when(i > 0)
        def _():
            dw_ref[...] += part
            loss_ref[...] += lrow

    row = pl.BlockSpec((tr, d), lambda i: (i, 0))
    vec = pl.BlockSpec((1, d), lambda i: (0, 0))
    return pl.pallas_call(
        body, name=name, grid=(s // tr,), in_specs=[row, vec, row],
        out_specs=[row, vec, pl.BlockSpec((1, LANES), lambda i: (0, 0))],
        out_shape=[SDS((s, d), F32), SDS((1, d), F32), SDS((1, LANES), F32)],
        compiler_params=_cp("arbitrary"))(h, w.reshape(1, d), target)


def swiglu_fwd(gu, *, name, tr=512, tc=1408):
    s, f2 = gu.shape
    f = f2 // 2
    tc = _tile(f, tc)
    nf = f // tc

    def body(g_ref, u_ref, o_ref):
        g = g_ref[...].astype(F32)
        o_ref[...] = (g * _sigmoid(g) * u_ref[...].astype(F32)).astype(BF16)

    return pl.pallas_call(
        body, name=name, grid=(s // tr, nf),
        in_specs=[pl.BlockSpec((tr, tc), lambda i, j: (i, j)), pl.BlockSpec((tr, tc), lambda i, j: (i, j + nf))],
        out_specs=pl.BlockSpec((tr, tc), lambda i, j: (i, j)), out_shape=SDS((s, f), BF16),
        compiler_params=_cp("parallel", "parallel"))(gu, gu)


def swiglu_bwd(gu, dact, *, name, tr=512, tc=1408):
    s, f2 = gu.shape
    f = f2 // 2
    tc = _tile(f, tc)
    nf = f // tc

    def body(g_ref, u_ref, d_ref, o_ref):
        j = pl.program_id(1)
        g = g_ref[...].astype(F32)
        d = d_ref[...].astype(F32)
        sg = _sigmoid(g)

        @pl.when(j < nf)
        def _():
            o_ref[...] = (d * u_ref[...].astype(F32) * sg * (1.0 + g * (1.0 - sg))).astype(BF16)

        @pl.when(j >= nf)
        def _():
            o_ref[...] = (d * g * sg).astype(BF16)

    return pl.pallas_call(
        body, name=name, grid=(s // tr, 2 * nf),
        in_specs=[pl.BlockSpec((tr, tc), lambda i, j: (i, j % nf)), pl.BlockSpec((tr, tc), lambda i, j: (i, nf + j % nf)),
                  pl.BlockSpec((tr, tc), lambda i, j: (i, j % nf))],
        out_specs=pl.BlockSpec((tr, tc), lambda i, j: (i, j)), out_shape=SDS((s, f2), BF16),
        compiler_params=_cp("parallel", "parallel"))(gu, gu, dact)


def _gate_z(gl_ref, w_ref, b_ref):
    glb = gl_ref[...].astype(BF16)
    return glb, _dot(glb, w_ref[...]) + b_ref[...]


def gate_fwd(proj, wg, bg, *, name, tr=512):
    s, inw = proj.shape
    qk = wg.shape[1]
    glc = inw // LANES - 1

    def body(gl_ref, w_ref, b_ref, g_ref):
        _, z = _gate_z(gl_ref, w_ref, b_ref)
        g_ref[...] = (jnp.minimum(z, 0.0) - jnp.log(1.0 + jnp.exp(-jnp.abs(z)))) * (1.0 / GLA_GATE_TAU)

    return pl.pallas_call(
        body, name=name, grid=(s // tr,),
        in_specs=[pl.BlockSpec((tr, LANES), lambda i: (i, glc)), pl.BlockSpec((LANES, qk), lambda i: (0, 0)),
                  pl.BlockSpec((1, qk), lambda i: (0, 0))],
        out_specs=pl.BlockSpec((tr, qk), lambda i: (i, 0)), out_shape=SDS((s, qk), F32),
        compiler_params=_cp("parallel"))(proj, wg, bg)


def gate_bwd(dg, proj, wg, bg, *, name, tr=512):
    s, inw = proj.shape
    qk = wg.shape[1]
    glc = inw // LANES - 1

    def body(dg_ref, gl_ref, w_ref, b_ref, dgl_ref, dw_ref, db_ref):
        i = pl.program_id(0)
        glb, z = _gate_z(gl_ref, w_ref, b_ref)
        dz = dg_ref[...] * (1.0 / (1.0 + jnp.exp(z))) * (1.0 / GLA_GATE_TAU)
        dzb = dz.astype(BF16)
        dgl_ref[...] = _dot(dzb, w_ref[...], 1, 1).astype(BF16)
        pw = _dot(glb, dzb, 0, 0)
        pb = jnp.sum(dz, axis=0, keepdims=True)

        @pl.when(i == 0)
        def _():
            dw_ref[...] = pw
            db_ref[...] = pb

        @pl.when(i > 0)
        def _():
            dw_ref[...] += pw
            db_ref[...] += pb

    return pl.pallas_call(
        body, name=name, grid=(s // tr,),
        in_specs=[pl.BlockSpec((tr, qk), lambda i: (i, 0)), pl.BlockSpec((tr, LANES), lambda i: (i, glc)),
                  pl.BlockSpec((LANES, qk), lambda i: (0, 0)), pl.BlockSpec((1, qk), lambda i: (0, 0))],
        out_specs=[pl.BlockSpec((tr, LANES), lambda i: (i, 0)), pl.BlockSpec((LANES, qk), lambda i: (0, 0)),
                   pl.BlockSpec((1, qk), lambda i: (0, 0))],
        out_shape=[SDS((s, LANES), BF16), SDS((LANES, qk), F32), SDS((1, qk), F32)],
        compiler_params=_cp("arbitrary"))(dg, proj, wg, bg)


def _gla_chunk_terms(q_ref, k_ref, g_ref, c, scale):
    q = q_ref[...] * scale
    k = k_ref[...]
    gg = g_ref[...]
    b = _split_dot(_tri(c, "le"), gg)
    row = lax.broadcasted_iota(jnp.int32, gg.shape, 0)
    bm = jnp.sum(jnp.where(row < c // 2, gg, 0.0), axis=0, keepdims=True)
    bl = jnp.sum(gg, axis=0, keepdims=True)
    eb, em, emi, el = jnp.exp(b), jnp.exp(b - bm), jnp.exp(bm - b), jnp.exp(bl - b)
    return q, k, bl, eb, em, emi, el


def _causal(a):
    r = lax.broadcasted_iota(jnp.int32, a.shape, 0)
    c = lax.broadcasted_iota(jnp.int32, a.shape, 1)
    return jnp.where(r >= c, a, 0.0)


def gla_fwd(proj, g, *, name, c=GLA_CHUNK):
    s = proj.shape[0]
    qk = g.shape[1]
    dk, dv = qk // GLA_HEADS, 2 * qk // GLA_HEADS
    nc = s // c
    scale = dk ** -0.5
    kq = qk // dk

    def body(q_ref, k_ref, v_ref, g_ref, o_ref, st_ref, state):
        @pl.when(pl.program_id(1) == 0)
        def _():
            state[...] = jnp.zeros_like(state)

        q, k, bl, eb, em, emi, el = _gla_chunk_terms(q_ref, k_ref, g_ref, c, scale)
        v2 = _hilo(v_ref[...])
        st = state[...]
        st_ref[...] = st
        a = _causal(_dot3(_hilo(q * em), _hilo(k * emi), 1, 1))
        o_ref[...] = _dot3(_hilo(q * eb), _hilo(st), 1, 1) + _dot3(_hilo(a), v2)
        state[...] = st * jnp.exp(bl) + _dot3(v2, _hilo(k * el), 0, 0)

    return pl.pallas_call(
        body, name=name, grid=(GLA_HEADS, nc),
        in_specs=[pl.BlockSpec((c, dk), lambda h, i: (i, h)), pl.BlockSpec((c, dk), lambda h, i: (i, kq + h)),
                  pl.BlockSpec((c, dv), lambda h, i: (i, kq + h)), pl.BlockSpec((c, dk), lambda h, i: (i, h))],
        out_specs=[pl.BlockSpec((c, dv), lambda h, i: (i, h)),
                   pl.BlockSpec((None, None, dv, dk), lambda h, i: (h, i, 0, 0))],
        out_shape=[SDS((s, 2 * qk), F32), SDS((GLA_HEADS, nc, dv, dk), F32)],
        scratch_shapes=[pltpu.VMEM((dv, dk), F32)],
        compiler_params=_cp("parallel", "arbitrary"))(proj, proj, proj, g)


def gla_bwd(proj, g, states, do, *, name, c=GLA_CHUNK):
    s = proj.shape[0]
    qk = g.shape[1]
    dk, dv = qk // GLA_HEADS, 2 * qk // GLA_HEADS
    nc = s // c
    scale = dk ** -0.5
    kq = qk // dk

    def body(q_ref, k_ref, v_ref, g_ref, do_ref, st_ref, dq_ref, dk_ref, dv_ref, dg_ref, dstate, dgc):
        @pl.when(pl.program_id(1) == 0)
        def _():
            dstate[...] = jnp.zeros_like(dstate)
            dgc[...] = jnp.zeros_like(dgc)

        q, k, bl, eb, em, emi, el = _gla_chunk_terms(q_ref, k_ref, g_ref, c, scale)
        v2, do2 = _hilo(v_ref[...]), _hilo(do_ref[...])
        qe, qm, km, kd = _hilo(q * eb), _hilo(q * em), _hilo(k * emi), _hilo(k * el)
        ds = dstate[...]
        ds2 = _hilo(ds)
        a = _hilo(_causal(_dot3(qm, km, 1, 1)))
        dv_ref[...] = (_dot3(a, do2, 0, 0) + _dot3(kd, ds2, 1, 1)).astype(BF16)
        da = _hilo(_causal(_dot3(do2, v2, 1, 1)))
        dq = _dot3(da, km) * em + _dot3(do2, _hilo(st_ref[...])) * eb
        dkk = _dot3(da, qm, 0, 0) * emi + _dot3(v2, ds2) * el
        dstate[...] = ds * jnp.exp(bl) + _dot3(do2, qe, 0, 0)
        db = q * dq - k * dkk
        dg_ref[...] = _split_dot(_tri(c, "ge"), db) + dgc[...]
        dgc[...] += jnp.sum(db, axis=0, keepdims=True)
        dq_ref[...] = (dq * scale).astype(BF16)
        dk_ref[...] = dkk.astype(BF16)

    rev = lambda i: nc - 1 - i
    qspec = pl.BlockSpec((c, dk), lambda h, i: (rev(i), h))
    vspec = pl.BlockSpec((c, dv), lambda h, i: (rev(i), h))
    return pl.pallas_call(
        body, name=name, grid=(GLA_HEADS, nc),
        in_specs=[qspec, pl.BlockSpec((c, dk), lambda h, i: (rev(i), kq + h)),
                  pl.BlockSpec((c, dv), lambda h, i: (rev(i), kq + h)), qspec, vspec,
                  pl.BlockSpec((None, None, dv, dk), lambda h, i: (h, rev(i), 0, 0))],
        out_specs=[qspec, qspec, vspec, qspec],
        out_shape=[SDS((s, qk), BF16), SDS((s, qk), BF16), SDS((s, 2 * qk), BF16), SDS((s, qk), F32)],
        scratch_shapes=[pltpu.VMEM((dv, dk), F32), pltpu.VMEM((1, dk), F32)],
        compiler_params=_cp("parallel", "arbitrary"))(proj, proj, proj, g, do, states)


def gnorm_fwd(o, proj, gw, *, name, tr=512):
    s, v = o.shape
    dv = v // GLA_HEADS
    roff = 2 * GLA_HEADS

    def body(o_ref, r_ref, w_ref, y_ref):
        of = o_ref[...]
        rs = lax.rsqrt(jnp.mean(of * of, axis=-1, keepdims=True) + EPS)
        r = r_ref[...]
        y_ref[...] = (of * rs * w_ref[...] * (r * _sigmoid(r))).astype(BF16)

    blk = pl.BlockSpec((tr, dv), lambda i, h: (i, h))
    return pl.pallas_call(
        body, name=name, grid=(s // tr, GLA_HEADS),
        in_specs=[blk, pl.BlockSpec((tr, dv), lambda i, h: (i, roff + h)), pl.BlockSpec((1, dv), lambda i, h: (0, 0))],
        out_specs=blk, out_shape=SDS((s, v), BF16), compiler_params=_cp("parallel", "parallel"))(o, proj, gw)


def gnorm_bwd(dy, o, proj, gw, *, name, tr=512):
    s, v = o.shape
    dv = v // GLA_HEADS
    roff = 2 * GLA_HEADS

    def body(dy_ref, o_ref, r_ref, w_ref, do_ref, dr_ref, dw_ref):
        first = jnp.logical_and(pl.program_id(0) == 0, pl.program_id(1) == 0)
        of = o_ref[...]
        rs = lax.rsqrt(jnp.mean(of * of, axis=-1, keepdims=True) + EPS)
        n = of * rs
        r = r_ref[...]
        sg = _sigmoid(r)
        dyf = dy_ref[...].astype(F32)
        dn_w = dyf * (r * sg)
        dr_ref[...] = (dyf * n * w_ref[...] * sg * (1.0 + r * (1.0 - sg))).astype(BF16)
        dn = dn_w * w_ref[...]
        do_ref[...] = rs * (dn - n * jnp.mean(dn * n, axis=-1, keepdims=True))
        part = jnp.sum(dn_w * n, axis=0, keepdims=True)

        @pl.when(first)
        def _():
            dw_ref[...] = part

        @pl.when(jnp.logical_not(first))
        def _():
            dw_ref[...] += part

    blk = pl.BlockSpec((tr, dv), lambda i, h: (i, h))
    vec = pl.BlockSpec((1, dv), lambda i, h: (0, 0))
    return pl.pallas_call(
        body, name=name, grid=(s // tr, GLA_HEADS),
        in_specs=[blk, blk, pl.BlockSpec((tr, dv), lambda i, h: (i, roff + h)), vec],
        out_specs=[blk, blk, vec], out_shape=[SDS((s, v), F32), SDS((s, v), BF16), SDS((1, dv), F32)],
        compiler_params=_cp("arbitrary", "arbitrary"))(dy, o, proj, gw)


def _sb_block(kblk, q, ks, q0, scale, carry):
    tk, tq = kblk.shape[0], q.shape[0]
    z = _dot(kblk, q, 1, 1) * scale
    kpos = ks + lax.broadcasted_iota(jnp.int32, (tk, tq), 0)
    qpos = q0 + lax.broadcasted_iota(jnp.int32, (tk, tq), 1)
    mask = kpos < qpos
    sp = jnp.maximum(z, 0.0) + jnp.log(1.0 + jnp.exp(-jnp.abs(z)))
    lf = jnp.where(mask, -sp, 0.0)
    later = _split_dot(_tri(tk, "gt"), lf)
    a = jnp.where(mask, jnp.exp(z - sp + later + carry), 0.0)
    return z, sp, mask, lf, later, a


def sb_fwd(q, k, v, *, name, tq=SB_TQ, tk=SB_TK):
    s, w = q.shape
    hd = w // SB_HEADS
    nq, nkb = s // tq, s // tk
    scale = hd ** -0.5

    def body(q_ref, k_ref, v_ref, o_ref, car_ref):
        qi = pl.program_id(1)
        qb = q_ref[...]
        q0 = qi * tq
        n = (qi + 1) * (tq // tk)
        car_ref[...] = jnp.zeros_like(car_ref)

        def step(i, st):
            o, carry = st
            kj = n - 1 - i
            ks = pl.multiple_of(kj * tk, tk)
            car_ref[pl.ds(kj, 1), :] = carry
            _, _, _, lf, _, a = _sb_block(k_ref[pl.ds(ks, tk), :], qb, ks, q0, scale, carry)
            o = o + _dot(a.astype(BF16), v_ref[pl.ds(ks, tk), :], 0, 0)
            return o, carry + jnp.sum(lf, axis=0, keepdims=True)

        o, _ = lax.fori_loop(0, n, step, (jnp.zeros((tq, hd), F32), jnp.zeros((1, tq), F32)))
        o_ref[...] = o.astype(BF16)

    qspec = pl.BlockSpec((tq, hd), lambda h, i: (i, h))
    head = pl.BlockSpec((s, hd), lambda h, i: (0, h))
    return pl.pallas_call(
        body, name=name, grid=(SB_HEADS, nq), in_specs=[qspec, head, head],
        out_specs=[qspec, pl.BlockSpec((None, None, nkb, tq), lambda h, i: (h, i, 0, 0))],
        out_shape=[SDS((s, w), BF16), SDS((SB_HEADS, nq, nkb, tq), F32)],
        compiler_params=_cp("parallel", "parallel"))(q, k, v)


def sb_bwd(q, k, v, do, car, *, name, tq=SB_TQ, tk=SB_TK):
    s, w = q.shape
    hd = w // SB_HEADS
    nq, nkb = s // tq, s // tk
    scale = hd ** -0.5

    def body(q_ref, k_ref, v_ref, do_ref, car_ref, dq_ref, dk_ref, dv_ref, dk_acc, dv_acc):
        qi = pl.program_id(1)
        qb = q_ref[...]
        dob = do_ref[...]
        q0 = qi * tq
        n = (qi + 1) * (tq // tk)

        @pl.when(qi == 0)
        def _():
            dk_acc[...] = jnp.zeros_like(dk_acc)
            dv_acc[...] = jnp.zeros_like(dv_acc)

        def step(kj, st):
            dq, pcar = st
            ks = pl.multiple_of(kj * tk, tk)
            kblk = k_ref[pl.ds(ks, tk), :]
            z, sp, mask, _, _, a = _sb_block(kblk, qb, ks, q0, scale, car_ref[pl.ds(kj, 1), :])
            p = a * _dot(v_ref[pl.ds(ks, tk), :], dob, 1, 1)
            before = _split_dot(_tri(tk, "lt"), p)
            sg = jnp.exp(z - sp)
            dz = (jnp.where(mask, p * (1.0 - sg) - (pcar + before) * sg, 0.0) * scale).astype(BF16)
            dk_acc[pl.ds(ks, tk), :] += _dot(dz, qb)
            dv_acc[pl.ds(ks, tk), :] += _dot(a.astype(BF16), dob)
            return dq + _dot(dz, kblk, 0, 0), pcar + jnp.sum(p, axis=0, keepdims=True)

        dq, _ = lax.fori_loop(0, n, step, (jnp.zeros((tq, hd), F32), jnp.zeros((1, tq), F32)))
        dq_ref[...] = dq.astype(BF16)

        @pl.when(qi == nq - 1)
        def _():
            dk_ref[...] = dk_acc[...].astype(BF16)
            dv_ref[...] = dv_acc[...].astype(BF16)

    qspec = pl.BlockSpec((tq, hd), lambda h, i: (i, h))
    head = pl.BlockSpec((s, hd), lambda h, i: (0, h))
    return pl.pallas_call(
        body, name=name, grid=(SB_HEADS, nq),
        in_specs=[qspec, head, head, qspec, pl.BlockSpec((None, None, nkb, tq), lambda h, i: (h, i, 0, 0))],
        out_specs=[qspec, head, head], out_shape=[SDS((s, w), BF16)] * 3,
        scratch_shapes=[pltpu.VMEM((s, hd), F32), pltpu.VMEM((s, hd), F32)],
        compiler_params=_cp("parallel", "arbitrary"))(q, k, v, do, car)


def adamw(w, g, m, v, *, name):
    shape = w.shape
    c = shape[-1]
    r = w.size // c
    tr = _tile(r, max(8, (3 * LANES * 1024) // c), unit=8) if r >= 8 else r

    def body(w_ref, g_ref, m_ref, v_ref, d_ref, nm_ref, nv_ref):
        gf = g_ref[...]
        mn = ADAM_B1 * m_ref[...] + (1.0 - ADAM_B1) * gf
        vn = ADAM_B2 * v_ref[...] + (1.0 - ADAM_B2) * (gf * gf)
        m_hat = mn / (1.0 - ADAM_B1 ** ADAM_STEP)
        v_hat = vn / (1.0 - ADAM_B2 ** ADAM_STEP)
        d_ref[...] = -ADAM_LR * (m_hat / (jnp.sqrt(v_hat) + ADAM_EPS) + ADAM_WD * w_ref[...])
        nm_ref[...] = mn
        nv_ref[...] = vn

    blk = pl.BlockSpec((tr, c), lambda i: (i, 0))
    outs = pl.pallas_call(
        body, name=name, grid=(r // tr,), in_specs=[blk] * 4, out_specs=[blk] * 3,
        out_shape=[SDS((r, c), F32)] * 3, compiler_params=_cp("parallel"))(
            *(t.reshape(r, c) for t in (w, g, m, v)))
    return tuple(o.reshape(shape) for o in outs)


def _ffn_fwd(h, nw, w_gu, w_dn, tag):
    f = rmsnorm_fwd(h, nw, name=f"{tag}_norm")
    gu = matmul(f, w_gu, name=f"{tag}_gate_up", out_dtype=BF16)
    act = swiglu_fwd(gu, name=f"{tag}_act")
    return matmul(act, w_dn, name=f"{tag}_down", residual=h), (f, gu, act)


def _ffn_bwd(dh, h, nw, w_gu, w_dn, saved, tag):
    f, gu, act = saved
    dact = matmul(dh, w_dn, name=f"{tag}_dact", tb=True, out_dtype=BF16)
    dw_dn = matmul(act, dh, name=f"{tag}_dw_down", ta=True, out_dtype=BF16)
    dgu = swiglu_bwd(gu, dact, name=f"{tag}_dgu")
    dw_gu = matmul(f, dgu, name=f"{tag}_dw_gate_up", ta=True, out_dtype=BF16)
    df = matmul(dgu, w_gu, name=f"{tag}_df", tb=True)
    dh_in, dnw = rmsnorm_bwd(df, h, nw, dh, name=f"{tag}_dnorm")
    return dh_in, dnw, dw_gu, dw_dn


def local_step(x, target, p):
    a0 = rmsnorm_fwd(x, p["an0"], name="l0_attn_norm")
    proj = matmul(a0, p["w_in"], name="gla_in")
    g = gate_fwd(proj, p["wg"], p["bg"], name="gla_gate")
    o, states = gla_fwd(proj, g, name="gla_scan")
    og = gnorm_fwd(o, proj, p["gw"], name="gla_outnorm")
    h1 = matmul(og, p["w_out"], name="gla_out", residual=x)
    h2, ffn0 = _ffn_fwd(h1, p["fn0"], p["w_gu0"], p["w_dn0"], "ffn0")
    kvn = rmsnorm_fwd(h2, p["kvn"], name="kv_norm")
    km = matmul(kvn, p["w_k"], name="sb_k", out_dtype=BF16)
    vm = matmul(kvn, p["w_v"], name="sb_v", out_dtype=BF16)
    a1 = rmsnorm_fwd(h2, p["an1"], name="l1_attn_norm")
    q2 = matmul(a1, p["w_q"], name="sb_q", out_dtype=BF16)
    o2, car = sb_fwd(q2, km, vm, name="sb_attn")
    h3 = matmul(o2, p["w_so"], name="sb_out", residual=h2)
    h4, ffn1 = _ffn_fwd(h3, p["fn1"], p["w_gu1"], p["w_dn1"], "ffn1")
    dh4, d_fin, loss_row = final_loss(h4, p["finn"], target, name="final_loss")

    dh3, d_fn1, dw_gu1, dw_dn1 = _ffn_bwd(dh4, h3, p["fn1"], p["w_gu1"], p["w_dn1"], ffn1, "ffn1")
    do2 = matmul(dh3, p["w_so"], name="sb_do", tb=True, out_dtype=BF16)
    dw_so = matmul(o2, dh3, name="sb_dw_out", ta=True, out_dtype=BF16)
    dq2, dkm, dvm = sb_bwd(q2, km, vm, do2, car, name="sb_attn_bwd")
    dw_q = matmul(a1, dq2, name="sb_dw_q", ta=True, out_dtype=BF16)
    da1 = matmul(dq2, p["w_q"], name="sb_da", tb=True)
    dh2, d_an1 = rmsnorm_bwd(da1, h2, p["an1"], dh3, name="l1_attn_dnorm")
    dw_k = matmul(kvn, dkm, name="sb_dw_k", ta=True, out_dtype=BF16)
    dw_v = matmul(kvn, dvm, name="sb_dw_v", ta=True, out_dtype=BF16)
    dkvn = matmul(dkm, p["w_k"], name="sb_dkvn_k", tb=True)
    dkvn = matmul(dvm, p["w_v"], name="sb_dkvn_v", tb=True, residual=dkvn)
    dh2, d_kvn = rmsnorm_bwd(dkvn, h2, p["kvn"], dh2, name="kv_dnorm")
    dh1, d_fn0, dw_gu0, dw_dn0 = _ffn_bwd(dh2, h1, p["fn0"], p["w_gu0"], p["w_dn0"], ffn0, "ffn0")
    dog = matmul(dh1, p["w_out"], name="gla_dog", tb=True, out_dtype=BF16)
    dw_out = matmul(og, dh1, name="gla_dw_out", ta=True, out_dtype=BF16)
    do, dr, d_gw = gnorm_bwd(dog, o, proj, p["gw"], name="gla_outnorm_bwd")
    dq, dk, dv, dg = gla_bwd(proj, g, states, do, name="gla_scan_bwd")
    dgl, d_wg, d_bg = gate_bwd(dg, proj, p["wg"], p["bg"], name="gla_gate_bwd")
    dproj = jnp.concatenate([dq, dk, dv, dr, dgl], axis=1)
    dw_in = matmul(a0, dproj, name="gla_dw_in", ta=True, out_dtype=BF16)
    da0 = matmul(dproj, p["w_in"], name="gla_da", tb=True)
    dx, d_an0 = rmsnorm_bwd(da0, x, p["an0"], dh1, name="l0_attn_dnorm")

    grads = dict(an0=d_an0, an1=d_an1, fn0=d_fn0, fn1=d_fn1, kvn=d_kvn, finn=d_fin, wg=d_wg, bg=d_bg, gw=d_gw,
                 w_in=dw_in, w_out=dw_out, w_k=dw_k, w_v=dw_v, w_q=dw_q, w_so=dw_so,
                 w_gu0=dw_gu0, w_gu1=dw_gu1, w_dn0=dw_dn0, w_dn1=dw_dn1)
    return loss_row, dx, grads


ANY = pl.BlockSpec(memory_space=pl.ANY)


def _coords():
    return lax.axis_index("x"), lax.axis_index("y"), lax.axis_index("c")


def _other_chips(x, y):
    return [(1 - x, y), (x, 1 - y), (1 - x, 1 - y)]


def _remote(src, dst, send_sem, recv_sem, dev):
    return pltpu.make_async_remote_copy(src_ref=src, dst_ref=dst, send_sem=send_sem, recv_sem=recv_sem,
                                        device_id=dev, device_id_type=MESH)


def allgather_chips(wp, *, name):
    rp, w = wp.shape
    rh = rp // 2

    def body(w_ref, out_ref, send_sems, recv_sems, local_sem):
        x, y, c = _coords()
        me = 2 * x + y
        chips = _other_chips(x, y)

        def copy(k, chip, half, dev, src=None):
            dst = out_ref.at[chip, pl.ds(half * rh, rh)]
            return _remote(dst if src is None else src, dst, send_sems.at[k], recv_sems.at[k], dev)

        mine = pltpu.make_async_copy(w_ref, out_ref.at[me], local_sem)
        mine.start()
        first = [copy(k, me, c, (cx, cy, c), src=w_ref.at[pl.ds(c * rh, rh)]) for k, (cx, cy) in enumerate(chips)]
        for cp in first:
            cp.start()
        passed = [copy(3 + k, 2 * cx + cy, c, (x, y, 1 - c)) for k, (cx, cy) in enumerate(chips)]
        for k, (cx, cy) in enumerate(chips):
            copy(k, 2 * cx + cy, c, (x, y, c)).wait_recv()
            passed[k].start()
        for k, (cx, cy) in enumerate(chips):
            copy(3 + k, 2 * cx + cy, 1 - c, (x, y, c)).wait_recv()
        for cp in first + passed:
            cp.wait_send()
        mine.wait()

    return pl.pallas_call(
        body, name=name, in_specs=[ANY], out_specs=ANY, out_shape=SDS((N_CHIPS, rp, w), wp.dtype),
        scratch_shapes=[pltpu.SemaphoreType.DMA((6,)), pltpu.SemaphoreType.DMA((6,)), pltpu.SemaphoreType.DMA(())])(wp)


def sibling_exchange(g, *, name):
    n, rp, w = g.shape
    rh = rp // 2

    def body(g_ref, a_ref, send_sem, recv_sem):
        x, y, c = _coords()
        cp = _remote(g_ref.at[:, pl.ds((1 - c) * rh, rh)], a_ref, send_sem, recv_sem, (x, y, 1 - c))
        cp.start()
        cp.wait()

    return pl.pallas_call(
        body, name=name, in_specs=[ANY], out_specs=ANY, out_shape=SDS((n, rh, w), g.dtype),
        scratch_shapes=[pltpu.SemaphoreType.DMA(()), pltpu.SemaphoreType.DMA(())])(g)


def half_add(g, a, core, *, name, tr=512):
    n, rp, w = g.shape
    rh = rp // 2
    tr = _tile(rh, tr, unit=16)
    nb = rh // tr

    def body(c_ref, g_ref, a_ref, o_ref):
        o_ref[...] = (g_ref[...].astype(F32) + a_ref[...].astype(F32)).astype(o_ref.dtype)

    blk = pl.BlockSpec((None, tr, w), lambda s, i, c_ref: (s, i, 0))
    return pl.pallas_call(
        body, name=name, out_shape=SDS((n, rh, w), g.dtype),
        grid_spec=pltpu.PrefetchScalarGridSpec(
            num_scalar_prefetch=1, grid=(n, nb),
            in_specs=[pl.BlockSpec((None, tr, w), lambda s, i, c_ref: (s, c_ref[0] * nb + i, 0)), blk], out_specs=blk),
        compiler_params=_cp("parallel", "parallel"))(core, g, a)


def chip_scatter(p, *, name):
    n, rh, w = p.shape

    def body(p_ref, b_ref, send_sems, recv_sems):
        x, y, c = _coords()
        cps = [_remote(p_ref.at[2 * cx + cy], b_ref.at[k], send_sems.at[k], recv_sems.at[k], (cx, cy, c))
               for k, (cx, cy) in enumerate(_other_chips(x, y))]
        for cp in cps:
            cp.start()
        for cp in cps:
            cp.wait()

    return pl.pallas_call(
        body, name=name, in_specs=[ANY], out_specs=ANY, out_shape=SDS((n - 1, rh, w), p.dtype),
        scratch_shapes=[pltpu.SemaphoreType.DMA((3,)), pltpu.SemaphoreType.DMA((3,))])(p)


def chip_sum(p, b, chip, *, name, tr=512):
    _, rh, w = p.shape
    tr = _tile(rh, tr, unit=16)

    def body(c_ref, p_ref, b_ref, o_ref):
        t = p_ref[...].astype(F32)
        for k in range(N_CHIPS - 1):
            t = t + b_ref[k].astype(F32)
        o_ref[...] = t

    return pl.pallas_call(
        body, name=name, out_shape=SDS((rh, w), F32),
        grid_spec=pltpu.PrefetchScalarGridSpec(
            num_scalar_prefetch=1, grid=(rh // tr,),
            in_specs=[pl.BlockSpec((None, tr, w), lambda i, c_ref: (c_ref[0], i, 0)),
                      pl.BlockSpec((N_CHIPS - 1, tr, w), lambda i, c_ref: (0, i, 0))],
            out_specs=pl.BlockSpec((tr, w), lambda i, c_ref: (i, 0))),
        compiler_params=_cp("parallel"))(chip, p, b)


def sibling_gather(t, *, name):
    rh, w = t.shape

    def body(t_ref, out_ref, send_sem, recv_sem, local_sem):
        x, y, c = _coords()
        here = out_ref.at[pl.ds(c * rh, rh)]
        loc = pltpu.make_async_copy(t_ref, here, local_sem)
        loc.start()
        cp = _remote(t_ref, here, send_sem, recv_sem, (x, y, 1 - c))
        cp.start()
        _remote(t_ref, out_ref.at[pl.ds((1 - c) * rh, rh)], send_sem, recv_sem, (x, y, 1 - c)).wait_recv()
        cp.wait_send()
        loc.wait()

    return pl.pallas_call(
        body, name=name, in_specs=[ANY], out_specs=ANY, out_shape=SDS((2 * rh, w), t.dtype),
        scratch_shapes=[pltpu.SemaphoreType.DMA(()), pltpu.SemaphoreType.DMA(()), pltpu.SemaphoreType.DMA(())])(t)


def allgather_all(sm, *, name):
    r, w = sm.shape

    def body(s_ref, out_ref, send_sems, recv_sems, local_sem):
        x, y, c = _coords()
        me = 4 * x + 2 * y + c

        def peer(rel):
            flip = lambda v, bit: 1 - v if bit else v
            return flip(x, rel & 4), flip(y, rel & 2), flip(c, rel & 1)

        loc = pltpu.make_async_copy(s_ref, out_ref.at[me], local_sem)
        loc.start()
        cps = [_remote(s_ref, out_ref.at[me], send_sems.at[rel - 1], recv_sems.at[rel - 1], peer(rel)) for rel in range(1, N_DEV)]
        for cp in cps:
            cp.start()
        for rel in range(1, N_DEV):
            px, py, pc = peer(rel)
            _remote(s_ref, out_ref.at[4 * px + 2 * py + pc], send_sems.at[rel - 1], recv_sems.at[rel - 1], (px, py, pc)).wait_recv()
        for cp in cps:
            cp.wait_send()
        loc.wait()

    return pl.pallas_call(
        body, name=name, in_specs=[ANY], out_specs=ANY, out_shape=SDS((N_DEV, r, w), sm.dtype),
        scratch_shapes=[pltpu.SemaphoreType.DMA((N_DEV - 1,)), pltpu.SemaphoreType.DMA((N_DEV - 1,)), pltpu.SemaphoreType.DMA(())])(sm)


def sum_blocks(a, *, name):
    n, r, w = a.shape

    def body(a_ref, o_ref):
        t = a_ref[0]
        for k in range(1, n):
            t = t + a_ref[k]
        o_ref[...] = t

    return pl.pallas_call(body, name=name, out_shape=SDS((r, w), F32))(a)


def _pack(parts, lead, width, row_unit):
    cat = jnp.concatenate(parts, axis=-1)
    n = cat.shape[-1]
    rows = -(-n // (width * row_unit)) * row_unit
    cat = jnp.pad(cat, [(0, 0)] * len(lead) + [(0, rows * width - n)])
    return cat.reshape(*lead, rows, width)


def _segments(flat, sizes):
    out, off = [], 0
    for n in sizes:
        out.append(flat[..., off:off + n])
        off += n
    return out


def kernel(x, attn_norm_w, ffn_norm_w, gla_w_in, gla_w_gate_up, gla_b_gate, gla_gnorm_w, gla_w_out, kv_norm_w, sb_w_kv, sb_w_q, sb_w_out, ffn_w_gate_up, ffn_w_down, final_norm_w, loss_target, m_attn_norm_w, m_ffn_norm_w, m_gla_w_in, m_gla_w_gate_up, m_gla_b_gate, m_gla_gnorm_w, m_gla_w_out, m_kv_norm_w, m_sb_w_kv, m_sb_w_q, m_sb_w_out, m_ffn_w_gate_up, m_ffn_w_down, m_final_norm_w, v_attn_norm_w, v_ffn_norm_w, v_gla_w_in, v_gla_w_gate_up, v_gla_b_gate, v_gla_gnorm_w, v_gla_w_out, v_kv_norm_w, v_sb_w_kv, v_sb_w_q, v_sb_w_out, v_ffn_w_gate_up, v_ffn_w_down, v_final_norm_w):
    xi, yi, ci = _coords()
    core = ci.astype(jnp.int32).reshape(1)
    chip = (2 * xi + yi).astype(jnp.int32)
    _, s, d = x.shape
    rank = gla_w_gate_up.shape[1]

    big = [gla_w_in, gla_w_out, sb_w_kv, sb_w_q, sb_w_out, ffn_w_gate_up, ffn_w_down]
    big_sizes = [a.size for a in big]
    gathered = allgather_chips(_pack([a.astype(BF16).reshape(-1) for a in big], (), PACK_W, PACK_ROW_UNIT), name="gather_weights")
    w_in, w_out, w_kv, w_q, w_so, w_gu, w_dn = _segments(gathered.reshape(N_CHIPS, -1), big_sizes)
    w_in = w_in.reshape(N_CHIPS, d, -1).transpose(1, 0, 2).reshape(d, -1)
    in_w = w_in.shape[1]
    w_in = jnp.pad(w_in, ((0, 0), (0, -in_w % LANES)))
    w_kv = w_kv.reshape(N_CHIPS, d, -1)
    half = N_CHIPS // 2
    w_gu = w_gu.reshape(N_CHIPS, 2, d, -1).transpose(1, 2, 0, 3).reshape(2, d, -1)
    w_dn = w_dn.reshape(N_CHIPS, 2, -1, d).transpose(1, 0, 2, 3).reshape(2, -1, d)
    small = [gla_w_gate_up, gla_b_gate, gla_gnorm_w]
    small_all = allgather_all(_pack([a.reshape(-1) for a in small], (), LANES, SUBLANES), name="gather_gate_weights")
    wg, bg, gw = _segments(small_all[::2].reshape(N_CHIPS, -1), [a.size for a in small])
    wg = wg.reshape(N_CHIPS, rank, -1).transpose(1, 0, 2).reshape(rank, -1)
    p = dict(
        an0=attn_norm_w[0], an1=attn_norm_w[1], fn0=ffn_norm_w[0], fn1=ffn_norm_w[1], kvn=kv_norm_w, finn=final_norm_w,
        w_in=w_in, wg=jnp.pad(wg, ((0, LANES - rank), (0, 0))).astype(BF16), bg=bg.reshape(1, -1), gw=gw.reshape(1, -1),
        w_out=w_out.reshape(-1, d), w_k=w_kv[:half].transpose(1, 0, 2).reshape(d, -1),
        w_v=w_kv[half:].transpose(1, 0, 2).reshape(d, -1), w_q=w_q.reshape(d, -1), w_so=w_so.reshape(-1, d),
        w_gu0=w_gu[0], w_gu1=w_gu[1], w_dn0=w_dn[0], w_dn1=w_dn[1])

    loss_row, dx, g = local_step(x[0], loss_target[0], p)
    loss = lax.psum(loss_row[0, 0], ("x", "y", "c"))

    parts = [
        g["w_in"][:, :in_w].reshape(d, N_CHIPS, -1).transpose(1, 0, 2),
        g["w_out"],
        jnp.concatenate([g["w_k"].reshape(d, half, -1).transpose(1, 0, 2), g["w_v"].reshape(d, half, -1).transpose(1, 0, 2)]),
        g["w_q"], g["w_so"],
        jnp.stack([g["w_gu0"], g["w_gu1"]]).reshape(2, d, N_CHIPS, -1).transpose(2, 0, 1, 3),
        jnp.stack([g["w_dn0"], g["w_dn1"]]).reshape(2, N_CHIPS, -1, d).transpose(1, 0, 2, 3),
    ]
    gp = _pack([t.reshape(N_CHIPS, -1) for t in parts], (N_CHIPS,), PACK_W, PACK_ROW_UNIT)
    from_sibling = sibling_exchange(gp, name="grads_to_sibling")
    pair = half_add(gp, from_sibling, core, name="grads_pair_sum")
    from_chips = chip_scatter(pair, name="grads_to_chips")
    mine = chip_sum(pair, from_chips, chip.reshape(1), name="grads_chip_sum")
    total = sibling_gather(mine, name="grads_from_sibling")
    big_grads = [t.reshape(a.shape) for t, a in zip(_segments(total.reshape(-1), big_sizes), big)]

    vecs = [jnp.concatenate([g["an0"], g["an1"]]), jnp.concatenate([g["fn0"], g["fn1"]]), g["kvn"], g["finn"],
            g["wg"][:rank], g["bg"], g["gw"]]
    vec_sizes = [t.size for t in vecs]
    gathered_vecs = allgather_all(_pack([t.reshape(-1) for t in vecs], (), LANES, SUBLANES), name="gather_small_grads")
    d_an, d_fn, d_kvn, d_fin, d_wg, d_bg, d_gw = _segments(sum_blocks(gathered_vecs, name="sum_small_grads").reshape(-1), vec_sizes)

    def shard(t, like):
        return lax.dynamic_index_in_dim(t.reshape(-1, N_CHIPS, like.shape[-1]), chip, axis=1, keepdims=False).reshape(like.shape)

    grads = dict(
        attn_norm_w=d_an.reshape(attn_norm_w.shape), ffn_norm_w=d_fn.reshape(ffn_norm_w.shape),
        gla_w_in=big_grads[0], gla_w_gate_up=shard(d_wg, gla_w_gate_up), gla_b_gate=shard(d_bg, gla_b_gate),
        gla_gnorm_w=shard(d_gw, gla_gnorm_w), gla_w_out=big_grads[1], kv_norm_w=d_kvn.reshape(kv_norm_w.shape),
        sb_w_kv=big_grads[2], sb_w_q=big_grads[3], sb_w_out=big_grads[4], ffn_w_gate_up=big_grads[5],
        ffn_w_down=big_grads[6], final_norm_w=d_fin.reshape(final_norm_w.shape))
    weights = dict(
        attn_norm_w=(attn_norm_w, m_attn_norm_w, v_attn_norm_w), ffn_norm_w=(ffn_norm_w, m_ffn_norm_w, v_ffn_norm_w),
        gla_w_in=(gla_w_in, m_gla_w_in, v_gla_w_in), gla_w_gate_up=(gla_w_gate_up, m_gla_w_gate_up, v_gla_w_gate_up),
        gla_b_gate=(gla_b_gate, m_gla_b_gate, v_gla_b_gate), gla_gnorm_w=(gla_gnorm_w, m_gla_gnorm_w, v_gla_gnorm_w),
        gla_w_out=(gla_w_out, m_gla_w_out, v_gla_w_out), kv_norm_w=(kv_norm_w, m_kv_norm_w, v_kv_norm_w),
        sb_w_kv=(sb_w_kv, m_sb_w_kv, v_sb_w_kv), sb_w_q=(sb_w_q, m_sb_w_q, v_sb_w_q), sb_w_out=(sb_w_out, m_sb_w_out, v_sb_w_out),
        ffn_w_gate_up=(ffn_w_gate_up, m_ffn_w_gate_up, v_ffn_w_gate_up), ffn_w_down=(ffn_w_down, m_ffn_w_down, v_ffn_w_down),
        final_norm_w=(final_norm_w, m_final_norm_w, v_final_norm_w))
    names = list(weights)
    stepped = [adamw(weights[n][0], grads[n], weights[n][1], weights[n][2], name=f"adamw_{n}") for n in names]
    return (loss, dx.reshape(x.shape), *[grads[n] for n in names], *[t[0] for t in stepped], *[t[1] for t in stepped],
            *[t[2] for t in stepped])
```

```python
import functools

import jax
import jax.numpy as jnp
from jax import lax
from jax.experimental import pallas as pl
from jax.experimental.pallas import tpu as pltpu

F32 = jnp.float32
BF16 = jnp.bfloat16
SDS = jax.ShapeDtypeStruct
MESH = pl.DeviceIdType.MESH

EPS = 1e-6
GLA_HEADS = 4
GLA_GATE_RANK = 16
GLA_GATE_TAU = 16.0
GLA_CHUNK = 128
SB_HEADS = 16
SB_TQ = 512
SB_TK = 128
ADAM_LR = 0.001
ADAM_B1 = 0.9
ADAM_B2 = 0.999
ADAM_EPS = 1e-08
ADAM_WD = 0.01
ADAM_STEP = 10

LANES = 128
SUBLANES = 8
N_CHIPS = 4
N_DEV = 8
VMEM_LIMIT = 56 * 1024 * 1024


def _tile(dim, target, unit=LANES):
    if dim <= target:
        return dim
    t = (target // unit) * unit
    while t >= unit:
        if dim % t == 0:
            return t
        t -= unit
    raise ValueError(f"no tile for {dim}")


def _cp(*sem):
    return pltpu.CompilerParams(dimension_semantics=sem, vmem_limit_bytes=VMEM_LIMIT)


def _sigmoid(x):
    return 1.0 / (1.0 + jnp.exp(-x))


def _dot(a, b, ca=1, cb=0):
    return lax.dot_general(a, b, (((ca,), (cb,)), ((), ())), preferred_element_type=F32)


def _split_dot(tri, x):
    hi = x.astype(BF16)
    lo = (x - hi.astype(F32)).astype(BF16)
    return _dot(tri, hi) + _dot(tri, lo)


def _hilo(x):
    hi = x.astype(BF16)
    return hi, (x - hi.astype(F32)).astype(BF16)


def _dot3(a, b, ca=1, cb=0):
    return _dot(a[0], b[0], ca, cb) + _dot(a[0], b[1], ca, cb) + _dot(a[1], b[0], ca, cb)


def _tri(n, kind):
    r = lax.broadcasted_iota(jnp.int32, (n, n), 0)
    c = lax.broadcasted_iota(jnp.int32, (n, n), 1)
    m = {"le": c <= r, "ge": c >= r, "lt": c < r, "gt": c > r}[kind]
    return jnp.where(m, 1.0, 0.0).astype(BF16)


def _div(i, n):
    return i if n == 1 else lax.div(i, n)


def _rem(i, n):
    return 0 if n == 1 else lax.rem(i, n)


class View:
    def __init__(self, arr, kind="plain", lead=(), g0=0, ng=None):
        self.arr, self.kind, self.lead, self.g0 = arr, kind, tuple(lead), g0
        self.ng = (arr.shape[0] - g0) if ng is None else ng
        r, c = arr.shape[-2:]
        self.runit, self.cunit = r, c
        self.shape = {"plain": (r, c), "cols": (r, self.ng * c), "rows": (self.ng * r, c)}[kind]

    def spec(self, br, bc, rfn, cfn):
        if self.kind == "plain":
            return pl.BlockSpec((br, bc), lambda *g: (rfn(*g), cfn(*g)))
        none = (None,) * (1 + len(self.lead))
        if self.kind == "cols":
            per = self.cunit // bc
            return pl.BlockSpec(none + (br, bc), lambda *g: (self.g0 + _div(cfn(*g), per), *self.lead, rfn(*g), _rem(cfn(*g), per)))
        per = self.runit // br
        return pl.BlockSpec(none + (br, bc), lambda *g: (self.g0 + _div(rfn(*g), per), *self.lead, _rem(rfn(*g), per), cfn(*g)))


def _as_view(a):
    return a if isinstance(a, View) else View(a)


def matmul(a, b, *, name, ta=False, tb=False, out_dtype=F32, residual=None, out_chips=None, tm=1024, tn=1024, tk=512):
    a, b = _as_view(a), _as_view(b)
    (k, m) = a.shape if ta else a.shape[::-1]
    (n, kb) = b.shape if tb else b.shape[::-1]
    assert k == kb, (a.shape, b.shape, ta, tb)
    m_unit = a.cunit if ta else a.runit
    ka_unit = a.runit if ta else a.cunit
    n_unit = b.runit if tb else b.cunit
    kb_unit = b.cunit if tb else b.runit
    if out_chips is not None:
        n_unit = min(n_unit, n // out_chips)
    tm, tn = _tile(min(m, m_unit), tm), _tile(min(n, n_unit), tn)
    tk = _tile(min(k, ka_unit, kb_unit), tk)
    assert ka_unit % tk == 0 and kb_unit % tk == 0, (ka_unit, kb_unit, tk)
    nk = k // tk
    ca, cb = (0 if ta else 1), (1 if tb else 0)

    def body(*refs):
        if residual is None:
            a_ref, b_ref, o_ref, acc = refs
        else:
            a_ref, b_ref, r_ref, o_ref, acc = refs
        kk = pl.program_id(2)

        @pl.when(kk == 0)
        def _():
            acc[...] = jnp.zeros_like(acc)

        acc[...] += _dot(a_ref[...].astype(BF16), b_ref[...].astype(BF16), ca, cb)

        @pl.when(kk == nk - 1)
        def _():
            r = acc[...]
            if residual is not None:
                r = r + r_ref[...]
            o_ref[...] = r.astype(out_dtype)

    gi, gj, gk = (lambda i, j, kk: i), (lambda i, j, kk: j), (lambda i, j, kk: kk)
    a_spec = a.spec(tk, tm, gk, gi) if ta else a.spec(tm, tk, gi, gk)
    b_spec = b.spec(tn, tk, gj, gk) if tb else b.spec(tk, tn, gk, gj)
    if out_chips is None:
        out = View(SDS((m, n), out_dtype))
    else:
        out = View(SDS((out_chips, m, n // out_chips), out_dtype), "cols")
    o_spec = out.spec(tm, tn, gi, gj)
    in_specs, args = [a_spec, b_spec], [a.arr, b.arr]
    if residual is not None:
        in_specs.append(pl.BlockSpec((tm, tn), lambda i, j, kk: (i, j)))
        args.append(residual)
    return pl.pallas_call(
        body, name=name, grid=(m // tm, n // tn, nk), in_specs=in_specs, out_specs=o_spec,
        out_shape=out.arr, scratch_shapes=[pltpu.VMEM((tm, tn), F32)],
        compiler_params=_cp("parallel", "parallel", "arbitrary"))(*args)


def rmsnorm_fwd(x, w, *, name, tr=256):
    s, d = x.shape

    def body(x_ref, w_ref, o_ref):
        xf = x_ref[...]
        r = lax.rsqrt(jnp.mean(xf * xf, axis=-1, keepdims=True) + EPS)
        o_ref[...] = (xf * r * w_ref[...]).astype(BF16)

    row = pl.BlockSpec((tr, d), lambda i: (i, 0))
    return pl.pallas_call(
        body, name=name, grid=(s // tr,), in_specs=[row, pl.BlockSpec((1, d), lambda i: (0, 0))], out_specs=row,
        out_shape=SDS((s, d), BF16), compiler_params=_cp("parallel"))(x, w.reshape(1, d))


def rmsnorm_bwd(dy, x, w, dres, *, name, tr=256):
    s, d = x.shape

    def body(dy_ref, x_ref, w_ref, dres_ref, dx_ref, dw_ref):
        i = pl.program_id(0)
        xf = x_ref[...]
        r = lax.rsqrt(jnp.mean(xf * xf, axis=-1, keepdims=True) + EPS)
        xh = xf * r
        dyf = dy_ref[...].astype(F32)
        dxh = dyf * w_ref[...]
        dx_ref[...] = dres_ref[...] + r * (dxh - xh * jnp.mean(dxh * xh, axis=-1, keepdims=True))
        part = jnp.sum(dyf * xh, axis=0, keepdims=True)

        @pl.when(i == 0)
        def _():
            dw_ref[...] = part

        @pl.when(i > 0)
        def _():
            dw_ref[...] += part

    row = pl.BlockSpec((tr, d), lambda i: (i, 0))
    vec = pl.BlockSpec((1, d), lambda i: (0, 0))
    return pl.pallas_call(
        body, name=name, grid=(s // tr,), in_specs=[row, row, vec, row], out_specs=[row, vec],
        out_shape=[SDS((s, d), F32), SDS((1, d), F32)], compiler_params=_cp("arbitrary"))(dy, x, w.reshape(1, d), dres)


def final_loss(h, w, target, *, name, tr=256):
    s, d = h.shape

    def body(h_ref, w_ref, t_ref, dh_ref, dw_ref, loss_ref):
        i = pl.program_id(0)
        xf = h_ref[...]
        r = lax.rsqrt(jnp.mean(xf * xf, axis=-1, keepdims=True) + EPS)
        xh = xf * r
        err = xh * w_ref[...] - t_ref[...]
        lpart = 0.5 * jnp.sum(jnp.sum(err * err, axis=-1, keepdims=True) * (1.0 / d), axis=0, keepdims=True)
        dy = err * (1.0 / d)
        dxh = dy * w_ref[...]
        dh_ref[...] = r * (dxh - xh * jnp.mean(dxh * xh, axis=-1, keepdims=True))
        part = jnp.sum(dy * xh, axis=0, keepdims=True)
        lrow = jnp.broadcast_to(lpart, (1, LANES))

        @pl.when(i == 0)
        def _():
            dw_ref[...] = part
            loss_ref[...] = lrow

        @pl.when(i > 0)
        def _():
            dw_ref[...] += part
            loss_ref[...] += lrow

    row = pl.BlockSpec((tr, d), lambda i: (i, 0))
    vec = pl.BlockSpec((1, d), lambda i: (0, 0))
    return pl.pallas_call(
        body, name=name, grid=(s // tr,), in_specs=[row, vec, row],
        out_specs=[row, vec, pl.BlockSpec((1, LANES), lambda i: (0, 0))],
        out_shape=[SDS((s, d), F32), SDS((1, d), F32), SDS((1, LANES), F32)],
        compiler_params=_cp("arbitrary"))(h, w.reshape(1, d), target)


def swiglu_fwd(gu, *, name, tr=512, tc=1408):
    s, f2 = gu.shape
    f = f2 // 2
    tc = _tile(f, tc)
    nf = f // tc

    def body(g_ref, u_ref, o_ref):
        g = g_ref[...].astype(F32)
        o_ref[...] = (g * _sigmoid(g) * u_ref[...].astype(F32)).astype(BF16)

    return pl.pallas_call(
        body, name=name, grid=(s // tr, nf),
        in_specs=[pl.BlockSpec((tr, tc), lambda i, j: (i, j)), pl.BlockSpec((tr, tc), lambda i, j: (i, j + nf))],
        out_specs=pl.BlockSpec((tr, tc), lambda i, j: (i, j)), out_shape=SDS((s, f), BF16),
        compiler_params=_cp("parallel", "parallel"))(gu, gu)


def swiglu_bwd(gu, dact, *, name, tr=512, tc=1408):
    s, f2 = gu.shape
    f = f2 // 2
    tc = _tile(f, tc)
    nf = f // tc

    def body(g_ref, u_ref, d_ref, o_ref):
        j = pl.program_id(1)
        g = g_ref[...].astype(F32)
        d = d_ref[...].astype(F32)
        sg = _sigmoid(g)

        @pl.when(j < nf)
        def _():
            o_ref[...] = (d * u_ref[...].astype(F32) * sg * (1.0 + g * (1.0 - sg))).astype(BF16)

        @pl.when(j >= nf)
        def _():
            o_ref[...] = (d * g * sg).astype(BF16)

    return pl.pallas_call(
        body, name=name, grid=(s // tr, 2 * nf),
        in_specs=[pl.BlockSpec((tr, tc), lambda i, j: (i, j % nf)), pl.BlockSpec((tr, tc), lambda i, j: (i, nf + j % nf)),
                  pl.BlockSpec((tr, tc), lambda i, j: (i, j % nf))],
        out_specs=pl.BlockSpec((tr, tc), lambda i, j: (i, j)), out_shape=SDS((s, f2), BF16),
        compiler_params=_cp("parallel", "parallel"))(gu, gu, dact)


def _gate_z(gl_ref, w_ref, b_ref):
    glb = gl_ref[...].astype(BF16)
    return glb, _dot(glb, w_ref[...]) + b_ref[...]


def gate_fwd(proj, wg, bg, *, name, tr=512):
    s, inw = proj.shape
    qk = wg.shape[1]
    glc = inw // LANES - 1

    def body(gl_ref, w_ref, b_ref, g_ref):
        _, z = _gate_z(gl_ref, w_ref, b_ref)
        g_ref[...] = (jnp.minimum(z, 0.0) - jnp.log(1.0 + jnp.exp(-jnp.abs(z)))) * (1.0 / GLA_GATE_TAU)

    return pl.pallas_call(
        body, name=name, grid=(s // tr,),
        in_specs=[pl.BlockSpec((tr, LANES), lambda i: (i, glc)), pl.BlockSpec((LANES, qk), lambda i: (0, 0)),
                  pl.BlockSpec((1, qk), lambda i: (0, 0))],
        out_specs=pl.BlockSpec((tr, qk), lambda i: (i, 0)), out_shape=SDS((s, qk), F32),
        compiler_params=_cp("parallel"))(proj, wg, bg)


def gate_bwd(dg, proj, wg, bg, *, name, tr=512):
    s, inw = proj.shape
    qk = wg.shape[1]
    glc = inw // LANES - 1

    def body(dg_ref, gl_ref, w_ref, b_ref, dgl_ref, dw_ref, db_ref):
        i = pl.program_id(0)
        glb, z = _gate_z(gl_ref, w_ref, b_ref)
        dz = dg_ref[...] * (1.0 / (1.0 + jnp.exp(z))) * (1.0 / GLA_GATE_TAU)
        dzb = dz.astype(BF16)
        dgl_ref[...] = _dot(dzb, w_ref[...], 1, 1).astype(BF16)
        pw = _dot(glb, dzb, 0, 0)
        pb = jnp.sum(dz, axis=0, keepdims=True)

        @pl.when(i == 0)
        def _():
            dw_ref[...] = pw
            db_ref[...] = pb

        @pl.when(i > 0)
        def _():
            dw_ref[...] += pw
            db_ref[...] += pb

    return pl.pallas_call(
        body, name=name, grid=(s // tr,),
        in_specs=[pl.BlockSpec((tr, qk), lambda i: (i, 0)), pl.BlockSpec((tr, LANES), lambda i: (i, glc)),
                  pl.BlockSpec((LANES, qk), lambda i: (0, 0)), pl.BlockSpec((1, qk), lambda i: (0, 0))],
        out_specs=[pl.BlockSpec((tr, LANES), lambda i: (i, 0)), pl.BlockSpec((LANES, qk), lambda i: (0, 0)),
                   pl.BlockSpec((1, qk), lambda i: (0, 0))],
        out_shape=[SDS((s, LANES), BF16), SDS((LANES, qk), F32), SDS((1, qk), F32)],
        compiler_params=_cp("arbitrary"))(dg, proj, wg, bg)


def _gla_chunk_terms(q_ref, k_ref, g_ref, c, scale):
    q = q_ref[...] * scale
    k = k_ref[...]
    gg = g_ref[...]
    b = _split_dot(_tri(c, "le"), gg)
    row = lax.broadcasted_iota(jnp.int32, gg.shape, 0)
    bm = jnp.sum(jnp.where(row < c // 2, gg, 0.0), axis=0, keepdims=True)
    bl = jnp.sum(gg, axis=0, keepdims=True)
    eb, em, emi, el = jnp.exp(b), jnp.exp(b - bm), jnp.exp(bm - b), jnp.exp(bl - b)
    return q, k, bl, eb, em, emi, el


def _causal(a):
    r = lax.broadcasted_iota(jnp.int32, a.shape, 0)
    c = lax.broadcasted_iota(jnp.int32, a.shape, 1)
    return jnp.where(r >= c, a, 0.0)


def gla_fwd(proj, g, *, name, c=GLA_CHUNK):
    s = proj.shape[0]
    qk = g.shape[1]
    dk, dv = qk // GLA_HEADS, 2 * qk // GLA_HEADS
    nc = s // c
    scale = dk ** -0.5
    kq = qk // dk

    def body(q_ref, k_ref, v_ref, g_ref, o_ref, st_ref, state):
        @pl.when(pl.program_id(1) == 0)
        def _():
            state[...] = jnp.zeros_like(state)

        q, k, bl, eb, em, emi, el = _gla_chunk_terms(q_ref, k_ref, g_ref, c, scale)
        v2 = _hilo(v_ref[...])
        st = state[...]
        st_ref[...] = st
        a = _causal(_dot3(_hilo(q * em), _hilo(k * emi), 1, 1))
        o_ref[...] = _dot3(_hilo(q * eb), _hilo(st), 1, 1) + _dot3(_hilo(a), v2)
        state[...] = st * jnp.exp(bl) + _dot3(v2, _hilo(k * el), 0, 0)

    return pl.pallas_call(
        body, name=name, grid=(GLA_HEADS, nc),
        in_specs=[pl.BlockSpec((c, dk), lambda h, i: (i, h)), pl.BlockSpec((c, dk), lambda h, i: (i, kq + h)),
                  pl.BlockSpec((c, dv), lambda h, i: (i, kq + h)), pl.BlockSpec((c, dk), lambda h, i: (i, h))],
        out_specs=[pl.BlockSpec((c, dv), lambda h, i: (i, h)),
                   pl.BlockSpec((None, None, dv, dk), lambda h, i: (h, i, 0, 0))],
        out_shape=[SDS((s, 2 * qk), F32), SDS((GLA_HEADS, nc, dv, dk), F32)],
        scratch_shapes=[pltpu.VMEM((dv, dk), F32)],
        compiler_params=_cp("parallel", "arbitrary"))(proj, proj, proj, g)


def gla_bwd(proj, g, states, do, *, name, c=GLA_CHUNK):
    s = proj.shape[0]
    qk = g.shape[1]
    dk, dv = qk // GLA_HEADS, 2 * qk // GLA_HEADS
    nc = s // c
    scale = dk ** -0.5
    kq = qk // dk

    def body(q_ref, k_ref, v_ref, g_ref, do_ref, st_ref, dq_ref, dk_ref, dv_ref, dg_ref, dstate, dgc):
        @pl.when(pl.program_id(1) == 0)
        def _():
            dstate[...] = jnp.zeros_like(dstate)
            dgc[...] = jnp.zeros_like(dgc)

        q, k, bl, eb, em, emi, el = _gla_chunk_terms(q_ref, k_ref, g_ref, c, scale)
        v2, do2 = _hilo(v_ref[...]), _hilo(do_ref[...])
        qe, qm, km, kd = _hilo(q * eb), _hilo(q * em), _hilo(k * emi), _hilo(k * el)
        ds = dstate[...]
        ds2 = _hilo(ds)
        a = _hilo(_causal(_dot3(qm, km, 1, 1)))
        dv_ref[...] = (_dot3(a, do2, 0, 0) + _dot3(kd, ds2, 1, 1)).astype(BF16)
        da = _hilo(_causal(_dot3(do2, v2, 1, 1)))
        dq = _dot3(da, km) * em + _dot3(do2, _hilo(st_ref[...])) * eb
        dkk = _dot3(da, qm, 0, 0) * emi + _dot3(v2, ds2) * el
        dstate[...] = ds * jnp.exp(bl) + _dot3(do2, qe, 0, 0)
        db = q * dq - k * dkk
        dg_ref[...] = _split_dot(_tri(c, "ge"), db) + dgc[...]
        dgc[...] += jnp.sum(db, axis=0, keepdims=True)
        dq_ref[...] = (dq * scale).astype(BF16)
        dk_ref[...] = dkk.astype(BF16)

    rev = lambda i: nc - 1 - i
    qspec = pl.BlockSpec((c, dk), lambda h, i: (rev(i), h))
    vspec = pl.BlockSpec((c, dv), lambda h, i: (rev(i), h))
    return pl.pallas_call(
        body, name=name, grid=(GLA_HEADS, nc),
        in_specs=[qspec, pl.BlockSpec((c, dk), lambda h, i: (rev(i), kq + h)),
                  pl.BlockSpec((c, dv), lambda h, i: (rev(i), kq + h)), qspec, vspec,
                  pl.BlockSpec((None, None, dv, dk), lambda h, i: (h, rev(i), 0, 0))],
        out_specs=[qspec, qspec, vspec, qspec],
        out_shape=[SDS((s, qk), BF16), SDS((s, qk), BF16), SDS((s, 2 * qk), BF16), SDS((s, qk), F32)],
        scratch_shapes=[pltpu.VMEM((dv, dk), F32), pltpu.VMEM((1, dk), F32)],
        compiler_params=_cp("parallel", "arbitrary"))(proj, proj, proj, g, do, states)


def gnorm_fwd(o, proj, gw, *, name, tr=512):
    s, v = o.shape
    dv = v // GLA_HEADS
    roff = 2 * GLA_HEADS

    def body(o_ref, r_ref, w_ref, y_ref):
        of = o_ref[...]
        rs = lax.rsqrt(jnp.mean(of * of, axis=-1, keepdims=True) + EPS)
        r = r_ref[...]
        y_ref[...] = (of * rs * w_ref[...] * (r * _sigmoid(r))).astype(BF16)

    blk = pl.BlockSpec((tr, dv), lambda i, h: (i, h))
    return pl.pallas_call(
        body, name=name, grid=(s // tr, GLA_HEADS),
        in_specs=[blk, pl.BlockSpec((tr, dv), lambda i, h: (i, roff + h)), pl.BlockSpec((1, dv), lambda i, h: (0, 0))],
        out_specs=blk, out_shape=SDS((s, v), BF16), compiler_params=_cp("parallel", "parallel"))(o, proj, gw)


def gnorm_bwd(dy, o, proj, gw, *, name, tr=512):
    s, v = o.shape
    dv = v // GLA_HEADS
    roff = 2 * GLA_HEADS

    def body(dy_ref, o_ref, r_ref, w_ref, do_ref, dr_ref, dw_ref):
        first = jnp.logical_and(pl.program_id(0) == 0, pl.program_id(1) == 0)
        of = o_ref[...]
        rs = lax.rsqrt(jnp.mean(of * of, axis=-1, keepdims=True) + EPS)
        n = of * rs
        r = r_ref[...]
        sg = _sigmoid(r)
        dyf = dy_ref[...].astype(F32)
        dn_w = dyf * (r * sg)
        dr_ref[...] = (dyf * n * w_ref[...] * sg * (1.0 + r * (1.0 - sg))).astype(BF16)
        dn = dn_w * w_ref[...]
        do_ref[...] = rs * (dn - n * jnp.mean(dn * n, axis=-1, keepdims=True))
        part = jnp.sum(dn_w * n, axis=0, keepdims=True)

        @pl.when(first)
        def _():
            dw_ref[...] = part

        @pl.when(jnp.logical_not(first))
        def _():
            dw_ref[...] += part

    blk = pl.BlockSpec((tr, dv), lambda i, h: (i, h))
    vec = pl.BlockSpec((1, dv), lambda i, h: (0, 0))
    return pl.pallas_call(
        body, name=name, grid=(s // tr, GLA_HEADS),
        in_specs=[blk, blk, pl.BlockSpec((tr, dv), lambda i, h: (i, roff + h)), vec],
        out_specs=[blk, blk, vec], out_shape=[SDS((s, v), F32), SDS((s, v), BF16), SDS((1, dv), F32)],
        compiler_params=_cp("arbitrary", "arbitrary"))(dy, o, proj, gw)


def _sb_block(kblk, q, ks, q0, scale, carry):
    tk, tq = kblk.shape[0], q.shape[0]
    z = _dot(kblk, q, 1, 1) * scale
    kpos = ks + lax.broadcasted_iota(jnp.int32, (tk, tq), 0)
    qpos = q0 + lax.broadcasted_iota(jnp.int32, (tk, tq), 1)
    mask = kpos < qpos
    sp = jnp.maximum(z, 0.0) + jnp.log(1.0 + jnp.exp(-jnp.abs(z)))
    lf = jnp.where(mask, -sp, 0.0)
    later = _split_dot(_tri(tk, "gt"), lf)
    a = jnp.where(mask, jnp.exp(z - sp + later + carry), 0.0)
    return z, sp, mask, lf, a


def sb_fwd(q, kv, *, name, tq=SB_TQ, tk=SB_TK):
    s, w = q.shape
    hd = w // SB_HEADS
    nq, nkb = s // tq, s // tk
    scale = hd ** -0.5
    per = tq // tk
    assert per % 2 == 0

    def body(q_ref, k_ref, v_ref, o_ref, car_ref, o_acc):
        qi = pl.program_id(1)
        qb = q_ref[...]
        q0 = qi * tq
        pairs = (qi + 1) * (per // 2)
        car_ref[...] = jnp.zeros_like(car_ref)
        o_acc[...] = jnp.zeros_like(o_acc)

        def step(i, carry):
            for t in range(2):
                kj = 2 * (pairs - 1 - i) + 1 - t
                ks = pl.multiple_of(kj * tk, tk)
                car_ref[pl.ds(kj, 1), :] = carry
                _, _, _, lf, a = _sb_block(k_ref[pl.ds(ks, tk), :], qb, ks, q0, scale, carry)
                o_acc[...] += _dot(a.astype(BF16), v_ref[pl.ds(ks, tk), :], 0, 0)
                carry = carry + jnp.sum(lf, axis=0, keepdims=True)
            return carry

        lax.fori_loop(0, pairs, step, jnp.zeros((1, tq), F32))
        o_ref[...] = o_acc[...].astype(BF16)

    qspec = pl.BlockSpec((tq, hd), lambda h, i: (i, h))
    return pl.pallas_call(
        body, name=name, grid=(SB_HEADS, nq),
        in_specs=[qspec, pl.BlockSpec((None, s, hd), lambda h, i: (0, 0, h)), pl.BlockSpec((None, s, hd), lambda h, i: (1, 0, h))],
        out_specs=[qspec, pl.BlockSpec((None, None, nkb, tq), lambda h, i: (h, i, 0, 0))],
        out_shape=[SDS((s, w), BF16), SDS((SB_HEADS, nq, nkb, tq), F32)],
        scratch_shapes=[pltpu.VMEM((tq, hd), F32)],
        compiler_params=_cp("parallel", "parallel"))(q, kv, kv)


def sb_bwd(q, kv, do, car, *, name, tq=SB_TQ, tk=SB_TK):
    s, w = q.shape
    hd = w // SB_HEADS
    nq, nkb = s // tq, s // tk
    scale = hd ** -0.5
    per = tq // tk
    assert per % 2 == 0

    def body(q_ref, k_ref, v_ref, do_ref, car_ref, dq_ref, dkv_ref, dq_acc, dk_acc, dv_acc):
        qi = pl.program_id(1)
        qb = q_ref[...]
        dob = do_ref[...]
        q0 = qi * tq
        pairs = (qi + 1) * (per // 2)
        dq_acc[...] = jnp.zeros_like(dq_acc)

        @pl.when(qi == 0)
        def _():
            dk_acc[...] = jnp.zeros_like(dk_acc)
            dv_acc[...] = jnp.zeros_like(dv_acc)

        def step(i, pcar):
            for t in range(2):
                kj = 2 * i + t
                ks = pl.multiple_of(kj * tk, tk)
                kblk = k_ref[pl.ds(ks, tk), :]
                z, sp, mask, _, a = _sb_block(kblk, qb, ks, q0, scale, car_ref[pl.ds(kj, 1), :])
                p = a * _dot(v_ref[pl.ds(ks, tk), :], dob, 1, 1)
                before = _split_dot(_tri(tk, "lt"), p)
                sg = jnp.exp(z - sp)
                dz = (jnp.where(mask, p * (1.0 - sg) - (pcar + before) * sg, 0.0) * scale).astype(BF16)
                dk_acc[pl.ds(ks, tk), :] += _dot(dz, qb)
                dv_acc[pl.ds(ks, tk), :] += _dot(a.astype(BF16), dob)
                dq_acc[...] += _dot(dz, kblk, 0, 0)
                pcar = pcar + jnp.sum(p, axis=0, keepdims=True)
            return pcar

        lax.fori_loop(0, pairs, step, jnp.zeros((1, tq), F32))
        dq_ref[...] = dq_acc[...].astype(BF16)

        @pl.when(qi == nq - 1)
        def _():
            dkv_ref[0] = dk_acc[...].astype(BF16)
            dkv_ref[1] = dv_acc[...].astype(BF16)

    qspec = pl.BlockSpec((tq, hd), lambda h, i: (i, h))
    return pl.pallas_call(
        body, name=name, grid=(SB_HEADS, nq),
        in_specs=[qspec, pl.BlockSpec((None, s, hd), lambda h, i: (0, 0, h)), pl.BlockSpec((None, s, hd), lambda h, i: (1, 0, h)),
                  qspec, pl.BlockSpec((None, None, nkb, tq), lambda h, i: (h, i, 0, 0))],
        out_specs=[qspec, pl.BlockSpec((2, s, hd), lambda h, i: (0, 0, h))],
        out_shape=[SDS((s, w), BF16), SDS((2, s, w), BF16)],
        scratch_shapes=[pltpu.VMEM((tq, hd), F32), pltpu.VMEM((s, hd), F32), pltpu.VMEM((s, hd), F32)],
        compiler_params=_cp("parallel", "arbitrary"))(q, kv, kv, do, car)


def adamw(w, g, m, v, *, name):
    shape = w.shape
    c = shape[-1]
    r = w.size // c
    tr = _tile(r, max(8, (3 * LANES * 1024) // c), unit=8) if r >= 8 else r

    def body(w_ref, g_ref, m_ref, v_ref, d_ref, nm_ref, nv_ref):
        gf = g_ref[...]
        mn = ADAM_B1 * m_ref[...] + (1.0 - ADAM_B1) * gf
        vn = ADAM_B2 * v_ref[...] + (1.0 - ADAM_B2) * (gf * gf)
        m_hat = mn / (1.0 - ADAM_B1 ** ADAM_STEP)
        v_hat = vn / (1.0 - ADAM_B2 ** ADAM_STEP)
        d_ref[...] = -ADAM_LR * (m_hat / (jnp.sqrt(v_hat) + ADAM_EPS) + ADAM_WD * w_ref[...])
        nm_ref[...] = mn
        nv_ref[...] = vn

    blk = pl.BlockSpec((tr, c), lambda i: (i, 0))
    outs = pl.pallas_call(
        body, name=name, grid=(r // tr,), in_specs=[blk] * 4, out_specs=[blk] * 3,
        out_shape=[SDS((r, c), F32)] * 3, compiler_params=_cp("parallel"))(
            *(t.reshape(r, c) for t in (w, g, m, v)))
    return tuple(o.reshape(shape) for o in outs)


def _ffn_fwd(h, nw, w_gu, w_dn, tag):
    f = rmsnorm_fwd(h, nw, name=f"{tag}_norm")
    gu = matmul(f, w_gu, name=f"{tag}_gate_up", out_dtype=BF16)
    act = swiglu_fwd(gu, name=f"{tag}_act")
    return matmul(act, w_dn, name=f"{tag}_down", residual=h), (f, gu, act)


def _ffn_bwd(dh, h, nw, w_gu, w_dn, saved, tag):
    f, gu, act = saved
    dact = matmul(dh, w_dn, name=f"{tag}_dact", tb=True, out_dtype=BF16)
    dw_dn = matmul(act, dh, name=f"{tag}_dw_down", ta=True, out_dtype=BF16)
    dgu = swiglu_bwd(gu, dact, name=f"{tag}_dgu")
    dw_gu = matmul(f, dgu, name=f"{tag}_dw_gate_up", ta=True, out_dtype=BF16, out_chips=N_CHIPS)
    df = matmul(dgu, w_gu, name=f"{tag}_df", tb=True)
    dh_in, dnw = rmsnorm_bwd(df, h, nw, dh, name=f"{tag}_dnorm")
    return dh_in, dnw, dw_gu, dw_dn


def local_step(x, target, p):
    gu = [View(p["w_gu"], "cols", lead=(l,)) for l in range(2)]
    dn = [View(p["w_dn"], "rows", lead=(l,)) for l in range(2)]
    w_kv = View(p["w_kv"], "cols")
    a0 = rmsnorm_fwd(x, p["an0"], name="l0_attn_norm")
    proj = matmul(a0, p["w_in"], name="gla_in")
    g = gate_fwd(proj, p["wg"], p["bg"], name="gla_gate")
    o, states = gla_fwd(proj, g, name="gla_scan")
    og = gnorm_fwd(o, proj, p["gw"], name="gla_outnorm")
    h1 = matmul(og, p["w_out"], name="gla_out", residual=x)
    h2, ffn0 = _ffn_fwd(h1, p["fn0"], gu[0], dn[0], "ffn0")
    kvn = rmsnorm_fwd(h2, p["kvn"], name="kv_norm")
    kv = matmul(kvn, w_kv, name="sb_kv", out_dtype=BF16, out_chips=2)
    a1 = rmsnorm_fwd(h2, p["an1"], name="l1_attn_norm")
    q2 = matmul(a1, p["w_q"], name="sb_q", out_dtype=BF16)
    o2, car = sb_fwd(q2, kv, name="sb_attn")
    h3 = matmul(o2, p["w_so"], name="sb_out", residual=h2)
    h4, ffn1 = _ffn_fwd(h3, p["fn1"], gu[1], dn[1], "ffn1")
    dh4, d_fin, loss_row = final_loss(h4, p["finn"], target, name="final_loss")

    dh3, d_fn1, dw_gu1, dw_dn1 = _ffn_bwd(dh4, h3, p["fn1"], gu[1], dn[1], ffn1, "ffn1")
    do2 = matmul(dh3, p["w_so"], name="sb_do", tb=True, out_dtype=BF16)
    dw_so = matmul(o2, dh3, name="sb_dw_out", ta=True, out_dtype=BF16)
    dq2, dkv = sb_bwd(q2, kv, do2, car, name="sb_attn_bwd")
    dkv = View(dkv, "cols")
    dw_q = matmul(a1, dq2, name="sb_dw_q", ta=True, out_dtype=BF16)
    da1 = matmul(dq2, p["w_q"], name="sb_da", tb=True)
    dh2, d_an1 = rmsnorm_bwd(da1, h2, p["an1"], dh3, name="l1_attn_dnorm")
    dw_kv = matmul(kvn, dkv, name="sb_dw_kv", ta=True, out_dtype=BF16, out_chips=N_CHIPS)
    dkvn = matmul(dkv, w_kv, name="sb_dkvn", tb=True)
    dh2, d_kvn = rmsnorm_bwd(dkvn, h2, p["kvn"], dh2, name="kv_dnorm")
    dh1, d_fn0, dw_gu0, dw_dn0 = _ffn_bwd(dh2, h1, p["fn0"], gu[0], dn[0], ffn0, "ffn0")
    dog = matmul(dh1, p["w_out"], name="gla_dog", tb=True, out_dtype=BF16)
    dw_out = matmul(og, dh1, name="gla_dw_out", ta=True, out_dtype=BF16)
    do, dr, d_gw = gnorm_bwd(dog, o, proj, p["gw"], name="gla_outnorm_bwd")
    dq, dk, dv, dg = gla_bwd(proj, g, states, do, name="gla_scan_bwd")
    dgl, d_wg, d_bg = gate_bwd(dg, proj, p["wg"], p["bg"], name="gla_gate_bwd")
    dproj = jnp.concatenate([dq, dk, dv, dr, dgl], axis=1)
    dw_in = matmul(a0, dproj, name="gla_dw_in", ta=True, out_dtype=BF16)
    da0 = matmul(dproj, p["w_in"], name="gla_da", tb=True)
    dx, d_an0 = rmsnorm_bwd(da0, x, p["an0"], dh1, name="l0_attn_dnorm")

    grads = dict(an0=d_an0, an1=d_an1, fn0=d_fn0, fn1=d_fn1, kvn=d_kvn, finn=d_fin, wg=d_wg, bg=d_bg, gw=d_gw,
                 w_in=dw_in, w_out=dw_out, w_kv=dw_kv, w_q=dw_q, w_so=dw_so,
                 w_gu0=dw_gu0, w_gu1=dw_gu1, w_dn0=dw_dn0, w_dn1=dw_dn1)
    return loss_row, dx, grads


ANY = pl.BlockSpec(memory_space=pl.ANY)


def _coords():
    return lax.axis_index("x"), lax.axis_index("y"), lax.axis_index("c")


def _other_chips(x, y):
    return [(1 - x, y), (x, 1 - y), (1 - x, 1 - y)]


def _remote(src, dst, send_sem, recv_sem, dev):
    return pltpu.make_async_remote_copy(src_ref=src, dst_ref=dst, send_sem=send_sem, recv_sem=recv_sem,
                                        device_id=dev, device_id_type=MESH)


def _dma_sems(n):
    return pltpu.SemaphoreType.DMA((n,))


def allgather_chips(ws, *, name):
    n = len(ws)

    def body(*refs):
        w_refs, out_refs = refs[:n], refs[n:2 * n]
        send_sems, recv_sems, local_sems = refs[2 * n:]
        x, y, c = _coords()
        me = 2 * x + y
        chips = _other_chips(x, y)

        def copy(i, k, chip, half, dev, src=None):
            dst = out_refs[i].at[chip, half]
            return _remote(dst if src is None else src, dst, send_sems.at[6 * i + k], recv_sems.at[6 * i + k], dev)

        mine = [pltpu.make_async_copy(w_refs[i], out_refs[i].at[me], local_sems.at[i]) for i in range(n)]
        first = [copy(i, k, me, c, (cx, cy, c), src=w_refs[i].at[c]) for i in range(n) for k, (cx, cy) in enumerate(chips)]
        for cp in mine + first:
            cp.start()
        passed = []
        for i in range(n):
            for k, (cx, cy) in enumerate(chips):
                copy(i, k, 2 * cx + cy, c, (x, y, c)).wait_recv()
                passed.append(copy(i, 3 + k, 2 * cx + cy, c, (x, y, 1 - c)))
                passed[-1].start()
        for i in range(n):
            for k, (cx, cy) in enumerate(chips):
                copy(i, 3 + k, 2 * cx + cy, 1 - c, (x, y, c)).wait_recv()
        for cp in first + passed:
            cp.wait_send()
        for cp in mine:
            cp.wait()

    return pl.pallas_call(
        body, name=name, in_specs=[ANY] * n, out_specs=[ANY] * n,
        out_shape=[SDS((N_CHIPS,) + w.shape, w.dtype) for w in ws],
        scratch_shapes=[_dma_sems(6 * n), _dma_sems(6 * n), _dma_sems(n)])(*ws)


def sibling_exchange(gs, *, name):
    n = len(gs)

    def body(*refs):
        g_refs, a_refs = refs[:n], refs[n:2 * n]
        send_sems, recv_sems = refs[2 * n:]
        x, y, c = _coords()
        cps = [_remote(g_refs[i].at[:, 1 - c], a_refs[i], send_sems.at[i], recv_sems.at[i], (x, y, 1 - c)) for i in range(n)]
        for cp in cps:
            cp.start()
        for cp in cps:
            cp.wait()

    return pl.pallas_call(
        body, name=name, in_specs=[ANY] * n, out_specs=[ANY] * n,
        out_shape=[SDS(g.shape[:1] + g.shape[2:], g.dtype) for g in gs],
        scratch_shapes=[_dma_sems(n), _dma_sems(n)])(*gs)


def _row_tile(r, c):
    return _tile(r, max(16, (4 * LANES * 1024) // c), unit=16)


def half_add(g, a, core, *, name):
    n, _, r, c = g.shape
    tr = _row_tile(r, c)

    def body(c_ref, g_ref, a_ref, o_ref):
        o_ref[...] = (g_ref[...].astype(F32) + a_ref[...].astype(F32)).astype(o_ref.dtype)

    blk = pl.BlockSpec((None, tr, c), lambda s, i, c_ref: (s, i, 0))
    return pl.pallas_call(
        body, name=name, out_shape=SDS((n, r, c), g.dtype),
        grid_spec=pltpu.PrefetchScalarGridSpec(
            num_scalar_prefetch=1, grid=(n, r // tr),
            in_specs=[pl.BlockSpec((None, None, tr, c), lambda s, i, c_ref: (s, c_ref[0], i, 0)), blk], out_specs=blk),
        compiler_params=_cp("parallel", "parallel"))(core, g, a)


def chip_scatter(ps, *, name):
    n = len(ps)

    def body(*refs):
        p_refs, b_refs = refs[:n], refs[n:2 * n]
        send_sems, recv_sems = refs[2 * n:]
        x, y, c = _coords()
        cps = [_remote(p_refs[i].at[2 * cx + cy], b_refs[i].at[k], send_sems.at[3 * i + k], recv_sems.at[3 * i + k], (cx, cy, c))
               for i in range(n) for k, (cx, cy) in enumerate(_other_chips(x, y))]
        for cp in cps:
            cp.start()
        for cp in cps:
            cp.wait()

    return pl.pallas_call(
        body, name=name, in_specs=[ANY] * n, out_specs=[ANY] * n,
        out_shape=[SDS((N_CHIPS - 1,) + p.shape[1:], p.dtype) for p in ps],
        scratch_shapes=[_dma_sems(3 * n), _dma_sems(3 * n)])(*ps)


def chip_sum(p, b, chip, *, name):
    _, r, c = p.shape
    tr = _row_tile(r, c)

    def body(c_ref, p_ref, b_ref, o_ref):
        t = p_ref[...].astype(F32)
        for k in range(N_CHIPS - 1):
            t = t + b_ref[k].astype(F32)
        o_ref[...] = t

    return pl.pallas_call(
        body, name=name, out_shape=SDS((r, c), F32),
        grid_spec=pltpu.PrefetchScalarGridSpec(
            num_scalar_prefetch=1, grid=(r // tr,),
            in_specs=[pl.BlockSpec((None, tr, c), lambda i, c_ref: (c_ref[0], i, 0)),
                      pl.BlockSpec((N_CHIPS - 1, tr, c), lambda i, c_ref: (0, i, 0))],
            out_specs=pl.BlockSpec((tr, c), lambda i, c_ref: (i, 0))),
        compiler_params=_cp("parallel"))(chip, p, b)


def sibling_gather(ts, *, name):
    n = len(ts)

    def body(*refs):
        t_refs, out_refs = refs[:n], refs[n:2 * n]
        send_sems, recv_sems, local_sems = refs[2 * n:]
        x, y, c = _coords()
        loc = [pltpu.make_async_copy(t_refs[i], out_refs[i].at[c], local_sems.at[i]) for i in range(n)]
        cps = [_remote(t_refs[i], out_refs[i].at[c], send_sems.at[i], recv_sems.at[i], (x, y, 1 - c)) for i in range(n)]
        for cp in loc + cps:
            cp.start()
        for i in range(n):
            _remote(t_refs[i], out_refs[i].at[1 - c], send_sems.at[i], recv_sems.at[i], (x, y, 1 - c)).wait_recv()
        for cp in cps:
            cp.wait_send()
        for cp in loc:
            cp.wait()

    return pl.pallas_call(
        body, name=name, in_specs=[ANY] * n, out_specs=[ANY] * n,
        out_shape=[SDS((2,) + t.shape, t.dtype) for t in ts],
        scratch_shapes=[_dma_sems(n), _dma_sems(n), _dma_sems(n)])(*ts)


def allgather_all(sm, *, name):
    r, w = sm.shape

    def body(s_ref, out_ref, send_sems, recv_sems, local_sem):
        x, y, c = _coords()
        me = 4 * x + 2 * y + c

        def peer(rel):
            flip = lambda v, bit: 1 - v if bit else v
            return flip(x, rel & 4), flip(y, rel & 2), flip(c, rel & 1)

        loc = pltpu.make_async_copy(s_ref, out_ref.at[me], local_sem.at[0])
        loc.start()
        cps = [_remote(s_ref, out_ref.at[me], send_sems.at[rel - 1], recv_sems.at[rel - 1], peer(rel)) for rel in range(1, N_DEV)]
        for cp in cps:
            cp.start()
        for rel in range(1, N_DEV):
            px, py, pc = peer(rel)
            _remote(s_ref, out_ref.at[4 * px + 2 * py + pc], send_sems.at[rel - 1], recv_sems.at[rel - 1], (px, py, pc)).wait_recv()
        for cp in cps:
            cp.wait_send()
        loc.wait()

    return pl.pallas_call(
        body, name=name, in_specs=[ANY], out_specs=ANY, out_shape=SDS((N_DEV, r, w), sm.dtype),
        scratch_shapes=[_dma_sems(N_DEV - 1), _dma_sems(N_DEV - 1), _dma_sems(1)])(sm)


def sum_blocks(a, *, name):
    n, r, w = a.shape

    def body(a_ref, o_ref):
        t = a_ref[0]
        for k in range(1, n):
            t = t + a_ref[k]
        o_ref[...] = t

    return pl.pallas_call(body, name=name, out_shape=SDS((r, w), F32))(a)


def _pack_rows(parts):
    cat = jnp.concatenate([t.reshape(-1) for t in parts])
    rows = -(-cat.size // (LANES * SUBLANES)) * SUBLANES
    return jnp.pad(cat, (0, rows * LANES - cat.size)).reshape(rows, LANES)


def _segments(flat, sizes):
    out, off = [], 0
    for n in sizes:
        out.append(flat[..., off:off + n])
        off += n
    return out


def _halves(a):
    return a.reshape(a.shape[:-2] + (2, a.shape[-2] // 2, a.shape[-1]))


def kernel(x, attn_norm_w, ffn_norm_w, gla_w_in, gla_w_gate_up, gla_b_gate, gla_gnorm_w, gla_w_out, kv_norm_w, sb_w_kv, sb_w_q, sb_w_out, ffn_w_gate_up, ffn_w_down, final_norm_w, loss_target, m_attn_norm_w, m_ffn_norm_w, m_gla_w_in, m_gla_w_gate_up, m_gla_b_gate, m_gla_gnorm_w, m_gla_w_out, m_kv_norm_w, m_sb_w_kv, m_sb_w_q, m_sb_w_out, m_ffn_w_gate_up, m_ffn_w_down, m_final_norm_w, v_attn_norm_w, v_ffn_norm_w, v_gla_w_in, v_gla_w_gate_up, v_gla_b_gate, v_gla_gnorm_w, v_gla_w_out, v_kv_norm_w, v_sb_w_kv, v_sb_w_q, v_sb_w_out, v_ffn_w_gate_up, v_ffn_w_down, v_final_norm_w):
    xi, yi, ci = _coords()
    core = ci.astype(jnp.int32).reshape(1)
    chip = (2 * xi + yi).astype(jnp.int32)
    _, s, d = x.shape
    rank = gla_w_gate_up.shape[1]

    local = [_halves(gla_w_in[0]), _halves(gla_w_out[0]), _halves(sb_w_kv), _halves(sb_w_q[0]), _halves(sb_w_out[0]),
             ffn_w_gate_up, ffn_w_down]
    w_in, w_out, w_kv, w_q, w_so, w_gu, w_dn = allgather_chips([a.astype(BF16) for a in local], name="gather_weights")
    in_w = N_CHIPS * w_in.shape[-1]
    w_in = w_in.reshape(N_CHIPS, d, -1).transpose(1, 0, 2).reshape(d, in_w)
    small = [gla_w_gate_up, gla_b_gate, gla_gnorm_w]
    small_all = allgather_all(_pack_rows(small), name="gather_gate_weights")
    wg, bg, gw = _segments(small_all[::2].reshape(N_CHIPS, -1), [a.size for a in small])
    wg = wg.reshape(N_CHIPS, rank, -1).transpose(1, 0, 2).reshape(rank, -1)
    p = dict(
        an0=attn_norm_w[0], an1=attn_norm_w[1], fn0=ffn_norm_w[0], fn1=ffn_norm_w[1], kvn=kv_norm_w, finn=final_norm_w,
        w_in=jnp.pad(w_in, ((0, 0), (0, -in_w % LANES))), wg=jnp.pad(wg, ((0, LANES - rank), (0, 0))).astype(BF16),
        bg=bg.reshape(1, -1), gw=gw.reshape(1, -1), w_out=w_out.reshape(-1, d), w_kv=w_kv.reshape(N_CHIPS, d, -1),
        w_q=w_q.reshape(d, -1), w_so=w_so.reshape(-1, d), w_gu=w_gu, w_dn=w_dn)

    loss_row, dx, g = local_step(x[0], loss_target[0], p)
    loss = lax.psum(loss_row[0, 0], ("x", "y", "c"))

    by_rows = lambda t: _halves(t.reshape(N_CHIPS, -1, t.shape[-1]))
    gs = [_halves(g["w_in"][:, :in_w].reshape(d, N_CHIPS, -1).transpose(1, 0, 2)), by_rows(g["w_out"]), _halves(g["w_kv"]),
          by_rows(g["w_q"]), by_rows(g["w_so"]), _halves(g["w_gu0"]), _halves(g["w_gu1"]), by_rows(g["w_dn0"]), by_rows(g["w_dn1"])]
    tags = ["w_in", "w_out", "w_kv", "w_q", "w_so", "w_gu0", "w_gu1", "w_dn0", "w_dn1"]
    from_sibling = sibling_exchange(gs, name="grads_to_sibling")
    pairs = [half_add(t, a, core, name=f"pair_sum_{n}") for t, a, n in zip(gs, from_sibling, tags)]
    from_chips = chip_scatter(pairs, name="grads_to_chips")
    mine = [chip_sum(t, b, chip.reshape(1), name=f"chip_sum_{n}") for t, b, n in zip(pairs, from_chips, tags)]
    tot = dict(zip(tags, sibling_gather(mine, name="grads_from_sibling")))

    vecs = [jnp.concatenate([g["an0"], g["an1"]]), jnp.concatenate([g["fn0"], g["fn1"]]), g["kvn"], g["finn"],
            g["wg"][:rank], g["bg"], g["gw"]]
    gathered_vecs = allgather_all(_pack_rows(vecs), name="gather_small_grads")
    d_an, d_fn, d_kvn, d_fin, d_wg, d_bg, d_gw = _segments(
        sum_blocks(gathered_vecs, name="sum_small_grads").reshape(-1), [t.size for t in vecs])

    def shard(t, like):
        return lax.dynamic_index_in_dim(t.reshape(-1, N_CHIPS, like.shape[-1]), chip, axis=1, keepdims=False).reshape(like.shape)

    grads = dict(
        attn_norm_w=d_an.reshape(attn_norm_w.shape), ffn_norm_w=d_fn.reshape(ffn_norm_w.shape),
        gla_w_in=tot["w_in"].reshape(gla_w_in.shape), gla_w_gate_up=shard(d_wg, gla_w_gate_up),
        gla_b_gate=shard(d_bg, gla_b_gate), gla_gnorm_w=shard(d_gw, gla_gnorm_w),
        gla_w_out=tot["w_out"].reshape(gla_w_out.shape), kv_norm_w=d_kvn.reshape(kv_norm_w.shape),
        sb_w_kv=tot["w_kv"].reshape(sb_w_kv.shape), sb_w_q=tot["w_q"].reshape(sb_w_q.shape),
        sb_w_out=tot["w_so"].reshape(sb_w_out.shape),
        ffn_w_gate_up=jnp.stack([tot["w_gu0"], tot["w_gu1"]]).reshape(ffn_w_gate_up.shape),
        ffn_w_down=jnp.stack([tot["w_dn0"], tot["w_dn1"]]).reshape(ffn_w_down.shape),
        final_norm_w=d_fin.reshape(final_norm_w.shape))
    weights = dict(
        attn_norm_w=(attn_norm_w, m_attn_norm_w, v_attn_norm_w), ffn_norm_w=(ffn_norm_w, m_ffn_norm_w, v_ffn_norm_w),
        gla_w_in=(gla_w_in, m_gla_w_in, v_gla_w_in), gla_w_gate_up=(gla_w_gate_up, m_gla_w_gate_up, v_gla_w_gate_up),
        gla_b_gate=(gla_b_gate, m_gla_b_gate, v_gla_b_gate), gla_gnorm_w=(gla_gnorm_w, m_gla_gnorm_w, v_gla_gnorm_w),
        gla_w_out=(gla_w_out, m_gla_w_out, v_gla_w_out), kv_norm_w=(kv_norm_w, m_kv_norm_w, v_kv_norm_w),
        sb_w_kv=(sb_w_kv, m_sb_w_kv, v_sb_w_kv), sb_w_q=(sb_w_q, m_sb_w_q, v_sb_w_q), sb_w_out=(sb_w_out, m_sb_w_out, v_sb_w_out),
        ffn_w_gate_up=(ffn_w_gate_up, m_ffn_w_gate_up, v_ffn_w_gate_up), ffn_w_down=(ffn_w_down, m_ffn_w_down, v_ffn_w_down),
        final_norm_w=(final_norm_w, m_final_norm_w, v_final_norm_w))
    names = list(weights)
    stepped = [adamw(weights[n][0], grads[n], weights[n][1], weights[n][2], name=f"adamw_{n}") for n in names]
    return (loss, dx.reshape(x.shape), *[grads[n] for n in names], *[t[0] for t in stepped], *[t[1] for t in stepped],
            *[t[2] for t in stepped])
```

```python
import functools

import jax
import jax.numpy as jnp
from jax import lax
from jax.experimental import pallas as pl
from jax.experimental.pallas import tpu as pltpu

F32 = jnp.float32
BF16 = jnp.bfloat16
SDS = jax.ShapeDtypeStruct
MESH = pl.DeviceIdType.MESH

EPS = 1e-6
GLA_HEADS = 4
GLA_GATE_RANK = 16
GLA_GATE_TAU = 16.0
GLA_CHUNK = 128
SB_HEADS = 16
SB_TQ = 512
SB_TK = 128
ADAM_LR = 0.001
ADAM_B1 = 0.9
ADAM_B2 = 0.999
ADAM_EPS = 1e-08
ADAM_WD = 0.01
ADAM_STEP = 10

LANES = 128
SUBLANES = 8
N_CHIPS = 4
N_DEV = 8
VMEM_LIMIT = 56 * 1024 * 1024


def _tile(dim, target, unit=LANES):
    if dim <= target:
        return dim
    t = (target // unit) * unit
    while t >= unit:
        if dim % t == 0:
            return t
        t -= unit
    raise ValueError(f"no tile for {dim}")


def _cp(*sem):
    return pltpu.CompilerParams(dimension_semantics=sem, vmem_limit_bytes=VMEM_LIMIT)


def _sigmoid(x):
    return 1.0 / (1.0 + jnp.exp(-x))


def _dot(a, b, ca=1, cb=0):
    return lax.dot_general(a, b, (((ca,), (cb,)), ((), ())), preferred_element_type=F32)


def _split_dot(tri, x):
    hi = x.astype(BF16)
    lo = (x - hi.astype(F32)).astype(BF16)
    return _dot(tri, hi) + _dot(tri, lo)


def _hilo(x):
    hi = x.astype(BF16)
    return hi, (x - hi.astype(F32)).astype(BF16)


def _dot3(a, b, ca=1, cb=0):
    return _dot(a[0], b[0], ca, cb) + _dot(a[0], b[1], ca, cb) + _dot(a[1], b[0], ca, cb)


def _tri(n, kind):
    r = lax.broadcasted_iota(jnp.int32, (n, n), 0)
    c = lax.broadcasted_iota(jnp.int32, (n, n), 1)
    m = {"le": c <= r, "ge": c >= r, "lt": c < r, "gt": c > r}[kind]
    return jnp.where(m, 1.0, 0.0).astype(BF16)


def _div(i, n):
    return i if n == 1 else lax.div(i, n)


def _rem(i, n):
    return 0 if n == 1 else lax.rem(i, n)


class View:
    def __init__(self, arr, kind="plain", lead=(), g0=0, ng=None):
        self.arr, self.kind, self.lead, self.g0 = arr, kind, tuple(lead), g0
        self.ng = (arr.shape[0] - g0) if ng is None else ng
        r, c = arr.shape[-2:]
        self.runit, self.cunit = r, c
        self.shape = {"plain": (r, c), "cols": (r, self.ng * c), "rows": (self.ng * r, c)}[kind]

    def spec(self, br, bc, rfn, cfn):
        if self.kind == "plain":
            return pl.BlockSpec((br, bc), lambda *g: (rfn(*g), cfn(*g)))
        none = (None,) * (1 + len(self.lead))
        if self.kind == "cols":
            per = self.cunit // bc
            return pl.BlockSpec(none + (br, bc), lambda *g: (self.g0 + _div(cfn(*g), per), *self.lead, rfn(*g), _rem(cfn(*g), per)))
        per = self.runit // br
        return pl.BlockSpec(none + (br, bc), lambda *g: (self.g0 + _div(rfn(*g), per), *self.lead, _rem(rfn(*g), per), cfn(*g)))


def _as_view(a):
    return a if isinstance(a, View) else View(a)


def matmul(a, b, *, name, ta=False, tb=False, out_dtype=F32, residual=None, out_chips=None, tm=1408, tn=1408, tk=2816):
    a, b = _as_view(a), _as_view(b)
    (k, m) = a.shape if ta else a.shape[::-1]
    (n, kb) = b.shape if tb else b.shape[::-1]
    assert k == kb, (a.shape, b.shape, ta, tb)
    m_unit = a.cunit if ta else a.runit
    ka_unit = a.runit if ta else a.cunit
    n_unit = b.runit if tb else b.cunit
    kb_unit = b.cunit if tb else b.runit
    if out_chips is not None:
        n_unit = min(n_unit, n // out_chips)
    tm, tn = _tile(min(m, m_unit), tm), _tile(min(n, n_unit), tn)
    tk = _tile(min(k, ka_unit, kb_unit), tk)
    assert ka_unit % tk == 0 and kb_unit % tk == 0, (ka_unit, kb_unit, tk)
    nk = k // tk
    ca, cb = (0 if ta else 1), (1 if tb else 0)

    def body(a_ref, b_ref, *refs):
        r_ref = refs[0] if residual is not None else None
        o_ref = refs[-1] if nk == 1 else refs[-2]

        def finish(r):
            if residual is not None:
                r = r + r_ref[...]
            o_ref[...] = r.astype(out_dtype)

        part = _dot(a_ref[...].astype(BF16), b_ref[...].astype(BF16), ca, cb)
        if nk == 1:
            finish(part)
            return
        acc = refs[-1]
        kk = pl.program_id(2)

        @pl.when(kk == 0)
        def _():
            acc[...] = part

        @pl.when(kk > 0)
        def _():
            acc[...] += part

        @pl.when(kk == nk - 1)
        def _():
            finish(acc[...])

    gi, gj, gk = (lambda i, j, kk: i), (lambda i, j, kk: j), (lambda i, j, kk: kk)
    a_spec = a.spec(tk, tm, gk, gi) if ta else a.spec(tm, tk, gi, gk)
    b_spec = b.spec(tn, tk, gj, gk) if tb else b.spec(tk, tn, gk, gj)
    if out_chips is None:
        out = View(SDS((m, n), out_dtype))
    else:
        out = View(SDS((out_chips, m, n // out_chips), out_dtype), "cols")
    o_spec = out.spec(tm, tn, gi, gj)
    in_specs, args = [a_spec, b_spec], [a.arr, b.arr]
    if residual is not None:
        in_specs.append(pl.BlockSpec((tm, tn), lambda i, j, kk: (i, j)))
        args.append(residual)
    return pl.pallas_call(
        body, name=name, grid=(m // tm, n // tn, nk), in_specs=in_specs, out_specs=o_spec,
        out_shape=out.arr, scratch_shapes=[] if nk == 1 else [pltpu.VMEM((tm, tn), F32)],
        compiler_params=_cp("parallel", "parallel", "arbitrary"))(*args)


def rmsnorm_fwd(x, w, *, name, tr=256):
    s, d = x.shape

    def body(x_ref, w_ref, o_ref):
        xf = x_ref[...]
        r = lax.rsqrt(jnp.mean(xf * xf, axis=-1, keepdims=True) + EPS)
        o_ref[...] = (xf * r * w_ref[...]).astype(BF16)

    row = pl.BlockSpec((tr, d), lambda i: (i, 0))
    return pl.pallas_call(
        body, name=name, grid=(s // tr,), in_specs=[row, pl.BlockSpec((1, d), lambda i: (0, 0))], out_specs=row,
        out_shape=SDS((s, d), BF16), compiler_params=_cp("parallel"))(x, w.reshape(1, d))


def rmsnorm_bwd(dy, x, w, dres, *, name, tr=256):
    s, d = x.shape

    def body(dy_ref, x_ref, w_ref, dres_ref, dx_ref, dw_ref):
        i = pl.program_id(0)
        xf = x_ref[...]
        r = lax.rsqrt(jnp.mean(xf * xf, axis=-1, keepdims=True) + EPS)
        xh = xf * r
        dyf = dy_ref[...].astype(F32)
        dxh = dyf * w_ref[...]
        dx_ref[...] = dres_ref[...] + r * (dxh - xh * jnp.mean(dxh * xh, axis=-1, keepdims=True))
        part = jnp.sum(dyf * xh, axis=0, keepdims=True)

        @pl.when(i == 0)
        def _():
            dw_ref[...] = part

        @pl.when(i > 0)
        def _():
            dw_ref[...] += part

    row = pl.BlockSpec((tr, d), lambda i: (i, 0))
    vec = pl.BlockSpec((1, d), lambda i: (0, 0))
    return pl.pallas_call(
        body, name=name, grid=(s // tr,), in_specs=[row, row, vec, row], out_specs=[row, vec],
        out_shape=[SDS((s, d), F32), SDS((1, d), F32)], compiler_params=_cp("arbitrary"))(dy, x, w.reshape(1, d), dres)


def final_loss(h, w, target, *, name, tr=256):
    s, d = h.shape

    def body(h_ref, w_ref, t_ref, dh_ref, dw_ref, loss_ref):
        i = pl.program_id(0)
        xf = h_ref[...]
        r = lax.rsqrt(jnp.mean(xf * xf, axis=-1, keepdims=True) + EPS)
        xh = xf * r
        err = xh * w_ref[...] - t_ref[...]
        lpart = 0.5 * jnp.sum(jnp.sum(err * err, axis=-1, keepdims=True) * (1.0 / d), axis=0, keepdims=True)
        dy = err * (1.0 / d)
        dxh = dy * w_ref[...]
        dh_ref[...] = r * (dxh - xh * jnp.mean(dxh * xh, axis=-1, keepdims=True))
        part = jnp.sum(dy * xh, axis=0, keepdims=True)
        lrow = jnp.broadcast_to(lpart, (1, LANES))

        @pl.when(i == 0)
        def _():
            dw_ref[...] = part
            loss_ref[...] = lrow

        @pl.when(i > 0)
        def _():
            dw_ref[...] += part
            loss_ref[...] += lrow

    row = pl.BlockSpec((tr, d), lambda i: (i, 0))
    vec = pl.BlockSpec((1, d), lambda i: (0, 0))
    return pl.pallas_call(
        body, name=name, grid=(s // tr,), in_specs=[row, vec, row],
        out_specs=[row, vec, pl.BlockSpec((1, LANES), lambda i: (0, 0))],
        out_shape=[SDS((s, d), F32), SDS((1, d), F32), SDS((1, LANES), F32)],
        compiler_params=_cp("arbitrary"))(h, w.reshape(1, d), target)


def swiglu_fwd(gu, *, name, tr=512, tc=1408):
    s, f2 = gu.shape
    f = f2 // 2
    tc = _tile(f, tc)
    nf = f // tc

    def body(g_ref, u_ref, o_ref):
        g = g_ref[...].astype(F32)
        o_ref[...] = (g * _sigmoid(g) * u_ref[...].astype(F32)).astype(BF16)

    return pl.pallas_call(
        body, name=name, grid=(s // tr, nf),
        in_specs=[pl.BlockSpec((tr, tc), lambda i, j: (i, j)), pl.BlockSpec((tr, tc), lambda i, j: (i, j + nf))],
        out_specs=pl.BlockSpec((tr, tc), lambda i, j: (i, j)), out_shape=SDS((s, f), BF16),
        compiler_params=_cp("parallel", "parallel"))(gu, gu)


def swiglu_bwd(gu, dact, *, name, tr=512, tc=1408):
    s, f2 = gu.shape
    f = f2 // 2
    tc = _tile(f, tc)
    nf = f // tc

    def body(g_ref, u_ref, d_ref, o_ref):
        j = pl.program_id(1)
        g = g_ref[...].astype(F32)
        d = d_ref[...].astype(F32)
        sg = _sigmoid(g)

        @pl.when(j < nf)
        def _():
            o_ref[...] = (d * u_ref[...].astype(F32) * sg * (1.0 + g * (1.0 - sg))).astype(BF16)

        @pl.when(j >= nf)
        def _():
            o_ref[...] = (d * g * sg).astype(BF16)

    return pl.pallas_call(
        body, name=name, grid=(s // tr, 2 * nf),
        in_specs=[pl.BlockSpec((tr, tc), lambda i, j: (i, j % nf)), pl.BlockSpec((tr, tc), lambda i, j: (i, nf + j % nf)),
                  pl.BlockSpec((tr, tc), lambda i, j: (i, j % nf))],
        out_specs=pl.BlockSpec((tr, tc), lambda i, j: (i, j)), out_shape=SDS((s, f2), BF16),
        compiler_params=_cp("parallel", "parallel"))(gu, gu, dact)


def _gate_z(gl_ref, w_ref, b_ref):
    glb = gl_ref[...].astype(BF16)
    return glb, _dot(glb, w_ref[...]) + b_ref[...]


def gate_fwd(proj, wg, bg, *, name, tr=512):
    s, inw = proj.shape
    qk = wg.shape[1]
    glc = inw // LANES - 1

    def body(gl_ref, w_ref, b_ref, g_ref):
        _, z = _gate_z(gl_ref, w_ref, b_ref)
        g_ref[...] = (jnp.minimum(z, 0.0) - jnp.log(1.0 + jnp.exp(-jnp.abs(z)))) * (1.0 / GLA_GATE_TAU)

    return pl.pallas_call(
        body, name=name, grid=(s // tr,),
        in_specs=[pl.BlockSpec((tr, LANES), lambda i: (i, glc)), pl.BlockSpec((LANES, qk), lambda i: (0, 0)),
                  pl.BlockSpec((1, qk), lambda i: (0, 0))],
        out_specs=pl.BlockSpec((tr, qk), lambda i: (i, 0)), out_shape=SDS((s, qk), F32),
        compiler_params=_cp("parallel"))(proj, wg, bg)


def gate_bwd(dg, proj, wg, bg, *, name, tr=512):
    s, inw = proj.shape
    qk = wg.shape[1]
    glc = inw // LANES - 1

    def body(dg_ref, gl_ref, w_ref, b_ref, dgl_ref, dw_ref, db_ref):
        i = pl.program_id(0)
        glb, z = _gate_z(gl_ref, w_ref, b_ref)
        dz = dg_ref[...] * (1.0 / (1.0 + jnp.exp(z))) * (1.0 / GLA_GATE_TAU)
        dzb = dz.astype(BF16)
        dgl_ref[...] = _dot(dzb, w_ref[...], 1, 1).astype(BF16)
        pw = _dot(glb, dzb, 0, 0)
        pb = jnp.sum(dz, axis=0, keepdims=True)

        @pl.when(i == 0)
        def _():
            dw_ref[...] = pw
            db_ref[...] = pb

        @pl.when(i > 0)
        def _():
            dw_ref[...] += pw
            db_ref[...] += pb

    return pl.pallas_call(
        body, name=name, grid=(s // tr,),
        in_specs=[pl.BlockSpec((tr, qk), lambda i: (i, 0)), pl.BlockSpec((tr, LANES), lambda i: (i, glc)),
                  pl.BlockSpec((LANES, qk), lambda i: (0, 0)), pl.BlockSpec((1, qk), lambda i: (0, 0))],
        out_specs=[pl.BlockSpec((tr, LANES), lambda i: (i, 0)), pl.BlockSpec((LANES, qk), lambda i: (0, 0)),
                   pl.BlockSpec((1, qk), lambda i: (0, 0))],
        out_shape=[SDS((s, LANES), BF16), SDS((LANES, qk), F32), SDS((1, qk), F32)],
        compiler_params=_cp("arbitrary"))(dg, proj, wg, bg)


def _gla_chunk_terms(q_ref, k_ref, g_ref, c, scale):
    q = q_ref[...] * scale
    k = k_ref[...]
    gg = g_ref[...]
    b = _split_dot(_tri(c, "le"), gg)
    row = lax.broadcasted_iota(jnp.int32, gg.shape, 0)
    bm = jnp.sum(jnp.where(row < c // 2, gg, 0.0), axis=0, keepdims=True)
    bl = jnp.sum(gg, axis=0, keepdims=True)
    eb, em, emi, el = jnp.exp(b), jnp.exp(b - bm), jnp.exp(bm - b), jnp.exp(bl - b)
    return q, k, bl, eb, em, emi, el


def _causal(a):
    r = lax.broadcasted_iota(jnp.int32, a.shape, 0)
    c = lax.broadcasted_iota(jnp.int32, a.shape, 1)
    return jnp.where(r >= c, a, 0.0)


def gla_fwd(proj, g, *, name, c=GLA_CHUNK):
    s = proj.shape[0]
    qk = g.shape[1]
    dk, dv = qk // GLA_HEADS, 2 * qk // GLA_HEADS
    nc = s // c
    scale = dk ** -0.5
    kq = qk // dk

    def body(q_ref, k_ref, v_ref, g_ref, o_ref, st_ref, state):
        @pl.when(pl.program_id(1) == 0)
        def _():
            state[...] = jnp.zeros_like(state)

        q, k, bl, eb, em, emi, el = _gla_chunk_terms(q_ref, k_ref, g_ref, c, scale)
        v2 = _hilo(v_ref[...])
        st = state[...]
        st_ref[...] = st
        a = _causal(_dot3(_hilo(q * em), _hilo(k * emi), 1, 1))
        o_ref[...] = _dot3(_hilo(q * eb), _hilo(st), 1, 1) + _dot3(_hilo(a), v2)
        state[...] = st * jnp.exp(bl) + _dot3(v2, _hilo(k * el), 0, 0)

    return pl.pallas_call(
        body, name=name, grid=(GLA_HEADS, nc),
        in_specs=[pl.BlockSpec((c, dk), lambda h, i: (i, h)), pl.BlockSpec((c, dk), lambda h, i: (i, kq + h)),
                  pl.BlockSpec((c, dv), lambda h, i: (i, kq + h)), pl.BlockSpec((c, dk), lambda h, i: (i, h))],
        out_specs=[pl.BlockSpec((c, dv), lambda h, i: (i, h)),
                   pl.BlockSpec((None, None, dv, dk), lambda h, i: (h, i, 0, 0))],
        out_shape=[SDS((s, 2 * qk), F32), SDS((GLA_HEADS, nc, dv, dk), F32)],
        scratch_shapes=[pltpu.VMEM((dv, dk), F32)],
        compiler_params=_cp("parallel", "arbitrary"))(proj, proj, proj, g)


def gla_bwd(proj, g, states, do, *, name, c=GLA_CHUNK):
    s = proj.shape[0]
    qk = g.shape[1]
    dk, dv = qk // GLA_HEADS, 2 * qk // GLA_HEADS
    nc = s // c
    scale = dk ** -0.5
    kq = qk // dk

    def body(q_ref, k_ref, v_ref, g_ref, do_ref, st_ref, dq_ref, dk_ref, dv_ref, dg_ref, dstate, dgc):
        @pl.when(pl.program_id(1) == 0)
        def _():
            dstate[...] = jnp.zeros_like(dstate)
            dgc[...] = jnp.zeros_like(dgc)

        q, k, bl, eb, em, emi, el = _gla_chunk_terms(q_ref, k_ref, g_ref, c, scale)
        v2, do2 = _hilo(v_ref[...]), _hilo(do_ref[...])
        qe, qm, km, kd = _hilo(q * eb), _hilo(q * em), _hilo(k * emi), _hilo(k * el)
        ds = dstate[...]
        ds2 = _hilo(ds)
        a = _hilo(_causal(_dot3(qm, km, 1, 1)))
        dv_ref[...] = (_dot3(a, do2, 0, 0) + _dot3(kd, ds2, 1, 1)).astype(BF16)
        da = _hilo(_causal(_dot3(do2, v2, 1, 1)))
        dq = _dot3(da, km) * em + _dot3(do2, _hilo(st_ref[...])) * eb
        dkk = _dot3(da, qm, 0, 0) * emi + _dot3(v2, ds2) * el
        dstate[...] = ds * jnp.exp(bl) + _dot3(do2, qe, 0, 0)
        db = q * dq - k * dkk
        dg_ref[...] = _split_dot(_tri(c, "ge"), db) + dgc[...]
        dgc[...] += jnp.sum(db, axis=0, keepdims=True)
        dq_ref[...] = (dq * scale).astype(BF16)
        dk_ref[...] = dkk.astype(BF16)

    rev = lambda i: nc - 1 - i
    qspec = pl.BlockSpec((c, dk), lambda h, i: (rev(i), h))
    vspec = pl.BlockSpec((c, dv), lambda h, i: (rev(i), h))
    return pl.pallas_call(
        body, name=name, grid=(GLA_HEADS, nc),
        in_specs=[qspec, pl.BlockSpec((c, dk), lambda h, i: (rev(i), kq + h)),
                  pl.BlockSpec((c, dv), lambda h, i: (rev(i), kq + h)), qspec, vspec,
                  pl.BlockSpec((None, None, dv, dk), lambda h, i: (h, rev(i), 0, 0))],
        out_specs=[qspec, qspec, vspec, qspec],
        out_shape=[SDS((s, qk), BF16), SDS((s, qk), BF16), SDS((s, 2 * qk), BF16), SDS((s, qk), F32)],
        scratch_shapes=[pltpu.VMEM((dv, dk), F32), pltpu.VMEM((1, dk), F32)],
        compiler_params=_cp("parallel", "arbitrary"))(proj, proj, proj, g, do, states)


def gnorm_fwd(o, proj, gw, *, name, tr=512):
    s, v = o.shape
    dv = v // GLA_HEADS
    roff = 2 * GLA_HEADS

    def body(o_ref, r_ref, w_ref, y_ref):
        of = o_ref[...]
        rs = lax.rsqrt(jnp.mean(of * of, axis=-1, keepdims=True) + EPS)
        r = r_ref[...]
        y_ref[...] = (of * rs * w_ref[...] * (r * _sigmoid(r))).astype(BF16)

    blk = pl.BlockSpec((tr, dv), lambda i, h: (i, h))
    return pl.pallas_call(
        body, name=name, grid=(s // tr, GLA_HEADS),
        in_specs=[blk, pl.BlockSpec((tr, dv), lambda i, h: (i, roff + h)), pl.BlockSpec((1, dv), lambda i, h: (0, 0))],
        out_specs=blk, out_shape=SDS((s, v), BF16), compiler_params=_cp("parallel", "parallel"))(o, proj, gw)


def gnorm_bwd(dy, o, proj, gw, *, name, tr=512):
    s, v = o.shape
    dv = v // GLA_HEADS
    roff = 2 * GLA_HEADS

    def body(dy_ref, o_ref, r_ref, w_ref, do_ref, dr_ref, dw_ref):
        first = jnp.logical_and(pl.program_id(0) == 0, pl.program_id(1) == 0)
        of = o_ref[...]
        rs = lax.rsqrt(jnp.mean(of * of, axis=-1, keepdims=True) + EPS)
        n = of * rs
        r = r_ref[...]
        sg = _sigmoid(r)
        dyf = dy_ref[...].astype(F32)
        dn_w = dyf * (r * sg)
        dr_ref[...] = (dyf * n * w_ref[...] * sg * (1.0 + r * (1.0 - sg))).astype(BF16)
        dn = dn_w * w_ref[...]
        do_ref[...] = rs * (dn - n * jnp.mean(dn * n, axis=-1, keepdims=True))
        part = jnp.sum(dn_w * n, axis=0, keepdims=True)

        @pl.when(first)
        def _():
            dw_ref[...] = part

        @pl.when(jnp.logical_not(first))
        def _():
            dw_ref[...] += part

    blk = pl.BlockSpec((tr, dv), lambda i, h: (i, h))
    vec = pl.BlockSpec((1, dv), lambda i, h: (0, 0))
    return pl.pallas_call(
        body, name=name, grid=(s // tr, GLA_HEADS),
        in_specs=[blk, blk, pl.BlockSpec((tr, dv), lambda i, h: (i, roff + h)), vec],
        out_specs=[blk, blk, vec], out_shape=[SDS((s, v), F32), SDS((s, v), BF16), SDS((1, dv), F32)],
        compiler_params=_cp("arbitrary", "arbitrary"))(dy, o, proj, gw)


def _sb_block(kblk, q, ks, q0, scale, carry):
    tk, tq = kblk.shape[0], q.shape[0]
    z = _dot(kblk, q, 1, 1) * scale
    kpos = ks + lax.broadcasted_iota(jnp.int32, (tk, tq), 0)
    qpos = q0 + lax.broadcasted_iota(jnp.int32, (tk, tq), 1)
    mask = kpos < qpos
    sp = jnp.maximum(z, 0.0) + jnp.log(1.0 + jnp.exp(-jnp.abs(z)))
    lf = jnp.where(mask, -sp, 0.0)
    later = _split_dot(_tri(tk, "gt"), lf)
    a = jnp.where(mask, jnp.exp(z - sp + later + carry), 0.0)
    return z, sp, mask, lf, a


def sb_fwd(q, kv, *, name, tq=SB_TQ, tk=SB_TK):
    s, w = q.shape
    hd = w // SB_HEADS
    nq, nkb = s // tq, s // tk
    scale = hd ** -0.5
    per = tq // tk
    assert per % 2 == 0

    def body(q_ref, k_ref, v_ref, o_ref, car_ref, o_acc):
        qi = pl.program_id(1)
        qb = q_ref[...]
        q0 = qi * tq
        pairs = (qi + 1) * (per // 2)
        car_ref[...] = jnp.zeros_like(car_ref)
        o_acc[...] = jnp.zeros_like(o_acc)

        def step(i, carry):
            for t in range(2):
                kj = 2 * (pairs - 1 - i) + 1 - t
                ks = pl.multiple_of(kj * tk, tk)
                car_ref[pl.ds(kj, 1), :] = carry
                _, _, _, lf, a = _sb_block(k_ref[pl.ds(ks, tk), :], qb, ks, q0, scale, carry)
                o_acc[...] += _dot(a.astype(BF16), v_ref[pl.ds(ks, tk), :], 0, 0)
                carry = carry + jnp.sum(lf, axis=0, keepdims=True)
            return carry

        lax.fori_loop(0, pairs, step, jnp.zeros((1, tq), F32))
        o_ref[...] = o_acc[...].astype(BF16)

    qspec = pl.BlockSpec((tq, hd), lambda h, i: (i, h))
    return pl.pallas_call(
        body, name=name, grid=(SB_HEADS, nq),
        in_specs=[qspec, pl.BlockSpec((None, s, hd), lambda h, i: (0, 0, h)), pl.BlockSpec((None, s, hd), lambda h, i: (1, 0, h))],
        out_specs=[qspec, pl.BlockSpec((None, None, nkb, tq), lambda h, i: (h, i, 0, 0))],
        out_shape=[SDS((s, w), BF16), SDS((SB_HEADS, nq, nkb, tq), F32)],
        scratch_shapes=[pltpu.VMEM((tq, hd), F32)],
        compiler_params=_cp("parallel", "parallel"))(q, kv, kv)


def sb_bwd(q, kv, do, car, *, name, tq=SB_TQ, tk=SB_TK):
    s, w = q.shape
    hd = w // SB_HEADS
    nq, nkb = s // tq, s // tk
    scale = hd ** -0.5
    per = tq // tk
    assert per % 2 == 0

    def body(q_ref, k_ref, v_ref, do_ref, car_ref, dq_ref, dkv_ref, dq_acc, dk_acc, dv_acc):
        qi = pl.program_id(1)
        qb = q_ref[...]
        dob = do_ref[...]
        q0 = qi * tq
        pairs = (qi + 1) * (per // 2)
        dq_acc[...] = jnp.zeros_like(dq_acc)

        @pl.when(qi == 0)
        def _():
            dk_acc[...] = jnp.zeros_like(dk_acc)
            dv_acc[...] = jnp.zeros_like(dv_acc)

        def step(i, pcar):
            for t in range(2):
                kj = 2 * i + t
                ks = pl.multiple_of(kj * tk, tk)
                kblk = k_ref[pl.ds(ks, tk), :]
                z, sp, mask, _, a = _sb_block(kblk, qb, ks, q0, scale, car_ref[pl.ds(kj, 1), :])
                p = a * _dot(v_ref[pl.ds(ks, tk), :], dob, 1, 1)
                before = _split_dot(_tri(tk, "lt"), p)
                sg = jnp.exp(z - sp)
                dz = (jnp.where(mask, p * (1.0 - sg) - (pcar + before) * sg, 0.0) * scale).astype(BF16)
                dk_acc[pl.ds(ks, tk), :] += _dot(dz, qb)
                dv_acc[pl.ds(ks, tk), :] += _dot(a.astype(BF16), dob)
                dq_acc[...] += _dot(dz, kblk, 0, 0)
                pcar = pcar + jnp.sum(p, axis=0, keepdims=True)
            return pcar

        lax.fori_loop(0, pairs, step, jnp.zeros((1, tq), F32))
        dq_ref[...] = dq_acc[...].astype(BF16)

        @pl.when(qi == nq - 1)
        def _():
            dkv_ref[0] = dk_acc[...].astype(BF16)
            dkv_ref[1] = dv_acc[...].astype(BF16)

    qspec = pl.BlockSpec((tq, hd), lambda h, i: (i, h))
    return pl.pallas_call(
        body, name=name, grid=(SB_HEADS, nq),
        in_specs=[qspec, pl.BlockSpec((None, s, hd), lambda h, i: (0, 0, h)), pl.BlockSpec((None, s, hd), lambda h, i: (1, 0, h)),
                  qspec, pl.BlockSpec((None, None, nkb, tq), lambda h, i: (h, i, 0, 0))],
        out_specs=[qspec, pl.BlockSpec((2, s, hd), lambda h, i: (0, 0, h))],
        out_shape=[SDS((s, w), BF16), SDS((2, s, w), BF16)],
        scratch_shapes=[pltpu.VMEM((tq, hd), F32), pltpu.VMEM((s, hd), F32), pltpu.VMEM((s, hd), F32)],
        compiler_params=_cp("parallel", "arbitrary"))(q, kv, kv, do, car)


def adamw(w, g, m, v, *, name):
    shape = w.shape
    c = shape[-1]
    r = w.size // c
    tr = _tile(r, max(8, (3 * LANES * 1024) // c), unit=8) if r >= 8 else r

    def body(w_ref, g_ref, m_ref, v_ref, d_ref, nm_ref, nv_ref):
        gf = g_ref[...]
        mn = ADAM_B1 * m_ref[...] + (1.0 - ADAM_B1) * gf
        vn = ADAM_B2 * v_ref[...] + (1.0 - ADAM_B2) * (gf * gf)
        m_hat = mn / (1.0 - ADAM_B1 ** ADAM_STEP)
        v_hat = vn / (1.0 - ADAM_B2 ** ADAM_STEP)
        d_ref[...] = -ADAM_LR * (m_hat / (jnp.sqrt(v_hat) + ADAM_EPS) + ADAM_WD * w_ref[...])
        nm_ref[...] = mn
        nv_ref[...] = vn

    blk = pl.BlockSpec((tr, c), lambda i: (i, 0))
    outs = pl.pallas_call(
        body, name=name, grid=(r // tr,), in_specs=[blk] * 4, out_specs=[blk] * 3,
        out_shape=[SDS((r, c), F32)] * 3, compiler_params=_cp("parallel"))(
            *(t.reshape(r, c) for t in (w, g, m, v)))
    return tuple(o.reshape(shape) for o in outs)


def _ffn_fwd(h, nw, w_gu, w_dn, tag):
    f = rmsnorm_fwd(h, nw, name=f"{tag}_norm")
    gu = matmul(f, w_gu, name=f"{tag}_gate_up", out_dtype=BF16)
    act = swiglu_fwd(gu, name=f"{tag}_act")
    return matmul(act, w_dn, name=f"{tag}_down", residual=h), (f, gu, act)


def _ffn_bwd(dh, h, nw, w_gu, w_dn, saved, tag):
    f, gu, act = saved
    dact = matmul(dh, w_dn, name=f"{tag}_dact", tb=True, out_dtype=BF16)
    dw_dn = matmul(act, dh, name=f"{tag}_dw_down", ta=True, out_dtype=BF16)
    dgu = swiglu_bwd(gu, dact, name=f"{tag}_dgu")
    dw_gu = matmul(f, dgu, name=f"{tag}_dw_gate_up", ta=True, out_dtype=BF16, out_chips=N_CHIPS)
    df = matmul(dgu, w_gu, name=f"{tag}_df", tb=True)
    dh_in, dnw = rmsnorm_bwd(df, h, nw, dh, name=f"{tag}_dnorm")
    return dh_in, dnw, dw_gu, dw_dn


def local_step(x, target, p):
    gu = [View(p["w_gu"], "cols", lead=(l,)) for l in range(2)]
    dn = [View(p["w_dn"], "rows", lead=(l,)) for l in range(2)]
    w_kv = View(p["w_kv"], "cols")
    a0 = rmsnorm_fwd(x, p["an0"], name="l0_attn_norm")
    proj = matmul(a0, p["w_in"], name="gla_in")
    g = gate_fwd(proj, p["wg"], p["bg"], name="gla_gate")
    o, states = gla_fwd(proj, g, name="gla_scan")
    og = gnorm_fwd(o, proj, p["gw"], name="gla_outnorm")
    h1 = matmul(og, p["w_out"], name="gla_out", residual=x)
    h2, ffn0 = _ffn_fwd(h1, p["fn0"], gu[0], dn[0], "ffn0")
    kvn = rmsnorm_fwd(h2, p["kvn"], name="kv_norm")
    kv = matmul(kvn, w_kv, name="sb_kv", out_dtype=BF16, out_chips=2)
    a1 = rmsnorm_fwd(h2, p["an1"], name="l1_attn_norm")
    q2 = matmul(a1, p["w_q"], name="sb_q", out_dtype=BF16)
    o2, car = sb_fwd(q2, kv, name="sb_attn")
    h3 = matmul(o2, p["w_so"], name="sb_out", residual=h2)
    h4, ffn1 = _ffn_fwd(h3, p["fn1"], gu[1], dn[1], "ffn1")
    dh4, d_fin, loss_row = final_loss(h4, p["finn"], target, name="final_loss")

    dh3, d_fn1, dw_gu1, dw_dn1 = _ffn_bwd(dh4, h3, p["fn1"], gu[1], dn[1], ffn1, "ffn1")
    do2 = matmul(dh3, p["w_so"], name="sb_do", tb=True, out_dtype=BF16)
    dw_so = matmul(o2, dh3, name="sb_dw_out", ta=True, out_dtype=BF16)
    dq2, dkv = sb_bwd(q2, kv, do2, car, name="sb_attn_bwd")
    dkv = View(dkv, "cols")
    dw_q = matmul(a1, dq2, name="sb_dw_q", ta=True, out_dtype=BF16)
    da1 = matmul(dq2, p["w_q"], name="sb_da", tb=True)
    dh2, d_an1 = rmsnorm_bwd(da1, h2, p["an1"], dh3, name="l1_attn_dnorm")
    dw_kv = matmul(kvn, dkv, name="sb_dw_kv", ta=True, out_dtype=BF16, out_chips=N_CHIPS)
    dkvn = matmul(dkv, w_kv, name="sb_dkvn", tb=True)
    dh2, d_kvn = rmsnorm_bwd(dkvn, h2, p["kvn"], dh2, name="kv_dnorm")
    dh1, d_fn0, dw_gu0, dw_dn0 = _ffn_bwd(dh2, h1, p["fn0"], gu[0], dn[0], ffn0, "ffn0")
    dog = matmul(dh1, p["w_out"], name="gla_dog", tb=True, out_dtype=BF16)
    dw_out = matmul(og, dh1, name="gla_dw_out", ta=True, out_dtype=BF16)
    do, dr, d_gw = gnorm_bwd(dog, o, proj, p["gw"], name="gla_outnorm_bwd")
    dq, dk, dv, dg = gla_bwd(proj, g, states, do, name="gla_scan_bwd")
    dgl, d_wg, d_bg = gate_bwd(dg, proj, p["wg"], p["bg"], name="gla_gate_bwd")
    dproj = jnp.concatenate([dq, dk, dv, dr, dgl], axis=1)
    dw_in = matmul(a0, dproj, name="gla_dw_in", ta=True, out_dtype=BF16)
    da0 = matmul(dproj, p["w_in"], name="gla_da", tb=True)
    dx, d_an0 = rmsnorm_bwd(da0, x, p["an0"], dh1, name="l0_attn_dnorm")

    grads = dict(an0=d_an0, an1=d_an1, fn0=d_fn0, fn1=d_fn1, kvn=d_kvn, finn=d_fin, wg=d_wg, bg=d_bg, gw=d_gw,
                 w_in=dw_in, w_out=dw_out, w_kv=dw_kv, w_q=dw_q, w_so=dw_so,
                 w_gu0=dw_gu0, w_gu1=dw_gu1, w_dn0=dw_dn0, w_dn1=dw_dn1)
    return loss_row, dx, grads


ANY = pl.BlockSpec(memory_space=pl.ANY)


def _coords():
    return lax.axis_index("x"), lax.axis_index("y"), lax.axis_index("c")


def _other_chips(x, y):
    return [(1 - x, y), (x, 1 - y), (1 - x, 1 - y)]


def _remote(src, dst, send_sem, recv_sem, dev):
    return pltpu.make_async_remote_copy(src_ref=src, dst_ref=dst, send_sem=send_sem, recv_sem=recv_sem,
                                        device_id=dev, device_id_type=MESH)


def _dma_sems(n):
    return pltpu.SemaphoreType.DMA((n,))


def _row_tile(r, c):
    return _tile(r, max(16, (4 * LANES * 1024) // c), unit=16)


def place_shard(w, chip, *, name):
    _, r, c = w.shape
    tr = _row_tile(r, c)

    def body(c_ref, w_ref, o_ref):
        o_ref[...] = w_ref[...].astype(BF16)

    return pl.pallas_call(
        body, name=name, out_shape=SDS((N_CHIPS,) + w.shape, BF16),
        grid_spec=pltpu.PrefetchScalarGridSpec(
            num_scalar_prefetch=1, grid=(2, r // tr),
            in_specs=[pl.BlockSpec((None, tr, c), lambda h, i, c_ref: (h, i, 0))],
            out_specs=pl.BlockSpec((None, None, tr, c), lambda h, i, c_ref: (c_ref[0], h, i, 0))),
        compiler_params=_cp("parallel", "parallel"))(chip, w)


def allgather_chips(ws, *, name):
    n = len(ws)

    def body(*refs):
        out_refs = refs[n:2 * n]
        send_sems, recv_sems = refs[2 * n:]
        x, y, c = _coords()
        me = 2 * x + y
        chips = _other_chips(x, y)

        def copy(i, k, chip, half, dev):
            blk = out_refs[i].at[chip, half]
            return _remote(blk, blk, send_sems.at[6 * i + k], recv_sems.at[6 * i + k], dev)

        first = [copy(i, k, me, c, (cx, cy, c)) for i in range(n) for k, (cx, cy) in enumerate(chips)]
        for cp in first:
            cp.start()
        passed = []
        for i in range(n):
            for k, (cx, cy) in enumerate(chips):
                copy(i, k, 2 * cx + cy, c, (x, y, c)).wait_recv()
                passed.append(copy(i, 3 + k, 2 * cx + cy, c, (x, y, 1 - c)))
                passed[-1].start()
        for i in range(n):
            for k, (cx, cy) in enumerate(chips):
                copy(i, 3 + k, 2 * cx + cy, 1 - c, (x, y, c)).wait_recv()
        for cp in first + passed:
            cp.wait_send()

    return pl.pallas_call(
        body, name=name, in_specs=[ANY] * n, out_specs=[ANY] * n, out_shape=[SDS(w.shape, w.dtype) for w in ws],
        input_output_aliases={i: i for i in range(n)}, scratch_shapes=[_dma_sems(6 * n), _dma_sems(6 * n)])(*ws)


def sibling_exchange(gs, *, name):
    n = len(gs)

    def body(*refs):
        g_refs, a_refs = refs[:n], refs[n:2 * n]
        send_sems, recv_sems = refs[2 * n:]
        x, y, c = _coords()
        cps = [_remote(g_refs[i].at[:, 1 - c], a_refs[i], send_sems.at[i], recv_sems.at[i], (x, y, 1 - c)) for i in range(n)]
        for cp in cps:
            cp.start()
        for cp in cps:
            cp.wait()

    return pl.pallas_call(
        body, name=name, in_specs=[ANY] * n, out_specs=[ANY] * n,
        out_shape=[SDS(g.shape[:1] + g.shape[2:], g.dtype) for g in gs],
        scratch_shapes=[_dma_sems(n), _dma_sems(n)])(*gs)


def half_add(g, a, core, *, name):
    n, _, r, c = g.shape
    tr = _row_tile(r, c)

    def body(c_ref, g_ref, a_ref, o_ref):
        o_ref[...] = (g_ref[...].astype(F32) + a_ref[...].astype(F32)).astype(o_ref.dtype)

    blk = pl.BlockSpec((None, tr, c), lambda s, i, c_ref: (s, i, 0))
    return pl.pallas_call(
        body, name=name, out_shape=SDS((n, r, c), g.dtype),
        grid_spec=pltpu.PrefetchScalarGridSpec(
            num_scalar_prefetch=1, grid=(n, r // tr),
            in_specs=[pl.BlockSpec((None, None, tr, c), lambda s, i, c_ref: (s, c_ref[0], i, 0)), blk], out_specs=blk),
        compiler_params=_cp("parallel", "parallel"))(core, g, a)


def chip_scatter(ps, *, name):
    n = len(ps)

    def body(*refs):
        p_refs, b_refs = refs[:n], refs[n:2 * n]
        send_sems, recv_sems = refs[2 * n:]
        x, y, c = _coords()
        cps = [_remote(p_refs[i].at[2 * cx + cy], b_refs[i].at[k], send_sems.at[3 * i + k], recv_sems.at[3 * i + k], (cx, cy, c))
               for i in range(n) for k, (cx, cy) in enumerate(_other_chips(x, y))]
        for cp in cps:
            cp.start()
        for cp in cps:
            cp.wait()

    return pl.pallas_call(
        body, name=name, in_specs=[ANY] * n, out_specs=[ANY] * n,
        out_shape=[SDS((N_CHIPS - 1,) + p.shape[1:], p.dtype) for p in ps],
        scratch_shapes=[_dma_sems(3 * n), _dma_sems(3 * n)])(*ps)


def chip_sum(p, b, chip, core, *, name):
    _, r, c = p.shape
    tr = _row_tile(r, c)

    def body(chip_ref, core_ref, p_ref, b_ref, o_ref):
        t = p_ref[...].astype(F32)
        for k in range(N_CHIPS - 1):
            t = t + b_ref[k].astype(F32)
        o_ref[...] = t

    return pl.pallas_call(
        body, name=name, out_shape=SDS((2, r, c), F32),
        grid_spec=pltpu.PrefetchScalarGridSpec(
            num_scalar_prefetch=2, grid=(r // tr,),
            in_specs=[pl.BlockSpec((None, tr, c), lambda i, chip_ref, core_ref: (chip_ref[0], i, 0)),
                      pl.BlockSpec((N_CHIPS - 1, tr, c), lambda i, chip_ref, core_ref: (0, i, 0))],
            out_specs=pl.BlockSpec((None, tr, c), lambda i, chip_ref, core_ref: (core_ref[0], i, 0))),
        compiler_params=_cp("parallel"))(chip, core, p, b)


def sibling_gather(ts, *, name):
    n = len(ts)

    def body(*refs):
        out_refs = refs[n:2 * n]
        send_sems, recv_sems = refs[2 * n:]
        x, y, c = _coords()
        cps = [_remote(out_refs[i].at[c], out_refs[i].at[c], send_sems.at[i], recv_sems.at[i], (x, y, 1 - c)) for i in range(n)]
        for cp in cps:
            cp.start()
        for i in range(n):
            _remote(out_refs[i].at[c], out_refs[i].at[1 - c], send_sems.at[i], recv_sems.at[i], (x, y, 1 - c)).wait_recv()
        for cp in cps:
            cp.wait_send()

    return pl.pallas_call(
        body, name=name, in_specs=[ANY] * n, out_specs=[ANY] * n, out_shape=[SDS(t.shape, t.dtype) for t in ts],
        input_output_aliases={i: i for i in range(n)}, scratch_shapes=[_dma_sems(n), _dma_sems(n)])(*ts)


def allgather_all(sm, *, name):
    r, w = sm.shape

    def body(s_ref, out_ref, send_sems, recv_sems, local_sem):
        x, y, c = _coords()
        me = 4 * x + 2 * y + c

        def peer(rel):
            flip = lambda v, bit: 1 - v if bit else v
            return flip(x, rel & 4), flip(y, rel & 2), flip(c, rel & 1)

        loc = pltpu.make_async_copy(s_ref, out_ref.at[me], local_sem.at[0])
        loc.start()
        cps = [_remote(s_ref, out_ref.at[me], send_sems.at[rel - 1], recv_sems.at[rel - 1], peer(rel)) for rel in range(1, N_DEV)]
        for cp in cps:
            cp.start()
        for rel in range(1, N_DEV):
            px, py, pc = peer(rel)
            _remote(s_ref, out_ref.at[4 * px + 2 * py + pc], send_sems.at[rel - 1], recv_sems.at[rel - 1], (px, py, pc)).wait_recv()
        for cp in cps:
            cp.wait_send()
        loc.wait()

    return pl.pallas_call(
        body, name=name, in_specs=[ANY], out_specs=ANY, out_shape=SDS((N_DEV, r, w), sm.dtype),
        scratch_shapes=[_dma_sems(N_DEV - 1), _dma_sems(N_DEV - 1), _dma_sems(1)])(sm)


def sum_blocks(a, *, name):
    n, r, w = a.shape

    def body(a_ref, o_ref):
        t = a_ref[0]
        for k in range(1, n):
            t = t + a_ref[k]
        o_ref[...] = t

    return pl.pallas_call(body, name=name, out_shape=SDS((r, w), F32))(a)


def _pack_rows(parts):
    cat = jnp.concatenate([t.reshape(-1) for t in parts])
    rows = -(-cat.size // (LANES * SUBLANES)) * SUBLANES
    return jnp.pad(cat, (0, rows * LANES - cat.size)).reshape(rows, LANES)


def _segments(flat, sizes):
    out, off = [], 0
    for n in sizes:
        out.append(flat[..., off:off + n])
        off += n
    return out


def _halves(a):
    return a.reshape(a.shape[:-2] + (2, a.shape[-2] // 2, a.shape[-1]))


def kernel(x, attn_norm_w, ffn_norm_w, gla_w_in, gla_w_gate_up, gla_b_gate, gla_gnorm_w, gla_w_out, kv_norm_w, sb_w_kv, sb_w_q, sb_w_out, ffn_w_gate_up, ffn_w_down, final_norm_w, loss_target, m_attn_norm_w, m_ffn_norm_w, m_gla_w_in, m_gla_w_gate_up, m_gla_b_gate, m_gla_gnorm_w, m_gla_w_out, m_kv_norm_w, m_sb_w_kv, m_sb_w_q, m_sb_w_out, m_ffn_w_gate_up, m_ffn_w_down, m_final_norm_w, v_attn_norm_w, v_ffn_norm_w, v_gla_w_in, v_gla_w_gate_up, v_gla_b_gate, v_gla_gnorm_w, v_gla_w_out, v_kv_norm_w, v_sb_w_kv, v_sb_w_q, v_sb_w_out, v_ffn_w_gate_up, v_ffn_w_down, v_final_norm_w):
    xi, yi, ci = _coords()
    core = ci.astype(jnp.int32).reshape(1)
    chip = (2 * xi + yi).astype(jnp.int32)
    _, s, d = x.shape
    rank = gla_w_gate_up.shape[1]

    local = [_halves(gla_w_in[0]), _halves(gla_w_out[0]), _halves(sb_w_kv), _halves(sb_w_q[0]), _halves(sb_w_out[0]),
             ffn_w_gate_up, ffn_w_down]
    tags = ["w_in", "w_out", "w_kv", "w_q", "w_so", "w_gu", "w_dn"]
    placed = [place_shard(a, chip.reshape(1), name=f"place_{n}") for a, n in zip(local, tags)]
    w_in, w_out, w_kv, w_q, w_so, w_gu, w_dn = allgather_chips(placed, name="gather_weights")
    in_w = N_CHIPS * w_in.shape[-1]
    w_in = w_in.reshape(N_CHIPS, d, -1).transpose(1, 0, 2).reshape(d, in_w)
    small = [gla_w_gate_up, gla_b_gate, gla_gnorm_w]
    small_all = allgather_all(_pack_rows(small), name="gather_gate_weights")
    wg, bg, gw = _segments(small_all[::2].reshape(N_CHIPS, -1), [a.size for a in small])
    wg = wg.reshape(N_CHIPS, rank, -1).transpose(1, 0, 2).reshape(rank, -1)
    p = dict(
        an0=attn_norm_w[0], an1=attn_norm_w[1], fn0=ffn_norm_w[0], fn1=ffn_norm_w[1], kvn=kv_norm_w, finn=final_norm_w,
        w_in=jnp.pad(w_in, ((0, 0), (0, -in_w % LANES))), wg=jnp.pad(wg, ((0, LANES - rank), (0, 0))).astype(BF16),
        bg=bg.reshape(1, -1), gw=gw.reshape(1, -1), w_out=w_out.reshape(-1, d), w_kv=w_kv.reshape(N_CHIPS, d, -1),
        w_q=w_q.reshape(d, -1), w_so=w_so.reshape(-1, d), w_gu=w_gu, w_dn=w_dn)

    loss_row, dx, g = local_step(x[0], loss_target[0], p)
    loss = lax.psum(loss_row[0, 0], ("x", "y", "c"))

    by_rows = lambda t: _halves(t.reshape(N_CHIPS, -1, t.shape[-1]))
    gs = [_halves(g["w_in"][:, :in_w].reshape(d, N_CHIPS, -1).transpose(1, 0, 2)), by_rows(g["w_out"]), _halves(g["w_kv"]),
          by_rows(g["w_q"]), by_rows(g["w_so"]), _halves(g["w_gu0"]), _halves(g["w_gu1"]), by_rows(g["w_dn0"]), by_rows(g["w_dn1"])]
    tags = ["w_in", "w_out", "w_kv", "w_q", "w_so", "w_gu0", "w_gu1", "w_dn0", "w_dn1"]
    from_sibling = sibling_exchange(gs, name="grads_to_sibling")
    pairs = [half_add(t, a, core, name=f"pair_sum_{n}") for t, a, n in zip(gs, from_sibling, tags)]
    from_chips = chip_scatter(pairs, name="grads_to_chips")
    mine = [chip_sum(t, b, chip.reshape(1), core, name=f"chip_sum_{n}") for t, b, n in zip(pairs, from_chips, tags)]
    tot = dict(zip(tags, sibling_gather(mine, name="grads_from_sibling")))

    vecs = [jnp.concatenate([g["an0"], g["an1"]]), jnp.concatenate([g["fn0"], g["fn1"]]), g["kvn"], g["finn"],
            g["wg"][:rank], g["bg"], g["gw"]]
    gathered_vecs = allgather_all(_pack_rows(vecs), name="gather_small_grads")
    d_an, d_fn, d_kvn, d_fin, d_wg, d_bg, d_gw = _segments(
        sum_blocks(gathered_vecs, name="sum_small_grads").reshape(-1), [t.size for t in vecs])

    def shard(t, like):
        return lax.dynamic_index_in_dim(t.reshape(-1, N_CHIPS, like.shape[-1]), chip, axis=1, keepdims=False).reshape(like.shape)

    grads = dict(
        attn_norm_w=d_an.reshape(attn_norm_w.shape), ffn_norm_w=d_fn.reshape(ffn_norm_w.shape),
        gla_w_in=tot["w_in"].reshape(gla_w_in.shape), gla_w_gate_up=shard(d_wg, gla_w_gate_up),
        gla_b_gate=shard(d_bg, gla_b_gate), gla_gnorm_w=shard(d_gw, gla_gnorm_w),
        gla_w_out=tot["w_out"].reshape(gla_w_out.shape), kv_norm_w=d_kvn.reshape(kv_norm_w.shape),
        sb_w_kv=tot["w_kv"].reshape(sb_w_kv.shape), sb_w_q=tot["w_q"].reshape(sb_w_q.shape),
        sb_w_out=tot["w_so"].reshape(sb_w_out.shape),
        ffn_w_gate_up=jnp.stack([tot["w_gu0"], tot["w_gu1"]]).reshape(ffn_w_gate_up.shape),
        ffn_w_down=jnp.stack([tot["w_dn0"], tot["w_dn1"]]).reshape(ffn_w_down.shape),
        final_norm_w=d_fin.reshape(final_norm_w.shape))
    weights = dict(
        attn_norm_w=(attn_norm_w, m_attn_norm_w, v_attn_norm_w), ffn_norm_w=(ffn_norm_w, m_ffn_norm_w, v_ffn_norm_w),
        gla_w_in=(gla_w_in, m_gla_w_in, v_gla_w_in), gla_w_gate_up=(gla_w_gate_up, m_gla_w_gate_up, v_gla_w_gate_up),
        gla_b_gate=(gla_b_gate, m_gla_b_gate, v_gla_b_gate), gla_gnorm_w=(gla_gnorm_w, m_gla_gnorm_w, v_gla_gnorm_w),
        gla_w_out=(gla_w_out, m_gla_w_out, v_gla_w_out), kv_norm_w=(kv_norm_w, m_kv_norm_w, v_kv_norm_w),
        sb_w_kv=(sb_w_kv, m_sb_w_kv, v_sb_w_kv), sb_w_q=(sb_w_q, m_sb_w_q, v_sb_w_q), sb_w_out=(sb_w_out, m_sb_w_out, v_sb_w_out),
        ffn_w_gate_up=(ffn_w_gate_up, m_ffn_w_gate_up, v_ffn_w_gate_up), ffn_w_down=(ffn_w_down, m_ffn_w_down, v_ffn_w_down),
        final_norm_w=(final_norm_w, m_final_norm_w, v_final_norm_w))
    names = list(weights)
    stepped = [adamw(weights[n][0], grads[n], weights[n][1], weights[n][2], name=f"adamw_{n}") for n in names]
    return (loss, dx.reshape(x.shape), *[grads[n] for n in names], *[t[0] for t in stepped], *[t[1] for t in stepped],
            *[t[2] for t in stepped])
```

```python
import functools

import jax
import jax.numpy as jnp
from jax import lax
from jax.experimental import pallas as pl
from jax.experimental.pallas import tpu as pltpu

F32 = jnp.float32
BF16 = jnp.bfloat16
SDS = jax.ShapeDtypeStruct
MESH = pl.DeviceIdType.MESH

EPS = 1e-6
GLA_HEADS = 4
GLA_GATE_RANK = 16
GLA_GATE_TAU = 16.0
GLA_CHUNK = 128
SB_HEADS = 16
SB_TQ = 1024
SB_TK = 128
SB_GROUP = 4
ADAM_LR = 0.001
ADAM_B1 = 0.9
ADAM_B2 = 0.999
ADAM_EPS = 1e-08
ADAM_WD = 0.01
ADAM_STEP = 10

LANES = 128
SUBLANES = 8
N_CHIPS = 4
N_DEV = 8
VMEM_LIMIT = 56 * 1024 * 1024


def _tile(dim, target, unit=LANES):
    if dim <= target:
        return dim
    t = (target // unit) * unit
    while t >= unit:
        if dim % t == 0:
            return t
        t -= unit
    raise ValueError(f"no tile for {dim}")


def _cp(*sem):
    return pltpu.CompilerParams(dimension_semantics=sem, vmem_limit_bytes=VMEM_LIMIT)


def _sigmoid(x):
    return 1.0 / (1.0 + jnp.exp(-x))


def _dot(a, b, ca=1, cb=0):
    return lax.dot_general(a, b, (((ca,), (cb,)), ((), ())), preferred_element_type=F32)


def _split_dot(tri, x):
    hi = x.astype(BF16)
    lo = (x - hi.astype(F32)).astype(BF16)
    return _dot(tri, hi) + _dot(tri, lo)


def _hilo(x):
    hi = x.astype(BF16)
    return hi, (x - hi.astype(F32)).astype(BF16)


def _dot3(a, b, ca=1, cb=0):
    return _dot(a[0], b[0], ca, cb) + _dot(a[0], b[1], ca, cb) + _dot(a[1], b[0], ca, cb)


def _tri(n, kind):
    r = lax.broadcasted_iota(jnp.int32, (n, n), 0)
    c = lax.broadcasted_iota(jnp.int32, (n, n), 1)
    m = {"le": c <= r, "ge": c >= r, "lt": c < r, "gt": c > r}[kind]
    return jnp.where(m, 1.0, 0.0).astype(BF16)


def _div(i, n):
    return i if n == 1 else lax.div(i, n)


def _rem(i, n):
    return 0 if n == 1 else lax.rem(i, n)


class View:
    def __init__(self, arr, kind="plain", lead=(), g0=0, ng=None):
        self.arr, self.kind, self.lead, self.g0 = arr, kind, tuple(lead), g0
        self.ng = (arr.shape[0] - g0) if ng is None else ng
        r, c = arr.shape[-2:]
        self.runit, self.cunit = r, c
        self.shape = {"plain": (r, c), "cols": (r, self.ng * c), "rows": (self.ng * r, c)}[kind]

    def spec(self, br, bc, rfn, cfn):
        if self.kind == "plain":
            return pl.BlockSpec((br, bc), lambda *g: (rfn(*g), cfn(*g)))
        none = (None,) * (1 + len(self.lead))
        if self.kind == "cols":
            per = self.cunit // bc
            return pl.BlockSpec(none + (br, bc), lambda *g: (self.g0 + _div(cfn(*g), per), *self.lead, rfn(*g), _rem(cfn(*g), per)))
        per = self.runit // br
        return pl.BlockSpec(none + (br, bc), lambda *g: (self.g0 + _div(rfn(*g), per), *self.lead, _rem(rfn(*g), per), cfn(*g)))


def _as_view(a):
    return a if isinstance(a, View) else View(a)


def matmul(a, b, *, name, ta=False, tb=False, out_dtype=F32, residual=None, out_chips=None, tm=1408, tn=1408, tk=2816):
    a, b = _as_view(a), _as_view(b)
    (k, m) = a.shape if ta else a.shape[::-1]
    (n, kb) = b.shape if tb else b.shape[::-1]
    assert k == kb, (a.shape, b.shape, ta, tb)
    m_unit = a.cunit if ta else a.runit
    ka_unit = a.runit if ta else a.cunit
    n_unit = b.runit if tb else b.cunit
    kb_unit = b.cunit if tb else b.runit
    if out_chips is not None:
        n_unit = min(n_unit, n // out_chips)
    tm, tn = _tile(min(m, m_unit), tm), _tile(min(n, n_unit), tn)
    tk = _tile(min(k, ka_unit, kb_unit), tk)
    assert ka_unit % tk == 0 and kb_unit % tk == 0, (ka_unit, kb_unit, tk)
    nk = k // tk
    ca, cb = (0 if ta else 1), (1 if tb else 0)

    def body(a_ref, b_ref, *refs):
        r_ref = refs[0] if residual is not None else None
        o_ref = refs[-1] if nk == 1 else refs[-2]

        def finish(r):
            if residual is not None:
                r = r + r_ref[...]
            o_ref[...] = r.astype(out_dtype)

        part = _dot(a_ref[...].astype(BF16), b_ref[...].astype(BF16), ca, cb)
        if nk == 1:
            finish(part)
            return
        acc = refs[-1]
        kk = pl.program_id(2)

        @pl.when(kk == 0)
        def _():
            acc[...] = part

        @pl.when(kk > 0)
        def _():
            acc[...] += part

        @pl.when(kk == nk - 1)
        def _():
            finish(acc[...])

    gi, gj, gk = (lambda i, j, kk: i), (lambda i, j, kk: j), (lambda i, j, kk: kk)
    a_spec = a.spec(tk, tm, gk, gi) if ta else a.spec(tm, tk, gi, gk)
    b_spec = b.spec(tn, tk, gj, gk) if tb else b.spec(tk, tn, gk, gj)
    if out_chips is None:
        out = View(SDS((m, n), out_dtype))
    else:
        out = View(SDS((out_chips, m, n // out_chips), out_dtype), "cols")
    o_spec = out.spec(tm, tn, gi, gj)
    in_specs, args = [a_spec, b_spec], [a.arr, b.arr]
    if residual is not None:
        in_specs.append(pl.BlockSpec((tm, tn), lambda i, j, kk: (i, j)))
        args.append(residual)
    return pl.pallas_call(
        body, name=name, grid=(m // tm, n // tn, nk), in_specs=in_specs, out_specs=o_spec,
        out_shape=out.arr, scratch_shapes=[] if nk == 1 else [pltpu.VMEM((tm, tn), F32)],
        compiler_params=_cp("parallel", "parallel", "arbitrary"))(*args)


def rmsnorm_fwd(x, w, *, name, tr=256):
    s, d = x.shape

    def body(x_ref, w_ref, o_ref):
        xf = x_ref[...]
        r = lax.rsqrt(jnp.mean(xf * xf, axis=-1, keepdims=True) + EPS)
        o_ref[...] = (xf * r * w_ref[...]).astype(BF16)

    row = pl.BlockSpec((tr, d), lambda i: (i, 0))
    return pl.pallas_call(
        body, name=name, grid=(s // tr,), in_specs=[row, pl.BlockSpec((1, d), lambda i: (0, 0))], out_specs=row,
        out_shape=SDS((s, d), BF16), compiler_params=_cp("parallel"))(x, w.reshape(1, d))


def rmsnorm_bwd(dy, x, w, dres, *, name, tr=256):
    s, d = x.shape

    def body(dy_ref, x_ref, w_ref, dres_ref, dx_ref, dw_ref):
        i = pl.program_id(0)
        xf = x_ref[...]
        r = lax.rsqrt(jnp.mean(xf * xf, axis=-1, keepdims=True) + EPS)
        xh = xf * r
        dyf = dy_ref[...].astype(F32)
        dxh = dyf * w_ref[...]
        dx_ref[...] = dres_ref[...] + r * (dxh - xh * jnp.mean(dxh * xh, axis=-1, keepdims=True))
        part = jnp.sum(dyf * xh, axis=0, keepdims=True)

        @pl.when(i == 0)
        def _():
            dw_ref[...] = part

        @pl.when(i > 0)
        def _():
            dw_ref[...] += part

    row = pl.BlockSpec((tr, d), lambda i: (i, 0))
    vec = pl.BlockSpec((1, d), lambda i: (0, 0))
    return pl.pallas_call(
        body, name=name, grid=(s // tr,), in_specs=[row, row, vec, row], out_specs=[row, vec],
        out_shape=[SDS((s, d), F32), SDS((1, d), F32)], compiler_params=_cp("arbitrary"))(dy, x, w.reshape(1, d), dres)


def final_loss(h, w, target, *, name, tr=256):
    s, d = h.shape

    def body(h_ref, w_ref, t_ref, dh_ref, dw_ref, loss_ref):
        i = pl.program_id(0)
        xf = h_ref[...]
        r = lax.rsqrt(jnp.mean(xf * xf, axis=-1, keepdims=True) + EPS)
        xh = xf * r
        err = xh * w_ref[...] - t_ref[...]
        lpart = 0.5 * jnp.sum(jnp.sum(err * err, axis=-1, keepdims=True) * (1.0 / d), axis=0, keepdims=True)
        dy = err * (1.0 / d)
        dxh = dy * w_ref[...]
        dh_ref[...] = r * (dxh - xh * jnp.mean(dxh * xh, axis=-1, keepdims=True))
        part = jnp.sum(dy * xh, axis=0, keepdims=True)
        lrow = jnp.broadcast_to(lpart, (1, LANES))

        @pl.when(i == 0)
        def _():
            dw_ref[...] = part
            loss_ref[...] = lrow

        @pl.when(i > 0)
        def _():
            dw_ref[...] += part
            loss_ref[...] += lrow

    row = pl.BlockSpec((tr, d), lambda i: (i, 0))
    vec = pl.BlockSpec((1, d), lambda i: (0, 0))
    return pl.pallas_call(
        body, name=name, grid=(s // tr,), in_specs=[row, vec, row],
        out_specs=[row, vec, pl.BlockSpec((1, LANES), lambda i: (0, 0))],
        out_shape=[SDS((s, d), F32), SDS((1, d), F32), SDS((1, LANES), F32)],
        compiler_params=_cp("arbitrary"))(h, w.reshape(1, d), target)


def swiglu_fwd(gu, *, name, tr=512, tc=1408):
    s, f2 = gu.shape
    f = f2 // 2
    tc = _tile(f, tc)
    nf = f // tc

    def body(g_ref, u_ref, o_ref):
        g = g_ref[...].astype(F32)
        o_ref[...] = (g * _sigmoid(g) * u_ref[...].astype(F32)).astype(BF16)

    return pl.pallas_call(
        body, name=name, grid=(s // tr, nf),
        in_specs=[pl.BlockSpec((tr, tc), lambda i, j: (i, j)), pl.BlockSpec((tr, tc), lambda i, j: (i, j + nf))],
        out_specs=pl.BlockSpec((tr, tc), lambda i, j: (i, j)), out_shape=SDS((s, f), BF16),
        compiler_params=_cp("parallel", "parallel"))(gu, gu)


def swiglu_bwd(gu, dact, *, name, tr=512, tc=1408):
    s, f2 = gu.shape
    f = f2 // 2
    tc = _tile(f, tc)
    nf = f // tc

    def body(g_ref, u_ref, d_ref, o_ref):
        j = pl.program_id(1)
        g = g_ref[...].astype(F32)
        d = d_ref[...].astype(F32)
        sg = _sigmoid(g)

        @pl.when(j < nf)
        def _():
            o_ref[...] = (d * u_ref[...].astype(F32) * sg * (1.0 + g * (1.0 - sg))).astype(BF16)

        @pl.when(j >= nf)
        def _():
            o_ref[...] = (d * g * sg).astype(BF16)

    return pl.pallas_call(
        body, name=name, grid=(s // tr, 2 * nf),
        in_specs=[pl.BlockSpec((tr, tc), lambda i, j: (i, j % nf)), pl.BlockSpec((tr, tc), lambda i, j: (i, nf + j % nf)),
                  pl.BlockSpec((tr, tc), lambda i, j: (i, j % nf))],
        out_specs=pl.BlockSpec((tr, tc), lambda i, j: (i, j)), out_shape=SDS((s, f2), BF16),
        compiler_params=_cp("parallel", "parallel"))(gu, gu, dact)


def _gate_z(gl_ref, w_ref, b_ref):
    glb = gl_ref[...].astype(BF16)
    return glb, _dot(glb, w_ref[...]) + b_ref[...]


def gate_fwd(proj, wg, bg, *, name, tr=512):
    s, inw = proj.shape
    qk = wg.shape[1]
    glc = inw // LANES - 1

    def body(gl_ref, w_ref, b_ref, g_ref):
        _, z = _gate_z(gl_ref, w_ref, b_ref)
        g_ref[...] = (jnp.minimum(z, 0.0) - jnp.log(1.0 + jnp.exp(-jnp.abs(z)))) * (1.0 / GLA_GATE_TAU)

    return pl.pallas_call(
        body, name=name, grid=(s // tr,),
        in_specs=[pl.BlockSpec((tr, LANES), lambda i: (i, glc)), pl.BlockSpec((LANES, qk), lambda i: (0, 0)),
                  pl.BlockSpec((1, qk), lambda i: (0, 0))],
        out_specs=pl.BlockSpec((tr, qk), lambda i: (i, 0)), out_shape=SDS((s, qk), F32),
        compiler_params=_cp("parallel"))(proj, wg, bg)


def gate_bwd(dg, proj, wg, bg, *, name, tr=512):
    s, inw = proj.shape
    qk = wg.shape[1]
    glc = inw // LANES - 1

    def body(dg_ref, gl_ref, w_ref, b_ref, dgl_ref, dw_ref, db_ref):
        i = pl.program_id(0)
        glb, z = _gate_z(gl_ref, w_ref, b_ref)
        dz = dg_ref[...] * (1.0 / (1.0 + jnp.exp(z))) * (1.0 / GLA_GATE_TAU)
        dzb = dz.astype(BF16)
        dgl_ref[...] = _dot(dzb, w_ref[...], 1, 1).astype(BF16)
        pw = _dot(glb, dzb, 0, 0)
        pb = jnp.sum(dz, axis=0, keepdims=True)

        @pl.when(i == 0)
        def _():
            dw_ref[...] = pw
            db_ref[...] = pb

        @pl.when(i > 0)
        def _():
            dw_ref[...] += pw
            db_ref[...] += pb

    return pl.pallas_call(
        body, name=name, grid=(s // tr,),
        in_specs=[pl.BlockSpec((tr, qk), lambda i: (i, 0)), pl.BlockSpec((tr, LANES), lambda i: (i, glc)),
                  pl.BlockSpec((LANES, qk), lambda i: (0, 0)), pl.BlockSpec((1, qk), lambda i: (0, 0))],
        out_specs=[pl.BlockSpec((tr, LANES), lambda i: (i, 0)), pl.BlockSpec((LANES, qk), lambda i: (0, 0)),
                   pl.BlockSpec((1, qk), lambda i: (0, 0))],
        out_shape=[SDS((s, LANES), BF16), SDS((LANES, qk), F32), SDS((1, qk), F32)],
        compiler_params=_cp("arbitrary"))(dg, proj, wg, bg)


def _gla_chunk_terms(q_ref, k_ref, g_ref, c, scale):
    q = q_ref[...] * scale
    k = k_ref[...]
    gg = g_ref[...]
    b = _split_dot(_tri(c, "le"), gg)
    row = lax.broadcasted_iota(jnp.int32, gg.shape, 0)
    bm = jnp.sum(jnp.where(row < c // 2, gg, 0.0), axis=0, keepdims=True)
    bl = jnp.sum(gg, axis=0, keepdims=True)
    eb, em, emi, el = jnp.exp(b), jnp.exp(b - bm), jnp.exp(bm - b), jnp.exp(bl - b)
    return q, k, bl, eb, em, emi, el


def _causal(a):
    r = lax.broadcasted_iota(jnp.int32, a.shape, 0)
    c = lax.broadcasted_iota(jnp.int32, a.shape, 1)
    return jnp.where(r >= c, a, 0.0)


def gla_fwd(proj, g, *, name, c=GLA_CHUNK):
    s = proj.shape[0]
    qk = g.shape[1]
    dk, dv = qk // GLA_HEADS, 2 * qk // GLA_HEADS
    nc = s // c
    scale = dk ** -0.5
    kq = qk // dk

    def body(q_ref, k_ref, v_ref, g_ref, o_ref, st_ref, state):
        @pl.when(pl.program_id(1) == 0)
        def _():
            state[...] = jnp.zeros_like(state)

        q, k, bl, eb, em, emi, el = _gla_chunk_terms(q_ref, k_ref, g_ref, c, scale)
        v2 = _hilo(v_ref[...])
        st = state[...]
        st_ref[...] = st
        a = _causal(_dot3(_hilo(q * em), _hilo(k * emi), 1, 1))
        o_ref[...] = _dot3(_hilo(q * eb), _hilo(st), 1, 1) + _dot3(_hilo(a), v2)
        state[...] = st * jnp.exp(bl) + _dot3(v2, _hilo(k * el), 0, 0)

    return pl.pallas_call(
        body, name=name, grid=(GLA_HEADS, nc),
        in_specs=[pl.BlockSpec((c, dk), lambda h, i: (i, h)), pl.BlockSpec((c, dk), lambda h, i: (i, kq + h)),
                  pl.BlockSpec((c, dv), lambda h, i: (i, kq + h)), pl.BlockSpec((c, dk), lambda h, i: (i, h))],
        out_specs=[pl.BlockSpec((c, dv), lambda h, i: (i, h)),
                   pl.BlockSpec((None, None, dv, dk), lambda h, i: (h, i, 0, 0))],
        out_shape=[SDS((s, 2 * qk), F32), SDS((GLA_HEADS, nc, dv, dk), F32)],
        scratch_shapes=[pltpu.VMEM((dv, dk), F32)],
        compiler_params=_cp("parallel", "arbitrary"))(proj, proj, proj, g)


def gla_bwd(proj, g, states, do, *, name, c=GLA_CHUNK):
    s = proj.shape[0]
    qk = g.shape[1]
    dk, dv = qk // GLA_HEADS, 2 * qk // GLA_HEADS
    nc = s // c
    scale = dk ** -0.5
    kq = qk // dk

    def body(q_ref, k_ref, v_ref, g_ref, do_ref, st_ref, dq_ref, dk_ref, dv_ref, dg_ref, dstate, dgc):
        @pl.when(pl.program_id(1) == 0)
        def _():
            dstate[...] = jnp.zeros_like(dstate)
            dgc[...] = jnp.zeros_like(dgc)

        q, k, bl, eb, em, emi, el = _gla_chunk_terms(q_ref, k_ref, g_ref, c, scale)
        v2, do2 = _hilo(v_ref[...]), _hilo(do_ref[...])
        qe, qm, km, kd = _hilo(q * eb), _hilo(q * em), _hilo(k * emi), _hilo(k * el)
        ds = dstate[...]
        ds2 = _hilo(ds)
        a = _hilo(_causal(_dot3(qm, km, 1, 1)))
        dv_ref[...] = (_dot3(a, do2, 0, 0) + _dot3(kd, ds2, 1, 1)).astype(BF16)
        da = _hilo(_causal(_dot3(do2, v2, 1, 1)))
        dq = _dot3(da, km) * em + _dot3(do2, _hilo(st_ref[...])) * eb
        dkk = _dot3(da, qm, 0, 0) * emi + _dot3(v2, ds2) * el
        dstate[...] = ds * jnp.exp(bl) + _dot3(do2, qe, 0, 0)
        db = q * dq - k * dkk
        dg_ref[...] = _split_dot(_tri(c, "ge"), db) + dgc[...]
        dgc[...] += jnp.sum(db, axis=0, keepdims=True)
        dq_ref[...] = (dq * scale).astype(BF16)
        dk_ref[...] = dkk.astype(BF16)

    rev = lambda i: nc - 1 - i
    qspec = pl.BlockSpec((c, dk), lambda h, i: (rev(i), h))
    vspec = pl.BlockSpec((c, dv), lambda h, i: (rev(i), h))
    return pl.pallas_call(
        body, name=name, grid=(GLA_HEADS, nc),
        in_specs=[qspec, pl.BlockSpec((c, dk), lambda h, i: (rev(i), kq + h)),
                  pl.BlockSpec((c, dv), lambda h, i: (rev(i), kq + h)), qspec, vspec,
                  pl.BlockSpec((None, None, dv, dk), lambda h, i: (h, rev(i), 0, 0))],
        out_specs=[qspec, qspec, vspec, qspec],
        out_shape=[SDS((s, qk), BF16), SDS((s, qk), BF16), SDS((s, 2 * qk), BF16), SDS((s, qk), F32)],
        scratch_shapes=[pltpu.VMEM((dv, dk), F32), pltpu.VMEM((1, dk), F32)],
        compiler_params=_cp("parallel", "arbitrary"))(proj, proj, proj, g, do, states)


def gnorm_fwd(o, proj, gw, *, name, tr=512):
    s, v = o.shape
    dv = v // GLA_HEADS
    roff = 2 * GLA_HEADS

    def body(o_ref, r_ref, w_ref, y_ref):
        of = o_ref[...]
        rs = lax.rsqrt(jnp.mean(of * of, axis=-1, keepdims=True) + EPS)
        r = r_ref[...]
        y_ref[...] = (of * rs * w_ref[...] * (r * _sigmoid(r))).astype(BF16)

    blk = pl.BlockSpec((tr, dv), lambda i, h: (i, h))
    return pl.pallas_call(
        body, name=name, grid=(s // tr, GLA_HEADS),
        in_specs=[blk, pl.BlockSpec((tr, dv), lambda i, h: (i, roff + h)), pl.BlockSpec((1, dv), lambda i, h: (0, 0))],
        out_specs=blk, out_shape=SDS((s, v), BF16), compiler_params=_cp("parallel", "parallel"))(o, proj, gw)


def gnorm_bwd(dy, o, proj, gw, *, name, tr=512):
    s, v = o.shape
    dv = v // GLA_HEADS
    roff = 2 * GLA_HEADS

    def body(dy_ref, o_ref, r_ref, w_ref, do_ref, dr_ref, dw_ref):
        first = jnp.logical_and(pl.program_id(0) == 0, pl.program_id(1) == 0)
        of = o_ref[...]
        rs = lax.rsqrt(jnp.mean(of * of, axis=-1, keepdims=True) + EPS)
        n = of * rs
        r = r_ref[...]
        sg = _sigmoid(r)
        dyf = dy_ref[...].astype(F32)
        dn_w = dyf * (r * sg)
        dr_ref[...] = (dyf * n * w_ref[...] * sg * (1.0 + r * (1.0 - sg))).astype(BF16)
        dn = dn_w * w_ref[...]
        do_ref[...] = rs * (dn - n * jnp.mean(dn * n, axis=-1, keepdims=True))
        part = jnp.sum(dn_w * n, axis=0, keepdims=True)

        @pl.when(first)
        def _():
            dw_ref[...] = part

        @pl.when(jnp.logical_not(first))
        def _():
            dw_ref[...] += part

    blk = pl.BlockSpec((tr, dv), lambda i, h: (i, h))
    vec = pl.BlockSpec((1, dv), lambda i, h: (0, 0))
    return pl.pallas_call(
        body, name=name, grid=(s // tr, GLA_HEADS),
        in_specs=[blk, blk, pl.BlockSpec((tr, dv), lambda i, h: (i, roff + h)), vec],
        out_specs=[blk, blk, vec], out_shape=[SDS((s, v), F32), SDS((s, v), BF16), SDS((1, dv), F32)],
        compiler_params=_cp("arbitrary", "arbitrary"))(dy, o, proj, gw)


def _sb_block(kblk, q, ks, q0, scale, carry, masked):
    tk, tq = kblk.shape[0], q.shape[0]
    z = _dot(kblk, q, 1, 1) * scale
    sp = jnp.maximum(z, 0.0) + jnp.log(1.0 + jnp.exp(-jnp.abs(z)))
    mask = None
    lf = -sp
    if masked:
        kpos = ks + lax.broadcasted_iota(jnp.int32, (tk, tq), 0)
        qpos = q0 + lax.broadcasted_iota(jnp.int32, (tk, tq), 1)
        mask = kpos < qpos
        lf = jnp.where(mask, lf, 0.0)
    later = _split_dot(_tri(tk, "gt"), lf)
    a = jnp.exp(z - sp + later + carry)
    if masked:
        a = jnp.where(mask, a, 0.0)
    return z, sp, mask, lf, a


def sb_fwd(q, kv, *, name, tq=SB_TQ, tk=SB_TK, group=SB_GROUP):
    s, w = q.shape
    hd = w // SB_HEADS
    nq, nkb = s // tq, s // tk
    scale = hd ** -0.5
    per = tq // tk
    assert per % group == 0

    def body(q_ref, k_ref, v_ref, o_ref, car_ref, o_acc):
        qi = pl.program_id(1)
        qb = q_ref[...]
        q0 = qi * tq
        car_ref[...] = jnp.zeros_like(car_ref)
        o_acc[...] = jnp.zeros_like(o_acc)

        def blocks(first, carry, masked):
            for t in reversed(range(group)):
                kj = first + t
                ks = pl.multiple_of(kj * tk, tk)
                car_ref[pl.ds(kj, 1), :] = carry
                _, _, _, lf, a = _sb_block(k_ref[pl.ds(ks, tk), :], qb, ks, q0, scale, carry, masked)
                o_acc[...] += _dot(a.astype(BF16), v_ref[pl.ds(ks, tk), :], 0, 0)
                carry = carry + jnp.sum(lf, axis=0, keepdims=True)
            return carry

        carry = jnp.zeros((1, tq), F32)
        for gidx in reversed(range(per // group)):
            carry = blocks(qi * per + gidx * group, carry, True)
        n_in = qi * (per // group)
        lax.fori_loop(0, n_in, lambda i, c: blocks((n_in - 1 - i) * group, c, False), carry)
        o_ref[...] = o_acc[...].astype(BF16)

    qspec = pl.BlockSpec((tq, hd), lambda h, i: (i, h))
    return pl.pallas_call(
        body, name=name, grid=(SB_HEADS, nq),
        in_specs=[qspec, pl.BlockSpec((None, s, hd), lambda h, i: (0, 0, h)), pl.BlockSpec((None, s, hd), lambda h, i: (1, 0, h))],
        out_specs=[qspec, pl.BlockSpec((None, None, nkb, tq), lambda h, i: (h, i, 0, 0))],
        out_shape=[SDS((s, w), BF16), SDS((SB_HEADS, nq, nkb, tq), F32)],
        scratch_shapes=[pltpu.VMEM((tq, hd), F32)],
        compiler_params=_cp("parallel", "parallel"))(q, kv, kv)


def sb_bwd(q, kv, do, car, *, name, tq=SB_TQ, tk=SB_TK, group=SB_GROUP):
    s, w = q.shape
    hd = w // SB_HEADS
    nq, nkb = s // tq, s // tk
    scale = hd ** -0.5
    per = tq // tk
    assert per % group == 0

    def body(q_ref, k_ref, v_ref, do_ref, car_ref, dq_ref, dkv_ref, dq_acc, dk_acc, dv_acc):
        qi = pl.program_id(1)
        qb = q_ref[...]
        dob = do_ref[...]
        q0 = qi * tq
        dq_acc[...] = jnp.zeros_like(dq_acc)

        @pl.when(qi == 0)
        def _():
            dk_acc[...] = jnp.zeros_like(dk_acc)
            dv_acc[...] = jnp.zeros_like(dv_acc)

        def blocks(first, pcar, masked):
            for t in range(group):
                kj = first + t
                ks = pl.multiple_of(kj * tk, tk)
                kblk = k_ref[pl.ds(ks, tk), :]
                z, sp, mask, _, a = _sb_block(kblk, qb, ks, q0, scale, car_ref[pl.ds(kj, 1), :], masked)
                p = a * _dot(v_ref[pl.ds(ks, tk), :], dob, 1, 1)
                before = _split_dot(_tri(tk, "lt"), p)
                sg = jnp.exp(z - sp)
                dz = p * (1.0 - sg) - (pcar + before) * sg
                if masked:
                    dz = jnp.where(mask, dz, 0.0)
                dz = (dz * scale).astype(BF16)
                dk_acc[pl.ds(ks, tk), :] += _dot(dz, qb)
                dv_acc[pl.ds(ks, tk), :] += _dot(a.astype(BF16), dob)
                dq_acc[...] += _dot(dz, kblk, 0, 0)
                pcar = pcar + jnp.sum(p, axis=0, keepdims=True)
            return pcar

        pcar = lax.fori_loop(0, qi * (per // group), lambda i, c: blocks(i * group, c, False), jnp.zeros((1, tq), F32))
        for gidx in range(per // group):
            pcar = blocks(qi * per + gidx * group, pcar, True)
        dq_ref[...] = dq_acc[...].astype(BF16)

        @pl.when(qi == nq - 1)
        def _():
            dkv_ref[0] = dk_acc[...].astype(BF16)
            dkv_ref[1] = dv_acc[...].astype(BF16)

    qspec = pl.BlockSpec((tq, hd), lambda h, i: (i, h))
    return pl.pallas_call(
        body, name=name, grid=(SB_HEADS, nq),
        in_specs=[qspec, pl.BlockSpec((None, s, hd), lambda h, i: (0, 0, h)), pl.BlockSpec((None, s, hd), lambda h, i: (1, 0, h)),
                  qspec, pl.BlockSpec((None, None, nkb, tq), lambda h, i: (h, i, 0, 0))],
        out_specs=[qspec, pl.BlockSpec((2, s, hd), lambda h, i: (0, 0, h))],
        out_shape=[SDS((s, w), BF16), SDS((2, s, w), BF16)],
        scratch_shapes=[pltpu.VMEM((tq, hd), F32), pltpu.VMEM((s, hd), F32), pltpu.VMEM((s, hd), F32)],
        compiler_params=_cp("parallel", "arbitrary"))(q, kv, kv, do, car)


def adamw(w, g, m, v, *, name):
    shape = w.shape
    c = shape[-1]
    r = w.size // c
    tr = _tile(r, max(8, (3 * LANES * 1024) // c), unit=8) if r >= 8 else r

    def body(w_ref, g_ref, m_ref, v_ref, d_ref, nm_ref, nv_ref):
        gf = g_ref[...]
        mn = ADAM_B1 * m_ref[...] + (1.0 - ADAM_B1) * gf
        vn = ADAM_B2 * v_ref[...] + (1.0 - ADAM_B2) * (gf * gf)
        m_hat = mn / (1.0 - ADAM_B1 ** ADAM_STEP)
        v_hat = vn / (1.0 - ADAM_B2 ** ADAM_STEP)
        d_ref[...] = -ADAM_LR * (m_hat / (jnp.sqrt(v_hat) + ADAM_EPS) + ADAM_WD * w_ref[...])
        nm_ref[...] = mn
        nv_ref[...] = vn

    blk = pl.BlockSpec((tr, c), lambda i: (i, 0))
    outs = pl.pallas_call(
        body, name=name, grid=(r // tr,), in_specs=[blk] * 4, out_specs=[blk] * 3,
        out_shape=[SDS((r, c), F32)] * 3, compiler_params=_cp("parallel"))(
            *(t.reshape(r, c) for t in (w, g, m, v)))
    return tuple(o.reshape(shape) for o in outs)


def _ffn_fwd(h, nw, w_gu, w_dn, tag):
    f = rmsnorm_fwd(h, nw, name=f"{tag}_norm")
    gu = matmul(f, w_gu, name=f"{tag}_gate_up", out_dtype=BF16)
    act = swiglu_fwd(gu, name=f"{tag}_act")
    return matmul(act, w_dn, name=f"{tag}_down", residual=h), (f, gu, act)


def _ffn_bwd(dh, h, nw, w_gu, w_dn, saved, tag):
    f, gu, act = saved
    dact = matmul(dh, w_dn, name=f"{tag}_dact", tb=True, out_dtype=BF16)
    dw_dn = matmul(act, dh, name=f"{tag}_dw_down", ta=True, out_dtype=BF16)
    dgu = swiglu_bwd(gu, dact, name=f"{tag}_dgu")
    dw_gu = matmul(f, dgu, name=f"{tag}_dw_gate_up", ta=True, out_dtype=BF16, out_chips=N_CHIPS)
    df = matmul(dgu, w_gu, name=f"{tag}_df", tb=True)
    dh_in, dnw = rmsnorm_bwd(df, h, nw, dh, name=f"{tag}_dnorm")
    return dh_in, dnw, dw_gu, dw_dn


def local_step(x, target, p):
    gu = [View(p["w_gu"], "cols", lead=(l,)) for l in range(2)]
    dn = [View(p["w_dn"], "rows", lead=(l,)) for l in range(2)]
    w_kv = View(p["w_kv"], "cols")
    a0 = rmsnorm_fwd(x, p["an0"], name="l0_attn_norm")
    proj = matmul(a0, p["w_in"], name="gla_in")
    g = gate_fwd(proj, p["wg"], p["bg"], name="gla_gate")
    o, states = gla_fwd(proj, g, name="gla_scan")
    og = gnorm_fwd(o, proj, p["gw"], name="gla_outnorm")
    h1 = matmul(og, p["w_out"], name="gla_out", residual=x)
    h2, ffn0 = _ffn_fwd(h1, p["fn0"], gu[0], dn[0], "ffn0")
    kvn = rmsnorm_fwd(h2, p["kvn"], name="kv_norm")
    kv = matmul(kvn, w_kv, name="sb_kv", out_dtype=BF16, out_chips=2)
    a1 = rmsnorm_fwd(h2, p["an1"], name="l1_attn_norm")
    q2 = matmul(a1, p["w_q"], name="sb_q", out_dtype=BF16)
    o2, car = sb_fwd(q2, kv, name="sb_attn")
    h3 = matmul(o2, p["w_so"], name="sb_out", residual=h2)
    h4, ffn1 = _ffn_fwd(h3, p["fn1"], gu[1], dn[1], "ffn1")
    dh4, d_fin, loss_row = final_loss(h4, p["finn"], target, name="final_loss")

    dh3, d_fn1, dw_gu1, dw_dn1 = _ffn_bwd(dh4, h3, p["fn1"], gu[1], dn[1], ffn1, "ffn1")
    do2 = matmul(dh3, p["w_so"], name="sb_do", tb=True, out_dtype=BF16)
    dw_so = matmul(o2, dh3, name="sb_dw_out", ta=True, out_dtype=BF16)
    dq2, dkv = sb_bwd(q2, kv, do2, car, name="sb_attn_bwd")
    dkv = View(dkv, "cols")
    dw_q = matmul(a1, dq2, name="sb_dw_q", ta=True, out_dtype=BF16)
    da1 = matmul(dq2, p["w_q"], name="sb_da", tb=True)
    dh2, d_an1 = rmsnorm_bwd(da1, h2, p["an1"], dh3, name="l1_attn_dnorm")
    dw_kv = matmul(kvn, dkv, name="sb_dw_kv", ta=True, out_dtype=BF16, out_chips=N_CHIPS)
    dkvn = matmul(dkv, w_kv, name="sb_dkvn", tb=True)
    dh2, d_kvn = rmsnorm_bwd(dkvn, h2, p["kvn"], dh2, name="kv_dnorm")
    dh1, d_fn0, dw_gu0, dw_dn0 = _ffn_bwd(dh2, h1, p["fn0"], gu[0], dn[0], ffn0, "ffn0")
    dog = matmul(dh1, p["w_out"], name="gla_dog", tb=True, out_dtype=BF16)
    dw_out = matmul(og, dh1, name="gla_dw_out", ta=True, out_dtype=BF16)
    do, dr, d_gw = gnorm_bwd(dog, o, proj, p["gw"], name="gla_outnorm_bwd")
    dq, dk, dv, dg = gla_bwd(proj, g, states, do, name="gla_scan_bwd")
    dgl, d_wg, d_bg = gate_bwd(dg, proj, p["wg"], p["bg"], name="gla_gate_bwd")
    dproj = jnp.concatenate([dq, dk, dv, dr, dgl], axis=1)
    dw_in = matmul(a0, dproj, name="gla_dw_in", ta=True, out_dtype=BF16)
    da0 = matmul(dproj, p["w_in"], name="gla_da", tb=True)
    dx, d_an0 = rmsnorm_bwd(da0, x, p["an0"], dh1, name="l0_attn_dnorm")

    grads = dict(an0=d_an0, an1=d_an1, fn0=d_fn0, fn1=d_fn1, kvn=d_kvn, finn=d_fin, wg=d_wg, bg=d_bg, gw=d_gw,
                 w_in=dw_in, w_out=dw_out, w_kv=dw_kv, w_q=dw_q, w_so=dw_so,
                 w_gu0=dw_gu0, w_gu1=dw_gu1, w_dn0=dw_dn0, w_dn1=dw_dn1)
    return loss_row, dx, grads


ANY = pl.BlockSpec(memory_space=pl.ANY)


def _coords():
    return lax.axis_index("x"), lax.axis_index("y"), lax.axis_index("c")


def _other_chips(x, y):
    return [(1 - x, y), (x, 1 - y), (1 - x, 1 - y)]


def _remote(src, dst, send_sem, recv_sem, dev):
    return pltpu.make_async_remote_copy(src_ref=src, dst_ref=dst, send_sem=send_sem, recv_sem=recv_sem,
                                        device_id=dev, device_id_type=MESH)


def _dma_sems(n):
    return pltpu.SemaphoreType.DMA((n,))


def _row_tile(r, c):
    return _tile(r, max(16, (4 * LANES * 1024) // c), unit=16)


def place_shard(w, chip, *, name):
    _, r, c = w.shape
    tr = _row_tile(r, c)

    def body(c_ref, w_ref, o_ref):
        o_ref[...] = w_ref[...].astype(BF16)

    return pl.pallas_call(
        body, name=name, out_shape=SDS((N_CHIPS,) + w.shape, BF16),
        grid_spec=pltpu.PrefetchScalarGridSpec(
            num_scalar_prefetch=1, grid=(2, r // tr),
            in_specs=[pl.BlockSpec((None, tr, c), lambda h, i, c_ref: (h, i, 0))],
            out_specs=pl.BlockSpec((None, None, tr, c), lambda h, i, c_ref: (c_ref[0], h, i, 0))),
        compiler_params=_cp("parallel", "parallel"))(chip, w)


def allgather_chips(ws, *, name):
    n = len(ws)

    def body(*refs):
        out_refs = refs[n:2 * n]
        send_sems, recv_sems = refs[2 * n:]
        x, y, c = _coords()
        me = 2 * x + y
        chips = _other_chips(x, y)

        def copy(i, k, chip, half, dev):
            blk = out_refs[i].at[chip, half]
            return _remote(blk, blk, send_sems.at[6 * i + k], recv_sems.at[6 * i + k], dev)

        first = [copy(i, k, me, c, (cx, cy, c)) for i in range(n) for k, (cx, cy) in enumerate(chips)]
        for cp in first:
            cp.start()
        passed = []
        for i in range(n):
            for k, (cx, cy) in enumerate(chips):
                copy(i, k, 2 * cx + cy, c, (x, y, c)).wait_recv()
                passed.append(copy(i, 3 + k, 2 * cx + cy, c, (x, y, 1 - c)))
                passed[-1].start()
        for i in range(n):
            for k, (cx, cy) in enumerate(chips):
                copy(i, 3 + k, 2 * cx + cy, 1 - c, (x, y, c)).wait_recv()
        for cp in first + passed:
            cp.wait_send()

    return pl.pallas_call(
        body, name=name, in_specs=[ANY] * n, out_specs=[ANY] * n, out_shape=[SDS(w.shape, w.dtype) for w in ws],
        input_output_aliases={i: i for i in range(n)}, scratch_shapes=[_dma_sems(6 * n), _dma_sems(6 * n)])(*ws)


def sibling_exchange(gs, *, name):
    n = len(gs)

    def body(*refs):
        g_refs, a_refs = refs[:n], refs[n:2 * n]
        send_sems, recv_sems = refs[2 * n:]
        x, y, c = _coords()
        cps = [_remote(g_refs[i].at[:, 1 - c], a_refs[i], send_sems.at[i], recv_sems.at[i], (x, y, 1 - c)) for i in range(n)]
        for cp in cps:
            cp.start()
        for cp in cps:
            cp.wait()

    return pl.pallas_call(
        body, name=name, in_specs=[ANY] * n, out_specs=[ANY] * n,
        out_shape=[SDS(g.shape[:1] + g.shape[2:], g.dtype) for g in gs],
        scratch_shapes=[_dma_sems(n), _dma_sems(n)])(*gs)


def half_add(g, a, core, *, name):
    n, _, r, c = g.shape
    tr = _row_tile(r, c)

    def body(c_ref, g_ref, a_ref, o_ref):
        o_ref[...] = (g_ref[...].astype(F32) + a_ref[...].astype(F32)).astype(o_ref.dtype)

    blk = pl.BlockSpec((None, tr, c), lambda s, i, c_ref: (s, i, 0))
    return pl.pallas_call(
        body, name=name, out_shape=SDS((n, r, c), g.dtype),
        grid_spec=pltpu.PrefetchScalarGridSpec(
            num_scalar_prefetch=1, grid=(n, r // tr),
            in_specs=[pl.BlockSpec((None, None, tr, c), lambda s, i, c_ref: (s, c_ref[0], i, 0)), blk], out_specs=blk),
        compiler_params=_cp("parallel", "parallel"))(core, g, a)


def chip_scatter(ps, *, name):
    n = len(ps)

    def body(*refs):
        p_refs, b_refs = refs[:n], refs[n:2 * n]
        send_sems, recv_sems = refs[2 * n:]
        x, y, c = _coords()
        cps = [_remote(p_refs[i].at[2 * cx + cy], b_refs[i].at[k], send_sems.at[3 * i + k], recv_sems.at[3 * i + k], (cx, cy, c))
               for i in range(n) for k, (cx, cy) in enumerate(_other_chips(x, y))]
        for cp in cps:
            cp.start()
        for cp in cps:
            cp.wait()

    return pl.pallas_call(
        body, name=name, in_specs=[ANY] * n, out_specs=[ANY] * n,
        out_shape=[SDS((N_CHIPS - 1,) + p.shape[1:], p.dtype) for p in ps],
        scratch_shapes=[_dma_sems(3 * n), _dma_sems(3 * n)])(*ps)


def chip_sum(p, b, chip, core, *, name):
    _, r, c = p.shape
    tr = _row_tile(r, c)

    def body(chip_ref, core_ref, p_ref, b_ref, o_ref):
        t = p_ref[...].astype(F32)
        for k in range(N_CHIPS - 1):
            t = t + b_ref[k].astype(F32)
        o_ref[...] = t

    return pl.pallas_call(
        body, name=name, out_shape=SDS((2, r, c), F32),
        grid_spec=pltpu.PrefetchScalarGridSpec(
            num_scalar_prefetch=2, grid=(r // tr,),
            in_specs=[pl.BlockSpec((None, tr, c), lambda i, chip_ref, core_ref: (chip_ref[0], i, 0)),
                      pl.BlockSpec((N_CHIPS - 1, tr, c), lambda i, chip_ref, core_ref: (0, i, 0))],
            out_specs=pl.BlockSpec((None, tr, c), lambda i, chip_ref, core_ref: (core_ref[0], i, 0))),
        compiler_params=_cp("parallel"))(chip, core, p, b)


def sibling_gather(ts, *, name):
    n = len(ts)

    def body(*refs):
        out_refs = refs[n:2 * n]
        send_sems, recv_sems = refs[2 * n:]
        x, y, c = _coords()
        cps = [_remote(out_refs[i].at[c], out_refs[i].at[c], send_sems.at[i], recv_sems.at[i], (x, y, 1 - c)) for i in range(n)]
        for cp in cps:
            cp.start()
        for i in range(n):
            _remote(out_refs[i].at[c], out_refs[i].at[1 - c], send_sems.at[i], recv_sems.at[i], (x, y, 1 - c)).wait_recv()
        for cp in cps:
            cp.wait_send()

    return pl.pallas_call(
        body, name=name, in_specs=[ANY] * n, out_specs=[ANY] * n, out_shape=[SDS(t.shape, t.dtype) for t in ts],
        input_output_aliases={i: i for i in range(n)}, scratch_shapes=[_dma_sems(n), _dma_sems(n)])(*ts)


def allgather_all(sm, *, name):
    r, w = sm.shape

    def body(s_ref, out_ref, send_sems, recv_sems, local_sem):
        x, y, c = _coords()
        me = 4 * x + 2 * y + c

        def peer(rel):
            flip = lambda v, bit: 1 - v if bit else v
            return flip(x, rel & 4), flip(y, rel & 2), flip(c, rel & 1)

        loc = pltpu.make_async_copy(s_ref, out_ref.at[me], local_sem.at[0])
        loc.start()
        cps = [_remote(s_ref, out_ref.at[me], send_sems.at[rel - 1], recv_sems.at[rel - 1], peer(rel)) for rel in range(1, N_DEV)]
        for cp in cps:
            cp.start()
        for rel in range(1, N_DEV):
            px, py, pc = peer(rel)
            _remote(s_ref, out_ref.at[4 * px + 2 * py + pc], send_sems.at[rel - 1], recv_sems.at[rel - 1], (px, py, pc)).wait_recv()
        for cp in cps:
            cp.wait_send()
        loc.wait()

    return pl.pallas_call(
        body, name=name, in_specs=[ANY], out_specs=ANY, out_shape=SDS((N_DEV, r, w), sm.dtype),
        scratch_shapes=[_dma_sems(N_DEV - 1), _dma_sems(N_DEV - 1), _dma_sems(1)])(sm)


def sum_blocks(a, *, name):
    n, r, w = a.shape

    def body(a_ref, o_ref):
        t = a_ref[0]
        for k in range(1, n):
            t = t + a_ref[k]
        o_ref[...] = t

    return pl.pallas_call(body, name=name, out_shape=SDS((r, w), F32))(a)


def _pack_rows(parts):
    cat = jnp.concatenate([t.reshape(-1) for t in parts])
    rows = -(-cat.size // (LANES * SUBLANES)) * SUBLANES
    return jnp.pad(cat, (0, rows * LANES - cat.size)).reshape(rows, LANES)


def _segments(flat, sizes):
    out, off = [], 0
    for n in sizes:
        out.append(flat[..., off:off + n])
        off += n
    return out


def _halves(a):
    return a.reshape(a.shape[:-2] + (2, a.shape[-2] // 2, a.shape[-1]))


def kernel(x, attn_norm_w, ffn_norm_w, gla_w_in, gla_w_gate_up, gla_b_gate, gla_gnorm_w, gla_w_out, kv_norm_w, sb_w_kv, sb_w_q, sb_w_out, ffn_w_gate_up, ffn_w_down, final_norm_w, loss_target, m_attn_norm_w, m_ffn_norm_w, m_gla_w_in, m_gla_w_gate_up, m_gla_b_gate, m_gla_gnorm_w, m_gla_w_out, m_kv_norm_w, m_sb_w_kv, m_sb_w_q, m_sb_w_out, m_ffn_w_gate_up, m_ffn_w_down, m_final_norm_w, v_attn_norm_w, v_ffn_norm_w, v_gla_w_in, v_gla_w_gate_up, v_gla_b_gate, v_gla_gnorm_w, v_gla_w_out, v_kv_norm_w, v_sb_w_kv, v_sb_w_q, v_sb_w_out, v_ffn_w_gate_up, v_ffn_w_down, v_final_norm_w):
    xi, yi, ci = _coords()
    core = ci.astype(jnp.int32).reshape(1)
    chip = (2 * xi + yi).astype(jnp.int32)
    _, s, d = x.shape
    rank = gla_w_gate_up.shape[1]

    local = [_halves(gla_w_in[0]), _halves(gla_w_out[0]), _halves(sb_w_kv), _halves(sb_w_q[0]), _halves(sb_w_out[0]),
             ffn_w_gate_up, ffn_w_down]
    tags = ["w_in", "w_out", "w_kv", "w_q", "w_so", "w_gu", "w_dn"]
    placed = [place_shard(a, chip.reshape(1), name=f"place_{n}") for a, n in zip(local, tags)]
    w_in, w_out, w_kv, w_q, w_so, w_gu, w_dn = allgather_chips(placed, name="gather_weights")
    in_w = N_CHIPS * w_in.shape[-1]
    w_in = w_in.reshape(N_CHIPS, d, -1).transpose(1, 0, 2).reshape(d, in_w)
    small = [gla_w_gate_up, gla_b_gate, gla_gnorm_w]
    small_all = allgather_all(_pack_rows(small), name="gather_gate_weights")
    wg, bg, gw = _segments(small_all[::2].reshape(N_CHIPS, -1), [a.size for a in small])
    wg = wg.reshape(N_CHIPS, rank, -1).transpose(1, 0, 2).reshape(rank, -1)
    p = dict(
        an0=attn_norm_w[0], an1=attn_norm_w[1], fn0=ffn_norm_w[0], fn1=ffn_norm_w[1], kvn=kv_norm_w, finn=final_norm_w,
        w_in=jnp.pad(w_in, ((0, 0), (0, -in_w % LANES))), wg=jnp.pad(wg, ((0, LANES - rank), (0, 0))).astype(BF16),
        bg=bg.reshape(1, -1), gw=gw.reshape(1, -1), w_out=w_out.reshape(-1, d), w_kv=w_kv.reshape(N_CHIPS, d, -1),
        w_q=w_q.reshape(d, -1), w_so=w_so.reshape(-1, d), w_gu=w_gu, w_dn=w_dn)

    loss_row, dx, g = local_step(x[0], loss_target[0], p)
    loss = lax.psum(loss_row[0, 0], ("x", "y", "c"))

    by_rows = lambda t: _halves(t.reshape(N_CHIPS, -1, t.shape[-1]))
    gs = [_halves(g["w_in"][:, :in_w].reshape(d, N_CHIPS, -1).transpose(1, 0, 2)), by_rows(g["w_out"]), _halves(g["w_kv"]),
          by_rows(g["w_q"]), by_rows(g["w_so"]), _halves(g["w_gu0"]), _halves(g["w_gu1"]), by_rows(g["w_dn0"]), by_rows(g["w_dn1"])]
    tags = ["w_in", "w_out", "w_kv", "w_q", "w_so", "w_gu0", "w_gu1", "w_dn0", "w_dn1"]
    from_sibling = sibling_exchange(gs, name="grads_to_sibling")
    pairs = [half_add(t, a, core, name=f"pair_sum_{n}") for t, a, n in zip(gs, from_sibling, tags)]
    from_chips = chip_scatter(pairs, name="grads_to_chips")
    mine = [chip_sum(t, b, chip.reshape(1), core, name=f"chip_sum_{n}") for t, b, n in zip(pairs, from_chips, tags)]
    tot = dict(zip(tags, sibling_gather(mine, name="grads_from_sibling")))

    vecs = [jnp.concatenate([g["an0"], g["an1"]]), jnp.concatenate([g["fn0"], g["fn1"]]), g["kvn"], g["finn"],
            g["wg"][:rank], g["bg"], g["gw"]]
    gathered_vecs = allgather_all(_pack_rows(vecs), name="gather_small_grads")
    d_an, d_fn, d_kvn, d_fin, d_wg, d_bg, d_gw = _segments(
        sum_blocks(gathered_vecs, name="sum_small_grads").reshape(-1), [t.size for t in vecs])

    def shard(t, like):
        return lax.dynamic_index_in_dim(t.reshape(-1, N_CHIPS, like.shape[-1]), chip, axis=1, keepdims=False).reshape(like.shape)

    grads = dict(
        attn_norm_w=d_an.reshape(attn_norm_w.shape), ffn_norm_w=d_fn.reshape(ffn_norm_w.shape),
        gla_w_in=tot["w_in"].reshape(gla_w_in.shape), gla_w_gate_up=shard(d_wg, gla_w_gate_up),
        gla_b_gate=shard(d_bg, gla_b_gate), gla_gnorm_w=shard(d_gw, gla_gnorm_w),
        gla_w_out=tot["w_out"].reshape(gla_w_out.shape), kv_norm_w=d_kvn.reshape(kv_norm_w.shape),
        sb_w_kv=tot["w_kv"].reshape(sb_w_kv.shape), sb_w_q=tot["w_q"].reshape(sb_w_q.shape),
        sb_w_out=tot["w_so"].reshape(sb_w_out.shape),
        ffn_w_gate_up=jnp.stack([tot["w_gu0"], tot["w_gu1"]]).reshape(ffn_w_gate_up.shape),
        ffn_w_down=jnp.stack([tot["w_dn0"], tot["w_dn1"]]).reshape(ffn_w_down.shape),
        final_norm_w=d_fin.reshape(final_norm_w.shape))
    weights = dict(
        attn_norm_w=(attn_norm_w, m_attn_norm_w, v_attn_norm_w), ffn_norm_w=(ffn_norm_w, m_ffn_norm_w, v_ffn_norm_w),
        gla_w_in=(gla_w_in, m_gla_w_in, v_gla_w_in), gla_w_gate_up=(gla_w_gate_up, m_gla_w_gate_up, v_gla_w_gate_up),
        gla_b_gate=(gla_b_gate, m_gla_b_gate, v_gla_b_gate), gla_gnorm_w=(gla_gnorm_w, m_gla_gnorm_w, v_gla_gnorm_w),
        gla_w_out=(gla_w_out, m_gla_w_out, v_gla_w_out), kv_norm_w=(kv_norm_w, m_kv_norm_w, v_kv_norm_w),
        sb_w_kv=(sb_w_kv, m_sb_w_kv, v_sb_w_kv), sb_w_q=(sb_w_q, m_sb_w_q, v_sb_w_q), sb_w_out=(sb_w_out, m_sb_w_out, v_sb_w_out),
        ffn_w_gate_up=(ffn_w_gate_up, m_ffn_w_gate_up, v_ffn_w_gate_up), ffn_w_down=(ffn_w_down, m_ffn_w_down, v_ffn_w_down),
        final_norm_w=(final_norm_w, m_final_norm_w, v_final_norm_w))
    names = list(weights)
    stepped = [adamw(weights[n][0], grads[n], weights[n][1], weights[n][2], name=f"adamw_{n}") for n in names]
    return (loss, dx.reshape(x.shape), *[grads[n] for n in names], *[t[0] for t in stepped], *[t[1] for t in stepped],
            *[t[2] for t in stepped])
```

```python
import functools

import jax
import jax.numpy as jnp
from jax import lax
from jax.experimental import pallas as pl
from jax.experimental.pallas import tpu as pltpu

F32 = jnp.float32
BF16 = jnp.bfloat16
SDS = jax.ShapeDtypeStruct
MESH = pl.DeviceIdType.MESH

EPS = 1e-6
GLA_HEADS = 4
GLA_GATE_RANK = 16
GLA_GATE_TAU = 16.0
GLA_CHUNK = 128
SB_HEADS = 16
SB_TQ = 1024
SB_TK = 128
SB_GROUP = 4
ADAM_LR = 0.001
ADAM_B1 = 0.9
ADAM_B2 = 0.999
ADAM_EPS = 1e-08
ADAM_WD = 0.01
ADAM_STEP = 10

LANES = 128
SUBLANES = 8
N_CHIPS = 4
N_DEV = 8
VMEM_LIMIT = 56 * 1024 * 1024


def _tile(dim, target, unit=LANES):
    if dim <= target:
        return dim
    t = (target // unit) * unit
    while t >= unit:
        if dim % t == 0:
            return t
        t -= unit
    raise ValueError(f"no tile for {dim}")


def _cp(*sem):
    return pltpu.CompilerParams(dimension_semantics=sem, vmem_limit_bytes=VMEM_LIMIT)


def _sigmoid(x):
    return 1.0 / (1.0 + jnp.exp(-x))


def _dot(a, b, ca=1, cb=0):
    return lax.dot_general(a, b, (((ca,), (cb,)), ((), ())), preferred_element_type=F32)


def _split_dot(tri, x):
    hi = x.astype(BF16)
    lo = (x - hi.astype(F32)).astype(BF16)
    return _dot(tri, hi) + _dot(tri, lo)


def _hilo(x):
    hi = x.astype(BF16)
    return hi, (x - hi.astype(F32)).astype(BF16)


def _dot3(a, b, ca=1, cb=0):
    return _dot(a[0], b[0], ca, cb) + _dot(a[0], b[1], ca, cb) + _dot(a[1], b[0], ca, cb)


def _tri(n, kind):
    r = lax.broadcasted_iota(jnp.int32, (n, n), 0)
    c = lax.broadcasted_iota(jnp.int32, (n, n), 1)
    m = {"le": c <= r, "ge": c >= r, "lt": c < r, "gt": c > r}[kind]
    return jnp.where(m, 1.0, 0.0).astype(BF16)


ANY = pl.BlockSpec(memory_space=pl.ANY)


def _coords():
    return lax.axis_index("x"), lax.axis_index("y"), lax.axis_index("c")


def _other_chips(x, y):
    return [(1 - x, y), (x, 1 - y), (1 - x, 1 - y)]


def _remote(src, dst, send_sem, recv_sem, dev):
    return pltpu.make_async_remote_copy(src_ref=src, dst_ref=dst, send_sem=send_sem, recv_sem=recv_sem,
                                        device_id=dev, device_id_type=MESH)


def _dma_sems(n):
    return pltpu.SemaphoreType.DMA((n,))


class GatherX:
    def __init__(self, bufs):
        self.ins, self.ios, self.outs, self.n_sems = [], list(bufs), [], 6 * len(bufs)

    def _copy(self, ios, send_sems, recv_sems, i, k, chip, half, dev):
        blk = ios[i].at[chip, half]
        return _remote(blk, blk, send_sems.at[6 * i + k], recv_sems.at[6 * i + k], dev)

    def _first(self, ios, send_sems, recv_sems):
        x, y, c = _coords()
        return [self._copy(ios, send_sems, recv_sems, i, k, 2 * x + y, c, (cx, cy, c))
                for i in range(len(ios)) for k, (cx, cy) in enumerate(_other_chips(x, y))]

    def start(self, ins, ios, outs, send_sems, recv_sems):
        for cp in self._first(ios, send_sems, recv_sems):
            cp.start()

    def finish(self, ins, ios, outs, send_sems, recv_sems):
        x, y, c = _coords()
        chips = _other_chips(x, y)
        copy = functools.partial(self._copy, ios, send_sems, recv_sems)
        passed = []
        for i in range(len(ios)):
            for k, (cx, cy) in enumerate(chips):
                copy(i, k, 2 * cx + cy, c, (x, y, c)).wait_recv()
                passed.append(copy(i, 3 + k, 2 * cx + cy, c, (x, y, 1 - c)))
                passed[-1].start()
        for i in range(len(ios)):
            for k, (cx, cy) in enumerate(chips):
                copy(i, 3 + k, 2 * cx + cy, 1 - c, (x, y, c)).wait_recv()
        for cp in self._first(ios, send_sems, recv_sems) + passed:
            cp.wait_send()


class ScatterX:
    def __init__(self, ps):
        self.ins, self.ios, self.n_sems = list(ps), [], 3 * len(ps)
        self.outs = [SDS((N_CHIPS - 1,) + p.shape[1:], p.dtype) for p in ps]

    def _copies(self, ins, outs, send_sems, recv_sems):
        x, y, c = _coords()
        return [_remote(ins[i].at[2 * cx + cy], outs[i].at[k], send_sems.at[3 * i + k], recv_sems.at[3 * i + k], (cx, cy, c))
                for i in range(len(ins)) for k, (cx, cy) in enumerate(_other_chips(x, y))]

    def start(self, ins, ios, outs, send_sems, recv_sems):
        for cp in self._copies(ins, outs, send_sems, recv_sems):
            cp.start()

    def finish(self, ins, ios, outs, send_sems, recv_sems):
        for cp in self._copies(ins, outs, send_sems, recv_sems):
            cp.wait()


def _exchange_operands(host):
    x_in = host.ins + host.ios
    x_out = [SDS(a.shape, a.dtype) for a in host.ios] + host.outs
    return x_in, x_out


def run_exchange(host, *, name):
    x_in, x_out = _exchange_operands(host)
    n_ins, n_ios = len(host.ins), len(host.ios)

    def body(*refs):
        xin, xout = refs[:len(x_in)], refs[len(x_in):len(x_in) + len(x_out)]
        send_sems, recv_sems = refs[len(x_in) + len(x_out):]
        x_refs = (xin[:n_ins], xout[:n_ios], xout[n_ios:])
        host.start(*x_refs, send_sems, recv_sems)
        host.finish(*x_refs, send_sems, recv_sems)

    return list(pl.pallas_call(
        body, name=name, in_specs=[ANY] * len(x_in), out_specs=[ANY] * len(x_out), out_shape=x_out,
        input_output_aliases={n_ins + i: i for i in range(n_ios)},
        scratch_shapes=[_dma_sems(host.n_sems), _dma_sems(host.n_sems)])(*x_in))


def _hosted_call(body, host, *, name, grid, in_specs, out_specs, out_shape, scratch_shapes, sem, args):
    if host is None:
        return list(pl.pallas_call(body, name=name, grid=grid, in_specs=in_specs, out_specs=out_specs, out_shape=out_shape,
                                   scratch_shapes=scratch_shapes, compiler_params=_cp(*sem))(*args))
    x_in, x_out = _exchange_operands(host)
    n_in, n_out, n_scr, n_ins, n_ios = len(in_specs), len(out_specs), len(scratch_shapes), len(host.ins), len(host.ios)

    def hosted(*refs):
        ins, xin = refs[:n_in], refs[n_in:n_in + len(x_in)]
        o0 = n_in + len(x_in)
        outs, xout = refs[o0:o0 + n_out], refs[o0 + n_out:o0 + n_out + len(x_out)]
        s0 = o0 + n_out + len(x_out)
        scr, (send_sems, recv_sems) = refs[s0:s0 + n_scr], refs[s0 + n_scr:]
        ids = [pl.program_id(ax) for ax in range(len(grid))]
        first = functools.reduce(jnp.logical_and, [i == 0 for i in ids])
        last = functools.reduce(jnp.logical_and, [i == n - 1 for i, n in zip(ids, grid)])
        x_refs = (xin[:n_ins], xout[:n_ios], xout[n_ios:])

        @pl.when(first)
        def _():
            host.start(*x_refs, send_sems, recv_sems)

        body(*ins, *outs, *scr)

        @pl.when(last)
        def _():
            host.finish(*x_refs, send_sems, recv_sems)

    res = pl.pallas_call(
        hosted, name=name, grid=grid, in_specs=list(in_specs) + [ANY] * len(x_in), out_specs=list(out_specs) + [ANY] * len(x_out),
        out_shape=list(out_shape) + x_out, scratch_shapes=list(scratch_shapes) + [_dma_sems(host.n_sems), _dma_sems(host.n_sems)],
        input_output_aliases={n_in + n_ins + i: n_out + i for i in range(n_ios)},
        compiler_params=_cp(*(("arbitrary",) * len(grid))))(*args, *x_in)
    return list(res[:n_out]), list(res[n_out:])


def _div(i, n):
    return i if n == 1 else lax.div(i, n)


def _rem(i, n):
    return 0 if n == 1 else lax.rem(i, n)


class View:
    def __init__(self, arr, kind="plain", lead=(), g0=0, ng=None):
        self.arr, self.kind, self.lead, self.g0 = arr, kind, tuple(lead), g0
        self.ng = (arr.shape[0] - g0) if ng is None else ng
        r, c = arr.shape[-2:]
        self.runit, self.cunit = r, c
        self.shape = {"plain": (r, c), "cols": (r, self.ng * c), "rows": (self.ng * r, c)}[kind]

    def spec(self, br, bc, rfn, cfn):
        if self.kind == "plain":
            return pl.BlockSpec((br, bc), lambda *g: (rfn(*g), cfn(*g)))
        none = (None,) * (1 + len(self.lead))
        if self.kind == "cols":
            per = self.cunit // bc
            return pl.BlockSpec(none + (br, bc), lambda *g: (self.g0 + _div(cfn(*g), per), *self.lead, rfn(*g), _rem(cfn(*g), per)))
        per = self.runit // br
        return pl.BlockSpec(none + (br, bc), lambda *g: (self.g0 + _div(rfn(*g), per), *self.lead, _rem(rfn(*g), per), cfn(*g)))


def _as_view(a):
    return a if isinstance(a, View) else View(a)


def matmul(a, b, *, name, ta=False, tb=False, out_dtype=F32, residual=None, out_chips=None, host=None, tm=1408, tn=1408, tk=2816):
    a, b = _as_view(a), _as_view(b)
    (k, m) = a.shape if ta else a.shape[::-1]
    (n, kb) = b.shape if tb else b.shape[::-1]
    assert k == kb, (a.shape, b.shape, ta, tb)
    m_unit = a.cunit if ta else a.runit
    ka_unit = a.runit if ta else a.cunit
    n_unit = b.runit if tb else b.cunit
    kb_unit = b.cunit if tb else b.runit
    if out_chips is not None:
        n_unit = min(n_unit, n // out_chips)
    tm, tn = _tile(min(m, m_unit), tm), _tile(min(n, n_unit), tn)
    tk = _tile(min(k, ka_unit, kb_unit), tk)
    assert ka_unit % tk == 0 and kb_unit % tk == 0, (ka_unit, kb_unit, tk)
    nk = k // tk
    ca, cb = (0 if ta else 1), (1 if tb else 0)

    def body(a_ref, b_ref, *refs):
        r_ref = refs[0] if residual is not None else None
        o_ref = refs[-1] if nk == 1 else refs[-2]

        def finish(r):
            if residual is not None:
                r = r + r_ref[...]
            o_ref[...] = r.astype(out_dtype)

        part = _dot(a_ref[...].astype(BF16), b_ref[...].astype(BF16), ca, cb)
        if nk == 1:
            finish(part)
            return
        acc = refs[-1]
        kk = pl.program_id(2)

        @pl.when(kk == 0)
        def _():
            acc[...] = part

        @pl.when(kk > 0)
        def _():
            acc[...] += part

        @pl.when(kk == nk - 1)
        def _():
            finish(acc[...])

    gi, gj, gk = (lambda i, j, kk: i), (lambda i, j, kk: j), (lambda i, j, kk: kk)
    a_spec = a.spec(tk, tm, gk, gi) if ta else a.spec(tm, tk, gi, gk)
    b_spec = b.spec(tn, tk, gj, gk) if tb else b.spec(tk, tn, gk, gj)
    if out_chips is None:
        out = View(SDS((m, n), out_dtype))
    else:
        out = View(SDS((out_chips, m, n // out_chips), out_dtype), "cols")
    o_spec = out.spec(tm, tn, gi, gj)
    in_specs, args = [a_spec, b_spec], [a.arr, b.arr]
    if residual is not None:
        in_specs.append(pl.BlockSpec((tm, tn), lambda i, j, kk: (i, j)))
        args.append(residual)
    res = _hosted_call(
        body, host, name=name, grid=(m // tm, n // tn, nk), in_specs=in_specs, out_specs=[o_spec], out_shape=[out.arr],
        scratch_shapes=[] if nk == 1 else [pltpu.VMEM((tm, tn), F32)], sem=("parallel", "parallel", "arbitrary"), args=args)
    return res[0] if host is None else (res[0][0], res[1])


def rmsnorm_fwd(x, w, *, name, tr=256):
    s, d = x.shape

    def body(x_ref, w_ref, o_ref):
        xf = x_ref[...]
        r = lax.rsqrt(jnp.mean(xf * xf, axis=-1, keepdims=True) + EPS)
        o_ref[...] = (xf * r * w_ref[...]).astype(BF16)

    row = pl.BlockSpec((tr, d), lambda i: (i, 0))
    return pl.pallas_call(
        body, name=name, grid=(s // tr,), in_specs=[row, pl.BlockSpec((1, d), lambda i: (0, 0))], out_specs=row,
        out_shape=SDS((s, d), BF16), compiler_params=_cp("parallel"))(x, w.reshape(1, d))


def rmsnorm_bwd(dy, x, w, dres, *, name, tr=256):
    s, d = x.shape

    def body(dy_ref, x_ref, w_ref, dres_ref, dx_ref, dw_ref):
        i = pl.program_id(0)
        xf = x_ref[...]
        r = lax.rsqrt(jnp.mean(xf * xf, axis=-1, keepdims=True) + EPS)
        xh = xf * r
        dyf = dy_ref[...].astype(F32)
        dxh = dyf * w_ref[...]
        dx_ref[...] = dres_ref[...] + r * (dxh - xh * jnp.mean(dxh * xh, axis=-1, keepdims=True))
        part = jnp.sum(dyf * xh, axis=0, keepdims=True)

        @pl.when(i == 0)
        def _():
            dw_ref[...] = part

        @pl.when(i > 0)
        def _():
            dw_ref[...] += part

    row = pl.BlockSpec((tr, d), lambda i: (i, 0))
    vec = pl.BlockSpec((1, d), lambda i: (0, 0))
    return pl.pallas_call(
        body, name=name, grid=(s // tr,), in_specs=[row, row, vec, row], out_specs=[row, vec],
        out_shape=[SDS((s, d), F32), SDS((1, d), F32)], compiler_params=_cp("arbitrary"))(dy, x, w.reshape(1, d), dres)


def final_loss(h, w, target, *, name, tr=256):
    s, d = h.shape

    def body(h_ref, w_ref, t_ref, dh_ref, dw_ref, loss_ref):
        i = pl.program_id(0)
        xf = h_ref[...]
        r = lax.rsqrt(jnp.mean(xf * xf, axis=-1, keepdims=True) + EPS)
        xh = xf * r
        err = xh * w_ref[...] - t_ref[...]
        lpart = 0.5 * jnp.sum(jnp.sum(err * err, axis=-1, keepdims=True) * (1.0 / d), axis=0, keepdims=True)
        dy = err * (1.0 / d)
        dxh = dy * w_ref[...]
        dh_ref[...] = r * (dxh - xh * jnp.mean(dxh * xh, axis=-1, keepdims=True))
        part = jnp.sum(dy * xh, axis=0, keepdims=True)
        lrow = jnp.broadcast_to(lpart, (1, LANES))

        @pl.when(i == 0)
        def _():
            dw_ref[...] = part
            loss_ref[...] = lrow

        @pl.when(i > 0)
        def _():
            dw_ref[...] += part
            loss_ref[...] += lrow

    row = pl.BlockSpec((tr, d), lambda i: (i, 0))
    vec = pl.BlockSpec((1, d), lambda i: (0, 0))
    return pl.pallas_call(
        body, name=name, grid=(s // tr,), in_specs=[row, vec, row],
        out_specs=[row, vec, pl.BlockSpec((1, LANES), lambda i: (0, 0))],
        out_shape=[SDS((s, d), F32), SDS((1, d), F32), SDS((1, LANES), F32)],
        compiler_params=_cp("arbitrary"))(h, w.reshape(1, d), target)


def swiglu_fwd(gu, *, name, tr=512, tc=1408):
    s, f2 = gu.shape
    f = f2 // 2
    tc = _tile(f, tc)
    nf = f // tc

    def body(g_ref, u_ref, o_ref):
        g = g_ref[...].astype(F32)
        o_ref[...] = (g * _sigmoid(g) * u_ref[...].astype(F32)).astype(BF16)

    return pl.pallas_call(
        body, name=name, grid=(s // tr, nf),
        in_specs=[pl.BlockSpec((tr, tc), lambda i, j: (i, j)), pl.BlockSpec((tr, tc), lambda i, j: (i, j + nf))],
        out_specs=pl.BlockSpec((tr, tc), lambda i, j: (i, j)), out_shape=SDS((s, f), BF16),
        compiler_params=_cp("parallel", "parallel"))(gu, gu)


def swiglu_bwd(gu, dact, *, name, tr=512, tc=1408):
    s, f2 = gu.shape
    f = f2 // 2
    tc = _tile(f, tc)
    nf = f // tc

    def body(g_ref, u_ref, d_ref, o_ref):
        j = pl.program_id(1)
        g = g_ref[...].astype(F32)
        d = d_ref[...].astype(F32)
        sg = _sigmoid(g)

        @pl.when(j < nf)
        def _():
            o_ref[...] = (d * u_ref[...].astype(F32) * sg * (1.0 + g * (1.0 - sg))).astype(BF16)

        @pl.when(j >= nf)
        def _():
            o_ref[...] = (d * g * sg).astype(BF16)

    return pl.pallas_call(
        body, name=name, grid=(s // tr, 2 * nf),
        in_specs=[pl.BlockSpec((tr, tc), lambda i, j: (i, j % nf)), pl.BlockSpec((tr, tc), lambda i, j: (i, nf + j % nf)),
                  pl.BlockSpec((tr, tc), lambda i, j: (i, j % nf))],
        out_specs=pl.BlockSpec((tr, tc), lambda i, j: (i, j)), out_shape=SDS((s, f2), BF16),
        compiler_params=_cp("parallel", "parallel"))(gu, gu, dact)


def _gate_z(gl_ref, w_ref, b_ref):
    glb = gl_ref[...].astype(BF16)
    return glb, _dot(glb, w_ref[...]) + b_ref[...]


def gate_fwd(proj, wg, bg, *, name, tr=512):
    s, inw = proj.shape
    qk = wg.shape[1]
    glc = inw // LANES - 1

    def body(gl_ref, w_ref, b_ref, g_ref):
        _, z = _gate_z(gl_ref, w_ref, b_ref)
        g_ref[...] = (jnp.minimum(z, 0.0) - jnp.log(1.0 + jnp.exp(-jnp.abs(z)))) * (1.0 / GLA_GATE_TAU)

    return pl.pallas_call(
        body, name=name, grid=(s // tr,),
        in_specs=[pl.BlockSpec((tr, LANES), lambda i: (i, glc)), pl.BlockSpec((LANES, qk), lambda i: (0, 0)),
                  pl.BlockSpec((1, qk), lambda i: (0, 0))],
        out_specs=pl.BlockSpec((tr, qk), lambda i: (i, 0)), out_shape=SDS((s, qk), F32),
        compiler_params=_cp("parallel"))(proj, wg, bg)


def gate_bwd(dg, proj, wg, bg, *, name, tr=512):
    s, inw = proj.shape
    qk = wg.shape[1]
    glc = inw // LANES - 1

    def body(dg_ref, gl_ref, w_ref, b_ref, dgl_ref, dw_ref, db_ref):
        i = pl.program_id(0)
        glb, z = _gate_z(gl_ref, w_ref, b_ref)
        dz = dg_ref[...] * (1.0 / (1.0 + jnp.exp(z))) * (1.0 / GLA_GATE_TAU)
        dzb = dz.astype(BF16)
        dgl_ref[...] = _dot(dzb, w_ref[...], 1, 1).astype(BF16)
        pw = _dot(glb, dzb, 0, 0)
        pb = jnp.sum(dz, axis=0, keepdims=True)

        @pl.when(i == 0)
        def _():
            dw_ref[...] = pw
            db_ref[...] = pb

        @pl.when(i > 0)
        def _():
            dw_ref[...] += pw
            db_ref[...] += pb

    return pl.pallas_call(
        body, name=name, grid=(s // tr,),
        in_specs=[pl.BlockSpec((tr, qk), lambda i: (i, 0)), pl.BlockSpec((tr, LANES), lambda i: (i, glc)),
                  pl.BlockSpec((LANES, qk), lambda i: (0, 0)), pl.BlockSpec((1, qk), lambda i: (0, 0))],
        out_specs=[pl.BlockSpec((tr, LANES), lambda i: (i, 0)), pl.BlockSpec((LANES, qk), lambda i: (0, 0)),
                   pl.BlockSpec((1, qk), lambda i: (0, 0))],
        out_shape=[SDS((s, LANES), BF16), SDS((LANES, qk), F32), SDS((1, qk), F32)],
        compiler_params=_cp("arbitrary"))(dg, proj, wg, bg)


def _gla_chunk_terms(q_ref, k_ref, g_ref, c, scale):
    q = q_ref[...] * scale
    k = k_ref[...]
    gg = g_ref[...]
    b = _split_dot(_tri(c, "le"), gg)
    row = lax.broadcasted_iota(jnp.int32, gg.shape, 0)
    bm = jnp.sum(jnp.where(row < c // 2, gg, 0.0), axis=0, keepdims=True)
    bl = jnp.sum(gg, axis=0, keepdims=True)
    eb, em, emi, el = jnp.exp(b), jnp.exp(b - bm), jnp.exp(bm - b), jnp.exp(bl - b)
    return q, k, bl, eb, em, emi, el


def _causal(a):
    r = lax.broadcasted_iota(jnp.int32, a.shape, 0)
    c = lax.broadcasted_iota(jnp.int32, a.shape, 1)
    return jnp.where(r >= c, a, 0.0)


def gla_fwd(proj, g, *, name, host=None, c=GLA_CHUNK):
    s = proj.shape[0]
    qk = g.shape[1]
    dk, dv = qk // GLA_HEADS, 2 * qk // GLA_HEADS
    nc = s // c
    scale = dk ** -0.5
    kq = qk // dk

    def body(q_ref, k_ref, v_ref, g_ref, o_ref, st_ref, state):
        @pl.when(pl.program_id(1) == 0)
        def _():
            state[...] = jnp.zeros_like(state)

        q, k, bl, eb, em, emi, el = _gla_chunk_terms(q_ref, k_ref, g_ref, c, scale)
        v2 = _hilo(v_ref[...])
        st = state[...]
        st_ref[...] = st
        a = _causal(_dot3(_hilo(q * em), _hilo(k * emi), 1, 1))
        o_ref[...] = _dot3(_hilo(q * eb), _hilo(st), 1, 1) + _dot3(_hilo(a), v2)
        state[...] = st * jnp.exp(bl) + _dot3(v2, _hilo(k * el), 0, 0)

    return _hosted_call(
        body, host, name=name, grid=(GLA_HEADS, nc),
        in_specs=[pl.BlockSpec((c, dk), lambda h, i: (i, h)), pl.BlockSpec((c, dk), lambda h, i: (i, kq + h)),
                  pl.BlockSpec((c, dv), lambda h, i: (i, kq + h)), pl.BlockSpec((c, dk), lambda h, i: (i, h))],
        out_specs=[pl.BlockSpec((c, dv), lambda h, i: (i, h)),
                   pl.BlockSpec((None, None, dv, dk), lambda h, i: (h, i, 0, 0))],
        out_shape=[SDS((s, 2 * qk), F32), SDS((GLA_HEADS, nc, dv, dk), F32)],
        scratch_shapes=[pltpu.VMEM((dv, dk), F32)], sem=("parallel", "arbitrary"), args=(proj, proj, proj, g))


def gla_bwd(proj, g, states, do, *, name, host=None, c=GLA_CHUNK):
    s = proj.shape[0]
    qk = g.shape[1]
    dk, dv = qk // GLA_HEADS, 2 * qk // GLA_HEADS
    nc = s // c
    scale = dk ** -0.5
    kq = qk // dk

    def body(q_ref, k_ref, v_ref, g_ref, do_ref, st_ref, dq_ref, dk_ref, dv_ref, dg_ref, dstate, dgc):
        @pl.when(pl.program_id(1) == 0)
        def _():
            dstate[...] = jnp.zeros_like(dstate)
            dgc[...] = jnp.zeros_like(dgc)

        q, k, bl, eb, em, emi, el = _gla_chunk_terms(q_ref, k_ref, g_ref, c, scale)
        v2, do2 = _hilo(v_ref[...]), _hilo(do_ref[...])
        qe, qm, km, kd = _hilo(q * eb), _hilo(q * em), _hilo(k * emi), _hilo(k * el)
        ds = dstate[...]
        ds2 = _hilo(ds)
        a = _hilo(_causal(_dot3(qm, km, 1, 1)))
        dv_ref[...] = (_dot3(a, do2, 0, 0) + _dot3(kd, ds2, 1, 1)).astype(BF16)
        da = _hilo(_causal(_dot3(do2, v2, 1, 1)))
        dq = _dot3(da, km) * em + _dot3(do2, _hilo(st_ref[...])) * eb
        dkk = _dot3(da, qm, 0, 0) * emi + _dot3(v2, ds2) * el
        dstate[...] = ds * jnp.exp(bl) + _dot3(do2, qe, 0, 0)
        db = q * dq - k * dkk
        dg_ref[...] = _split_dot(_tri(c, "ge"), db) + dgc[...]
        dgc[...] += jnp.sum(db, axis=0, keepdims=True)
        dq_ref[...] = (dq * scale).astype(BF16)
        dk_ref[...] = dkk.astype(BF16)

    rev = lambda i: nc - 1 - i
    qspec = pl.BlockSpec((c, dk), lambda h, i: (rev(i), h))
    vspec = pl.BlockSpec((c, dv), lambda h, i: (rev(i), h))
    return _hosted_call(
        body, host, name=name, grid=(GLA_HEADS, nc),
        in_specs=[qspec, pl.BlockSpec((c, dk), lambda h, i: (rev(i), kq + h)),
                  pl.BlockSpec((c, dv), lambda h, i: (rev(i), kq + h)), qspec, vspec,
                  pl.BlockSpec((None, None, dv, dk), lambda h, i: (h, rev(i), 0, 0))],
        out_specs=[qspec, qspec, vspec, qspec],
        out_shape=[SDS((s, qk), BF16), SDS((s, qk), BF16), SDS((s, 2 * qk), BF16), SDS((s, qk), F32)],
        scratch_shapes=[pltpu.VMEM((dv, dk), F32), pltpu.VMEM((1, dk), F32)], sem=("parallel", "arbitrary"),
        args=(proj, proj, proj, g, do, states))


def gnorm_fwd(o, proj, gw, *, name, tr=512):
    s, v = o.shape
    dv = v // GLA_HEADS
    roff = 2 * GLA_HEADS

    def body(o_ref, r_ref, w_ref, y_ref):
        of = o_ref[...]
        rs = lax.rsqrt(jnp.mean(of * of, axis=-1, keepdims=True) + EPS)
        r = r_ref[...]
        y_ref[...] = (of * rs * w_ref[...] * (r * _sigmoid(r))).astype(BF16)

    blk = pl.BlockSpec((tr, dv), lambda i, h: (i, h))
    return pl.pallas_call(
        body, name=name, grid=(s // tr, GLA_HEADS),
        in_specs=[blk, pl.BlockSpec((tr, dv), lambda i, h: (i, roff + h)), pl.BlockSpec((1, dv), lambda i, h: (0, 0))],
        out_specs=blk, out_shape=SDS((s, v), BF16), compiler_params=_cp("parallel", "parallel"))(o, proj, gw)


def gnorm_bwd(dy, o, proj, gw, *, name, tr=512):
    s, v = o.shape
    dv = v // GLA_HEADS
    roff = 2 * GLA_HEADS

    def body(dy_ref, o_ref, r_ref, w_ref, do_ref, dr_ref, dw_ref):
        first = jnp.logical_and(pl.program_id(0) == 0, pl.program_id(1) == 0)
        of = o_ref[...]
        rs = lax.rsqrt(jnp.mean(of * of, axis=-1, keepdims=True) + EPS)
        n = of * rs
        r = r_ref[...]
        sg = _sigmoid(r)
        dyf = dy_ref[...].astype(F32)
        dn_w = dyf * (r * sg)
        dr_ref[...] = (dyf * n * w_ref[...] * sg * (1.0 + r * (1.0 - sg))).astype(BF16)
        dn = dn_w * w_ref[...]
        do_ref[...] = rs * (dn - n * jnp.mean(dn * n, axis=-1, keepdims=True))
        part = jnp.sum(dn_w * n, axis=0, keepdims=True)

        @pl.when(first)
        def _():
            dw_ref[...] = part

        @pl.when(jnp.logical_not(first))
        def _():
            dw_ref[...] += part

    blk = pl.BlockSpec((tr, dv), lambda i, h: (i, h))
    vec = pl.BlockSpec((1, dv), lambda i, h: (0, 0))
    return pl.pallas_call(
        body, name=name, grid=(s // tr, GLA_HEADS),
        in_specs=[blk, blk, pl.BlockSpec((tr, dv), lambda i, h: (i, roff + h)), vec],
        out_specs=[blk, blk, vec], out_shape=[SDS((s, v), F32), SDS((s, v), BF16), SDS((1, dv), F32)],
        compiler_params=_cp("arbitrary", "arbitrary"))(dy, o, proj, gw)


def _sb_block(kblk, q, ks, q0, scale, carry, masked):
    tk, tq = kblk.shape[0], q.shape[0]
    z = _dot(kblk, q, 1, 1) * scale
    sp = jnp.maximum(z, 0.0) + jnp.log(1.0 + jnp.exp(-jnp.abs(z)))
    mask = None
    lf = -sp
    if masked:
        kpos = ks + lax.broadcasted_iota(jnp.int32, (tk, tq), 0)
        qpos = q0 + lax.broadcasted_iota(jnp.int32, (tk, tq), 1)
        mask = kpos < qpos
        lf = jnp.where(mask, lf, 0.0)
    later = _split_dot(_tri(tk, "gt"), lf)
    a = jnp.exp(z - sp + later + carry)
    if masked:
        a = jnp.where(mask, a, 0.0)
    return z, sp, mask, lf, a


def sb_fwd(q, kv, *, name, host=None, tq=SB_TQ, tk=SB_TK, group=SB_GROUP):
    s, w = q.shape
    hd = w // SB_HEADS
    nq, nkb = s // tq, s // tk
    scale = hd ** -0.5
    per = tq // tk
    assert per % group == 0

    def body(q_ref, k_ref, v_ref, o_ref, car_ref, o_acc):
        qi = pl.program_id(1)
        qb = q_ref[...]
        q0 = qi * tq
        car_ref[...] = jnp.zeros_like(car_ref)
        o_acc[...] = jnp.zeros_like(o_acc)

        def blocks(first, carry, masked):
            for t in reversed(range(group)):
                kj = first + t
                ks = pl.multiple_of(kj * tk, tk)
                car_ref[pl.ds(kj, 1), :] = carry
                _, _, _, lf, a = _sb_block(k_ref[pl.ds(ks, tk), :], qb, ks, q0, scale, carry, masked)
                o_acc[...] += _dot(a.astype(BF16), v_ref[pl.ds(ks, tk), :], 0, 0)
                carry = carry + jnp.sum(lf, axis=0, keepdims=True)
            return carry

        carry = jnp.zeros((1, tq), F32)
        for gidx in reversed(range(per // group)):
            carry = blocks(qi * per + gidx * group, carry, True)
        n_in = qi * (per // group)
        lax.fori_loop(0, n_in, lambda i, c: blocks((n_in - 1 - i) * group, c, False), carry)
        o_ref[...] = o_acc[...].astype(BF16)

    qspec = pl.BlockSpec((tq, hd), lambda h, i: (i, h))
    return _hosted_call(
        body, host, name=name, grid=(SB_HEADS, nq),
        in_specs=[qspec, pl.BlockSpec((None, s, hd), lambda h, i: (0, 0, h)), pl.BlockSpec((None, s, hd), lambda h, i: (1, 0, h))],
        out_specs=[qspec, pl.BlockSpec((None, None, nkb, tq), lambda h, i: (h, i, 0, 0))],
        out_shape=[SDS((s, w), BF16), SDS((SB_HEADS, nq, nkb, tq), F32)],
        scratch_shapes=[pltpu.VMEM((tq, hd), F32)], sem=("parallel", "parallel"), args=(q, kv, kv))


def sb_bwd(q, kv, do, car, *, name, host=None, tq=SB_TQ, tk=SB_TK, group=SB_GROUP):
    s, w = q.shape
    hd = w // SB_HEADS
    nq, nkb = s // tq, s // tk
    scale = hd ** -0.5
    per = tq // tk
    assert per % group == 0

    def body(q_ref, k_ref, v_ref, do_ref, car_ref, dq_ref, dkv_ref, dq_acc, dk_acc, dv_acc):
        qi = pl.program_id(1)
        qb = q_ref[...]
        dob = do_ref[...]
        q0 = qi * tq
        dq_acc[...] = jnp.zeros_like(dq_acc)

        @pl.when(qi == 0)
        def _():
            dk_acc[...] = jnp.zeros_like(dk_acc)
            dv_acc[...] = jnp.zeros_like(dv_acc)

        def blocks(first, pcar, masked):
            for t in range(group):
                kj = first + t
                ks = pl.multiple_of(kj * tk, tk)
                kblk = k_ref[pl.ds(ks, tk), :]
                z, sp, mask, _, a = _sb_block(kblk, qb, ks, q0, scale, car_ref[pl.ds(kj, 1), :], masked)
                p = a * _dot(v_ref[pl.ds(ks, tk), :], dob, 1, 1)
                before = _split_dot(_tri(tk, "lt"), p)
                sg = jnp.exp(z - sp)
                dz = p * (1.0 - sg) - (pcar + before) * sg
                if masked:
                    dz = jnp.where(mask, dz, 0.0)
                dz = (dz * scale).astype(BF16)
                dk_acc[pl.ds(ks, tk), :] += _dot(dz, qb)
                dv_acc[pl.ds(ks, tk), :] += _dot(a.astype(BF16), dob)
                dq_acc[...] += _dot(dz, kblk, 0, 0)
                pcar = pcar + jnp.sum(p, axis=0, keepdims=True)
            return pcar

        pcar = lax.fori_loop(0, qi * (per // group), lambda i, c: blocks(i * group, c, False), jnp.zeros((1, tq), F32))
        for gidx in range(per // group):
            pcar = blocks(qi * per + gidx * group, pcar, True)
        dq_ref[...] = dq_acc[...].astype(BF16)

        @pl.when(qi == nq - 1)
        def _():
            dkv_ref[0] = dk_acc[...].astype(BF16)
            dkv_ref[1] = dv_acc[...].astype(BF16)

    qspec = pl.BlockSpec((tq, hd), lambda h, i: (i, h))
    return _hosted_call(
        body, host, name=name, grid=(SB_HEADS, nq),
        in_specs=[qspec, pl.BlockSpec((None, s, hd), lambda h, i: (0, 0, h)), pl.BlockSpec((None, s, hd), lambda h, i: (1, 0, h)),
                  qspec, pl.BlockSpec((None, None, nkb, tq), lambda h, i: (h, i, 0, 0))],
        out_specs=[qspec, pl.BlockSpec((2, s, hd), lambda h, i: (0, 0, h))],
        out_shape=[SDS((s, w), BF16), SDS((2, s, w), BF16)],
        scratch_shapes=[pltpu.VMEM((tq, hd), F32), pltpu.VMEM((s, hd), F32), pltpu.VMEM((s, hd), F32)],
        sem=("parallel", "arbitrary"), args=(q, kv, kv, do, car))


def adamw(w, g, m, v, *, name):
    shape = w.shape
    c = shape[-1]
    r = w.size // c
    tr = _tile(r, max(8, (3 * LANES * 1024) // c), unit=8) if r >= 8 else r

    def body(w_ref, g_ref, m_ref, v_ref, d_ref, nm_ref, nv_ref):
        gf = g_ref[...]
        mn = ADAM_B1 * m_ref[...] + (1.0 - ADAM_B1) * gf
        vn = ADAM_B2 * v_ref[...] + (1.0 - ADAM_B2) * (gf * gf)
        m_hat = mn / (1.0 - ADAM_B1 ** ADAM_STEP)
        v_hat = vn / (1.0 - ADAM_B2 ** ADAM_STEP)
        d_ref[...] = -ADAM_LR * (m_hat / (jnp.sqrt(v_hat) + ADAM_EPS) + ADAM_WD * w_ref[...])
        nm_ref[...] = mn
        nv_ref[...] = vn

    blk = pl.BlockSpec((tr, c), lambda i: (i, 0))
    outs = pl.pallas_call(
        body, name=name, grid=(r // tr,), in_specs=[blk] * 4, out_specs=[blk] * 3,
        out_shape=[SDS((r, c), F32)] * 3, compiler_params=_cp("parallel"))(
            *(t.reshape(r, c) for t in (w, g, m, v)))
    return tuple(o.reshape(shape) for o in outs)


def _row_tile(r, c):
    return _tile(r, max(16, (4 * LANES * 1024) // c), unit=16)


def _halves(a):
    return a.reshape(a.shape[:-2] + (2, a.shape[-2] // 2, a.shape[-1]))


def place_shard(w, chip, *, name, layer=None):
    r, c = w.shape[-2:]
    tr = _row_tile(r, c)

    def body(c_ref, w_ref, o_ref):
        o_ref[...] = w_ref[...].astype(BF16)

    if layer is None:
        w_spec = pl.BlockSpec((None, tr, c), lambda h, i, c_ref: (h, i, 0))
    else:
        w_spec = pl.BlockSpec((None, None, tr, c), lambda h, i, c_ref: (layer, h, i, 0))
    return pl.pallas_call(
        body, name=name, out_shape=SDS((N_CHIPS, 2, r, c), BF16),
        grid_spec=pltpu.PrefetchScalarGridSpec(
            num_scalar_prefetch=1, grid=(2, r // tr), in_specs=[w_spec],
            out_specs=pl.BlockSpec((None, None, tr, c), lambda h, i, c_ref: (c_ref[0], h, i, 0))),
        compiler_params=_cp("parallel", "parallel"))(chip, w)


def sibling_exchange(gs, *, name):
    n = len(gs)

    def body(*refs):
        g_refs, a_refs = refs[:n], refs[n:2 * n]
        send_sems, recv_sems = refs[2 * n:]
        x, y, c = _coords()
        cps = [_remote(g_refs[i].at[:, 1 - c], a_refs[i], send_sems.at[i], recv_sems.at[i], (x, y, 1 - c)) for i in range(n)]
        for cp in cps:
            cp.start()
        for cp in cps:
            cp.wait()

    return pl.pallas_call(
        body, name=name, in_specs=[ANY] * n, out_specs=[ANY] * n,
        out_shape=[SDS(g.shape[:1] + g.shape[2:], g.dtype) for g in gs],
        scratch_shapes=[_dma_sems(n), _dma_sems(n)])(*gs)


def half_add(g, a, core, *, name):
    n, _, r, c = g.shape
    tr = _row_tile(r, c)

    def body(c_ref, g_ref, a_ref, o_ref):
        o_ref[...] = (g_ref[...].astype(F32) + a_ref[...].astype(F32)).astype(o_ref.dtype)

    blk = pl.BlockSpec((None, tr, c), lambda s, i, c_ref: (s, i, 0))
    return pl.pallas_call(
        body, name=name, out_shape=SDS((n, r, c), g.dtype),
        grid_spec=pltpu.PrefetchScalarGridSpec(
            num_scalar_prefetch=1, grid=(n, r // tr),
            in_specs=[pl.BlockSpec((None, None, tr, c), lambda s, i, c_ref: (s, c_ref[0], i, 0)), blk], out_specs=blk),
        compiler_params=_cp("parallel", "parallel"))(core, g, a)


def chip_sum(p, b, chip, core, *, name):
    _, r, c = p.shape
    tr = _row_tile(r, c)

    def body(chip_ref, core_ref, p_ref, b_ref, o_ref):
        t = p_ref[...].astype(F32)
        for k in range(N_CHIPS - 1):
            t = t + b_ref[k].astype(F32)
        o_ref[...] = t

    return pl.pallas_call(
        body, name=name, out_shape=SDS((2, r, c), F32),
        grid_spec=pltpu.PrefetchScalarGridSpec(
            num_scalar_prefetch=2, grid=(r // tr,),
            in_specs=[pl.BlockSpec((None, tr, c), lambda i, chip_ref, core_ref: (chip_ref[0], i, 0)),
                      pl.BlockSpec((N_CHIPS - 1, tr, c), lambda i, chip_ref, core_ref: (0, i, 0))],
            out_specs=pl.BlockSpec((None, tr, c), lambda i, chip_ref, core_ref: (core_ref[0], i, 0))),
        compiler_params=_cp("parallel"))(chip, core, p, b)


def sibling_gather(ts, *, name):
    n = len(ts)

    def body(*refs):
        out_refs = refs[n:2 * n]
        send_sems, recv_sems = refs[2 * n:]
        x, y, c = _coords()
        cps = [_remote(out_refs[i].at[c], out_refs[i].at[c], send_sems.at[i], recv_sems.at[i], (x, y, 1 - c)) for i in range(n)]
        for cp in cps:
            cp.start()
        for i in range(n):
            _remote(out_refs[i].at[c], out_refs[i].at[1 - c], send_sems.at[i], recv_sems.at[i], (x, y, 1 - c)).wait_recv()
        for cp in cps:
            cp.wait_send()

    return pl.pallas_call(
        body, name=name, in_specs=[ANY] * n, out_specs=[ANY] * n, out_shape=[SDS(t.shape, t.dtype) for t in ts],
        input_output_aliases={i: i for i in range(n)}, scratch_shapes=[_dma_sems(n), _dma_sems(n)])(*ts)


def allgather_all(sm, *, name):
    r, w = sm.shape

    def body(s_ref, out_ref, send_sems, recv_sems, local_sem):
        x, y, c = _coords()
        me = 4 * x + 2 * y + c

        def peer(rel):
            flip = lambda v, bit: 1 - v if bit else v
            return flip(x, rel & 4), flip(y, rel & 2), flip(c, rel & 1)

        loc = pltpu.make_async_copy(s_ref, out_ref.at[me], local_sem.at[0])
        loc.start()
        cps = [_remote(s_ref, out_ref.at[me], send_sems.at[rel - 1], recv_sems.at[rel - 1], peer(rel)) for rel in range(1, N_DEV)]
        for cp in cps:
            cp.start()
        for rel in range(1, N_DEV):
            px, py, pc = peer(rel)
            _remote(s_ref, out_ref.at[4 * px + 2 * py + pc], send_sems.at[rel - 1], recv_sems.at[rel - 1], (px, py, pc)).wait_recv()
        for cp in cps:
            cp.wait_send()
        loc.wait()

    return pl.pallas_call(
        body, name=name, in_specs=[ANY], out_specs=ANY, out_shape=SDS((N_DEV, r, w), sm.dtype),
        scratch_shapes=[_dma_sems(N_DEV - 1), _dma_sems(N_DEV - 1), _dma_sems(1)])(sm)


def sum_blocks(a, *, name):
    n, r, w = a.shape

    def body(a_ref, o_ref):
        t = a_ref[0]
        for k in range(1, n):
            t = t + a_ref[k]
        o_ref[...] = t

    return pl.pallas_call(body, name=name, out_shape=SDS((r, w), F32))(a)


class Exchanges:
    def __init__(self, placed=None, chip=None, core=None):
        self.placed, self.chip, self.core = placed, chip, core
        self.mine = {}

    def gather(self, names):
        return None if self.placed is None else GatherX([self.placed[n] for n in names])

    def reduce(self, grads, call):
        if self.placed is None:
            return call(None)
        names = list(grads)
        from_sibling = sibling_exchange([grads[n] for n in names], name="grads_to_sibling_" + names[0])
        pairs = [half_add(grads[n], a, self.core, name=f"pair_sum_{n}") for n, a in zip(names, from_sibling)]
        result, from_chips = call(ScatterX(pairs))
        for n, t, b in zip(names, pairs, from_chips):
            self.mine[n] = chip_sum(t, b, self.chip, self.core, name=f"chip_sum_{n}")
        return result


def _usable(name, buf, d):
    if name == "w_in":
        cols = N_CHIPS * buf.shape[-1]
        full = buf.reshape(N_CHIPS, d, -1).transpose(1, 0, 2).reshape(d, cols)
        return jnp.pad(full, ((0, 0), (0, -cols % LANES)))
    if name in ("w_out", "w_q", "w_so"):
        return buf.reshape(-1, buf.shape[-1])
    return buf.reshape(N_CHIPS, -1, buf.shape[-1])


def _by_rows(t):
    return _halves(t.reshape(N_CHIPS, -1, t.shape[-1]))


def _ffn_fwd(h, nw, w_gu, w_dn, tag, host=None):
    f = rmsnorm_fwd(h, nw, name=f"{tag}_norm")
    gu = matmul(f, View(w_gu, "cols"), name=f"{tag}_gate_up", out_dtype=BF16, host=host)
    gu, got = gu if host is not None else (gu, None)
    act = swiglu_fwd(gu, name=f"{tag}_act")
    return matmul(act, View(w_dn, "rows"), name=f"{tag}_down", residual=h), (f, gu, act), got


def _ffn_bwd(dh, h, nw, w_gu, w_dn, saved, tag, ex, carried):
    f, gu, act = saved
    dact = matmul(dh, View(w_dn, "rows"), name=f"{tag}_dact", tb=True, out_dtype=BF16)
    dw_dn = matmul(act, dh, name=f"{tag}_dw_down", ta=True, out_dtype=BF16)
    dgu = swiglu_bwd(gu, dact, name=f"{tag}_dgu")

    def dw_gate_up(host):
        return matmul(f, dgu, name=f"{tag}_dw_gate_up", ta=True, out_dtype=BF16, out_chips=N_CHIPS, host=host)

    dw_gu = ex.reduce(carried, dw_gate_up) if carried else dw_gate_up(None)
    df = matmul(dgu, View(w_gu, "cols"), name=f"{tag}_df", tb=True)
    dh_in, dnw = rmsnorm_bwd(df, h, nw, dh, name=f"{tag}_dnorm")
    return dh_in, dnw, dw_gu, dw_dn


def local_step(x, target, p, ex):
    d = x.shape[1]
    hosted = ex.placed is not None
    w = {} if hosted else {n: _usable(n, p[n], d) for n in ("w_in", "w_out", "w_kv", "w_q", "w_so", "w_gu0", "w_gu1", "w_dn0", "w_dn1")}

    def take(names, got):
        for n, buf in zip(names, got or []):
            w[n] = _usable(n, buf, d)

    def carry(call, names):
        host = ex.gather(names)
        if host is None:
            return call(None)
        res, got = call(host)
        take(names, got)
        return res

    if hosted:
        take(["w_in", "w_out"], run_exchange(ex.gather(["w_in", "w_out"]), name="gather_gla"))
    a0 = rmsnorm_fwd(x, p["an0"], name="l0_attn_norm")
    proj = carry(lambda host: matmul(a0, w["w_in"], name="gla_in", host=host), ["w_dn0"])
    g = gate_fwd(proj, p["wg"], p["bg"], name="gla_gate")
    o, states = carry(lambda host: gla_fwd(proj, g, name="gla_scan", host=host), ["w_gu0"])
    og = gnorm_fwd(o, proj, p["gw"], name="gla_outnorm")
    h1 = matmul(og, w["w_out"], name="gla_out", residual=x)
    sb_names = ["w_kv", "w_q", "w_so"]
    h2, ffn0, got = _ffn_fwd(h1, p["fn0"], w["w_gu0"], w["w_dn0"], "ffn0", host=ex.gather(sb_names))
    take(sb_names, got)
    w_kv = View(w["w_kv"], "cols")
    kvn = rmsnorm_fwd(h2, p["kvn"], name="kv_norm")
    kv = matmul(kvn, w_kv, name="sb_kv", out_dtype=BF16, out_chips=2)
    a1 = rmsnorm_fwd(h2, p["an1"], name="l1_attn_norm")
    q2 = matmul(a1, w["w_q"], name="sb_q", out_dtype=BF16)
    o2, car = carry(lambda host: sb_fwd(q2, kv, name="sb_attn", host=host), ["w_gu1", "w_dn1"])
    h3 = matmul(o2, w["w_so"], name="sb_out", residual=h2)
    h4, ffn1, _ = _ffn_fwd(h3, p["fn1"], w["w_gu1"], w["w_dn1"], "ffn1")
    dh4, d_fin, loss_row = final_loss(h4, p["finn"], target, name="final_loss")

    dh3, d_fn1, dw_gu1, dw_dn1 = _ffn_bwd(dh4, h3, p["fn1"], w["w_gu1"], w["w_dn1"], ffn1, "ffn1", ex, {})
    do2 = matmul(dh3, w["w_so"], name="sb_do", tb=True, out_dtype=BF16)
    dw_so = matmul(o2, dh3, name="sb_dw_out", ta=True, out_dtype=BF16)
    dq2, dkv = ex.reduce({"w_gu1": _halves(dw_gu1), "w_dn1": _by_rows(dw_dn1)},
                         lambda host: sb_bwd(q2, kv, do2, car, name="sb_attn_bwd", host=host))
    dkv = View(dkv, "cols")
    dw_q = matmul(a1, dq2, name="sb_dw_q", ta=True, out_dtype=BF16)
    da1 = matmul(dq2, w["w_q"], name="sb_da", tb=True)
    dh2, d_an1 = rmsnorm_bwd(da1, h2, p["an1"], dh3, name="l1_attn_dnorm")
    dw_kv = matmul(kvn, dkv, name="sb_dw_kv", ta=True, out_dtype=BF16, out_chips=N_CHIPS)
    dkvn = matmul(dkv, w_kv, name="sb_dkvn", tb=True)
    dh2, d_kvn = rmsnorm_bwd(dkvn, h2, p["kvn"], dh2, name="kv_dnorm")
    sb_grads = {"w_kv": _halves(dw_kv), "w_q": _by_rows(dw_q), "w_so": _by_rows(dw_so)}
    dh1, d_fn0, dw_gu0, dw_dn0 = _ffn_bwd(dh2, h1, p["fn0"], w["w_gu0"], w["w_dn0"], ffn0, "ffn0", ex, sb_grads)
    dog = matmul(dh1, w["w_out"], name="gla_dog", tb=True, out_dtype=BF16)
    dw_out = matmul(og, dh1, name="gla_dw_out", ta=True, out_dtype=BF16)
    do, dr, d_gw = gnorm_bwd(dog, o, proj, p["gw"], name="gla_outnorm_bwd")
    dq, dk, dv, dg = ex.reduce({"w_gu0": _halves(dw_gu0), "w_dn0": _by_rows(dw_dn0)},
                               lambda host: gla_bwd(proj, g, states, do, name="gla_scan_bwd", host=host))
    dgl, d_wg, d_bg = gate_bwd(dg, proj, p["wg"], p["bg"], name="gla_gate_bwd")
    dproj = jnp.concatenate([dq, dk, dv, dr, dgl], axis=1)
    dw_in = matmul(a0, dproj, name="gla_dw_in", ta=True, out_dtype=BF16)
    da0 = matmul(dproj, w["w_in"], name="gla_da", tb=True)
    dx, d_an0 = rmsnorm_bwd(da0, x, p["an0"], dh1, name="l0_attn_dnorm")
    in_w = p["in_w"]
    gla_grads = {"w_in": _halves(dw_in[:, :in_w].reshape(d, N_CHIPS, -1).transpose(1, 0, 2)), "w_out": _by_rows(dw_out)}
    ex.reduce(gla_grads, lambda host: (None, run_exchange(host, name="grads_to_chips_gla")[len(host.ios):]) if host is not None else None)

    small = dict(an0=d_an0, an1=d_an1, fn0=d_fn0, fn1=d_fn1, kvn=d_kvn, finn=d_fin, wg=d_wg, bg=d_bg, gw=d_gw)
    big = {}
    if not hosted:
        big = dict(gla_grads, **sb_grads, w_gu0=_halves(dw_gu0), w_gu1=_halves(dw_gu1), w_dn0=_by_rows(dw_dn0), w_dn1=_by_rows(dw_dn1))
    return loss_row, dx, small, big


def _pack_rows(parts):
    cat = jnp.concatenate([t.reshape(-1) for t in parts])
    rows = -(-cat.size // (LANES * SUBLANES)) * SUBLANES
    return jnp.pad(cat, (0, rows * LANES - cat.size)).reshape(rows, LANES)


def _segments(flat, sizes):
    out, off = [], 0
    for n in sizes:
        out.append(flat[..., off:off + n])
        off += n
    return out


def kernel(x, attn_norm_w, ffn_norm_w, gla_w_in, gla_w_gate_up, gla_b_gate, gla_gnorm_w, gla_w_out, kv_norm_w, sb_w_kv, sb_w_q, sb_w_out, ffn_w_gate_up, ffn_w_down, final_norm_w, loss_target, m_attn_norm_w, m_ffn_norm_w, m_gla_w_in, m_gla_w_gate_up, m_gla_b_gate, m_gla_gnorm_w, m_gla_w_out, m_kv_norm_w, m_sb_w_kv, m_sb_w_q, m_sb_w_out, m_ffn_w_gate_up, m_ffn_w_down, m_final_norm_w, v_attn_norm_w, v_ffn_norm_w, v_gla_w_in, v_gla_w_gate_up, v_gla_b_gate, v_gla_gnorm_w, v_gla_w_out, v_kv_norm_w, v_sb_w_kv, v_sb_w_q, v_sb_w_out, v_ffn_w_gate_up, v_ffn_w_down, v_final_norm_w):
    xi, yi, ci = _coords()
    core = ci.astype(jnp.int32).reshape(1)
    chip1 = (2 * xi + yi).astype(jnp.int32)
    chip = chip1.reshape(1)
    rank = gla_w_gate_up.shape[1]

    gu4, dn4 = _halves(ffn_w_gate_up), _halves(ffn_w_down)
    shards = dict(w_in=(_halves(gla_w_in[0]), None), w_out=(_halves(gla_w_out[0]), None), w_kv=(_halves(sb_w_kv), None),
                  w_q=(_halves(sb_w_q[0]), None), w_so=(_halves(sb_w_out[0]), None),
                  w_gu0=(gu4, 0), w_gu1=(gu4, 1), w_dn0=(dn4, 0), w_dn1=(dn4, 1))
    placed = {n: place_shard(a, chip, name=f"place_{n}", layer=l) for n, (a, l) in shards.items()}
    small_w = [gla_w_gate_up, gla_b_gate, gla_gnorm_w]
    small_all = allgather_all(_pack_rows(small_w), name="gather_gate_weights")
    wg, bg, gw = _segments(small_all[::2].reshape(N_CHIPS, -1), [a.size for a in small_w])
    wg = wg.reshape(N_CHIPS, rank, -1).transpose(1, 0, 2).reshape(rank, -1)
    p = dict(an0=attn_norm_w[0], an1=attn_norm_w[1], fn0=ffn_norm_w[0], fn1=ffn_norm_w[1], kvn=kv_norm_w, finn=final_norm_w,
             wg=jnp.pad(wg, ((0, LANES - rank), (0, 0))).astype(BF16), bg=bg.reshape(1, -1), gw=gw.reshape(1, -1),
             in_w=N_CHIPS * gla_w_in.shape[-1])

    ex = Exchanges(placed, chip, core)
    loss_row, dx, g, _ = local_step(x[0], loss_target[0], p, ex)
    loss = lax.psum(loss_row[0, 0], ("x", "y", "c"))

    tags = list(ex.mine)
    tot = dict(zip(tags, sibling_gather([ex.mine[n] for n in tags], name="grads_from_sibling")))

    vecs = [jnp.concatenate([g["an0"], g["an1"]]), jnp.concatenate([g["fn0"], g["fn1"]]), g["kvn"], g["finn"],
            g["wg"][:rank], g["bg"], g["gw"]]
    gathered_vecs = allgather_all(_pack_rows(vecs), name="gather_small_grads")
    d_an, d_fn, d_kvn, d_fin, d_wg, d_bg, d_gw = _segments(
        sum_blocks(gathered_vecs, name="sum_small_grads").reshape(-1), [t.size for t in vecs])

    def shard(t, like):
        return lax.dynamic_index_in_dim(t.reshape(-1, N_CHIPS, like.shape[-1]), chip1, axis=1, keepdims=False).reshape(like.shape)

    grads = dict(
        attn_norm_w=d_an.reshape(attn_norm_w.shape), ffn_norm_w=d_fn.reshape(ffn_norm_w.shape),
        gla_w_in=tot["w_in"].reshape(gla_w_in.shape), gla_w_gate_up=shard(d_wg, gla_w_gate_up),
        gla_b_gate=shard(d_bg, gla_b_gate), gla_gnorm_w=shard(d_gw, gla_gnorm_w),
        gla_w_out=tot["w_out"].reshape(gla_w_out.shape), kv_norm_w=d_kvn.reshape(kv_norm_w.shape),
        sb_w_kv=tot["w_kv"].reshape(sb_w_kv.shape), sb_w_q=tot["w_q"].reshape(sb_w_q.shape),
        sb_w_out=tot["w_so"].reshape(sb_w_out.shape),
        ffn_w_gate_up=jnp.stack([tot["w_gu0"], tot["w_gu1"]]).reshape(ffn_w_gate_up.shape),
        ffn_w_down=jnp.stack([tot["w_dn0"], tot["w_dn1"]]).reshape(ffn_w_down.shape),
        final_norm_w=d_fin.reshape(final_norm_w.shape))
    weights = dict(
        attn_norm_w=(attn_norm_w, m_attn_norm_w, v_attn_norm_w), ffn_norm_w=(ffn_norm_w, m_ffn_norm_w, v_ffn_norm_w),
        gla_w_in=(gla_w_in, m_gla_w_in, v_gla_w_in), gla_w_gate_up=(gla_w_gate_up, m_gla_w_gate_up, v_gla_w_gate_up),
        gla_b_gate=(gla_b_gate, m_gla_b_gate, v_gla_b_gate), gla_gnorm_w=(gla_gnorm_w, m_gla_gnorm_w, v_gla_gnorm_w),
        gla_w_out=(gla_w_out, m_gla_w_out, v_gla_w_out), kv_norm_w=(kv_norm_w, m_kv_norm_w, v_kv_norm_w),
        sb_w_kv=(sb_w_kv, m_sb_w_kv, v_sb_w_kv), sb_w_q=(sb_w_q, m_sb_w_q, v_sb_w_q), sb_w_out=(sb_w_out, m_sb_w_out, v_sb_w_out),
        ffn_w_gate_up=(ffn_w_gate_up, m_ffn_w_gate_up, v_ffn_w_gate_up), ffn_w_down=(ffn_w_down, m_ffn_w_down, v_ffn_w_down),
        final_norm_w=(final_norm_w, m_final_norm_w, v_final_norm_w))
    names = list(weights)
    stepped = [adamw(weights[n][0], grads[n], weights[n][1], weights[n][2], name=f"adamw_{n}") for n in names]
    return (loss, dx.reshape(x.shape), *[grads[n] for n in names], *[t[0] for t in stepped], *[t[1] for t in stepped],
            *[t[2] for t in stepped])
```

```python
import functools

import jax
import jax.numpy as jnp
from jax import lax
from jax.experimental import pallas as pl
from jax.experimental.pallas import tpu as pltpu

F32 = jnp.float32
BF16 = jnp.bfloat16
SDS = jax.ShapeDtypeStruct
MESH = pl.DeviceIdType.MESH

EPS = 1e-6
GLA_HEADS = 4
GLA_GATE_RANK = 16
GLA_GATE_TAU = 16.0
GLA_CHUNK = 128
SB_HEADS = 16
SB_TQ = 1024
SB_TK = 128
SB_GROUP = 4
ADAM_LR = 0.001
ADAM_B1 = 0.9
ADAM_B2 = 0.999
ADAM_EPS = 1e-08
ADAM_WD = 0.01
ADAM_STEP = 10

LANES = 128
SUBLANES = 8
N_CHIPS = 4
N_DEV = 8
VMEM_LIMIT = 56 * 1024 * 1024


def _tile(dim, target, unit=LANES):
    if dim <= target:
        return dim
    t = (target // unit) * unit
    while t >= unit:
        if dim % t == 0:
            return t
        t -= unit
    raise ValueError(f"no tile for {dim}")


def _cp(*sem):
    return pltpu.CompilerParams(dimension_semantics=sem, vmem_limit_bytes=VMEM_LIMIT)


def _sigmoid(x):
    return 1.0 / (1.0 + jnp.exp(-x))


def _dot(a, b, ca=1, cb=0):
    return lax.dot_general(a, b, (((ca,), (cb,)), ((), ())), preferred_element_type=F32)


def _split_dot(tri, x):
    hi = x.astype(BF16)
    lo = (x - hi.astype(F32)).astype(BF16)
    return _dot(tri, hi) + _dot(tri, lo)


def _hilo(x):
    hi = x.astype(BF16)
    return hi, (x - hi.astype(F32)).astype(BF16)


def _dot3(a, b, ca=1, cb=0):
    return _dot(a[0], b[0], ca, cb) + _dot(a[0], b[1], ca, cb) + _dot(a[1], b[0], ca, cb)


def _tri(n, kind):
    r = lax.broadcasted_iota(jnp.int32, (n, n), 0)
    c = lax.broadcasted_iota(jnp.int32, (n, n), 1)
    m = {"le": c <= r, "ge": c >= r, "lt": c < r, "gt": c > r}[kind]
    return jnp.where(m, 1.0, 0.0).astype(BF16)


ANY = pl.BlockSpec(memory_space=pl.ANY)


def _coords():
    return lax.axis_index("x"), lax.axis_index("y"), lax.axis_index("c")


def _other_chips(x, y):
    return [(1 - x, y), (x, 1 - y), (1 - x, 1 - y)]


def _remote(src, dst, send_sem, recv_sem, dev):
    return pltpu.make_async_remote_copy(src_ref=src, dst_ref=dst, send_sem=send_sem, recv_sem=recv_sem,
                                        device_id=dev, device_id_type=MESH)


def _dma_sems(n):
    return pltpu.SemaphoreType.DMA((n,))


class GatherX:
    def __init__(self, bufs):
        self.ins, self.ios, self.outs, self.n_sems = [], list(bufs), [], 6 * len(bufs)

    def _copy(self, ios, send_sems, recv_sems, i, k, chip, half, dev):
        blk = ios[i].at[chip, half]
        return _remote(blk, blk, send_sems.at[6 * i + k], recv_sems.at[6 * i + k], dev)

    def _first(self, ios, send_sems, recv_sems):
        x, y, c = _coords()
        return [self._copy(ios, send_sems, recv_sems, i, k, 2 * x + y, c, (cx, cy, c))
                for i in range(len(ios)) for k, (cx, cy) in enumerate(_other_chips(x, y))]

    def start(self, ins, ios, outs, send_sems, recv_sems):
        for cp in self._first(ios, send_sems, recv_sems):
            cp.start()

    def finish(self, ins, ios, outs, send_sems, recv_sems):
        x, y, c = _coords()
        chips = _other_chips(x, y)
        copy = functools.partial(self._copy, ios, send_sems, recv_sems)
        passed = []
        for i in range(len(ios)):
            for k, (cx, cy) in enumerate(chips):
                copy(i, k, 2 * cx + cy, c, (x, y, c)).wait_recv()
                passed.append(copy(i, 3 + k, 2 * cx + cy, c, (x, y, 1 - c)))
                passed[-1].start()
        for i in range(len(ios)):
            for k, (cx, cy) in enumerate(chips):
                copy(i, 3 + k, 2 * cx + cy, 1 - c, (x, y, c)).wait_recv()
        for cp in self._first(ios, send_sems, recv_sems) + passed:
            cp.wait_send()


class ScatterX:
    def __init__(self, ps):
        self.ins, self.ios, self.n_sems = list(ps), [], 3 * len(ps)
        self.outs = [SDS((N_CHIPS - 1,) + p.shape[1:], p.dtype) for p in ps]

    def _copies(self, ins, outs, send_sems, recv_sems):
        x, y, c = _coords()
        return [_remote(ins[i].at[2 * cx + cy], outs[i].at[k], send_sems.at[3 * i + k], recv_sems.at[3 * i + k], (cx, cy, c))
                for i in range(len(ins)) for k, (cx, cy) in enumerate(_other_chips(x, y))]

    def start(self, ins, ios, outs, send_sems, recv_sems):
        for cp in self._copies(ins, outs, send_sems, recv_sems):
            cp.start()

    def finish(self, ins, ios, outs, send_sems, recv_sems):
        for cp in self._copies(ins, outs, send_sems, recv_sems):
            cp.wait()


def _exchange_operands(host):
    x_in = host.ins + host.ios
    x_out = [SDS(a.shape, a.dtype) for a in host.ios] + host.outs
    return x_in, x_out


def run_exchange(host, *, name):
    x_in, x_out = _exchange_operands(host)
    n_ins, n_ios = len(host.ins), len(host.ios)

    def body(*refs):
        xin, xout = refs[:len(x_in)], refs[len(x_in):len(x_in) + len(x_out)]
        send_sems, recv_sems = refs[len(x_in) + len(x_out):]
        x_refs = (xin[:n_ins], xout[:n_ios], xout[n_ios:])
        host.start(*x_refs, send_sems, recv_sems)
        host.finish(*x_refs, send_sems, recv_sems)

    return list(pl.pallas_call(
        body, name=name, in_specs=[ANY] * len(x_in), out_specs=[ANY] * len(x_out), out_shape=x_out,
        input_output_aliases={n_ins + i: i for i in range(n_ios)},
        scratch_shapes=[_dma_sems(host.n_sems), _dma_sems(host.n_sems)])(*x_in))


def _hosted_call(body, host, *, name, grid, in_specs, out_specs, out_shape, scratch_shapes, sem, args):
    if host is None:
        return list(pl.pallas_call(body, name=name, grid=grid, in_specs=in_specs, out_specs=out_specs, out_shape=out_shape,
                                   scratch_shapes=scratch_shapes, compiler_params=_cp(*sem))(*args))
    x_in, x_out = _exchange_operands(host)
    n_in, n_out, n_scr, n_ins, n_ios = len(in_specs), len(out_specs), len(scratch_shapes), len(host.ins), len(host.ios)

    def hosted(*refs):
        ins, xin = refs[:n_in], refs[n_in:n_in + len(x_in)]
        o0 = n_in + len(x_in)
        outs, xout = refs[o0:o0 + n_out], refs[o0 + n_out:o0 + n_out + len(x_out)]
        s0 = o0 + n_out + len(x_out)
        scr, (send_sems, recv_sems) = refs[s0:s0 + n_scr], refs[s0 + n_scr:]
        ids = [pl.program_id(ax) for ax in range(len(grid))]
        first = functools.reduce(jnp.logical_and, [i == 0 for i in ids])
        last = functools.reduce(jnp.logical_and, [i == n - 1 for i, n in zip(ids, grid)])
        x_refs = (xin[:n_ins], xout[:n_ios], xout[n_ios:])

        @pl.when(first)
        def _():
            host.start(*x_refs, send_sems, recv_sems)

        body(*ins, *outs, *scr)

        @pl.when(last)
        def _():
            host.finish(*x_refs, send_sems, recv_sems)

    res = pl.pallas_call(
        hosted, name=name, grid=grid, in_specs=list(in_specs) + [ANY] * len(x_in), out_specs=list(out_specs) + [ANY] * len(x_out),
        out_shape=list(out_shape) + x_out, scratch_shapes=list(scratch_shapes) + [_dma_sems(host.n_sems), _dma_sems(host.n_sems)],
        input_output_aliases={n_in + n_ins + i: n_out + i for i in range(n_ios)},
        compiler_params=_cp(*(("arbitrary",) * len(grid))))(*args, *x_in)
    return list(res[:n_out]), list(res[n_out:])


def _div(i, n):
    return i if n == 1 else lax.div(i, n)


def _rem(i, n):
    return 0 if n == 1 else lax.rem(i, n)


class View:
    def __init__(self, arr, kind="plain", lead=(), g0=0, ng=None):
        self.arr, self.kind, self.lead, self.g0 = arr, kind, tuple(lead), g0
        self.ng = (arr.shape[0] - g0) if ng is None else ng
        r, c = arr.shape[-2:]
        self.runit, self.cunit = r, c
        self.shape = {"plain": (r, c), "cols": (r, self.ng * c), "rows": (self.ng * r, c)}[kind]

    def spec(self, br, bc, rfn, cfn):
        if self.kind == "plain":
            return pl.BlockSpec((br, bc), lambda *g: (rfn(*g), cfn(*g)))
        none = (None,) * (1 + len(self.lead))
        if self.kind == "cols":
            per = self.cunit // bc
            return pl.BlockSpec(none + (br, bc), lambda *g: (self.g0 + _div(cfn(*g), per), *self.lead, rfn(*g), _rem(cfn(*g), per)))
        per = self.runit // br
        return pl.BlockSpec(none + (br, bc), lambda *g: (self.g0 + _div(rfn(*g), per), *self.lead, _rem(rfn(*g), per), cfn(*g)))


def _as_view(a):
    return a if isinstance(a, View) else View(a)


def matmul(a, b, *, name, ta=False, tb=False, out_dtype=F32, residual=None, out_chips=None, host=None, tm=1408, tn=1408, tk=2816):
    a, b = _as_view(a), _as_view(b)
    (k, m) = a.shape if ta else a.shape[::-1]
    (n, kb) = b.shape if tb else b.shape[::-1]
    assert k == kb, (a.shape, b.shape, ta, tb)
    m_unit = a.cunit if ta else a.runit
    ka_unit = a.runit if ta else a.cunit
    n_unit = b.runit if tb else b.cunit
    kb_unit = b.cunit if tb else b.runit
    if out_chips is not None:
        n_unit = min(n_unit, n // out_chips)
    tm, tn = _tile(min(m, m_unit), tm), _tile(min(n, n_unit), tn)
    tk = _tile(min(k, ka_unit, kb_unit), tk)
    assert ka_unit % tk == 0 and kb_unit % tk == 0, (ka_unit, kb_unit, tk)
    nk = k // tk
    ca, cb = (0 if ta else 1), (1 if tb else 0)

    def body(a_ref, b_ref, *refs):
        r_ref = refs[0] if residual is not None else None
        o_ref = refs[-1] if nk == 1 else refs[-2]

        def finish(r):
            if residual is not None:
                r = r + r_ref[...]
            o_ref[...] = r.astype(out_dtype)

        part = _dot(a_ref[...].astype(BF16), b_ref[...].astype(BF16), ca, cb)
        if nk == 1:
            finish(part)
            return
        acc = refs[-1]
        kk = pl.program_id(2)

        @pl.when(kk == 0)
        def _():
            acc[...] = part

        @pl.when(kk > 0)
        def _():
            acc[...] += part

        @pl.when(kk == nk - 1)
        def _():
            finish(acc[...])

    gi, gj, gk = (lambda i, j, kk: i), (lambda i, j, kk: j), (lambda i, j, kk: kk)
    a_spec = a.spec(tk, tm, gk, gi) if ta else a.spec(tm, tk, gi, gk)
    b_spec = b.spec(tn, tk, gj, gk) if tb else b.spec(tk, tn, gk, gj)
    if out_chips is None:
        out = View(SDS((m, n), out_dtype))
    else:
        out = View(SDS((out_chips, m, n // out_chips), out_dtype), "cols")
    o_spec = out.spec(tm, tn, gi, gj)
    in_specs, args = [a_spec, b_spec], [a.arr, b.arr]
    if residual is not None:
        in_specs.append(pl.BlockSpec((tm, tn), lambda i, j, kk: (i, j)))
        args.append(residual)
    res = _hosted_call(
        body, host, name=name, grid=(m // tm, n // tn, nk), in_specs=in_specs, out_specs=[o_spec], out_shape=[out.arr],
        scratch_shapes=[] if nk == 1 else [pltpu.VMEM((tm, tn), F32)], sem=("parallel", "parallel", "arbitrary"), args=args)
    return res[0] if host is None else (res[0][0], res[1])


def rmsnorm_fwd(x, w, *, name, tr=256):
    s, d = x.shape

    def body(x_ref, w_ref, o_ref):
        xf = x_ref[...]
        r = lax.rsqrt(jnp.mean(xf * xf, axis=-1, keepdims=True) + EPS)
        o_ref[...] = (xf * r * w_ref[...]).astype(BF16)

    row = pl.BlockSpec((tr, d), lambda i: (i, 0))
    return pl.pallas_call(
        body, name=name, grid=(s // tr,), in_specs=[row, pl.BlockSpec((1, d), lambda i: (0, 0))], out_specs=row,
        out_shape=SDS((s, d), BF16), compiler_params=_cp("parallel"))(x, w.reshape(1, d))


def rmsnorm_bwd(dy, x, w, dres, *, name, tr=256):
    s, d = x.shape

    def body(dy_ref, x_ref, w_ref, dres_ref, dx_ref, dw_ref):
        i = pl.program_id(0)
        xf = x_ref[...]
        r = lax.rsqrt(jnp.mean(xf * xf, axis=-1, keepdims=True) + EPS)
        xh = xf * r
        dyf = dy_ref[...].astype(F32)
        dxh = dyf * w_ref[...]
        dx_ref[...] = dres_ref[...] + r * (dxh - xh * jnp.mean(dxh * xh, axis=-1, keepdims=True))
        part = jnp.sum(dyf * xh, axis=0, keepdims=True)

        @pl.when(i == 0)
        def _():
            dw_ref[...] = part

        @pl.when(i > 0)
        def _():
            dw_ref[...] += part

    row = pl.BlockSpec((tr, d), lambda i: (i, 0))
    vec = pl.BlockSpec((1, d), lambda i: (0, 0))
    return pl.pallas_call(
        body, name=name, grid=(s // tr,), in_specs=[row, row, vec, row], out_specs=[row, vec],
        out_shape=[SDS((s, d), F32), SDS((1, d), F32)], compiler_params=_cp("arbitrary"))(dy, x, w.reshape(1, d), dres)


def final_loss(h, w, target, *, name, tr=256):
    s, d = h.shape

    def body(h_ref, w_ref, t_ref, dh_ref, dw_ref, loss_ref):
        i = pl.program_id(0)
        xf = h_ref[...]
        r = lax.rsqrt(jnp.mean(xf * xf, axis=-1, keepdims=True) + EPS)
        xh = xf * r
        err = xh * w_ref[...] - t_ref[...]
        lpart = 0.5 * jnp.sum(jnp.sum(err * err, axis=-1, keepdims=True) * (1.0 / d), axis=0, keepdims=True)
        dy = err * (1.0 / d)
        dxh = dy * w_ref[...]
        dh_ref[...] = r * (dxh - xh * jnp.mean(dxh * xh, axis=-1, keepdims=True))
        part = jnp.sum(dy * xh, axis=0, keepdims=True)
        lrow = jnp.broadcast_to(lpart, (1, LANES))

        @pl.when(i == 0)
        def _():
            dw_ref[...] = part
            loss_ref[...] = lrow

        @pl.when(i > 0)
        def _():
            dw_ref[...] += part
            loss_ref[...] += lrow

    row = pl.BlockSpec((tr, d), lambda i: (i, 0))
    vec = pl.BlockSpec((1, d), lambda i: (0, 0))
    return pl.pallas_call(
        body, name=name, grid=(s // tr,), in_specs=[row, vec, row],
        out_specs=[row, vec, pl.BlockSpec((1, LANES), lambda i: (0, 0))],
        out_shape=[SDS((s, d), F32), SDS((1, d), F32), SDS((1, LANES), F32)],
        compiler_params=_cp("arbitrary"))(h, w.reshape(1, d), target)


def swiglu_fwd(gu, *, name, tr=512, tc=1408):
    s, f2 = gu.shape
    f = f2 // 2
    tc = _tile(f, tc)
    nf = f // tc

    def body(g_ref, u_ref, o_ref):
        g = g_ref[...].astype(F32)
        o_ref[...] = (g * _sigmoid(g) * u_ref[...].astype(F32)).astype(BF16)

    return pl.pallas_call(
        body, name=name, grid=(s // tr, nf),
        in_specs=[pl.BlockSpec((tr, tc), lambda i, j: (i, j)), pl.BlockSpec((tr, tc), lambda i, j: (i, j + nf))],
        out_specs=pl.BlockSpec((tr, tc), lambda i, j: (i, j)), out_shape=SDS((s, f), BF16),
        compiler_params=_cp("parallel", "parallel"))(gu, gu)


def swiglu_bwd(gu, dact, *, name, tr=256):
    s, f2 = gu.shape
    f = f2 // 2

    def body(gu_ref, d_ref, o_ref):
        g = gu_ref[:, :f].astype(F32)
        d = d_ref[...].astype(F32)
        sg = _sigmoid(g)
        o_ref[:, :f] = (d * gu_ref[:, f:].astype(F32) * sg * (1.0 + g * (1.0 - sg))).astype(BF16)
        o_ref[:, f:] = (d * g * sg).astype(BF16)

    return pl.pallas_call(
        body, name=name, grid=(s // tr,),
        in_specs=[pl.BlockSpec((tr, f2), lambda i: (i, 0)), pl.BlockSpec((tr, f), lambda i: (i, 0))],
        out_specs=pl.BlockSpec((tr, f2), lambda i: (i, 0)), out_shape=SDS((s, f2), BF16),
        compiler_params=_cp("parallel"))(gu, dact)


def _gate_z(gl_ref, w_ref, b_ref):
    glb = gl_ref[...].astype(BF16)
    return glb, _dot(glb, w_ref[...]) + b_ref[...]


def gate_fwd(proj, wg, bg, *, name, tr=512):
    s, inw = proj.shape
    qk = wg.shape[1]
    glc = inw // LANES - 1

    def body(gl_ref, w_ref, b_ref, g_ref):
        _, z = _gate_z(gl_ref, w_ref, b_ref)
        g_ref[...] = (jnp.minimum(z, 0.0) - jnp.log(1.0 + jnp.exp(-jnp.abs(z)))) * (1.0 / GLA_GATE_TAU)

    return pl.pallas_call(
        body, name=name, grid=(s // tr,),
        in_specs=[pl.BlockSpec((tr, LANES), lambda i: (i, glc)), pl.BlockSpec((LANES, qk), lambda i: (0, 0)),
                  pl.BlockSpec((1, qk), lambda i: (0, 0))],
        out_specs=pl.BlockSpec((tr, qk), lambda i: (i, 0)), out_shape=SDS((s, qk), F32),
        compiler_params=_cp("parallel"))(proj, wg, bg)


def gate_bwd(dg, proj, wg, bg, *, name, tr=512):
    s, inw = proj.shape
    qk = wg.shape[1]
    glc = inw // LANES - 1

    def body(dg_ref, gl_ref, w_ref, b_ref, dgl_ref, dw_ref, db_ref):
        i = pl.program_id(0)
        glb, z = _gate_z(gl_ref, w_ref, b_ref)
        dz = dg_ref[...] * (1.0 / (1.0 + jnp.exp(z))) * (1.0 / GLA_GATE_TAU)
        dzb = dz.astype(BF16)
        dgl_ref[...] = _dot(dzb, w_ref[...], 1, 1).astype(BF16)
        pw = _dot(glb, dzb, 0, 0)
        pb = jnp.sum(dz, axis=0, keepdims=True)

        @pl.when(i == 0)
        def _():
            dw_ref[...] = pw
            db_ref[...] = pb

        @pl.when(i > 0)
        def _():
            dw_ref[...] += pw
            db_ref[...] += pb

    return pl.pallas_call(
        body, name=name, grid=(s // tr,),
        in_specs=[pl.BlockSpec((tr, qk), lambda i: (i, 0)), pl.BlockSpec((tr, LANES), lambda i: (i, glc)),
                  pl.BlockSpec((LANES, qk), lambda i: (0, 0)), pl.BlockSpec((1, qk), lambda i: (0, 0))],
        out_specs=[pl.BlockSpec((tr, LANES), lambda i: (i, 0)), pl.BlockSpec((LANES, qk), lambda i: (0, 0)),
                   pl.BlockSpec((1, qk), lambda i: (0, 0))],
        out_shape=[SDS((s, LANES), BF16), SDS((LANES, qk), F32), SDS((1, qk), F32)],
        compiler_params=_cp("arbitrary"))(dg, proj, wg, bg)


def _gla_chunk_terms(q_ref, k_ref, g_ref, c, scale):
    q = q_ref[...] * scale
    k = k_ref[...]
    gg = g_ref[...]
    b = _split_dot(_tri(c, "le"), gg)
    row = lax.broadcasted_iota(jnp.int32, gg.shape, 0)
    bm = jnp.sum(jnp.where(row < c // 2, gg, 0.0), axis=0, keepdims=True)
    bl = jnp.sum(gg, axis=0, keepdims=True)
    eb, em, emi, el = jnp.exp(b), jnp.exp(b - bm), jnp.exp(bm - b), jnp.exp(bl - b)
    return q, k, bl, eb, em, emi, el


def _causal(a):
    r = lax.broadcasted_iota(jnp.int32, a.shape, 0)
    c = lax.broadcasted_iota(jnp.int32, a.shape, 1)
    return jnp.where(r >= c, a, 0.0)


def gla_fwd(proj, g, *, name, host=None, c=GLA_CHUNK):
    s = proj.shape[0]
    qk = g.shape[1]
    dk, dv = qk // GLA_HEADS, 2 * qk // GLA_HEADS
    nc = s // c
    scale = dk ** -0.5
    kq = qk // dk

    def body(q_ref, k_ref, v_ref, g_ref, o_ref, st_ref, state):
        @pl.when(pl.program_id(1) == 0)
        def _():
            state[...] = jnp.zeros_like(state)

        q, k, bl, eb, em, emi, el = _gla_chunk_terms(q_ref, k_ref, g_ref, c, scale)
        v2 = _hilo(v_ref[...])
        st = state[...]
        st_ref[...] = st
        a = _causal(_dot3(_hilo(q * em), _hilo(k * emi), 1, 1))
        o_ref[...] = _dot3(_hilo(q * eb), _hilo(st), 1, 1) + _dot3(_hilo(a), v2)
        state[...] = st * jnp.exp(bl) + _dot3(v2, _hilo(k * el), 0, 0)

    return _hosted_call(
        body, host, name=name, grid=(GLA_HEADS, nc),
        in_specs=[pl.BlockSpec((c, dk), lambda h, i: (i, h)), pl.BlockSpec((c, dk), lambda h, i: (i, kq + h)),
                  pl.BlockSpec((c, dv), lambda h, i: (i, kq + h)), pl.BlockSpec((c, dk), lambda h, i: (i, h))],
        out_specs=[pl.BlockSpec((c, dv), lambda h, i: (i, h)),
                   pl.BlockSpec((None, None, dv, dk), lambda h, i: (h, i, 0, 0))],
        out_shape=[SDS((s, 2 * qk), F32), SDS((GLA_HEADS, nc, dv, dk), F32)],
        scratch_shapes=[pltpu.VMEM((dv, dk), F32)], sem=("parallel", "arbitrary"), args=(proj, proj, proj, g))


def gla_bwd(proj, g, states, do, *, name, host=None, c=GLA_CHUNK):
    s = proj.shape[0]
    qk = g.shape[1]
    dk, dv = qk // GLA_HEADS, 2 * qk // GLA_HEADS
    nc = s // c
    scale = dk ** -0.5
    kq = qk // dk

    def body(q_ref, k_ref, v_ref, g_ref, do_ref, st_ref, dq_ref, dk_ref, dv_ref, dg_ref, dstate, dgc):
        @pl.when(pl.program_id(1) == 0)
        def _():
            dstate[...] = jnp.zeros_like(dstate)
            dgc[...] = jnp.zeros_like(dgc)

        q, k, bl, eb, em, emi, el = _gla_chunk_terms(q_ref, k_ref, g_ref, c, scale)
        v2, do2 = _hilo(v_ref[...]), _hilo(do_ref[...])
        qe, qm, km, kd = _hilo(q * eb), _hilo(q * em), _hilo(k * emi), _hilo(k * el)
        ds = dstate[...]
        ds2 = _hilo(ds)
        a = _hilo(_causal(_dot3(qm, km, 1, 1)))
        dv_ref[...] = (_dot3(a, do2, 0, 0) + _dot3(kd, ds2, 1, 1)).astype(BF16)
        da = _hilo(_causal(_dot3(do2, v2, 1, 1)))
        dq = _dot3(da, km) * em + _dot3(do2, _hilo(st_ref[...])) * eb
        dkk = _dot3(da, qm, 0, 0) * emi + _dot3(v2, ds2) * el
        dstate[...] = ds * jnp.exp(bl) + _dot3(do2, qe, 0, 0)
        db = q * dq - k * dkk
        dg_ref[...] = _split_dot(_tri(c, "ge"), db) + dgc[...]
        dgc[...] += jnp.sum(db, axis=0, keepdims=True)
        dq_ref[...] = (dq * scale).astype(BF16)
        dk_ref[...] = dkk.astype(BF16)

    rev = lambda i: nc - 1 - i
    qspec = pl.BlockSpec((c, dk), lambda h, i: (rev(i), h))
    vspec = pl.BlockSpec((c, dv), lambda h, i: (rev(i), h))
    return _hosted_call(
        body, host, name=name, grid=(GLA_HEADS, nc),
        in_specs=[qspec, pl.BlockSpec((c, dk), lambda h, i: (rev(i), kq + h)),
                  pl.BlockSpec((c, dv), lambda h, i: (rev(i), kq + h)), qspec, vspec,
                  pl.BlockSpec((None, None, dv, dk), lambda h, i: (h, rev(i), 0, 0))],
        out_specs=[qspec, qspec, vspec, qspec],
        out_shape=[SDS((s, qk), BF16), SDS((s, qk), BF16), SDS((s, 2 * qk), BF16), SDS((s, qk), F32)],
        scratch_shapes=[pltpu.VMEM((dv, dk), F32), pltpu.VMEM((1, dk), F32)], sem=("parallel", "arbitrary"),
        args=(proj, proj, proj, g, do, states))


def gnorm_fwd(o, proj, gw, *, name, tr=512):
    s, v = o.shape
    dv = v // GLA_HEADS
    roff = 2 * GLA_HEADS

    def body(o_ref, r_ref, w_ref, y_ref):
        of = o_ref[...]
        rs = lax.rsqrt(jnp.mean(of * of, axis=-1, keepdims=True) + EPS)
        r = r_ref[...]
        y_ref[...] = (of * rs * w_ref[...] * (r * _sigmoid(r))).astype(BF16)

    blk = pl.BlockSpec((tr, dv), lambda i, h: (i, h))
    return pl.pallas_call(
        body, name=name, grid=(s // tr, GLA_HEADS),
        in_specs=[blk, pl.BlockSpec((tr, dv), lambda i, h: (i, roff + h)), pl.BlockSpec((1, dv), lambda i, h: (0, 0))],
        out_specs=blk, out_shape=SDS((s, v), BF16), compiler_params=_cp("parallel", "parallel"))(o, proj, gw)


def gnorm_bwd(dy, o, proj, gw, *, name, tr=512):
    s, v = o.shape
    dv = v // GLA_HEADS
    roff = 2 * GLA_HEADS

    def body(dy_ref, o_ref, r_ref, w_ref, do_ref, dr_ref, dw_ref):
        first = jnp.logical_and(pl.program_id(0) == 0, pl.program_id(1) == 0)
        of = o_ref[...]
        rs = lax.rsqrt(jnp.mean(of * of, axis=-1, keepdims=True) + EPS)
        n = of * rs
        r = r_ref[...]
        sg = _sigmoid(r)
        dyf = dy_ref[...].astype(F32)
        dn_w = dyf * (r * sg)
        dr_ref[...] = (dyf * n * w_ref[...] * sg * (1.0 + r * (1.0 - sg))).astype(BF16)
        dn = dn_w * w_ref[...]
        do_ref[...] = rs * (dn - n * jnp.mean(dn * n, axis=-1, keepdims=True))
        part = jnp.sum(dn_w * n, axis=0, keepdims=True)

        @pl.when(first)
        def _():
            dw_ref[...] = part

        @pl.when(jnp.logical_not(first))
        def _():
            dw_ref[...] += part

    blk = pl.BlockSpec((tr, dv), lambda i, h: (i, h))
    vec = pl.BlockSpec((1, dv), lambda i, h: (0, 0))
    return pl.pallas_call(
        body, name=name, grid=(s // tr, GLA_HEADS),
        in_specs=[blk, blk, pl.BlockSpec((tr, dv), lambda i, h: (i, roff + h)), vec],
        out_specs=[blk, blk, vec], out_shape=[SDS((s, v), F32), SDS((s, v), BF16), SDS((1, dv), F32)],
        compiler_params=_cp("arbitrary", "arbitrary"))(dy, o, proj, gw)


def _sb_block(kblk, q, ks, q0, scale, carry, masked):
    tk, tq = kblk.shape[0], q.shape[0]
    z = _dot(kblk, q, 1, 1) * scale
    sp = jnp.maximum(z, 0.0) + jnp.log(1.0 + jnp.exp(-jnp.abs(z)))
    mask = None
    lf = -sp
    if masked:
        kpos = ks + lax.broadcasted_iota(jnp.int32, (tk, tq), 0)
        qpos = q0 + lax.broadcasted_iota(jnp.int32, (tk, tq), 1)
        mask = kpos < qpos
        lf = jnp.where(mask, lf, 0.0)
    later = _split_dot(_tri(tk, "gt"), lf)
    a = jnp.exp(z - sp + later + carry)
    if masked:
        a = jnp.where(mask, a, 0.0)
    return z, sp, mask, lf, a


def sb_fwd(q, kv, *, name, host=None, tq=SB_TQ, tk=SB_TK, group=SB_GROUP):
    s, w = q.shape
    hd = w // SB_HEADS
    nq, nkb = s // tq, s // tk
    scale = hd ** -0.5
    per = tq // tk
    assert per % group == 0

    def body(q_ref, k_ref, v_ref, o_ref, car_ref, o_acc):
        qi = pl.program_id(1)
        qb = q_ref[...]
        q0 = qi * tq
        car_ref[...] = jnp.zeros_like(car_ref)
        o_acc[...] = jnp.zeros_like(o_acc)

        def blocks(first, carry, skip=None):
            for t in reversed(range(group)):
                kj = first + t
                ks = pl.multiple_of(kj * tk, tk)
                car_ref[pl.ds(kj, 1), :] = carry
                lo = 0 if skip is None else (skip + t) * tk
                _, _, _, lf, a = _sb_block(k_ref[pl.ds(ks, tk), :], qb[lo:, :], ks, q0 + lo, scale, carry[:, lo:], skip is not None)
                o_acc[lo:, :] += _dot(a.astype(BF16), v_ref[pl.ds(ks, tk), :], 0, 0)
                add = jnp.sum(lf, axis=0, keepdims=True)
                carry = carry + (add if lo == 0 else jnp.concatenate([jnp.zeros((1, lo), F32), add], axis=1))
            return carry

        carry = jnp.zeros((1, tq), F32)
        for gidx in reversed(range(per // group)):
            carry = blocks(qi * per + gidx * group, carry, gidx * group)
        n_in = qi * (per // group)
        lax.fori_loop(0, n_in, lambda i, c: blocks((n_in - 1 - i) * group, c), carry)
        o_ref[...] = o_acc[...].astype(BF16)

    qspec = pl.BlockSpec((tq, hd), lambda h, i: (i, h))
    return _hosted_call(
        body, host, name=name, grid=(SB_HEADS, nq),
        in_specs=[qspec, pl.BlockSpec((None, s, hd), lambda h, i: (0, 0, h)), pl.BlockSpec((None, s, hd), lambda h, i: (1, 0, h))],
        out_specs=[qspec, pl.BlockSpec((None, None, nkb, tq), lambda h, i: (h, i, 0, 0))],
        out_shape=[SDS((s, w), BF16), SDS((SB_HEADS, nq, nkb, tq), F32)],
        scratch_shapes=[pltpu.VMEM((tq, hd), F32)], sem=("parallel", "parallel"), args=(q, kv, kv))


def sb_bwd(q, kv, do, car, *, name, host=None, tq=SB_TQ, tk=SB_TK, group=SB_GROUP):
    s, w = q.shape
    hd = w // SB_HEADS
    nq, nkb = s // tq, s // tk
    scale = hd ** -0.5
    per = tq // tk
    assert per % group == 0

    def body(q_ref, k_ref, v_ref, do_ref, car_ref, dq_ref, dkv_ref, dq_acc, dk_acc, dv_acc):
        qi = pl.program_id(1)
        qb = q_ref[...]
        dob = do_ref[...]
        q0 = qi * tq
        dq_acc[...] = jnp.zeros_like(dq_acc)

        @pl.when(qi == 0)
        def _():
            dk_acc[...] = jnp.zeros_like(dk_acc)
            dv_acc[...] = jnp.zeros_like(dv_acc)

        def blocks(first, pcar, skip=None):
            for t in range(group):
                kj = first + t
                ks = pl.multiple_of(kj * tk, tk)
                kblk = k_ref[pl.ds(ks, tk), :]
                lo = 0 if skip is None else (skip + t) * tk
                masked = skip is not None
                qs, dos = qb[lo:, :], dob[lo:, :]
                z, sp, mask, _, a = _sb_block(kblk, qs, ks, q0 + lo, scale, car_ref[pl.ds(kj, 1), :][:, lo:], masked)
                p = a * _dot(v_ref[pl.ds(ks, tk), :], dos, 1, 1)
                before = _split_dot(_tri(tk, "lt"), p)
                sg = jnp.exp(z - sp)
                dz = p * (1.0 - sg) - (pcar[:, lo:] + before) * sg
                if masked:
                    dz = jnp.where(mask, dz, 0.0)
                dz = (dz * scale).astype(BF16)
                dk_acc[pl.ds(ks, tk), :] += _dot(dz, qs)
                dv_acc[pl.ds(ks, tk), :] += _dot(a.astype(BF16), dos)
                dq_acc[lo:, :] += _dot(dz, kblk, 0, 0)
                add = jnp.sum(p, axis=0, keepdims=True)
                pcar = pcar + (add if lo == 0 else jnp.concatenate([jnp.zeros((1, lo), F32), add], axis=1))
            return pcar

        pcar = lax.fori_loop(0, qi * (per // group), lambda i, c: blocks(i * group, c), jnp.zeros((1, tq), F32))
        for gidx in range(per // group):
            pcar = blocks(qi * per + gidx * group, pcar, gidx * group)
        dq_ref[...] = dq_acc[...].astype(BF16)

        @pl.when(qi == nq - 1)
        def _():
            dkv_ref[0] = dk_acc[...].astype(BF16)
            dkv_ref[1] = dv_acc[...].astype(BF16)

    qspec = pl.BlockSpec((tq, hd), lambda h, i: (i, h))
    return _hosted_call(
        body, host, name=name, grid=(SB_HEADS, nq),
        in_specs=[qspec, pl.BlockSpec((None, s, hd), lambda h, i: (0, 0, h)), pl.BlockSpec((None, s, hd), lambda h, i: (1, 0, h)),
                  qspec, pl.BlockSpec((None, None, nkb, tq), lambda h, i: (h, i, 0, 0))],
        out_specs=[qspec, pl.BlockSpec((2, s, hd), lambda h, i: (0, 0, h))],
        out_shape=[SDS((s, w), BF16), SDS((2, s, w), BF16)],
        scratch_shapes=[pltpu.VMEM((tq, hd), F32), pltpu.VMEM((s, hd), F32), pltpu.VMEM((s, hd), F32)],
        sem=("parallel", "arbitrary"), args=(q, kv, kv, do, car))


def adamw(w, g, m, v, *, name, host=None):
    shape = w.shape
    c = shape[-1]
    r = w.size // c
    tr = _tile(r, max(8, (3 * LANES * 1024) // c), unit=8) if r >= 8 else r

    def body(w_ref, g_ref, m_ref, v_ref, d_ref, nm_ref, nv_ref):
        gf = g_ref[...]
        mn = ADAM_B1 * m_ref[...] + (1.0 - ADAM_B1) * gf
        vn = ADAM_B2 * v_ref[...] + (1.0 - ADAM_B2) * (gf * gf)
        m_hat = mn / (1.0 - ADAM_B1 ** ADAM_STEP)
        v_hat = vn / (1.0 - ADAM_B2 ** ADAM_STEP)
        d_ref[...] = -ADAM_LR * (m_hat / (jnp.sqrt(v_hat) + ADAM_EPS) + ADAM_WD * w_ref[...])
        nm_ref[...] = mn
        nv_ref[...] = vn

    blk = pl.BlockSpec((tr, c), lambda i: (i, 0))
    res = _hosted_call(
        body, host, name=name, grid=(r // tr,), in_specs=[blk] * 4, out_specs=[blk] * 3, out_shape=[SDS((r, c), F32)] * 3,
        scratch_shapes=[], sem=("parallel",), args=tuple(t.reshape(r, c) for t in (w, g, m, v)))
    outs, got = (res, None) if host is None else res
    outs = tuple(o.reshape(shape) for o in outs)
    return outs if host is None else (outs, got)


def _row_tile(r, c):
    return _tile(r, max(16, (4 * LANES * 1024) // c), unit=16)


def _halves(a):
    return a.reshape(a.shape[:-2] + (2, a.shape[-2] // 2, a.shape[-1]))


def place_shard(w, chip, *, name, layer=None):
    r, c = w.shape[-2:]
    tr = _row_tile(r, c)

    def body(c_ref, w_ref, o_ref):
        o_ref[...] = w_ref[...].astype(BF16)

    if layer is None:
        w_spec = pl.BlockSpec((None, tr, c), lambda h, i, c_ref: (h, i, 0))
    else:
        w_spec = pl.BlockSpec((None, None, tr, c), lambda h, i, c_ref: (layer, h, i, 0))
    return pl.pallas_call(
        body, name=name, out_shape=SDS((N_CHIPS, 2, r, c), BF16),
        grid_spec=pltpu.PrefetchScalarGridSpec(
            num_scalar_prefetch=1, grid=(2, r // tr), in_specs=[w_spec],
            out_specs=pl.BlockSpec((None, None, tr, c), lambda h, i, c_ref: (c_ref[0], h, i, 0))),
        compiler_params=_cp("parallel", "parallel"))(chip, w)


def sibling_exchange(gs, *, name):
    n = len(gs)

    def body(*refs):
        g_refs, a_refs = refs[:n], refs[n:2 * n]
        send_sems, recv_sems = refs[2 * n:]
        x, y, c = _coords()
        cps = [_remote(g_refs[i].at[:, 1 - c], a_refs[i], send_sems.at[i], recv_sems.at[i], (x, y, 1 - c)) for i in range(n)]
        for cp in cps:
            cp.start()
        for cp in cps:
            cp.wait()

    return pl.pallas_call(
        body, name=name, in_specs=[ANY] * n, out_specs=[ANY] * n,
        out_shape=[SDS(g.shape[:1] + g.shape[2:], g.dtype) for g in gs],
        scratch_shapes=[_dma_sems(n), _dma_sems(n)])(*gs)


def half_add(g, a, core, *, name):
    n, _, r, c = g.shape
    tr = _row_tile(r, c)

    def body(c_ref, g_ref, a_ref, o_ref):
        o_ref[...] = (g_ref[...].astype(F32) + a_ref[...].astype(F32)).astype(o_ref.dtype)

    blk = pl.BlockSpec((None, tr, c), lambda s, i, c_ref: (s, i, 0))
    return pl.pallas_call(
        body, name=name, out_shape=SDS((n, r, c), g.dtype),
        grid_spec=pltpu.PrefetchScalarGridSpec(
            num_scalar_prefetch=1, grid=(n, r // tr),
            in_specs=[pl.BlockSpec((None, None, tr, c), lambda s, i, c_ref: (s, c_ref[0], i, 0)), blk], out_specs=blk),
        compiler_params=_cp("parallel", "parallel"))(core, g, a)


def chip_sum(p, b, chip, core, *, name):
    _, r, c = p.shape
    tr = _row_tile(r, c)

    def body(chip_ref, core_ref, p_ref, b_ref, o_ref):
        t = p_ref[...].astype(F32)
        for k in range(N_CHIPS - 1):
            t = t + b_ref[k].astype(F32)
        o_ref[...] = t

    return pl.pallas_call(
        body, name=name, out_shape=SDS((2, r, c), F32),
        grid_spec=pltpu.PrefetchScalarGridSpec(
            num_scalar_prefetch=2, grid=(r // tr,),
            in_specs=[pl.BlockSpec((None, tr, c), lambda i, chip_ref, core_ref: (chip_ref[0], i, 0)),
                      pl.BlockSpec((N_CHIPS - 1, tr, c), lambda i, chip_ref, core_ref: (0, i, 0))],
            out_specs=pl.BlockSpec((None, tr, c), lambda i, chip_ref, core_ref: (core_ref[0], i, 0))),
        compiler_params=_cp("parallel"))(chip, core, p, b)


def sibling_gather(ts, *, name):
    n = len(ts)

    def body(*refs):
        out_refs = refs[n:2 * n]
        send_sems, recv_sems = refs[2 * n:]
        x, y, c = _coords()
        cps = [_remote(out_refs[i].at[c], out_refs[i].at[c], send_sems.at[i], recv_sems.at[i], (x, y, 1 - c)) for i in range(n)]
        for cp in cps:
            cp.start()
        for i in range(n):
            _remote(out_refs[i].at[c], out_refs[i].at[1 - c], send_sems.at[i], recv_sems.at[i], (x, y, 1 - c)).wait_recv()
        for cp in cps:
            cp.wait_send()

    return pl.pallas_call(
        body, name=name, in_specs=[ANY] * n, out_specs=[ANY] * n, out_shape=[SDS(t.shape, t.dtype) for t in ts],
        input_output_aliases={i: i for i in range(n)}, scratch_shapes=[_dma_sems(n), _dma_sems(n)])(*ts)


def allgather_all(sm, *, name):
    r, w = sm.shape

    def body(s_ref, out_ref, send_sems, recv_sems, local_sem):
        x, y, c = _coords()
        me = 4 * x + 2 * y + c

        def peer(rel):
            flip = lambda v, bit: 1 - v if bit else v
            return flip(x, rel & 4), flip(y, rel & 2), flip(c, rel & 1)

        loc = pltpu.make_async_copy(s_ref, out_ref.at[me], local_sem.at[0])
        loc.start()
        cps = [_remote(s_ref, out_ref.at[me], send_sems.at[rel - 1], recv_sems.at[rel - 1], peer(rel)) for rel in range(1, N_DEV)]
        for cp in cps:
            cp.start()
        for rel in range(1, N_DEV):
            px, py, pc = peer(rel)
            _remote(s_ref, out_ref.at[4 * px + 2 * py + pc], send_sems.at[rel - 1], recv_sems.at[rel - 1], (px, py, pc)).wait_recv()
        for cp in cps:
            cp.wait_send()
        loc.wait()

    return pl.pallas_call(
        body, name=name, in_specs=[ANY], out_specs=ANY, out_shape=SDS((N_DEV, r, w), sm.dtype),
        scratch_shapes=[_dma_sems(N_DEV - 1), _dma_sems(N_DEV - 1), _dma_sems(1)])(sm)


def sum_blocks(a, *, name):
    n, r, w = a.shape

    def body(a_ref, o_ref):
        t = a_ref[0]
        for k in range(1, n):
            t = t + a_ref[k]
        o_ref[...] = t

    return pl.pallas_call(body, name=name, out_shape=SDS((r, w), F32))(a)


class Exchanges:
    def __init__(self, placed=None, chip=None, core=None):
        self.placed, self.chip, self.core = placed, chip, core
        self.mine = {}
        self.pending = {}

    def gather(self, names):
        return None if self.placed is None else GatherX([self.placed[n] for n in names])

    def reduce(self, grads, call):
        if self.placed is None:
            return None if call is None else call(None)
        names = list(grads)
        from_sibling = sibling_exchange([grads[n] for n in names], name="grads_to_sibling_" + names[0])
        pairs = [half_add(grads[n], a, self.core, name=f"pair_sum_{n}") for n, a in zip(names, from_sibling)]
        if call is None:
            self.pending.update(zip(names, pairs))
            return None
        return self.scatter(names, pairs, call)

    def scatter(self, names, pairs, call):
        result, from_chips = call(ScatterX(pairs))
        for n, t, b in zip(names, pairs, from_chips):
            self.mine[n] = chip_sum(t, b, self.chip, self.core, name=f"chip_sum_{n}")
        return result


def _usable(name, buf, d):
    if name == "w_in":
        cols = N_CHIPS * buf.shape[-1]
        full = buf.reshape(N_CHIPS, d, -1).transpose(1, 0, 2).reshape(d, cols)
        return jnp.pad(full, ((0, 0), (0, -cols % LANES)))
    if name in ("w_out", "w_q", "w_so"):
        return buf.reshape(-1, buf.shape[-1])
    return buf.reshape(N_CHIPS, -1, buf.shape[-1])


def _by_rows(t):
    return _halves(t.reshape(N_CHIPS, -1, t.shape[-1]))


def _ffn_fwd(h, nw, w_gu, w_dn, tag, host=None):
    f = rmsnorm_fwd(h, nw, name=f"{tag}_norm")
    gu = matmul(f, View(w_gu, "cols"), name=f"{tag}_gate_up", out_dtype=BF16, host=host)
    gu, got = gu if host is not None else (gu, None)
    act = swiglu_fwd(gu, name=f"{tag}_act")
    return matmul(act, View(w_dn, "rows"), name=f"{tag}_down", residual=h), (f, gu, act), got


def _ffn_bwd(dh, h, nw, w_gu, w_dn, saved, tag, ex, carried):
    f, gu, act = saved
    dact = matmul(dh, View(w_dn, "rows"), name=f"{tag}_dact", tb=True, out_dtype=BF16)
    dw_dn = matmul(act, dh, name=f"{tag}_dw_down", ta=True, out_dtype=BF16)
    dgu = swiglu_bwd(gu, dact, name=f"{tag}_dgu")

    def dw_gate_up(host):
        return matmul(f, dgu, name=f"{tag}_dw_gate_up", ta=True, out_dtype=BF16, out_chips=N_CHIPS, host=host)

    dw_gu = ex.reduce(carried, dw_gate_up) if carried else dw_gate_up(None)
    df = matmul(dgu, View(w_gu, "cols"), name=f"{tag}_df", tb=True)
    dh_in, dnw = rmsnorm_bwd(df, h, nw, dh, name=f"{tag}_dnorm")
    return dh_in, dnw, dw_gu, dw_dn


def local_step(x, target, p, ex):
    d = x.shape[1]
    hosted = ex.placed is not None
    w = {} if hosted else {n: _usable(n, p[n], d) for n in ("w_in", "w_out", "w_kv", "w_q", "w_so", "w_gu0", "w_gu1", "w_dn0", "w_dn1")}

    def take(names, got):
        for n, buf in zip(names, got or []):
            w[n] = _usable(n, buf, d)

    def carry(call, names):
        host = ex.gather(names)
        if host is None:
            return call(None)
        res, got = call(host)
        take(names, got)
        return res

    if hosted:
        take(["w_in", "w_out"], run_exchange(ex.gather(["w_in", "w_out"]), name="gather_gla"))
    a0 = rmsnorm_fwd(x, p["an0"], name="l0_attn_norm")
    proj = carry(lambda host: matmul(a0, w["w_in"], name="gla_in", host=host), ["w_dn0"])
    g = gate_fwd(proj, p["wg"], p["bg"], name="gla_gate")
    o, states = carry(lambda host: gla_fwd(proj, g, name="gla_scan", host=host), ["w_gu0"])
    og = gnorm_fwd(o, proj, p["gw"], name="gla_outnorm")
    h1 = matmul(og, w["w_out"], name="gla_out", residual=x)
    sb_names = ["w_kv", "w_q", "w_so"]
    h2, ffn0, got = _ffn_fwd(h1, p["fn0"], w["w_gu0"], w["w_dn0"], "ffn0", host=ex.gather(sb_names))
    take(sb_names, got)
    w_kv = View(w["w_kv"], "cols")
    kvn = rmsnorm_fwd(h2, p["kvn"], name="kv_norm")
    kv = matmul(kvn, w_kv, name="sb_kv", out_dtype=BF16, out_chips=2)
    a1 = rmsnorm_fwd(h2, p["an1"], name="l1_attn_norm")
    q2 = matmul(a1, w["w_q"], name="sb_q", out_dtype=BF16)
    o2, car = carry(lambda host: sb_fwd(q2, kv, name="sb_attn", host=host), ["w_gu1", "w_dn1"])
    h3 = matmul(o2, w["w_so"], name="sb_out", residual=h2)
    h4, ffn1, _ = _ffn_fwd(h3, p["fn1"], w["w_gu1"], w["w_dn1"], "ffn1")
    dh4, d_fin, loss_row = final_loss(h4, p["finn"], target, name="final_loss")

    dh3, d_fn1, dw_gu1, dw_dn1 = _ffn_bwd(dh4, h3, p["fn1"], w["w_gu1"], w["w_dn1"], ffn1, "ffn1", ex, {})
    do2 = matmul(dh3, w["w_so"], name="sb_do", tb=True, out_dtype=BF16)
    dw_so = matmul(o2, dh3, name="sb_dw_out", ta=True, out_dtype=BF16)
    dq2, dkv = ex.reduce({"w_gu1": _halves(dw_gu1), "w_dn1": _by_rows(dw_dn1)},
                         lambda host: sb_bwd(q2, kv, do2, car, name="sb_attn_bwd", host=host))
    dkv = View(dkv, "cols")
    dw_q = matmul(a1, dq2, name="sb_dw_q", ta=True, out_dtype=BF16)
    da1 = matmul(dq2, w["w_q"], name="sb_da", tb=True)
    dh2, d_an1 = rmsnorm_bwd(da1, h2, p["an1"], dh3, name="l1_attn_dnorm")
    dw_kv = matmul(kvn, dkv, name="sb_dw_kv", ta=True, out_dtype=BF16, out_chips=N_CHIPS)
    dkvn = matmul(dkv, w_kv, name="sb_dkvn", tb=True)
    dh2, d_kvn = rmsnorm_bwd(dkvn, h2, p["kvn"], dh2, name="kv_dnorm")
    sb_grads = {"w_kv": _halves(dw_kv), "w_q": _by_rows(dw_q), "w_so": _by_rows(dw_so)}
    dh1, d_fn0, dw_gu0, dw_dn0 = _ffn_bwd(dh2, h1, p["fn0"], w["w_gu0"], w["w_dn0"], ffn0, "ffn0", ex, sb_grads)
    dog = matmul(dh1, w["w_out"], name="gla_dog", tb=True, out_dtype=BF16)
    dw_out = matmul(og, dh1, name="gla_dw_out", ta=True, out_dtype=BF16)
    do, dr, d_gw = gnorm_bwd(dog, o, proj, p["gw"], name="gla_outnorm_bwd")
    dq, dk, dv, dg = ex.reduce({"w_gu0": _halves(dw_gu0), "w_dn0": _by_rows(dw_dn0)},
                               lambda host: gla_bwd(proj, g, states, do, name="gla_scan_bwd", host=host))
    dgl, d_wg, d_bg = gate_bwd(dg, proj, p["wg"], p["bg"], name="gla_gate_bwd")
    dproj = jnp.concatenate([dq, dk, dv, dr, dgl], axis=1)
    dw_in = matmul(a0, dproj, name="gla_dw_in", ta=True, out_dtype=BF16)
    da0 = matmul(dproj, w["w_in"], name="gla_da", tb=True)
    dx, d_an0 = rmsnorm_bwd(da0, x, p["an0"], dh1, name="l0_attn_dnorm")
    in_w = p["in_w"]
    gla_grads = {"w_in": _halves(dw_in[:, :in_w].reshape(d, N_CHIPS, -1).transpose(1, 0, 2)), "w_out": _by_rows(dw_out)}
    ex.reduce(gla_grads, None)

    small = dict(an0=d_an0, an1=d_an1, fn0=d_fn0, fn1=d_fn1, kvn=d_kvn, finn=d_fin, wg=d_wg, bg=d_bg, gw=d_gw)
    big = {}
    if not hosted:
        big = dict(gla_grads, **sb_grads, w_gu0=_halves(dw_gu0), w_gu1=_halves(dw_gu1), w_dn0=_by_rows(dw_dn0), w_dn1=_by_rows(dw_dn1))
    return loss_row, dx, small, big


def _pack_rows(parts):
    cat = jnp.concatenate([t.reshape(-1) for t in parts])
    rows = -(-cat.size // (LANES * SUBLANES)) * SUBLANES
    return jnp.pad(cat, (0, rows * LANES - cat.size)).reshape(rows, LANES)


def _segments(flat, sizes):
    out, off = [], 0
    for n in sizes:
        out.append(flat[..., off:off + n])
        off += n
    return out


def kernel(x, attn_norm_w, ffn_norm_w, gla_w_in, gla_w_gate_up, gla_b_gate, gla_gnorm_w, gla_w_out, kv_norm_w, sb_w_kv, sb_w_q, sb_w_out, ffn_w_gate_up, ffn_w_down, final_norm_w, loss_target, m_attn_norm_w, m_ffn_norm_w, m_gla_w_in, m_gla_w_gate_up, m_gla_b_gate, m_gla_gnorm_w, m_gla_w_out, m_kv_norm_w, m_sb_w_kv, m_sb_w_q, m_sb_w_out, m_ffn_w_gate_up, m_ffn_w_down, m_final_norm_w, v_attn_norm_w, v_ffn_norm_w, v_gla_w_in, v_gla_w_gate_up, v_gla_b_gate, v_gla_gnorm_w, v_gla_w_out, v_kv_norm_w, v_sb_w_kv, v_sb_w_q, v_sb_w_out, v_ffn_w_gate_up, v_ffn_w_down, v_final_norm_w):
    xi, yi, ci = _coords()
    core = ci.astype(jnp.int32).reshape(1)
    chip1 = (2 * xi + yi).astype(jnp.int32)
    chip = chip1.reshape(1)
    rank = gla_w_gate_up.shape[1]

    gu4, dn4 = _halves(ffn_w_gate_up), _halves(ffn_w_down)
    shards = dict(w_in=(_halves(gla_w_in[0]), None), w_out=(_halves(gla_w_out[0]), None), w_kv=(_halves(sb_w_kv), None),
                  w_q=(_halves(sb_w_q[0]), None), w_so=(_halves(sb_w_out[0]), None),
                  w_gu0=(gu4, 0), w_gu1=(gu4, 1), w_dn0=(dn4, 0), w_dn1=(dn4, 1))
    placed = {n: place_shard(a, chip, name=f"place_{n}", layer=l) for n, (a, l) in shards.items()}
    small_w = [gla_w_gate_up, gla_b_gate, gla_gnorm_w]
    small_all = allgather_all(_pack_rows(small_w), name="gather_gate_weights")
    wg, bg, gw = _segments(small_all[::2].reshape(N_CHIPS, -1), [a.size for a in small_w])
    wg = wg.reshape(N_CHIPS, rank, -1).transpose(1, 0, 2).reshape(rank, -1)
    p = dict(an0=attn_norm_w[0], an1=attn_norm_w[1], fn0=ffn_norm_w[0], fn1=ffn_norm_w[1], kvn=kv_norm_w, finn=final_norm_w,
             wg=jnp.pad(wg, ((0, LANES - rank), (0, 0))).astype(BF16), bg=bg.reshape(1, -1), gw=gw.reshape(1, -1),
             in_w=N_CHIPS * gla_w_in.shape[-1])

    ex = Exchanges(placed, chip, core)
    loss_row, dx, g, _ = local_step(x[0], loss_target[0], p, ex)
    loss = lax.psum(loss_row[0, 0], ("x", "y", "c"))

    tags = list(ex.mine)
    tot = dict(zip(tags, sibling_gather([ex.mine[n] for n in tags], name="grads_from_sibling")))

    vecs = [jnp.concatenate([g["an0"], g["an1"]]), jnp.concatenate([g["fn0"], g["fn1"]]), g["kvn"], g["finn"],
            g["wg"][:rank], g["bg"], g["gw"]]
    gathered_vecs = allgather_all(_pack_rows(vecs), name="gather_small_grads")
    d_an, d_fn, d_kvn, d_fin, d_wg, d_bg, d_gw = _segments(
        sum_blocks(gathered_vecs, name="sum_small_grads").reshape(-1), [t.size for t in vecs])

    def shard(t, like):
        return lax.dynamic_index_in_dim(t.reshape(-1, N_CHIPS, like.shape[-1]), chip1, axis=1, keepdims=False).reshape(like.shape)

    weights = dict(
        attn_norm_w=(attn_norm_w, m_attn_norm_w, v_attn_norm_w), ffn_norm_w=(ffn_norm_w, m_ffn_norm_w, v_ffn_norm_w),
        gla_w_in=(gla_w_in, m_gla_w_in, v_gla_w_in), gla_w_gate_up=(gla_w_gate_up, m_gla_w_gate_up, v_gla_w_gate_up),
        gla_b_gate=(gla_b_gate, m_gla_b_gate, v_gla_b_gate), gla_gnorm_w=(gla_gnorm_w, m_gla_gnorm_w, v_gla_gnorm_w),
        gla_w_out=(gla_w_out, m_gla_w_out, v_gla_w_out), kv_norm_w=(kv_norm_w, m_kv_norm_w, v_kv_norm_w),
        sb_w_kv=(sb_w_kv, m_sb_w_kv, v_sb_w_kv), sb_w_q=(sb_w_q, m_sb_w_q, v_sb_w_q), sb_w_out=(sb_w_out, m_sb_w_out, v_sb_w_out),
        ffn_w_gate_up=(ffn_w_gate_up, m_ffn_w_gate_up, v_ffn_w_gate_up), ffn_w_down=(ffn_w_down, m_ffn_w_down, v_ffn_w_down),
        final_norm_w=(final_norm_w, m_final_norm_w, v_final_norm_w))
    grads = dict(
        attn_norm_w=d_an.reshape(attn_norm_w.shape), ffn_norm_w=d_fn.reshape(ffn_norm_w.shape),
        gla_w_gate_up=shard(d_wg, gla_w_gate_up), gla_b_gate=shard(d_bg, gla_b_gate), gla_gnorm_w=shard(d_gw, gla_gnorm_w),
        kv_norm_w=d_kvn.reshape(kv_norm_w.shape), sb_w_kv=tot["w_kv"].reshape(sb_w_kv.shape),
        sb_w_q=tot["w_q"].reshape(sb_w_q.shape), sb_w_out=tot["w_so"].reshape(sb_w_out.shape),
        ffn_w_gate_up=jnp.stack([tot["w_gu0"], tot["w_gu1"]]).reshape(ffn_w_gate_up.shape),
        ffn_w_down=jnp.stack([tot["w_dn0"], tot["w_dn1"]]).reshape(ffn_w_down.shape),
        final_norm_w=d_fin.reshape(final_norm_w.shape))

    def step(n, host=None):
        return adamw(weights[n][0], grads[n], weights[n][1], weights[n][2], name=f"adamw_{n}", host=host)

    stepped = {}
    for host_name, n in (("ffn_w_gate_up", "w_in"), ("ffn_w_down", "w_out")):
        stepped[host_name] = ex.scatter([n], [ex.pending[n]], lambda host: step(host_name, host))
    late = sibling_gather([ex.mine["w_in"], ex.mine["w_out"]], name="grads_from_sibling_gla")
    grads["gla_w_in"], grads["gla_w_out"] = late[0].reshape(gla_w_in.shape), late[1].reshape(gla_w_out.shape)
    names = list(weights)
    for n in names:
        if n not in stepped:
            stepped[n] = step(n)
    return (loss, dx.reshape(x.shape), *[grads[n] for n in names], *[stepped[n][0] for n in names],
            *[stepped[n][1] for n in names], *[stepped[n][2] for n in names])
```

```python
import functools

import jax
import jax.numpy as jnp
from jax import lax
from jax.experimental import pallas as pl
from jax.experimental.pallas import tpu as pltpu

F32 = jnp.float32
BF16 = jnp.bfloat16
SDS = jax.ShapeDtypeStruct
MESH = pl.DeviceIdType.MESH

EPS = 1e-6
GLA_HEADS = 4
GLA_GATE_RANK = 16
GLA_GATE_TAU = 16.0
GLA_CHUNK = 128
SB_HEADS = 16
SB_TQ = 1024
SB_TK = 128
SB_GROUP = 4
SB_DEAD = -110.0
SB_SKIPPED = -1e30
ADAM_LR = 0.001
ADAM_B1 = 0.9
ADAM_B2 = 0.999
ADAM_EPS = 1e-08
ADAM_WD = 0.01
ADAM_STEP = 10

LANES = 128
SUBLANES = 8
N_CHIPS = 4
N_DEV = 8
VMEM_LIMIT = 56 * 1024 * 1024


def _tile(dim, target, unit=LANES):
    if dim <= target:
        return dim
    t = (target // unit) * unit
    while t >= unit:
        if dim % t == 0:
            return t
        t -= unit
    raise ValueError(f"no tile for {dim}")


def _cp(*sem):
    return pltpu.CompilerParams(dimension_semantics=sem, vmem_limit_bytes=VMEM_LIMIT)


def _sigmoid(x):
    return 1.0 / (1.0 + jnp.exp(-x))


def _dot(a, b, ca=1, cb=0):
    return lax.dot_general(a, b, (((ca,), (cb,)), ((), ())), preferred_element_type=F32)


def _split_dot(tri, x):
    hi = x.astype(BF16)
    lo = (x - hi.astype(F32)).astype(BF16)
    return _dot(tri, hi) + _dot(tri, lo)


def _hilo(x):
    hi = x.astype(BF16)
    return hi, (x - hi.astype(F32)).astype(BF16)


def _dot3(a, b, ca=1, cb=0):
    return _dot(a[0], b[0], ca, cb) + _dot(a[0], b[1], ca, cb) + _dot(a[1], b[0], ca, cb)


def _tri(n, kind):
    r = lax.broadcasted_iota(jnp.int32, (n, n), 0)
    c = lax.broadcasted_iota(jnp.int32, (n, n), 1)
    m = {"le": c <= r, "ge": c >= r, "lt": c < r, "gt": c > r}[kind]
    return jnp.where(m, 1.0, 0.0).astype(BF16)


ANY = pl.BlockSpec(memory_space=pl.ANY)


def _coords():
    return lax.axis_index("x"), lax.axis_index("y"), lax.axis_index("c")


def _other_chips(x, y):
    return [(1 - x, y), (x, 1 - y), (1 - x, 1 - y)]


def _remote(src, dst, send_sem, recv_sem, dev):
    return pltpu.make_async_remote_copy(src_ref=src, dst_ref=dst, send_sem=send_sem, recv_sem=recv_sem,
                                        device_id=dev, device_id_type=MESH)


def _dma_sems(n):
    return pltpu.SemaphoreType.DMA((n,))


class GatherX:
    def __init__(self, bufs):
        self.ins, self.ios, self.outs, self.n_sems = [], list(bufs), [], 6 * len(bufs)

    def _copy(self, ios, send_sems, recv_sems, i, k, chip, half, dev):
        blk = ios[i].at[chip, half]
        return _remote(blk, blk, send_sems.at[6 * i + k], recv_sems.at[6 * i + k], dev)

    def _first(self, ios, send_sems, recv_sems):
        x, y, c = _coords()
        return [self._copy(ios, send_sems, recv_sems, i, k, 2 * x + y, c, (cx, cy, c))
                for i in range(len(ios)) for k, (cx, cy) in enumerate(_other_chips(x, y))]

    def start(self, ins, ios, outs, send_sems, recv_sems):
        for cp in self._first(ios, send_sems, recv_sems):
            cp.start()

    def finish(self, ins, ios, outs, send_sems, recv_sems):
        x, y, c = _coords()
        chips = _other_chips(x, y)
        copy = functools.partial(self._copy, ios, send_sems, recv_sems)
        passed = []
        for i in range(len(ios)):
            for k, (cx, cy) in enumerate(chips):
                copy(i, k, 2 * cx + cy, c, (x, y, c)).wait_recv()
                passed.append(copy(i, 3 + k, 2 * cx + cy, c, (x, y, 1 - c)))
                passed[-1].start()
        for i in range(len(ios)):
            for k, (cx, cy) in enumerate(chips):
                copy(i, 3 + k, 2 * cx + cy, 1 - c, (x, y, c)).wait_recv()
        for cp in self._first(ios, send_sems, recv_sems) + passed:
            cp.wait_send()


class ScatterX:
    def __init__(self, ps):
        self.ins, self.ios, self.n_sems = list(ps), [], 3 * len(ps)
        self.outs = [SDS((N_CHIPS - 1,) + p.shape[1:], p.dtype) for p in ps]

    def _copies(self, ins, outs, send_sems, recv_sems):
        x, y, c = _coords()
        return [_remote(ins[i].at[2 * cx + cy], outs[i].at[k], send_sems.at[3 * i + k], recv_sems.at[3 * i + k], (cx, cy, c))
                for i in range(len(ins)) for k, (cx, cy) in enumerate(_other_chips(x, y))]

    def start(self, ins, ios, outs, send_sems, recv_sems):
        for cp in self._copies(ins, outs, send_sems, recv_sems):
            cp.start()

    def finish(self, ins, ios, outs, send_sems, recv_sems):
        for cp in self._copies(ins, outs, send_sems, recv_sems):
            cp.wait()


def _exchange_operands(host):
    x_in = host.ins + host.ios
    x_out = [SDS(a.shape, a.dtype) for a in host.ios] + host.outs
    return x_in, x_out


def run_exchange(host, *, name):
    x_in, x_out = _exchange_operands(host)
    n_ins, n_ios = len(host.ins), len(host.ios)

    def body(*refs):
        xin, xout = refs[:len(x_in)], refs[len(x_in):len(x_in) + len(x_out)]
        send_sems, recv_sems = refs[len(x_in) + len(x_out):]
        x_refs = (xin[:n_ins], xout[:n_ios], xout[n_ios:])
        host.start(*x_refs, send_sems, recv_sems)
        host.finish(*x_refs, send_sems, recv_sems)

    return list(pl.pallas_call(
        body, name=name, in_specs=[ANY] * len(x_in), out_specs=[ANY] * len(x_out), out_shape=x_out,
        input_output_aliases={n_ins + i: i for i in range(n_ios)},
        scratch_shapes=[_dma_sems(host.n_sems), _dma_sems(host.n_sems)])(*x_in))


def _hosted_call(body, host, *, name, grid, in_specs, out_specs, out_shape, scratch_shapes, sem, args):
    if host is None:
        return list(pl.pallas_call(body, name=name, grid=grid, in_specs=in_specs, out_specs=out_specs, out_shape=out_shape,
                                   scratch_shapes=scratch_shapes, compiler_params=_cp(*sem))(*args))
    x_in, x_out = _exchange_operands(host)
    n_in, n_out, n_scr, n_ins, n_ios = len(in_specs), len(out_specs), len(scratch_shapes), len(host.ins), len(host.ios)

    def hosted(*refs):
        ins, xin = refs[:n_in], refs[n_in:n_in + len(x_in)]
        o0 = n_in + len(x_in)
        outs, xout = refs[o0:o0 + n_out], refs[o0 + n_out:o0 + n_out + len(x_out)]
        s0 = o0 + n_out + len(x_out)
        scr, (send_sems, recv_sems) = refs[s0:s0 + n_scr], refs[s0 + n_scr:]
        ids = [pl.program_id(ax) for ax in range(len(grid))]
        first = functools.reduce(jnp.logical_and, [i == 0 for i in ids])
        last = functools.reduce(jnp.logical_and, [i == n - 1 for i, n in zip(ids, grid)])
        x_refs = (xin[:n_ins], xout[:n_ios], xout[n_ios:])

        @pl.when(first)
        def _():
            host.start(*x_refs, send_sems, recv_sems)

        body(*ins, *outs, *scr)

        @pl.when(last)
        def _():
            host.finish(*x_refs, send_sems, recv_sems)

    res = pl.pallas_call(
        hosted, name=name, grid=grid, in_specs=list(in_specs) + [ANY] * len(x_in), out_specs=list(out_specs) + [ANY] * len(x_out),
        out_shape=list(out_shape) + x_out, scratch_shapes=list(scratch_shapes) + [_dma_sems(host.n_sems), _dma_sems(host.n_sems)],
        input_output_aliases={n_in + n_ins + i: n_out + i for i in range(n_ios)},
        compiler_params=_cp(*(("arbitrary",) * len(grid))))(*args, *x_in)
    return list(res[:n_out]), list(res[n_out:])


def _div(i, n):
    return i if n == 1 else lax.div(i, n)


def _rem(i, n):
    return 0 if n == 1 else lax.rem(i, n)


class View:
    def __init__(self, arr, kind="plain", lead=(), g0=0, ng=None):
        self.arr, self.kind, self.lead, self.g0 = arr, kind, tuple(lead), g0
        self.ng = (arr.shape[0] - g0) if ng is None else ng
        r, c = arr.shape[-2:]
        self.runit, self.cunit = r, c
        self.shape = {"plain": (r, c), "cols": (r, self.ng * c), "rows": (self.ng * r, c)}[kind]

    def spec(self, br, bc, rfn, cfn):
        if self.kind == "plain":
            return pl.BlockSpec((br, bc), lambda *g: (rfn(*g), cfn(*g)))
        none = (None,) * (1 + len(self.lead))
        if self.kind == "cols":
            per = self.cunit // bc
            return pl.BlockSpec(none + (br, bc), lambda *g: (self.g0 + _div(cfn(*g), per), *self.lead, rfn(*g), _rem(cfn(*g), per)))
        per = self.runit // br
        return pl.BlockSpec(none + (br, bc), lambda *g: (self.g0 + _div(rfn(*g), per), *self.lead, _rem(rfn(*g), per), cfn(*g)))


def _as_view(a):
    return a if isinstance(a, View) else View(a)


def matmul(a, b, *, name, ta=False, tb=False, out_dtype=F32, residual=None, out_chips=None, host=None, tm=1408, tn=1408, tk=2816):
    a, b = _as_view(a), _as_view(b)
    (k, m) = a.shape if ta else a.shape[::-1]
    (n, kb) = b.shape if tb else b.shape[::-1]
    assert k == kb, (a.shape, b.shape, ta, tb)
    m_unit = a.cunit if ta else a.runit
    ka_unit = a.runit if ta else a.cunit
    n_unit = b.runit if tb else b.cunit
    kb_unit = b.cunit if tb else b.runit
    if out_chips is not None:
        n_unit = min(n_unit, n // out_chips)
    tm, tn = _tile(min(m, m_unit), tm), _tile(min(n, n_unit), tn)
    tk = _tile(min(k, ka_unit, kb_unit), tk)
    assert ka_unit % tk == 0 and kb_unit % tk == 0, (ka_unit, kb_unit, tk)
    nk = k // tk
    ca, cb = (0 if ta else 1), (1 if tb else 0)

    def body(a_ref, b_ref, *refs):
        r_ref = refs[0] if residual is not None else None
        o_ref = refs[-1] if nk == 1 else refs[-2]

        def finish(r):
            if residual is not None:
                r = r + r_ref[...]
            o_ref[...] = r.astype(out_dtype)

        part = _dot(a_ref[...].astype(BF16), b_ref[...].astype(BF16), ca, cb)
        if nk == 1:
            finish(part)
            return
        acc = refs[-1]
        kk = pl.program_id(2)

        @pl.when(kk == 0)
        def _():
            acc[...] = part

        @pl.when(kk > 0)
        def _():
            acc[...] += part

        @pl.when(kk == nk - 1)
        def _():
            finish(acc[...])

    gi, gj, gk = (lambda i, j, kk: i), (lambda i, j, kk: j), (lambda i, j, kk: kk)
    a_spec = a.spec(tk, tm, gk, gi) if ta else a.spec(tm, tk, gi, gk)
    b_spec = b.spec(tn, tk, gj, gk) if tb else b.spec(tk, tn, gk, gj)
    if out_chips is None:
        out = View(SDS((m, n), out_dtype))
    else:
        out = View(SDS((out_chips, m, n // out_chips), out_dtype), "cols")
    o_spec = out.spec(tm, tn, gi, gj)
    in_specs, args = [a_spec, b_spec], [a.arr, b.arr]
    if residual is not None:
        in_specs.append(pl.BlockSpec((tm, tn), lambda i, j, kk: (i, j)))
        args.append(residual)
    res = _hosted_call(
        body, host, name=name, grid=(m // tm, n // tn, nk), in_specs=in_specs, out_specs=[o_spec], out_shape=[out.arr],
        scratch_shapes=[] if nk == 1 else [pltpu.VMEM((tm, tn), F32)], sem=("parallel", "parallel", "arbitrary"), args=args)
    return res[0] if host is None else (res[0][0], res[1])


def rmsnorm_fwd(x, w, *, name, tr=256):
    s, d = x.shape

    def body(x_ref, w_ref, o_ref):
        xf = x_ref[...]
        r = lax.rsqrt(jnp.mean(xf * xf, axis=-1, keepdims=True) + EPS)
        o_ref[...] = (xf * r * w_ref[...]).astype(BF16)

    row = pl.BlockSpec((tr, d), lambda i: (i, 0))
    return pl.pallas_call(
        body, name=name, grid=(s // tr,), in_specs=[row, pl.BlockSpec((1, d), lambda i: (0, 0))], out_specs=row,
        out_shape=SDS((s, d), BF16), compiler_params=_cp("parallel"))(x, w.reshape(1, d))


def rmsnorm_bwd(dy, x, w, dres, *, name, tr=256):
    s, d = x.shape

    def body(dy_ref, x_ref, w_ref, dres_ref, dx_ref, dw_ref):
        i = pl.program_id(0)
        xf = x_ref[...]
        r = lax.rsqrt(jnp.mean(xf * xf, axis=-1, keepdims=True) + EPS)
        xh = xf * r
        dyf = dy_ref[...].astype(F32)
        dxh = dyf * w_ref[...]
        dx_ref[...] = dres_ref[...] + r * (dxh - xh * jnp.mean(dxh * xh, axis=-1, keepdims=True))
        part = jnp.sum(dyf * xh, axis=0, keepdims=True)

        @pl.when(i == 0)
        def _():
            dw_ref[...] = part

        @pl.when(i > 0)
        def _():
            dw_ref[...] += part

    row = pl.BlockSpec((tr, d), lambda i: (i, 0))
    vec = pl.BlockSpec((1, d), lambda i: (0, 0))
    return pl.pallas_call(
        body, name=name, grid=(s // tr,), in_specs=[row, row, vec, row], out_specs=[row, vec],
        out_shape=[SDS((s, d), F32), SDS((1, d), F32)], compiler_params=_cp("arbitrary"))(dy, x, w.reshape(1, d), dres)


def final_loss(h, w, target, *, name, tr=256):
    s, d = h.shape

    def body(h_ref, w_ref, t_ref, dh_ref, dw_ref, loss_ref):
        i = pl.program_id(0)
        xf = h_ref[...]
        r = lax.rsqrt(jnp.mean(xf * xf, axis=-1, keepdims=True) + EPS)
        xh = xf * r
        err = xh * w_ref[...] - t_ref[...]
        lpart = 0.5 * jnp.sum(jnp.sum(err * err, axis=-1, keepdims=True) * (1.0 / d), axis=0, keepdims=True)
        dy = err * (1.0 / d)
        dxh = dy * w_ref[...]
        dh_ref[...] = r * (dxh - xh * jnp.mean(dxh * xh, axis=-1, keepdims=True))
        part = jnp.sum(dy * xh, axis=0, keepdims=True)
        lrow = jnp.broadcast_to(lpart, (1, LANES))

        @pl.when(i == 0)
        def _():
            dw_ref[...] = part
            loss_ref[...] = lrow

        @pl.when(i > 0)
        def _():
            dw_ref[...] += part
            loss_ref[...] += lrow

    row = pl.BlockSpec((tr, d), lambda i: (i, 0))
    vec = pl.BlockSpec((1, d), lambda i: (0, 0))
    return pl.pallas_call(
        body, name=name, grid=(s // tr,), in_specs=[row, vec, row],
        out_specs=[row, vec, pl.BlockSpec((1, LANES), lambda i: (0, 0))],
        out_shape=[SDS((s, d), F32), SDS((1, d), F32), SDS((1, LANES), F32)],
        compiler_params=_cp("arbitrary"))(h, w.reshape(1, d), target)


def swiglu_fwd(gu, *, name, tr=512, tc=1408):
    s, f2 = gu.shape
    f = f2 // 2
    tc = _tile(f, tc)
    nf = f // tc

    def body(g_ref, u_ref, o_ref):
        g = g_ref[...].astype(F32)
        o_ref[...] = (g * _sigmoid(g) * u_ref[...].astype(F32)).astype(BF16)

    return pl.pallas_call(
        body, name=name, grid=(s // tr, nf),
        in_specs=[pl.BlockSpec((tr, tc), lambda i, j: (i, j)), pl.BlockSpec((tr, tc), lambda i, j: (i, j + nf))],
        out_specs=pl.BlockSpec((tr, tc), lambda i, j: (i, j)), out_shape=SDS((s, f), BF16),
        compiler_params=_cp("parallel", "parallel"))(gu, gu)


def swiglu_bwd(gu, dact, *, name, tr=256):
    s, f2 = gu.shape
    f = f2 // 2

    def body(gu_ref, d_ref, o_ref):
        g = gu_ref[:, :f].astype(F32)
        d = d_ref[...].astype(F32)
        sg = _sigmoid(g)
        o_ref[:, :f] = (d * gu_ref[:, f:].astype(F32) * sg * (1.0 + g * (1.0 - sg))).astype(BF16)
        o_ref[:, f:] = (d * g * sg).astype(BF16)

    return pl.pallas_call(
        body, name=name, grid=(s // tr,),
        in_specs=[pl.BlockSpec((tr, f2), lambda i: (i, 0)), pl.BlockSpec((tr, f), lambda i: (i, 0))],
        out_specs=pl.BlockSpec((tr, f2), lambda i: (i, 0)), out_shape=SDS((s, f2), BF16),
        compiler_params=_cp("parallel"))(gu, dact)


def _gate_z(gl_ref, w_ref, b_ref):
    glb = gl_ref[...].astype(BF16)
    return glb, _dot(glb, w_ref[...]) + b_ref[...]


def gate_fwd(proj, wg, bg, *, name, tr=512):
    s, inw = proj.shape
    qk = wg.shape[1]
    glc = inw // LANES - 1

    def body(gl_ref, w_ref, b_ref, g_ref):
        _, z = _gate_z(gl_ref, w_ref, b_ref)
        g_ref[...] = (jnp.minimum(z, 0.0) - jnp.log(1.0 + jnp.exp(-jnp.abs(z)))) * (1.0 / GLA_GATE_TAU)

    return pl.pallas_call(
        body, name=name, grid=(s // tr,),
        in_specs=[pl.BlockSpec((tr, LANES), lambda i: (i, glc)), pl.BlockSpec((LANES, qk), lambda i: (0, 0)),
                  pl.BlockSpec((1, qk), lambda i: (0, 0))],
        out_specs=pl.BlockSpec((tr, qk), lambda i: (i, 0)), out_shape=SDS((s, qk), F32),
        compiler_params=_cp("parallel"))(proj, wg, bg)


def gate_bwd(dg, proj, wg, bg, *, name, tr=512):
    s, inw = proj.shape
    qk = wg.shape[1]
    glc = inw // LANES - 1

    def body(dg_ref, gl_ref, w_ref, b_ref, dgl_ref, dw_ref, db_ref):
        i = pl.program_id(0)
        glb, z = _gate_z(gl_ref, w_ref, b_ref)
        dz = dg_ref[...] * (1.0 / (1.0 + jnp.exp(z))) * (1.0 / GLA_GATE_TAU)
        dzb = dz.astype(BF16)
        dgl_ref[...] = _dot(dzb, w_ref[...], 1, 1).astype(BF16)
        pw = _dot(glb, dzb, 0, 0)
        pb = jnp.sum(dz, axis=0, keepdims=True)

        @pl.when(i == 0)
        def _():
            dw_ref[...] = pw
            db_ref[...] = pb

        @pl.when(i > 0)
        def _():
            dw_ref[...] += pw
            db_ref[...] += pb

    return pl.pallas_call(
        body, name=name, grid=(s // tr,),
        in_specs=[pl.BlockSpec((tr, qk), lambda i: (i, 0)), pl.BlockSpec((tr, LANES), lambda i: (i, glc)),
                  pl.BlockSpec((LANES, qk), lambda i: (0, 0)), pl.BlockSpec((1, qk), lambda i: (0, 0))],
        out_specs=[pl.BlockSpec((tr, LANES), lambda i: (i, 0)), pl.BlockSpec((LANES, qk), lambda i: (0, 0)),
                   pl.BlockSpec((1, qk), lambda i: (0, 0))],
        out_shape=[SDS((s, LANES), BF16), SDS((LANES, qk), F32), SDS((1, qk), F32)],
        compiler_params=_cp("arbitrary"))(dg, proj, wg, bg)


def _gla_chunk_terms(q_ref, k_ref, g_ref, c, scale):
    q = q_ref[...] * scale
    k = k_ref[...]
    gg = g_ref[...]
    b = _split_dot(_tri(c, "le"), gg)
    row = lax.broadcasted_iota(jnp.int32, gg.shape, 0)
    bm = jnp.sum(jnp.where(row < c // 2, gg, 0.0), axis=0, keepdims=True)
    bl = jnp.sum(gg, axis=0, keepdims=True)
    eb, em, emi, el = jnp.exp(b), jnp.exp(b - bm), jnp.exp(bm - b), jnp.exp(bl - b)
    return q, k, bl, eb, em, emi, el


def _causal(a):
    r = lax.broadcasted_iota(jnp.int32, a.shape, 0)
    c = lax.broadcasted_iota(jnp.int32, a.shape, 1)
    return jnp.where(r >= c, a, 0.0)


def gla_fwd(proj, g, *, name, host=None, c=GLA_CHUNK):
    s = proj.shape[0]
    qk = g.shape[1]
    dk, dv = qk // GLA_HEADS, 2 * qk // GLA_HEADS
    nc = s // c
    scale = dk ** -0.5
    kq = qk // dk

    def body(q_ref, k_ref, v_ref, g_ref, o_ref, st_ref, state):
        @pl.when(pl.program_id(1) == 0)
        def _():
            state[...] = jnp.zeros_like(state)

        q, k, bl, eb, em, emi, el = _gla_chunk_terms(q_ref, k_ref, g_ref, c, scale)
        v2 = _hilo(v_ref[...])
        st = state[...]
        st_ref[...] = st
        a = _causal(_dot3(_hilo(q * em), _hilo(k * emi), 1, 1))
        o_ref[...] = _dot3(_hilo(q * eb), _hilo(st), 1, 1) + _dot3(_hilo(a), v2)
        state[...] = st * jnp.exp(bl) + _dot3(v2, _hilo(k * el), 0, 0)

    return _hosted_call(
        body, host, name=name, grid=(GLA_HEADS, nc),
        in_specs=[pl.BlockSpec((c, dk), lambda h, i: (i, h)), pl.BlockSpec((c, dk), lambda h, i: (i, kq + h)),
                  pl.BlockSpec((c, dv), lambda h, i: (i, kq + h)), pl.BlockSpec((c, dk), lambda h, i: (i, h))],
        out_specs=[pl.BlockSpec((c, dv), lambda h, i: (i, h)),
                   pl.BlockSpec((None, None, dv, dk), lambda h, i: (h, i, 0, 0))],
        out_shape=[SDS((s, 2 * qk), F32), SDS((GLA_HEADS, nc, dv, dk), F32)],
        scratch_shapes=[pltpu.VMEM((dv, dk), F32)], sem=("parallel", "arbitrary"), args=(proj, proj, proj, g))


def gla_bwd(proj, g, states, do, *, name, host=None, c=GLA_CHUNK):
    s = proj.shape[0]
    qk = g.shape[1]
    dk, dv = qk // GLA_HEADS, 2 * qk // GLA_HEADS
    nc = s // c
    scale = dk ** -0.5
    kq = qk // dk

    def body(q_ref, k_ref, v_ref, g_ref, do_ref, st_ref, dq_ref, dk_ref, dv_ref, dg_ref, dstate, dgc):
        @pl.when(pl.program_id(1) == 0)
        def _():
            dstate[...] = jnp.zeros_like(dstate)
            dgc[...] = jnp.zeros_like(dgc)

        q, k, bl, eb, em, emi, el = _gla_chunk_terms(q_ref, k_ref, g_ref, c, scale)
        v2, do2 = _hilo(v_ref[...]), _hilo(do_ref[...])
        qe, qm, km, kd = _hilo(q * eb), _hilo(q * em), _hilo(k * emi), _hilo(k * el)
        ds = dstate[...]
        ds2 = _hilo(ds)
        a = _hilo(_causal(_dot3(qm, km, 1, 1)))
        dv_ref[...] = (_dot3(a, do2, 0, 0) + _dot3(kd, ds2, 1, 1)).astype(BF16)
        da = _hilo(_causal(_dot3(do2, v2, 1, 1)))
        dq = _dot3(da, km) * em + _dot3(do2, _hilo(st_ref[...])) * eb
        dkk = _dot3(da, qm, 0, 0) * emi + _dot3(v2, ds2) * el
        dstate[...] = ds * jnp.exp(bl) + _dot3(do2, qe, 0, 0)
        db = q * dq - k * dkk
        dg_ref[...] = _split_dot(_tri(c, "ge"), db) + dgc[...]
        dgc[...] += jnp.sum(db, axis=0, keepdims=True)
        dq_ref[...] = (dq * scale).astype(BF16)
        dk_ref[...] = dkk.astype(BF16)

    rev = lambda i: nc - 1 - i
    qspec = pl.BlockSpec((c, dk), lambda h, i: (rev(i), h))
    vspec = pl.BlockSpec((c, dv), lambda h, i: (rev(i), h))
    return _hosted_call(
        body, host, name=name, grid=(GLA_HEADS, nc),
        in_specs=[qspec, pl.BlockSpec((c, dk), lambda h, i: (rev(i), kq + h)),
                  pl.BlockSpec((c, dv), lambda h, i: (rev(i), kq + h)), qspec, vspec,
                  pl.BlockSpec((None, None, dv, dk), lambda h, i: (h, rev(i), 0, 0))],
        out_specs=[qspec, qspec, vspec, qspec],
        out_shape=[SDS((s, qk), BF16), SDS((s, qk), BF16), SDS((s, 2 * qk), BF16), SDS((s, qk), F32)],
        scratch_shapes=[pltpu.VMEM((dv, dk), F32), pltpu.VMEM((1, dk), F32)], sem=("parallel", "arbitrary"),
        args=(proj, proj, proj, g, do, states))


def gnorm_fwd(o, proj, gw, *, name, tr=512):
    s, v = o.shape
    dv = v // GLA_HEADS
    roff = 2 * GLA_HEADS

    def body(o_ref, r_ref, w_ref, y_ref):
        of = o_ref[...]
        rs = lax.rsqrt(jnp.mean(of * of, axis=-1, keepdims=True) + EPS)
        r = r_ref[...]
        y_ref[...] = (of * rs * w_ref[...] * (r * _sigmoid(r))).astype(BF16)

    blk = pl.BlockSpec((tr, dv), lambda i, h: (i, h))
    return pl.pallas_call(
        body, name=name, grid=(s // tr, GLA_HEADS),
        in_specs=[blk, pl.BlockSpec((tr, dv), lambda i, h: (i, roff + h)), pl.BlockSpec((1, dv), lambda i, h: (0, 0))],
        out_specs=blk, out_shape=SDS((s, v), BF16), compiler_params=_cp("parallel", "parallel"))(o, proj, gw)


def gnorm_bwd(dy, o, proj, gw, *, name, tr=512):
    s, v = o.shape
    dv = v // GLA_HEADS
    roff = 2 * GLA_HEADS

    def body(dy_ref, o_ref, r_ref, w_ref, do_ref, dr_ref, dw_ref):
        first = jnp.logical_and(pl.program_id(0) == 0, pl.program_id(1) == 0)
        of = o_ref[...]
        rs = lax.rsqrt(jnp.mean(of * of, axis=-1, keepdims=True) + EPS)
        n = of * rs
        r = r_ref[...]
        sg = _sigmoid(r)
        dyf = dy_ref[...].astype(F32)
        dn_w = dyf * (r * sg)
        dr_ref[...] = (dyf * n * w_ref[...] * sg * (1.0 + r * (1.0 - sg))).astype(BF16)
        dn = dn_w * w_ref[...]
        do_ref[...] = rs * (dn - n * jnp.mean(dn * n, axis=-1, keepdims=True))
        part = jnp.sum(dn_w * n, axis=0, keepdims=True)

        @pl.when(first)
        def _():
            dw_ref[...] = part

        @pl.when(jnp.logical_not(first))
        def _():
            dw_ref[...] += part

    blk = pl.BlockSpec((tr, dv), lambda i, h: (i, h))
    vec = pl.BlockSpec((1, dv), lambda i, h: (0, 0))
    return pl.pallas_call(
        body, name=name, grid=(s // tr, GLA_HEADS),
        in_specs=[blk, blk, pl.BlockSpec((tr, dv), lambda i, h: (i, roff + h)), vec],
        out_specs=[blk, blk, vec], out_shape=[SDS((s, v), F32), SDS((s, v), BF16), SDS((1, dv), F32)],
        compiler_params=_cp("arbitrary", "arbitrary"))(dy, o, proj, gw)


def _sb_block(kblk, q, ks, q0, scale, carry, masked):
    tk, tq = kblk.shape[0], q.shape[0]
    z = _dot(kblk, q, 1, 1) * scale
    sp = jnp.maximum(z, 0.0) + jnp.log(1.0 + jnp.exp(-jnp.abs(z)))
    mask = None
    lf = -sp
    if masked:
        kpos = ks + lax.broadcasted_iota(jnp.int32, (tk, tq), 0)
        qpos = q0 + lax.broadcasted_iota(jnp.int32, (tk, tq), 1)
        mask = kpos < qpos
        lf = jnp.where(mask, lf, 0.0)
    later = _split_dot(_tri(tk, "gt"), lf)
    a = jnp.exp(z - sp + later + carry)
    if masked:
        a = jnp.where(mask, a, 0.0)
    return z, sp, mask, lf, a


def sb_fwd(q, kv, *, name, host=None, tq=SB_TQ, tk=SB_TK, group=SB_GROUP):
    s, w = q.shape
    hd = w // SB_HEADS
    nq, nkb = s // tq, s // tk
    scale = hd ** -0.5
    per = tq // tk
    assert per % group == 0

    def body(q_ref, k_ref, v_ref, o_ref, car_ref, o_acc):
        qi = pl.program_id(1)
        qb = q_ref[...]
        q0 = qi * tq
        car_ref[...] = jnp.full(car_ref.shape, SB_SKIPPED, F32)
        o_acc[...] = jnp.zeros_like(o_acc)

        def blocks(first, carry, skip=None):
            for t in reversed(range(group)):
                kj = first + t
                ks = pl.multiple_of(kj * tk, tk)
                car_ref[pl.ds(kj, 1), :] = carry
                lo = 0 if skip is None else (skip + t) * tk
                _, _, _, lf, a = _sb_block(k_ref[pl.ds(ks, tk), :], qb[lo:, :], ks, q0 + lo, scale, carry[:, lo:], skip is not None)
                o_acc[lo:, :] += _dot(a.astype(BF16), v_ref[pl.ds(ks, tk), :], 0, 0)
                add = jnp.sum(lf, axis=0, keepdims=True)
                carry = carry + (add if lo == 0 else jnp.concatenate([jnp.zeros((1, lo), F32), add], axis=1))
            return carry

        carry = jnp.zeros((1, tq), F32)
        for gidx in reversed(range(per // group)):
            carry = blocks(qi * per + gidx * group, carry, gidx * group)
        n_in = qi * (per // group)
        lax.while_loop(lambda st: jnp.logical_and(st[0] < n_in, jnp.max(st[1]) > SB_DEAD),
                       lambda st: (st[0] + 1, blocks((n_in - 1 - st[0]) * group, st[1])), (jnp.int32(0), carry))
        o_ref[...] = o_acc[...].astype(BF16)

    qspec = pl.BlockSpec((tq, hd), lambda h, i: (i, h))
    return _hosted_call(
        body, host, name=name, grid=(SB_HEADS, nq),
        in_specs=[qspec, pl.BlockSpec((None, s, hd), lambda h, i: (0, 0, h)), pl.BlockSpec((None, s, hd), lambda h, i: (1, 0, h))],
        out_specs=[qspec, pl.BlockSpec((None, None, nkb, tq), lambda h, i: (h, i, 0, 0))],
        out_shape=[SDS((s, w), BF16), SDS((SB_HEADS, nq, nkb, tq), F32)],
        scratch_shapes=[pltpu.VMEM((tq, hd), F32)], sem=("parallel", "parallel"), args=(q, kv, kv))


def sb_bwd(q, kv, do, car, *, name, host=None, tq=SB_TQ, tk=SB_TK, group=SB_GROUP):
    s, w = q.shape
    hd = w // SB_HEADS
    nq, nkb = s // tq, s // tk
    scale = hd ** -0.5
    per = tq // tk
    assert per % group == 0

    def body(q_ref, k_ref, v_ref, do_ref, car_ref, dq_ref, dkv_ref, dq_acc, dk_acc, dv_acc):
        qi = pl.program_id(1)
        qb = q_ref[...]
        dob = do_ref[...]
        q0 = qi * tq
        dq_acc[...] = jnp.zeros_like(dq_acc)

        @pl.when(qi == 0)
        def _():
            dk_acc[...] = jnp.zeros_like(dk_acc)
            dv_acc[...] = jnp.zeros_like(dv_acc)

        def blocks(first, pcar, skip=None):
            for t in range(group):
                kj = first + t
                ks = pl.multiple_of(kj * tk, tk)
                kblk = k_ref[pl.ds(ks, tk), :]
                lo = 0 if skip is None else (skip + t) * tk
                masked = skip is not None
                qs, dos = qb[lo:, :], dob[lo:, :]
                z, sp, mask, _, a = _sb_block(kblk, qs, ks, q0 + lo, scale, car_ref[pl.ds(kj, 1), :][:, lo:], masked)
                p = a * _dot(v_ref[pl.ds(ks, tk), :], dos, 1, 1)
                before = _split_dot(_tri(tk, "lt"), p)
                sg = jnp.exp(z - sp)
                dz = p * (1.0 - sg) - (pcar[:, lo:] + before) * sg
                if masked:
                    dz = jnp.where(mask, dz, 0.0)
                dz = (dz * scale).astype(BF16)
                dk_acc[pl.ds(ks, tk), :] += _dot(dz, qs)
                dv_acc[pl.ds(ks, tk), :] += _dot(a.astype(BF16), dos)
                dq_acc[lo:, :] += _dot(dz, kblk, 0, 0)
                add = jnp.sum(p, axis=0, keepdims=True)
                pcar = pcar + (add if lo == 0 else jnp.concatenate([jnp.zeros((1, lo), F32), add], axis=1))
            return pcar

        n_in = qi * (per // group)

        def reached(g):
            return (jnp.max(car_ref[pl.ds(g * group + group - 1, 1), :]) > SB_DEAD).astype(jnp.int32)

        start = n_in - lax.fori_loop(0, n_in, lambda g, n: n + reached(g), jnp.int32(0))
        pcar = lax.fori_loop(start, n_in, lambda i, c: blocks(i * group, c), jnp.zeros((1, tq), F32))
        for gidx in range(per // group):
            pcar = blocks(qi * per + gidx * group, pcar, gidx * group)
        dq_ref[...] = dq_acc[...].astype(BF16)

        @pl.when(qi == nq - 1)
        def _():
            dkv_ref[0] = dk_acc[...].astype(BF16)
            dkv_ref[1] = dv_acc[...].astype(BF16)

    qspec = pl.BlockSpec((tq, hd), lambda h, i: (i, h))
    return _hosted_call(
        body, host, name=name, grid=(SB_HEADS, nq),
        in_specs=[qspec, pl.BlockSpec((None, s, hd), lambda h, i: (0, 0, h)), pl.BlockSpec((None, s, hd), lambda h, i: (1, 0, h)),
                  qspec, pl.BlockSpec((None, None, nkb, tq), lambda h, i: (h, i, 0, 0))],
        out_specs=[qspec, pl.BlockSpec((2, s, hd), lambda h, i: (0, 0, h))],
        out_shape=[SDS((s, w), BF16), SDS((2, s, w), BF16)],
        scratch_shapes=[pltpu.VMEM((tq, hd), F32), pltpu.VMEM((s, hd), F32), pltpu.VMEM((s, hd), F32)],
        sem=("parallel", "arbitrary"), args=(q, kv, kv, do, car))


def adamw(w, g, m, v, *, name, host=None):
    shape = w.shape
    c = shape[-1]
    r = w.size // c
    tr = _tile(r, max(8, (3 * LANES * 1024) // c), unit=8) if r >= 8 else r

    def body(w_ref, g_ref, m_ref, v_ref, d_ref, nm_ref, nv_ref):
        gf = g_ref[...]
        mn = ADAM_B1 * m_ref[...] + (1.0 - ADAM_B1) * gf
        vn = ADAM_B2 * v_ref[...] + (1.0 - ADAM_B2) * (gf * gf)
        m_hat = mn / (1.0 - ADAM_B1 ** ADAM_STEP)
        v_hat = vn / (1.0 - ADAM_B2 ** ADAM_STEP)
        d_ref[...] = -ADAM_LR * (m_hat / (jnp.sqrt(v_hat) + ADAM_EPS) + ADAM_WD * w_ref[...])
        nm_ref[...] = mn
        nv_ref[...] = vn

    blk = pl.BlockSpec((tr, c), lambda i: (i, 0))
    res = _hosted_call(
        body, host, name=name, grid=(r // tr,), in_specs=[blk] * 4, out_specs=[blk] * 3, out_shape=[SDS((r, c), F32)] * 3,
        scratch_shapes=[], sem=("parallel",), args=tuple(t.reshape(r, c) for t in (w, g, m, v)))
    outs, got = (res, None) if host is None else res
    outs = tuple(o.reshape(shape) for o in outs)
    return outs if host is None else (outs, got)


def _row_tile(r, c):
    return _tile(r, max(16, (4 * LANES * 1024) // c), unit=16)


def _halves(a):
    return a.reshape(a.shape[:-2] + (2, a.shape[-2] // 2, a.shape[-1]))


def place_shard(w, chip, *, name, layer=None):
    r, c = w.shape[-2:]
    tr = _row_tile(r, c)

    def body(c_ref, w_ref, o_ref):
        o_ref[...] = w_ref[...].astype(BF16)

    if layer is None:
        w_spec = pl.BlockSpec((None, tr, c), lambda h, i, c_ref: (h, i, 0))
    else:
        w_spec = pl.BlockSpec((None, None, tr, c), lambda h, i, c_ref: (layer, h, i, 0))
    return pl.pallas_call(
        body, name=name, out_shape=SDS((N_CHIPS, 2, r, c), BF16),
        grid_spec=pltpu.PrefetchScalarGridSpec(
            num_scalar_prefetch=1, grid=(2, r // tr), in_specs=[w_spec],
            out_specs=pl.BlockSpec((None, None, tr, c), lambda h, i, c_ref: (c_ref[0], h, i, 0))),
        compiler_params=_cp("parallel", "parallel"))(chip, w)


def sibling_exchange(gs, *, name):
    n = len(gs)

    def body(*refs):
        g_refs, a_refs = refs[:n], refs[n:2 * n]
        send_sems, recv_sems = refs[2 * n:]
        x, y, c = _coords()
        cps = [_remote(g_refs[i].at[:, 1 - c], a_refs[i], send_sems.at[i], recv_sems.at[i], (x, y, 1 - c)) for i in range(n)]
        for cp in cps:
            cp.start()
        for cp in cps:
            cp.wait()

    return pl.pallas_call(
        body, name=name, in_specs=[ANY] * n, out_specs=[ANY] * n,
        out_shape=[SDS(g.shape[:1] + g.shape[2:], g.dtype) for g in gs],
        scratch_shapes=[_dma_sems(n), _dma_sems(n)])(*gs)


def half_add(g, a, core, *, name):
    n, _, r, c = g.shape
    tr = _row_tile(r, c)

    def body(c_ref, g_ref, a_ref, o_ref):
        o_ref[...] = (g_ref[...].astype(F32) + a_ref[...].astype(F32)).astype(o_ref.dtype)

    blk = pl.BlockSpec((None, tr, c), lambda s, i, c_ref: (s, i, 0))
    return pl.pallas_call(
        body, name=name, out_shape=SDS((n, r, c), g.dtype),
        grid_spec=pltpu.PrefetchScalarGridSpec(
            num_scalar_prefetch=1, grid=(n, r // tr),
            in_specs=[pl.BlockSpec((None, None, tr, c), lambda s, i, c_ref: (s, c_ref[0], i, 0)), blk], out_specs=blk),
        compiler_params=_cp("parallel", "parallel"))(core, g, a)


def chip_sum(p, b, chip, core, *, name):
    _, r, c = p.shape
    tr = _row_tile(r, c)

    def body(chip_ref, core_ref, p_ref, b_ref, o_ref):
        t = p_ref[...].astype(F32)
        for k in range(N_CHIPS - 1):
            t = t + b_ref[k].astype(F32)
        o_ref[...] = t

    return pl.pallas_call(
        body, name=name, out_shape=SDS((2, r, c), F32),
        grid_spec=pltpu.PrefetchScalarGridSpec(
            num_scalar_prefetch=2, grid=(r // tr,),
            in_specs=[pl.BlockSpec((None, tr, c), lambda i, chip_ref, core_ref: (chip_ref[0], i, 0)),
                      pl.BlockSpec((N_CHIPS - 1, tr, c), lambda i, chip_ref, core_ref: (0, i, 0))],
            out_specs=pl.BlockSpec((None, tr, c), lambda i, chip_ref, core_ref: (core_ref[0], i, 0))),
        compiler_params=_cp("parallel"))(chip, core, p, b)


def sibling_gather(ts, *, name):
    n = len(ts)

    def body(*refs):
        out_refs = refs[n:2 * n]
        send_sems, recv_sems = refs[2 * n:]
        x, y, c = _coords()
        cps = [_remote(out_refs[i].at[c], out_refs[i].at[c], send_sems.at[i], recv_sems.at[i], (x, y, 1 - c)) for i in range(n)]
        for cp in cps:
            cp.start()
        for i in range(n):
            _remote(out_refs[i].at[c], out_refs[i].at[1 - c], send_sems.at[i], recv_sems.at[i], (x, y, 1 - c)).wait_recv()
        for cp in cps:
            cp.wait_send()

    return pl.pallas_call(
        body, name=name, in_specs=[ANY] * n, out_specs=[ANY] * n, out_shape=[SDS(t.shape, t.dtype) for t in ts],
        input_output_aliases={i: i for i in range(n)}, scratch_shapes=[_dma_sems(n), _dma_sems(n)])(*ts)


def allgather_all(sm, *, name):
    r, w = sm.shape

    def body(s_ref, out_ref, send_sems, recv_sems, local_sem):
        x, y, c = _coords()
        me = 4 * x + 2 * y + c

        def peer(rel):
            flip = lambda v, bit: 1 - v if bit else v
            return flip(x, rel & 4), flip(y, rel & 2), flip(c, rel & 1)

        loc = pltpu.make_async_copy(s_ref, out_ref.at[me], local_sem.at[0])
        loc.start()
        cps = [_remote(s_ref, out_ref.at[me], send_sems.at[rel - 1], recv_sems.at[rel - 1], peer(rel)) for rel in range(1, N_DEV)]
        for cp in cps:
            cp.start()
        for rel in range(1, N_DEV):
            px, py, pc = peer(rel)
            _remote(s_ref, out_ref.at[4 * px + 2 * py + pc], send_sems.at[rel - 1], recv_sems.at[rel - 1], (px, py, pc)).wait_recv()
        for cp in cps:
            cp.wait_send()
        loc.wait()

    return pl.pallas_call(
        body, name=name, in_specs=[ANY], out_specs=ANY, out_shape=SDS((N_DEV, r, w), sm.dtype),
        scratch_shapes=[_dma_sems(N_DEV - 1), _dma_sems(N_DEV - 1), _dma_sems(1)])(sm)


def sum_blocks(a, *, name):
    n, r, w = a.shape

    def body(a_ref, o_ref):
        t = a_ref[0]
        for k in range(1, n):
            t = t + a_ref[k]
        o_ref[...] = t

    return pl.pallas_call(body, name=name, out_shape=SDS((r, w), F32))(a)


class Exchanges:
    def __init__(self, placed=None, chip=None, core=None):
        self.placed, self.chip, self.core = placed, chip, core
        self.mine = {}
        self.pending = {}

    def gather(self, names):
        return None if self.placed is None else GatherX([self.placed[n] for n in names])

    def reduce(self, grads, call):
        if self.placed is None:
            return None if call is None else call(None)
        names = list(grads)
        from_sibling = sibling_exchange([grads[n] for n in names], name="grads_to_sibling_" + names[0])
        pairs = [half_add(grads[n], a, self.core, name=f"pair_sum_{n}") for n, a in zip(names, from_sibling)]
        if call is None:
            self.pending.update(zip(names, pairs))
            return None
        return self.scatter(names, pairs, call)

    def scatter(self, names, pairs, call):
        result, from_chips = call(ScatterX(pairs))
        for n, t, b in zip(names, pairs, from_chips):
            self.mine[n] = chip_sum(t, b, self.chip, self.core, name=f"chip_sum_{n}")
        return result


def _usable(name, buf, d):
    if name == "w_in":
        cols = N_CHIPS * buf.shape[-1]
        full = buf.reshape(N_CHIPS, d, -1).transpose(1, 0, 2).reshape(d, cols)
        return jnp.pad(full, ((0, 0), (0, -cols % LANES)))
    if name in ("w_out", "w_q", "w_so"):
        return buf.reshape(-1, buf.shape[-1])
    return buf.reshape(N_CHIPS, -1, buf.shape[-1])


def _by_rows(t):
    return _halves(t.reshape(N_CHIPS, -1, t.shape[-1]))


def _ffn_fwd(h, nw, w_gu, w_dn, tag, host=None):
    f = rmsnorm_fwd(h, nw, name=f"{tag}_norm")
    gu = matmul(f, View(w_gu, "cols"), name=f"{tag}_gate_up", out_dtype=BF16, host=host)
    gu, got = gu if host is not None else (gu, None)
    act = swiglu_fwd(gu, name=f"{tag}_act")
    return matmul(act, View(w_dn, "rows"), name=f"{tag}_down", residual=h), (f, gu, act), got


def _ffn_bwd(dh, h, nw, w_gu, w_dn, saved, tag, ex, carried):
    f, gu, act = saved
    dact = matmul(dh, View(w_dn, "rows"), name=f"{tag}_dact", tb=True, out_dtype=BF16)
    dw_dn = matmul(act, dh, name=f"{tag}_dw_down", ta=True, out_dtype=BF16)
    dgu = swiglu_bwd(gu, dact, name=f"{tag}_dgu")

    def dw_gate_up(host):
        return matmul(f, dgu, name=f"{tag}_dw_gate_up", ta=True, out_dtype=BF16, out_chips=N_CHIPS, host=host)

    dw_gu = ex.reduce(carried, dw_gate_up) if carried else dw_gate_up(None)
    df = matmul(dgu, View(w_gu, "cols"), name=f"{tag}_df", tb=True)
    dh_in, dnw = rmsnorm_bwd(df, h, nw, dh, name=f"{tag}_dnorm")
    return dh_in, dnw, dw_gu, dw_dn


def local_step(x, target, p, ex):
    d = x.shape[1]
    hosted = ex.placed is not None
    w = {} if hosted else {n: _usable(n, p[n], d) for n in ("w_in", "w_out", "w_kv", "w_q", "w_so", "w_gu0", "w_gu1", "w_dn0", "w_dn1")}

    def take(names, got):
        for n, buf in zip(names, got or []):
            w[n] = _usable(n, buf, d)

    def carry(call, names):
        host = ex.gather(names)
        if host is None:
            return call(None)
        res, got = call(host)
        take(names, got)
        return res

    if hosted:
        take(["w_in", "w_out"], run_exchange(ex.gather(["w_in", "w_out"]), name="gather_gla"))
    a0 = rmsnorm_fwd(x, p["an0"], name="l0_attn_norm")
    proj = carry(lambda host: matmul(a0, w["w_in"], name="gla_in", host=host), ["w_dn0"])
    g = gate_fwd(proj, p["wg"], p["bg"], name="gla_gate")
    o, states = carry(lambda host: gla_fwd(proj, g, name="gla_scan", host=host), ["w_gu0"])
    og = gnorm_fwd(o, proj, p["gw"], name="gla_outnorm")
    h1 = matmul(og, w["w_out"], name="gla_out", residual=x)
    sb_names = ["w_kv", "w_q", "w_so"]
    h2, ffn0, got = _ffn_fwd(h1, p["fn0"], w["w_gu0"], w["w_dn0"], "ffn0", host=ex.gather(sb_names))
    take(sb_names, got)
    w_kv = View(w["w_kv"], "cols")
    kvn = rmsnorm_fwd(h2, p["kvn"], name="kv_norm")
    kv = matmul(kvn, w_kv, name="sb_kv", out_dtype=BF16, out_chips=2)
    a1 = rmsnorm_fwd(h2, p["an1"], name="l1_attn_norm")
    q2 = matmul(a1, w["w_q"], name="sb_q", out_dtype=BF16)
    o2, car = carry(lambda host: sb_fwd(q2, kv, name="sb_attn", host=host), ["w_gu1", "w_dn1"])
    h3 = matmul(o2, w["w_so"], name="sb_out", residual=h2)
    h4, ffn1, _ = _ffn_fwd(h3, p["fn1"], w["w_gu1"], w["w_dn1"], "ffn1")
    dh4, d_fin, loss_row = final_loss(h4, p["finn"], target, name="final_loss")

    dh3, d_fn1, dw_gu1, dw_dn1 = _ffn_bwd(dh4, h3, p["fn1"], w["w_gu1"], w["w_dn1"], ffn1, "ffn1", ex, {})
    do2 = matmul(dh3, w["w_so"], name="sb_do", tb=True, out_dtype=BF16)
    dw_so = matmul(o2, dh3, name="sb_dw_out", ta=True, out_dtype=BF16)
    dq2, dkv = ex.reduce({"w_gu1": _halves(dw_gu1), "w_dn1": _by_rows(dw_dn1)},
                         lambda host: sb_bwd(q2, kv, do2, car, name="sb_attn_bwd", host=host))
    dkv = View(dkv, "cols")
    dw_q = matmul(a1, dq2, name="sb_dw_q", ta=True, out_dtype=BF16)
    da1 = matmul(dq2, w["w_q"], name="sb_da", tb=True)
    dh2, d_an1 = rmsnorm_bwd(da1, h2, p["an1"], dh3, name="l1_attn_dnorm")
    dw_kv = matmul(kvn, dkv, name="sb_dw_kv", ta=True, out_dtype=BF16, out_chips=N_CHIPS)
    dkvn = matmul(dkv, w_kv, name="sb_dkvn", tb=True)
    dh2, d_kvn = rmsnorm_bwd(dkvn, h2, p["kvn"], dh2, name="kv_dnorm")
    sb_grads = {"w_kv": _halves(dw_kv), "w_q": _by_rows(dw_q), "w_so": _by_rows(dw_so)}
    dh1, d_fn0, dw_gu0, dw_dn0 = _ffn_bwd(dh2, h1, p["fn0"], w["w_gu0"], w["w_dn0"], ffn0, "ffn0", ex, sb_grads)
    dog = matmul(dh1, w["w_out"], name="gla_dog", tb=True, out_dtype=BF16)
    dw_out = matmul(og, dh1, name="gla_dw_out", ta=True, out_dtype=BF16)
    do, dr, d_gw = gnorm_bwd(dog, o, proj, p["gw"], name="gla_outnorm_bwd")
    dq, dk, dv, dg = ex.reduce({"w_gu0": _halves(dw_gu0), "w_dn0": _by_rows(dw_dn0)},
                               lambda host: gla_bwd(proj, g, states, do, name="gla_scan_bwd", host=host))
    dgl, d_wg, d_bg = gate_bwd(dg, proj, p["wg"], p["bg"], name="gla_gate_bwd")
    dproj = jnp.concatenate([dq, dk, dv, dr, dgl], axis=1)
    dw_in = matmul(a0, dproj, name="gla_dw_in", ta=True, out_dtype=BF16)
    da0 = matmul(dproj, w["w_in"], name="gla_da", tb=True)
    dx, d_an0 = rmsnorm_bwd(da0, x, p["an0"], dh1, name="l0_attn_dnorm")
    in_w = p["in_w"]
    gla_grads = {"w_in": _halves(dw_in[:, :in_w].reshape(d, N_CHIPS, -1).transpose(1, 0, 2)), "w_out": _by_rows(dw_out)}
    ex.reduce(gla_grads, None)

    small = dict(an0=d_an0, an1=d_an1, fn0=d_fn0, fn1=d_fn1, kvn=d_kvn, finn=d_fin, wg=d_wg, bg=d_bg, gw=d_gw)
    big = {}
    if not hosted:
        big = dict(gla_grads, **sb_grads, w_gu0=_halves(dw_gu0), w_gu1=_halves(dw_gu1), w_dn0=_by_rows(dw_dn0), w_dn1=_by_rows(dw_dn1))
    return loss_row, dx, small, big


def _pack_rows(parts):
    cat = jnp.concatenate([t.reshape(-1) for t in parts])
    rows = -(-cat.size // (LANES * SUBLANES)) * SUBLANES
    return jnp.pad(cat, (0, rows * LANES - cat.size)).reshape(rows, LANES)


def _segments(flat, sizes):
    out, off = [], 0
    for n in sizes:
        out.append(flat[..., off:off + n])
        off += n
    return out


def kernel(x, attn_norm_w, ffn_norm_w, gla_w_in, gla_w_gate_up, gla_b_gate, gla_gnorm_w, gla_w_out, kv_norm_w, sb_w_kv, sb_w_q, sb_w_out, ffn_w_gate_up, ffn_w_down, final_norm_w, loss_target, m_attn_norm_w, m_ffn_norm_w, m_gla_w_in, m_gla_w_gate_up, m_gla_b_gate, m_gla_gnorm_w, m_gla_w_out, m_kv_norm_w, m_sb_w_kv, m_sb_w_q, m_sb_w_out, m_ffn_w_gate_up, m_ffn_w_down, m_final_norm_w, v_attn_norm_w, v_ffn_norm_w, v_gla_w_in, v_gla_w_gate_up, v_gla_b_gate, v_gla_gnorm_w, v_gla_w_out, v_kv_norm_w, v_sb_w_kv, v_sb_w_q, v_sb_w_out, v_ffn_w_gate_up, v_ffn_w_down, v_final_norm_w):
    xi, yi, ci = _coords()
    core = ci.astype(jnp.int32).reshape(1)
    chip1 = (2 * xi + yi).astype(jnp.int32)
    chip = chip1.reshape(1)
    rank = gla_w_gate_up.shape[1]

    gu4, dn4 = _halves(ffn_w_gate_up), _halves(ffn_w_down)
    shards = dict(w_in=(_halves(gla_w_in[0]), None), w_out=(_halves(gla_w_out[0]), None), w_kv=(_halves(sb_w_kv), None),
                  w_q=(_halves(sb_w_q[0]), None), w_so=(_halves(sb_w_out[0]), None),
                  w_gu0=(gu4, 0), w_gu1=(gu4, 1), w_dn0=(dn4, 0), w_dn1=(dn4, 1))
    placed = {n: place_shard(a, chip, name=f"place_{n}", layer=l) for n, (a, l) in shards.items()}
    small_w = [gla_w_gate_up, gla_b_gate, gla_gnorm_w]
    small_all = allgather_all(_pack_rows(small_w), name="gather_gate_weights")
    wg, bg, gw = _segments(small_all[::2].reshape(N_CHIPS, -1), [a.size for a in small_w])
    wg = wg.reshape(N_CHIPS, rank, -1).transpose(1, 0, 2).reshape(rank, -1)
    p = dict(an0=attn_norm_w[0], an1=attn_norm_w[1], fn0=ffn_norm_w[0], fn1=ffn_norm_w[1], kvn=kv_norm_w, finn=final_norm_w,
             wg=jnp.pad(wg, ((0, LANES - rank), (0, 0))).astype(BF16), bg=bg.reshape(1, -1), gw=gw.reshape(1, -1),
             in_w=N_CHIPS * gla_w_in.shape[-1])

    ex = Exchanges(placed, chip, core)
    loss_row, dx, g, _ = local_step(x[0], loss_target[0], p, ex)
    loss = lax.psum(loss_row[0, 0], ("x", "y", "c"))

    tags = list(ex.mine)
    tot = dict(zip(tags, sibling_gather([ex.mine[n] for n in tags], name="grads_from_sibling")))

    vecs = [jnp.concatenate([g["an0"], g["an1"]]), jnp.concatenate([g["fn0"], g["fn1"]]), g["kvn"], g["finn"],
            g["wg"][:rank], g["bg"], g["gw"]]
    gathered_vecs = allgather_all(_pack_rows(vecs), name="gather_small_grads")
    d_an, d_fn, d_kvn, d_fin, d_wg, d_bg, d_gw = _segments(
        sum_blocks(gathered_vecs, name="sum_small_grads").reshape(-1), [t.size for t in vecs])

    def shard(t, like):
        return lax.dynamic_index_in_dim(t.reshape(-1, N_CHIPS, like.shape[-1]), chip1, axis=1, keepdims=False).reshape(like.shape)

    weights = dict(
        attn_norm_w=(attn_norm_w, m_attn_norm_w, v_attn_norm_w), ffn_norm_w=(ffn_norm_w, m_ffn_norm_w, v_ffn_norm_w),
        gla_w_in=(gla_w_in, m_gla_w_in, v_gla_w_in), gla_w_gate_up=(gla_w_gate_up, m_gla_w_gate_up, v_gla_w_gate_up),
        gla_b_gate=(gla_b_gate, m_gla_b_gate, v_gla_b_gate), gla_gnorm_w=(gla_gnorm_w, m_gla_gnorm_w, v_gla_gnorm_w),
        gla_w_out=(gla_w_out, m_gla_w_out, v_gla_w_out), kv_norm_w=(kv_norm_w, m_kv_norm_w, v_kv_norm_w),
        sb_w_kv=(sb_w_kv, m_sb_w_kv, v_sb_w_kv), sb_w_q=(sb_w_q, m_sb_w_q, v_sb_w_q), sb_w_out=(sb_w_out, m_sb_w_out, v_sb_w_out),
        ffn_w_gate_up=(ffn_w_gate_up, m_ffn_w_gate_up, v_ffn_w_gate_up), ffn_w_down=(ffn_w_down, m_ffn_w_down, v_ffn_w_down),
        final_norm_w=(final_norm_w, m_final_norm_w, v_final_norm_w))
    grads = dict(
        attn_norm_w=d_an.reshape(attn_norm_w.shape), ffn_norm_w=d_fn.reshape(ffn_norm_w.shape),
        gla_w_gate_up=shard(d_wg, gla_w_gate_up), gla_b_gate=shard(d_bg, gla_b_gate), gla_gnorm_w=shard(d_gw, gla_gnorm_w),
        kv_norm_w=d_kvn.reshape(kv_norm_w.shape), sb_w_kv=tot["w_kv"].reshape(sb_w_kv.shape),
        sb_w_q=tot["w_q"].reshape(sb_w_q.shape), sb_w_out=tot["w_so"].reshape(sb_w_out.shape),
        ffn_w_gate_up=jnp.stack([tot["w_gu0"], tot["w_gu1"]]).reshape(ffn_w_gate_up.shape),
        ffn_w_down=jnp.stack([tot["w_dn0"], tot["w_dn1"]]).reshape(ffn_w_down.shape),
        final_norm_w=d_fin.reshape(final_norm_w.shape))

    def step(n, host=None):
        return adamw(weights[n][0], grads[n], weights[n][1], weights[n][2], name=f"adamw_{n}", host=host)

    stepped = {}
    for host_name, n in (("ffn_w_gate_up", "w_in"), ("ffn_w_down", "w_out")):
        stepped[host_name] = ex.scatter([n], [ex.pending[n]], lambda host: step(host_name, host))
    late = sibling_gather([ex.mine["w_in"], ex.mine["w_out"]], name="grads_from_sibling_gla")
    grads["gla_w_in"], grads["gla_w_out"] = late[0].reshape(gla_w_in.shape), late[1].reshape(gla_w_out.shape)
    names = list(weights)
    for n in names:
        if n not in stepped:
            stepped[n] = step(n)
    return (loss, dx.reshape(x.shape), *[grads[n] for n in names], *[stepped[n][0] for n in names],
            *[stepped[n][1] for n in names], *[stepped[n][2] for n in names])
```

```python
import functools

import jax
import jax.numpy as jnp
from jax import lax
from jax.experimental import pallas as pl
from jax.experimental.pallas import tpu as pltpu

F32 = jnp.float32
BF16 = jnp.bfloat16
SDS = jax.ShapeDtypeStruct
MESH = pl.DeviceIdType.MESH

EPS = 1e-6
GLA_HEADS = 4
GLA_GATE_RANK = 16
GLA_GATE_TAU = 16.0
GLA_CHUNK = 128
SB_HEADS = 16
SB_TQ = 512
SB_TK = 128
SB_GROUP = 2
SB_DEAD = -110.0
SB_SKIPPED = -1e30
ADAM_LR = 0.001
ADAM_B1 = 0.9
ADAM_B2 = 0.999
ADAM_EPS = 1e-08
ADAM_WD = 0.01
ADAM_STEP = 10

LANES = 128
SUBLANES = 8
N_CHIPS = 4
N_DEV = 8
VMEM_LIMIT = 56 * 1024 * 1024


def _tile(dim, target, unit=LANES):
    if dim <= target:
        return dim
    t = (target // unit) * unit
    while t >= unit:
        if dim % t == 0:
            return t
        t -= unit
    raise ValueError(f"no tile for {dim}")


def _cp(*sem):
    return pltpu.CompilerParams(dimension_semantics=sem, vmem_limit_bytes=VMEM_LIMIT)


def _sigmoid(x):
    return 1.0 / (1.0 + jnp.exp(-x))


def _dot(a, b, ca=1, cb=0):
    return lax.dot_general(a, b, (((ca,), (cb,)), ((), ())), preferred_element_type=F32)


def _split_dot(tri, x):
    hi = x.astype(BF16)
    lo = (x - hi.astype(F32)).astype(BF16)
    return _dot(tri, hi) + _dot(tri, lo)


def _hilo(x):
    hi = x.astype(BF16)
    return hi, (x - hi.astype(F32)).astype(BF16)


def _dot3(a, b, ca=1, cb=0):
    return _dot(a[0], b[0], ca, cb) + _dot(a[0], b[1], ca, cb) + _dot(a[1], b[0], ca, cb)


def _tri(n, kind):
    r = lax.broadcasted_iota(jnp.int32, (n, n), 0)
    c = lax.broadcasted_iota(jnp.int32, (n, n), 1)
    m = {"le": c <= r, "ge": c >= r, "lt": c < r, "gt": c > r}[kind]
    return jnp.where(m, 1.0, 0.0).astype(BF16)


ANY = pl.BlockSpec(memory_space=pl.ANY)


def _coords():
    return lax.axis_index("x"), lax.axis_index("y"), lax.axis_index("c")


def _other_chips(x, y):
    return [(1 - x, y), (x, 1 - y), (1 - x, 1 - y)]


def _remote(src, dst, send_sem, recv_sem, dev):
    return pltpu.make_async_remote_copy(src_ref=src, dst_ref=dst, send_sem=send_sem, recv_sem=recv_sem,
                                        device_id=dev, device_id_type=MESH)


def _dma_sems(n):
    return pltpu.SemaphoreType.DMA((n,))


class GatherX:
    def __init__(self, bufs):
        self.ins, self.ios, self.outs, self.n_sems = [], list(bufs), [], 6 * len(bufs)

    def _copy(self, ios, send_sems, recv_sems, i, k, chip, half, dev):
        blk = ios[i].at[chip, half]
        return _remote(blk, blk, send_sems.at[6 * i + k], recv_sems.at[6 * i + k], dev)

    def _first(self, ios, send_sems, recv_sems):
        x, y, c = _coords()
        return [self._copy(ios, send_sems, recv_sems, i, k, 2 * x + y, c, (cx, cy, c))
                for i in range(len(ios)) for k, (cx, cy) in enumerate(_other_chips(x, y))]

    def start(self, ins, ios, outs, send_sems, recv_sems):
        for cp in self._first(ios, send_sems, recv_sems):
            cp.start()

    def finish(self, ins, ios, outs, send_sems, recv_sems):
        x, y, c = _coords()
        chips = _other_chips(x, y)
        copy = functools.partial(self._copy, ios, send_sems, recv_sems)
        passed = []
        for i in range(len(ios)):
            for k, (cx, cy) in enumerate(chips):
                copy(i, k, 2 * cx + cy, c, (x, y, c)).wait_recv()
                passed.append(copy(i, 3 + k, 2 * cx + cy, c, (x, y, 1 - c)))
                passed[-1].start()
        for i in range(len(ios)):
            for k, (cx, cy) in enumerate(chips):
                copy(i, 3 + k, 2 * cx + cy, 1 - c, (x, y, c)).wait_recv()
        for cp in self._first(ios, send_sems, recv_sems) + passed:
            cp.wait_send()


class ScatterX:
    def __init__(self, ps):
        self.ins, self.ios, self.n_sems = list(ps), [], 3 * len(ps)
        self.outs = [SDS((N_CHIPS - 1,) + p.shape[1:], p.dtype) for p in ps]

    def _copies(self, ins, outs, send_sems, recv_sems):
        x, y, c = _coords()
        return [_remote(ins[i].at[2 * cx + cy], outs[i].at[k], send_sems.at[3 * i + k], recv_sems.at[3 * i + k], (cx, cy, c))
                for i in range(len(ins)) for k, (cx, cy) in enumerate(_other_chips(x, y))]

    def start(self, ins, ios, outs, send_sems, recv_sems):
        for cp in self._copies(ins, outs, send_sems, recv_sems):
            cp.start()

    def finish(self, ins, ios, outs, send_sems, recv_sems):
        for cp in self._copies(ins, outs, send_sems, recv_sems):
            cp.wait()


def _exchange_operands(host):
    x_in = host.ins + host.ios
    x_out = [SDS(a.shape, a.dtype) for a in host.ios] + host.outs
    return x_in, x_out


def run_exchange(host, *, name):
    x_in, x_out = _exchange_operands(host)
    n_ins, n_ios = len(host.ins), len(host.ios)

    def body(*refs):
        xin, xout = refs[:len(x_in)], refs[len(x_in):len(x_in) + len(x_out)]
        send_sems, recv_sems = refs[len(x_in) + len(x_out):]
        x_refs = (xin[:n_ins], xout[:n_ios], xout[n_ios:])
        host.start(*x_refs, send_sems, recv_sems)
        host.finish(*x_refs, send_sems, recv_sems)

    return list(pl.pallas_call(
        body, name=name, in_specs=[ANY] * len(x_in), out_specs=[ANY] * len(x_out), out_shape=x_out,
        input_output_aliases={n_ins + i: i for i in range(n_ios)},
        scratch_shapes=[_dma_sems(host.n_sems), _dma_sems(host.n_sems)])(*x_in))


def _hosted_call(body, host, *, name, grid, in_specs, out_specs, out_shape, scratch_shapes, sem, args):
    if host is None:
        return list(pl.pallas_call(body, name=name, grid=grid, in_specs=in_specs, out_specs=out_specs, out_shape=out_shape,
                                   scratch_shapes=scratch_shapes, compiler_params=_cp(*sem))(*args))
    x_in, x_out = _exchange_operands(host)
    n_in, n_out, n_scr, n_ins, n_ios = len(in_specs), len(out_specs), len(scratch_shapes), len(host.ins), len(host.ios)

    def hosted(*refs):
        ins, xin = refs[:n_in], refs[n_in:n_in + len(x_in)]
        o0 = n_in + len(x_in)
        outs, xout = refs[o0:o0 + n_out], refs[o0 + n_out:o0 + n_out + len(x_out)]
        s0 = o0 + n_out + len(x_out)
        scr, (send_sems, recv_sems) = refs[s0:s0 + n_scr], refs[s0 + n_scr:]
        ids = [pl.program_id(ax) for ax in range(len(grid))]
        first = functools.reduce(jnp.logical_and, [i == 0 for i in ids])
        last = functools.reduce(jnp.logical_and, [i == n - 1 for i, n in zip(ids, grid)])
        x_refs = (xin[:n_ins], xout[:n_ios], xout[n_ios:])

        @pl.when(first)
        def _():
            host.start(*x_refs, send_sems, recv_sems)

        body(*ins, *outs, *scr)

        @pl.when(last)
        def _():
            host.finish(*x_refs, send_sems, recv_sems)

    res = pl.pallas_call(
        hosted, name=name, grid=grid, in_specs=list(in_specs) + [ANY] * len(x_in), out_specs=list(out_specs) + [ANY] * len(x_out),
        out_shape=list(out_shape) + x_out, scratch_shapes=list(scratch_shapes) + [_dma_sems(host.n_sems), _dma_sems(host.n_sems)],
        input_output_aliases={n_in + n_ins + i: n_out + i for i in range(n_ios)},
        compiler_params=_cp(*(("arbitrary",) * len(grid))))(*args, *x_in)
    return list(res[:n_out]), list(res[n_out:])


def _div(i, n):
    return i if n == 1 else lax.div(i, n)


def _rem(i, n):
    return 0 if n == 1 else lax.rem(i, n)


class View:
    def __init__(self, arr, kind="plain", lead=(), g0=0, ng=None):
        self.arr, self.kind, self.lead, self.g0 = arr, kind, tuple(lead), g0
        self.ng = (arr.shape[0] - g0) if ng is None else ng
        r, c = arr.shape[-2:]
        self.runit, self.cunit = r, c
        self.shape = {"plain": (r, c), "cols": (r, self.ng * c), "rows": (self.ng * r, c)}[kind]

    def spec(self, br, bc, rfn, cfn):
        if self.kind == "plain":
            return pl.BlockSpec((br, bc), lambda *g: (rfn(*g), cfn(*g)))
        none = (None,) * (1 + len(self.lead))
        if self.kind == "cols":
            per = self.cunit // bc
            return pl.BlockSpec(none + (br, bc), lambda *g: (self.g0 + _div(cfn(*g), per), *self.lead, rfn(*g), _rem(cfn(*g), per)))
        per = self.runit // br
        return pl.BlockSpec(none + (br, bc), lambda *g: (self.g0 + _div(rfn(*g), per), *self.lead, _rem(rfn(*g), per), cfn(*g)))


def _as_view(a):
    return a if isinstance(a, View) else View(a)


def matmul(a, b, *, name, ta=False, tb=False, out_dtype=F32, residual=None, out_chips=None, host=None, tm=1408, tn=1408, tk=2816):
    a, b = _as_view(a), _as_view(b)
    (k, m) = a.shape if ta else a.shape[::-1]
    (n, kb) = b.shape if tb else b.shape[::-1]
    assert k == kb, (a.shape, b.shape, ta, tb)
    m_unit = a.cunit if ta else a.runit
    ka_unit = a.runit if ta else a.cunit
    n_unit = b.runit if tb else b.cunit
    kb_unit = b.cunit if tb else b.runit
    if out_chips is not None:
        n_unit = min(n_unit, n // out_chips)
    tm, tn = _tile(min(m, m_unit), tm), _tile(min(n, n_unit), tn)
    tk = _tile(min(k, ka_unit, kb_unit), tk)
    assert ka_unit % tk == 0 and kb_unit % tk == 0, (ka_unit, kb_unit, tk)
    nk = k // tk
    ca, cb = (0 if ta else 1), (1 if tb else 0)

    def body(a_ref, b_ref, *refs):
        r_ref = refs[0] if residual is not None else None
        o_ref = refs[-1] if nk == 1 else refs[-2]

        def finish(r):
            if residual is not None:
                r = r + r_ref[...]
            o_ref[...] = r.astype(out_dtype)

        part = _dot(a_ref[...].astype(BF16), b_ref[...].astype(BF16), ca, cb)
        if nk == 1:
            finish(part)
            return
        acc = refs[-1]
        kk = pl.program_id(2)

        @pl.when(kk == 0)
        def _():
            acc[...] = part

        @pl.when(kk > 0)
        def _():
            acc[...] += part

        @pl.when(kk == nk - 1)
        def _():
            finish(acc[...])

    gi, gj, gk = (lambda i, j, kk: i), (lambda i, j, kk: j), (lambda i, j, kk: kk)
    a_spec = a.spec(tk, tm, gk, gi) if ta else a.spec(tm, tk, gi, gk)
    b_spec = b.spec(tn, tk, gj, gk) if tb else b.spec(tk, tn, gk, gj)
    if out_chips is None:
        out = View(SDS((m, n), out_dtype))
    else:
        out = View(SDS((out_chips, m, n // out_chips), out_dtype), "cols")
    o_spec = out.spec(tm, tn, gi, gj)
    in_specs, args = [a_spec, b_spec], [a.arr, b.arr]
    if residual is not None:
        in_specs.append(pl.BlockSpec((tm, tn), lambda i, j, kk: (i, j)))
        args.append(residual)
    res = _hosted_call(
        body, host, name=name, grid=(m // tm, n // tn, nk), in_specs=in_specs, out_specs=[o_spec], out_shape=[out.arr],
        scratch_shapes=[] if nk == 1 else [pltpu.VMEM((tm, tn), F32)], sem=("parallel", "parallel", "arbitrary"), args=args)
    return res[0] if host is None else (res[0][0], res[1])


def rmsnorm_fwd(x, w, *, name, tr=256):
    s, d = x.shape

    def body(x_ref, w_ref, o_ref):
        xf = x_ref[...]
        r = lax.rsqrt(jnp.mean(xf * xf, axis=-1, keepdims=True) + EPS)
        o_ref[...] = (xf * r * w_ref[...]).astype(BF16)

    row = pl.BlockSpec((tr, d), lambda i: (i, 0))
    return pl.pallas_call(
        body, name=name, grid=(s // tr,), in_specs=[row, pl.BlockSpec((1, d), lambda i: (0, 0))], out_specs=row,
        out_shape=SDS((s, d), BF16), compiler_params=_cp("parallel"))(x, w.reshape(1, d))


def rmsnorm_bwd(dy, x, w, dres, *, name, tr=256):
    s, d = x.shape

    def body(dy_ref, x_ref, w_ref, dres_ref, dx_ref, dw_ref):
        i = pl.program_id(0)
        xf = x_ref[...]
        r = lax.rsqrt(jnp.mean(xf * xf, axis=-1, keepdims=True) + EPS)
        xh = xf * r
        dyf = dy_ref[...].astype(F32)
        dxh = dyf * w_ref[...]
        dx_ref[...] = dres_ref[...] + r * (dxh - xh * jnp.mean(dxh * xh, axis=-1, keepdims=True))
        part = jnp.sum(dyf * xh, axis=0, keepdims=True)

        @pl.when(i == 0)
        def _():
            dw_ref[...] = part

        @pl.when(i > 0)
        def _():
            dw_ref[...] += part

    row = pl.BlockSpec((tr, d), lambda i: (i, 0))
    vec = pl.BlockSpec((1, d), lambda i: (0, 0))
    return pl.pallas_call(
        body, name=name, grid=(s // tr,), in_specs=[row, row, vec, row], out_specs=[row, vec],
        out_shape=[SDS((s, d), F32), SDS((1, d), F32)], compiler_params=_cp("arbitrary"))(dy, x, w.reshape(1, d), dres)


def final_loss(h, w, target, *, name, tr=256):
    s, d = h.shape

    def body(h_ref, w_ref, t_ref, dh_ref, dw_ref, loss_ref):
        i = pl.program_id(0)
        xf = h_ref[...]
        r = lax.rsqrt(jnp.mean(xf * xf, axis=-1, keepdims=True) + EPS)
        xh = xf * r
        err = xh * w_ref[...] - t_ref[...]
        lpart = 0.5 * jnp.sum(jnp.sum(err * err, axis=-1, keepdims=True) * (1.0 / d), axis=0, keepdims=True)
        dy = err * (1.0 / d)
        dxh = dy * w_ref[...]
        dh_ref[...] = r * (dxh - xh * jnp.mean(dxh * xh, axis=-1, keepdims=True))
        part = jnp.sum(dy * xh, axis=0, keepdims=True)
        lrow = jnp.broadcast_to(lpart, (1, LANES))

        @pl.when(i == 0)
        def _():
            dw_ref[...] = part
            loss_ref[...] = lrow

        @pl.when(i > 0)
        def _():
            dw_ref[...] += part
            loss_ref[...] += lrow

    row = pl.BlockSpec((tr, d), lambda i: (i, 0))
    vec = pl.BlockSpec((1, d), lambda i: (0, 0))
    return pl.pallas_call(
        body, name=name, grid=(s // tr,), in_specs=[row, vec, row],
        out_specs=[row, vec, pl.BlockSpec((1, LANES), lambda i: (0, 0))],
        out_shape=[SDS((s, d), F32), SDS((1, d), F32), SDS((1, LANES), F32)],
        compiler_params=_cp("arbitrary"))(h, w.reshape(1, d), target)


def swiglu_fwd(gu, *, name, tr=512, tc=1408):
    s, f2 = gu.shape
    f = f2 // 2
    tc = _tile(f, tc)
    nf = f // tc

    def body(g_ref, u_ref, o_ref):
        g = g_ref[...].astype(F32)
        o_ref[...] = (g * _sigmoid(g) * u_ref[...].astype(F32)).astype(BF16)

    return pl.pallas_call(
        body, name=name, grid=(s // tr, nf),
        in_specs=[pl.BlockSpec((tr, tc), lambda i, j: (i, j)), pl.BlockSpec((tr, tc), lambda i, j: (i, j + nf))],
        out_specs=pl.BlockSpec((tr, tc), lambda i, j: (i, j)), out_shape=SDS((s, f), BF16),
        compiler_params=_cp("parallel", "parallel"))(gu, gu)


def swiglu_bwd(gu, dact, *, name, tr=256):
    s, f2 = gu.shape
    f = f2 // 2

    def body(gu_ref, d_ref, o_ref):
        g = gu_ref[:, :f].astype(F32)
        d = d_ref[...].astype(F32)
        sg = _sigmoid(g)
        o_ref[:, :f] = (d * gu_ref[:, f:].astype(F32) * sg * (1.0 + g * (1.0 - sg))).astype(BF16)
        o_ref[:, f:] = (d * g * sg).astype(BF16)

    return pl.pallas_call(
        body, name=name, grid=(s // tr,),
        in_specs=[pl.BlockSpec((tr, f2), lambda i: (i, 0)), pl.BlockSpec((tr, f), lambda i: (i, 0))],
        out_specs=pl.BlockSpec((tr, f2), lambda i: (i, 0)), out_shape=SDS((s, f2), BF16),
        compiler_params=_cp("parallel"))(gu, dact)


def _gate_z(gl_ref, w_ref, b_ref):
    glb = gl_ref[...].astype(BF16)
    return glb, _dot(glb, w_ref[...]) + b_ref[...]


def gate_fwd(proj, wg, bg, *, name, tr=512):
    s, inw = proj.shape
    qk = wg.shape[1]
    glc = inw // LANES - 1

    def body(gl_ref, w_ref, b_ref, g_ref):
        _, z = _gate_z(gl_ref, w_ref, b_ref)
        g_ref[...] = (jnp.minimum(z, 0.0) - jnp.log(1.0 + jnp.exp(-jnp.abs(z)))) * (1.0 / GLA_GATE_TAU)

    return pl.pallas_call(
        body, name=name, grid=(s // tr,),
        in_specs=[pl.BlockSpec((tr, LANES), lambda i: (i, glc)), pl.BlockSpec((LANES, qk), lambda i: (0, 0)),
                  pl.BlockSpec((1, qk), lambda i: (0, 0))],
        out_specs=pl.BlockSpec((tr, qk), lambda i: (i, 0)), out_shape=SDS((s, qk), F32),
        compiler_params=_cp("parallel"))(proj, wg, bg)


def gate_bwd(dg, proj, wg, bg, *, name, tr=512):
    s, inw = proj.shape
    qk = wg.shape[1]
    glc = inw // LANES - 1

    def body(dg_ref, gl_ref, w_ref, b_ref, dgl_ref, dw_ref, db_ref):
        i = pl.program_id(0)
        glb, z = _gate_z(gl_ref, w_ref, b_ref)
        dz = dg_ref[...] * (1.0 / (1.0 + jnp.exp(z))) * (1.0 / GLA_GATE_TAU)
        dzb = dz.astype(BF16)
        dgl_ref[...] = _dot(dzb, w_ref[...], 1, 1).astype(BF16)
        pw = _dot(glb, dzb, 0, 0)
        pb = jnp.sum(dz, axis=0, keepdims=True)

        @pl.when(i == 0)
        def _():
            dw_ref[...] = pw
            db_ref[...] = pb

        @pl.when(i > 0)
        def _():
            dw_ref[...] += pw
            db_ref[...] += pb

    return pl.pallas_call(
        body, name=name, grid=(s // tr,),
        in_specs=[pl.BlockSpec((tr, qk), lambda i: (i, 0)), pl.BlockSpec((tr, LANES), lambda i: (i, glc)),
                  pl.BlockSpec((LANES, qk), lambda i: (0, 0)), pl.BlockSpec((1, qk), lambda i: (0, 0))],
        out_specs=[pl.BlockSpec((tr, LANES), lambda i: (i, 0)), pl.BlockSpec((LANES, qk), lambda i: (0, 0)),
                   pl.BlockSpec((1, qk), lambda i: (0, 0))],
        out_shape=[SDS((s, LANES), BF16), SDS((LANES, qk), F32), SDS((1, qk), F32)],
        compiler_params=_cp("arbitrary"))(dg, proj, wg, bg)


def _gla_chunk_terms(q_ref, k_ref, g_ref, c, scale):
    q = q_ref[...] * scale
    k = k_ref[...]
    gg = g_ref[...]
    b = _split_dot(_tri(c, "le"), gg)
    row = lax.broadcasted_iota(jnp.int32, gg.shape, 0)
    bm = jnp.sum(jnp.where(row < c // 2, gg, 0.0), axis=0, keepdims=True)
    bl = jnp.sum(gg, axis=0, keepdims=True)
    eb, em, emi, el = jnp.exp(b), jnp.exp(b - bm), jnp.exp(bm - b), jnp.exp(bl - b)
    return q, k, bl, eb, em, emi, el


def _causal(a):
    r = lax.broadcasted_iota(jnp.int32, a.shape, 0)
    c = lax.broadcasted_iota(jnp.int32, a.shape, 1)
    return jnp.where(r >= c, a, 0.0)


def gla_fwd(proj, g, *, name, host=None, c=GLA_CHUNK):
    s = proj.shape[0]
    qk = g.shape[1]
    dk, dv = qk // GLA_HEADS, 2 * qk // GLA_HEADS
    nc = s // c
    scale = dk ** -0.5
    kq = qk // dk

    def body(q_ref, k_ref, v_ref, g_ref, o_ref, st_ref, state):
        @pl.when(pl.program_id(1) == 0)
        def _():
            state[...] = jnp.zeros_like(state)

        q, k, bl, eb, em, emi, el = _gla_chunk_terms(q_ref, k_ref, g_ref, c, scale)
        v2 = _hilo(v_ref[...])
        st = state[...]
        st_ref[...] = st
        a = _causal(_dot3(_hilo(q * em), _hilo(k * emi), 1, 1))
        o_ref[...] = _dot3(_hilo(q * eb), _hilo(st), 1, 1) + _dot3(_hilo(a), v2)
        state[...] = st * jnp.exp(bl) + _dot3(v2, _hilo(k * el), 0, 0)

    return _hosted_call(
        body, host, name=name, grid=(GLA_HEADS, nc),
        in_specs=[pl.BlockSpec((c, dk), lambda h, i: (i, h)), pl.BlockSpec((c, dk), lambda h, i: (i, kq + h)),
                  pl.BlockSpec((c, dv), lambda h, i: (i, kq + h)), pl.BlockSpec((c, dk), lambda h, i: (i, h))],
        out_specs=[pl.BlockSpec((c, dv), lambda h, i: (i, h)),
                   pl.BlockSpec((None, None, dv, dk), lambda h, i: (h, i, 0, 0))],
        out_shape=[SDS((s, 2 * qk), F32), SDS((GLA_HEADS, nc, dv, dk), F32)],
        scratch_shapes=[pltpu.VMEM((dv, dk), F32)], sem=("parallel", "arbitrary"), args=(proj, proj, proj, g))


def gla_bwd(proj, g, states, do, *, name, host=None, c=GLA_CHUNK):
    s = proj.shape[0]
    qk = g.shape[1]
    dk, dv = qk // GLA_HEADS, 2 * qk // GLA_HEADS
    nc = s // c
    scale = dk ** -0.5
    kq = qk // dk

    def body(q_ref, k_ref, v_ref, g_ref, do_ref, st_ref, dq_ref, dk_ref, dv_ref, dg_ref, dstate, dgc):
        @pl.when(pl.program_id(1) == 0)
        def _():
            dstate[...] = jnp.zeros_like(dstate)
            dgc[...] = jnp.zeros_like(dgc)

        q, k, bl, eb, em, emi, el = _gla_chunk_terms(q_ref, k_ref, g_ref, c, scale)
        v2, do2 = _hilo(v_ref[...]), _hilo(do_ref[...])
        qe, qm, km, kd = _hilo(q * eb), _hilo(q * em), _hilo(k * emi), _hilo(k * el)
        ds = dstate[...]
        ds2 = _hilo(ds)
        a = _hilo(_causal(_dot3(qm, km, 1, 1)))
        dv_ref[...] = (_dot3(a, do2, 0, 0) + _dot3(kd, ds2, 1, 1)).astype(BF16)
        da = _hilo(_causal(_dot3(do2, v2, 1, 1)))
        dq = _dot3(da, km) * em + _dot3(do2, _hilo(st_ref[...])) * eb
        dkk = _dot3(da, qm, 0, 0) * emi + _dot3(v2, ds2) * el
        dstate[...] = ds * jnp.exp(bl) + _dot3(do2, qe, 0, 0)
        db = q * dq - k * dkk
        dg_ref[...] = _split_dot(_tri(c, "ge"), db) + dgc[...]
        dgc[...] += jnp.sum(db, axis=0, keepdims=True)
        dq_ref[...] = (dq * scale).astype(BF16)
        dk_ref[...] = dkk.astype(BF16)

    rev = lambda i: nc - 1 - i
    qspec = pl.BlockSpec((c, dk), lambda h, i: (rev(i), h))
    vspec = pl.BlockSpec((c, dv), lambda h, i: (rev(i), h))
    return _hosted_call(
        body, host, name=name, grid=(GLA_HEADS, nc),
        in_specs=[qspec, pl.BlockSpec((c, dk), lambda h, i: (rev(i), kq + h)),
                  pl.BlockSpec((c, dv), lambda h, i: (rev(i), kq + h)), qspec, vspec,
                  pl.BlockSpec((None, None, dv, dk), lambda h, i: (h, rev(i), 0, 0))],
        out_specs=[qspec, qspec, vspec, qspec],
        out_shape=[SDS((s, qk), BF16), SDS((s, qk), BF16), SDS((s, 2 * qk), BF16), SDS((s, qk), F32)],
        scratch_shapes=[pltpu.VMEM((dv, dk), F32), pltpu.VMEM((1, dk), F32)], sem=("parallel", "arbitrary"),
        args=(proj, proj, proj, g, do, states))


def gnorm_fwd(o, proj, gw, *, name, tr=512):
    s, v = o.shape
    dv = v // GLA_HEADS
    roff = 2 * GLA_HEADS

    def body(o_ref, r_ref, w_ref, y_ref):
        of = o_ref[...]
        rs = lax.rsqrt(jnp.mean(of * of, axis=-1, keepdims=True) + EPS)
        r = r_ref[...]
        y_ref[...] = (of * rs * w_ref[...] * (r * _sigmoid(r))).astype(BF16)

    blk = pl.BlockSpec((tr, dv), lambda i, h: (i, h))
    return pl.pallas_call(
        body, name=name, grid=(s // tr, GLA_HEADS),
        in_specs=[blk, pl.BlockSpec((tr, dv), lambda i, h: (i, roff + h)), pl.BlockSpec((1, dv), lambda i, h: (0, 0))],
        out_specs=blk, out_shape=SDS((s, v), BF16), compiler_params=_cp("parallel", "parallel"))(o, proj, gw)


def gnorm_bwd(dy, o, proj, gw, *, name, tr=512):
    s, v = o.shape
    dv = v // GLA_HEADS
    roff = 2 * GLA_HEADS

    def body(dy_ref, o_ref, r_ref, w_ref, do_ref, dr_ref, dw_ref):
        first = jnp.logical_and(pl.program_id(0) == 0, pl.program_id(1) == 0)
        of = o_ref[...]
        rs = lax.rsqrt(jnp.mean(of * of, axis=-1, keepdims=True) + EPS)
        n = of * rs
        r = r_ref[...]
        sg = _sigmoid(r)
        dyf = dy_ref[...].astype(F32)
        dn_w = dyf * (r * sg)
        dr_ref[...] = (dyf * n * w_ref[...] * sg * (1.0 + r * (1.0 - sg))).astype(BF16)
        dn = dn_w * w_ref[...]
        do_ref[...] = rs * (dn - n * jnp.mean(dn * n, axis=-1, keepdims=True))
        part = jnp.sum(dn_w * n, axis=0, keepdims=True)

        @pl.when(first)
        def _():
            dw_ref[...] = part

        @pl.when(jnp.logical_not(first))
        def _():
            dw_ref[...] += part

    blk = pl.BlockSpec((tr, dv), lambda i, h: (i, h))
    vec = pl.BlockSpec((1, dv), lambda i, h: (0, 0))
    return pl.pallas_call(
        body, name=name, grid=(s // tr, GLA_HEADS),
        in_specs=[blk, blk, pl.BlockSpec((tr, dv), lambda i, h: (i, roff + h)), vec],
        out_specs=[blk, blk, vec], out_shape=[SDS((s, v), F32), SDS((s, v), BF16), SDS((1, dv), F32)],
        compiler_params=_cp("arbitrary", "arbitrary"))(dy, o, proj, gw)


def _sb_block(kblk, q, ks, q0, scale, carry, masked):
    tk, tq = kblk.shape[0], q.shape[0]
    z = _dot(kblk, q, 1, 1) * scale
    sp = jnp.maximum(z, 0.0) + jnp.log(1.0 + jnp.exp(-jnp.abs(z)))
    mask = None
    lf = -sp
    if masked:
        kpos = ks + lax.broadcasted_iota(jnp.int32, (tk, tq), 0)
        qpos = q0 + lax.broadcasted_iota(jnp.int32, (tk, tq), 1)
        mask = kpos < qpos
        lf = jnp.where(mask, lf, 0.0)
    later = _split_dot(_tri(tk, "gt"), lf)
    a = jnp.exp(z - sp + later + carry)
    if masked:
        a = jnp.where(mask, a, 0.0)
    return z, sp, mask, lf, a


def sb_fwd(q, kv, *, name, host=None, tq=SB_TQ, tk=SB_TK, group=SB_GROUP):
    s, w = q.shape
    hd = w // SB_HEADS
    nq, nkb = s // tq, s // tk
    scale = hd ** -0.5
    per = tq // tk
    assert per % group == 0

    def body(q_ref, k_ref, v_ref, o_ref, car_ref, o_acc):
        qi = pl.program_id(1)
        qb = q_ref[...]
        q0 = qi * tq
        car_ref[...] = jnp.full(car_ref.shape, SB_SKIPPED, F32)
        o_acc[...] = jnp.zeros_like(o_acc)

        def blocks(first, carry, skip=None):
            for t in reversed(range(group)):
                kj = first + t
                ks = pl.multiple_of(kj * tk, tk)
                car_ref[pl.ds(kj, 1), :] = carry
                lo = 0 if skip is None else (skip + t) * tk
                _, _, _, lf, a = _sb_block(k_ref[pl.ds(ks, tk), :], qb[lo:, :], ks, q0 + lo, scale, carry[:, lo:], skip is not None)
                o_acc[lo:, :] += _dot(a.astype(BF16), v_ref[pl.ds(ks, tk), :], 0, 0)
                add = jnp.sum(lf, axis=0, keepdims=True)
                carry = carry + (add if lo == 0 else jnp.concatenate([jnp.zeros((1, lo), F32), add], axis=1))
            return carry

        carry = jnp.zeros((1, tq), F32)
        for gidx in reversed(range(per // group)):
            carry = blocks(qi * per + gidx * group, carry, gidx * group)
        n_in = qi * (per // group)
        lax.while_loop(lambda st: jnp.logical_and(st[0] < n_in, jnp.max(st[1]) > SB_DEAD),
                       lambda st: (st[0] + 1, blocks((n_in - 1 - st[0]) * group, st[1])), (jnp.int32(0), carry))
        o_ref[...] = o_acc[...].astype(BF16)

    qspec = pl.BlockSpec((tq, hd), lambda h, i: (i, h))
    return _hosted_call(
        body, host, name=name, grid=(SB_HEADS, nq),
        in_specs=[qspec, pl.BlockSpec((None, s, hd), lambda h, i: (0, 0, h)), pl.BlockSpec((None, s, hd), lambda h, i: (1, 0, h))],
        out_specs=[qspec, pl.BlockSpec((None, None, nkb, tq), lambda h, i: (h, i, 0, 0))],
        out_shape=[SDS((s, w), BF16), SDS((SB_HEADS, nq, nkb, tq), F32)],
        scratch_shapes=[pltpu.VMEM((tq, hd), F32)], sem=("parallel", "parallel"), args=(q, kv, kv))


def sb_bwd(q, kv, do, car, *, name, host=None, tq=SB_TQ, tk=SB_TK, group=SB_GROUP):
    s, w = q.shape
    hd = w // SB_HEADS
    nq, nkb = s // tq, s // tk
    scale = hd ** -0.5
    per = tq // tk
    assert per % group == 0

    def body(q_ref, k_ref, v_ref, do_ref, car_ref, dq_ref, dkv_ref, dq_acc, dk_acc, dv_acc):
        qi = pl.program_id(1)
        qb = q_ref[...]
        dob = do_ref[...]
        q0 = qi * tq
        dq_acc[...] = jnp.zeros_like(dq_acc)

        @pl.when(qi == 0)
        def _():
            dk_acc[...] = jnp.zeros_like(dk_acc)
            dv_acc[...] = jnp.zeros_like(dv_acc)

        def blocks(first, pcar, skip=None):
            for t in range(group):
                kj = first + t
                ks = pl.multiple_of(kj * tk, tk)
                kblk = k_ref[pl.ds(ks, tk), :]
                lo = 0 if skip is None else (skip + t) * tk
                masked = skip is not None
                qs, dos = qb[lo:, :], dob[lo:, :]
                z, sp, mask, _, a = _sb_block(kblk, qs, ks, q0 + lo, scale, car_ref[pl.ds(kj, 1), :][:, lo:], masked)
                p = a * _dot(v_ref[pl.ds(ks, tk), :], dos, 1, 1)
                before = _split_dot(_tri(tk, "lt"), p)
                sg = jnp.exp(z - sp)
                dz = p * (1.0 - sg) - (pcar[:, lo:] + before) * sg
                if masked:
                    dz = jnp.where(mask, dz, 0.0)
                dz = (dz * scale).astype(BF16)
                dk_acc[pl.ds(ks, tk), :] += _dot(dz, qs)
                dv_acc[pl.ds(ks, tk), :] += _dot(a.astype(BF16), dos)
                dq_acc[lo:, :] += _dot(dz, kblk, 0, 0)
                add = jnp.sum(p, axis=0, keepdims=True)
                pcar = pcar + (add if lo == 0 else jnp.concatenate([jnp.zeros((1, lo), F32), add], axis=1))
            return pcar

        n_in = qi * (per // group)

        def reached(g):
            return (jnp.max(car_ref[pl.ds(g * group + group - 1, 1), :]) > SB_DEAD).astype(jnp.int32)

        start = n_in - lax.fori_loop(0, n_in, lambda g, n: n + reached(g), jnp.int32(0))
        pcar = lax.fori_loop(start, n_in, lambda i, c: blocks(i * group, c), jnp.zeros((1, tq), F32))
        for gidx in range(per // group):
            pcar = blocks(qi * per + gidx * group, pcar, gidx * group)
        dq_ref[...] = dq_acc[...].astype(BF16)

        @pl.when(qi == nq - 1)
        def _():
            dkv_ref[0] = dk_acc[...].astype(BF16)
            dkv_ref[1] = dv_acc[...].astype(BF16)

    qspec = pl.BlockSpec((tq, hd), lambda h, i: (i, h))
    return _hosted_call(
        body, host, name=name, grid=(SB_HEADS, nq),
        in_specs=[qspec, pl.BlockSpec((None, s, hd), lambda h, i: (0, 0, h)), pl.BlockSpec((None, s, hd), lambda h, i: (1, 0, h)),
                  qspec, pl.BlockSpec((None, None, nkb, tq), lambda h, i: (h, i, 0, 0))],
        out_specs=[qspec, pl.BlockSpec((2, s, hd), lambda h, i: (0, 0, h))],
        out_shape=[SDS((s, w), BF16), SDS((2, s, w), BF16)],
        scratch_shapes=[pltpu.VMEM((tq, hd), F32), pltpu.VMEM((s, hd), F32), pltpu.VMEM((s, hd), F32)],
        sem=("parallel", "arbitrary"), args=(q, kv, kv, do, car))


def adamw(w, g, m, v, *, name):
    shape = w.shape
    c = shape[-1]
    r = w.size // c
    tr = _tile(r, max(8, (3 * LANES * 1024) // c), unit=8) if r >= 8 else r

    def body(w_ref, g_ref, m_ref, v_ref, d_ref, nm_ref, nv_ref):
        gf = g_ref[...]
        mn = ADAM_B1 * m_ref[...] + (1.0 - ADAM_B1) * gf
        vn = ADAM_B2 * v_ref[...] + (1.0 - ADAM_B2) * (gf * gf)
        m_hat = mn / (1.0 - ADAM_B1 ** ADAM_STEP)
        v_hat = vn / (1.0 - ADAM_B2 ** ADAM_STEP)
        d_ref[...] = -ADAM_LR * (m_hat / (jnp.sqrt(v_hat) + ADAM_EPS) + ADAM_WD * w_ref[...])
        nm_ref[...] = mn
        nv_ref[...] = vn

    blk = pl.BlockSpec((tr, c), lambda i: (i, 0))
    outs = pl.pallas_call(
        body, name=name, grid=(r // tr,), in_specs=[blk] * 4, out_specs=[blk] * 3,
        out_shape=[SDS((r, c), F32)] * 3, compiler_params=_cp("parallel"))(
            *(t.reshape(r, c) for t in (w, g, m, v)))
    return tuple(o.reshape(shape) for o in outs)


def _row_tile(r, c):
    return _tile(r, max(16, (4 * LANES * 1024) // c), unit=16)


def _halves(a):
    return a.reshape(a.shape[:-2] + (2, a.shape[-2] // 2, a.shape[-1]))


def place_shard(w, chip, *, name, layer=None):
    r, c = w.shape[-2:]
    tr = _row_tile(r, c)

    def body(c_ref, w_ref, o_ref):
        o_ref[...] = w_ref[...].astype(BF16)

    if layer is None:
        w_spec = pl.BlockSpec((None, tr, c), lambda h, i, c_ref: (h, i, 0))
    else:
        w_spec = pl.BlockSpec((None, None, tr, c), lambda h, i, c_ref: (layer, h, i, 0))
    return pl.pallas_call(
        body, name=name, out_shape=SDS((N_CHIPS, 2, r, c), BF16),
        grid_spec=pltpu.PrefetchScalarGridSpec(
            num_scalar_prefetch=1, grid=(2, r // tr), in_specs=[w_spec],
            out_specs=pl.BlockSpec((None, None, tr, c), lambda h, i, c_ref: (c_ref[0], h, i, 0))),
        compiler_params=_cp("parallel", "parallel"))(chip, w)


def sibling_exchange(gs, *, name):
    n = len(gs)

    def body(*refs):
        g_refs, a_refs = refs[:n], refs[n:2 * n]
        send_sems, recv_sems = refs[2 * n:]
        x, y, c = _coords()
        cps = [_remote(g_refs[i].at[:, 1 - c], a_refs[i], send_sems.at[i], recv_sems.at[i], (x, y, 1 - c)) for i in range(n)]
        for cp in cps:
            cp.start()
        for cp in cps:
            cp.wait()

    return pl.pallas_call(
        body, name=name, in_specs=[ANY] * n, out_specs=[ANY] * n,
        out_shape=[SDS(g.shape[:1] + g.shape[2:], g.dtype) for g in gs],
        scratch_shapes=[_dma_sems(n), _dma_sems(n)])(*gs)


def half_add(g, a, core, *, name):
    n, _, r, c = g.shape
    tr = _row_tile(r, c)

    def body(c_ref, g_ref, a_ref, o_ref):
        o_ref[...] = (g_ref[...].astype(F32) + a_ref[...].astype(F32)).astype(o_ref.dtype)

    blk = pl.BlockSpec((None, tr, c), lambda s, i, c_ref: (s, i, 0))
    return pl.pallas_call(
        body, name=name, out_shape=SDS((n, r, c), g.dtype),
        grid_spec=pltpu.PrefetchScalarGridSpec(
            num_scalar_prefetch=1, grid=(n, r // tr),
            in_specs=[pl.BlockSpec((None, None, tr, c), lambda s, i, c_ref: (s, c_ref[0], i, 0)), blk], out_specs=blk),
        compiler_params=_cp("parallel", "parallel"))(core, g, a)


def chip_sum(p, b, chip, core, *, name):
    _, r, c = p.shape
    tr = _row_tile(r, c)

    def body(chip_ref, core_ref, p_ref, b_ref, o_ref):
        t = p_ref[...].astype(F32)
        for k in range(N_CHIPS - 1):
            t = t + b_ref[k].astype(F32)
        o_ref[...] = t

    return pl.pallas_call(
        body, name=name, out_shape=SDS((2, r, c), F32),
        grid_spec=pltpu.PrefetchScalarGridSpec(
            num_scalar_prefetch=2, grid=(r // tr,),
            in_specs=[pl.BlockSpec((None, tr, c), lambda i, chip_ref, core_ref: (chip_ref[0], i, 0)),
                      pl.BlockSpec((N_CHIPS - 1, tr, c), lambda i, chip_ref, core_ref: (0, i, 0))],
            out_specs=pl.BlockSpec((None, tr, c), lambda i, chip_ref, core_ref: (core_ref[0], i, 0))),
        compiler_params=_cp("parallel"))(chip, core, p, b)


def sibling_gather(ts, *, name):
    n = len(ts)

    def body(*refs):
        out_refs = refs[n:2 * n]
        send_sems, recv_sems = refs[2 * n:]
        x, y, c = _coords()
        cps = [_remote(out_refs[i].at[c], out_refs[i].at[c], send_sems.at[i], recv_sems.at[i], (x, y, 1 - c)) for i in range(n)]
        for cp in cps:
            cp.start()
        for i in range(n):
            _remote(out_refs[i].at[c], out_refs[i].at[1 - c], send_sems.at[i], recv_sems.at[i], (x, y, 1 - c)).wait_recv()
        for cp in cps:
            cp.wait_send()

    return pl.pallas_call(
        body, name=name, in_specs=[ANY] * n, out_specs=[ANY] * n, out_shape=[SDS(t.shape, t.dtype) for t in ts],
        input_output_aliases={i: i for i in range(n)}, scratch_shapes=[_dma_sems(n), _dma_sems(n)])(*ts)


def allgather_all(sm, *, name):
    r, w = sm.shape

    def body(s_ref, out_ref, send_sems, recv_sems, local_sem):
        x, y, c = _coords()
        me = 4 * x + 2 * y + c

        def peer(rel):
            flip = lambda v, bit: 1 - v if bit else v
            return flip(x, rel & 4), flip(y, rel & 2), flip(c, rel & 1)

        loc = pltpu.make_async_copy(s_ref, out_ref.at[me], local_sem.at[0])
        loc.start()
        cps = [_remote(s_ref, out_ref.at[me], send_sems.at[rel - 1], recv_sems.at[rel - 1], peer(rel)) for rel in range(1, N_DEV)]
        for cp in cps:
            cp.start()
        for rel in range(1, N_DEV):
            px, py, pc = peer(rel)
            _remote(s_ref, out_ref.at[4 * px + 2 * py + pc], send_sems.at[rel - 1], recv_sems.at[rel - 1], (px, py, pc)).wait_recv()
        for cp in cps:
            cp.wait_send()
        loc.wait()

    return pl.pallas_call(
        body, name=name, in_specs=[ANY], out_specs=ANY, out_shape=SDS((N_DEV, r, w), sm.dtype),
        scratch_shapes=[_dma_sems(N_DEV - 1), _dma_sems(N_DEV - 1), _dma_sems(1)])(sm)


def sum_blocks(a, *, name):
    n, r, w = a.shape

    def body(a_ref, o_ref):
        t = a_ref[0]
        for k in range(1, n):
            t = t + a_ref[k]
        o_ref[...] = t

    return pl.pallas_call(body, name=name, out_shape=SDS((r, w), F32))(a)


class Exchanges:
    def __init__(self, placed=None, chip=None, core=None):
        self.placed, self.chip, self.core = placed, chip, core
        self.mine = {}

    def gather(self, names):
        return None if self.placed is None else GatherX([self.placed[n] for n in names])

    def reduce(self, grads, call):
        if self.placed is None:
            return call(None)
        names = list(grads)
        from_sibling = sibling_exchange([grads[n] for n in names], name="grads_to_sibling_" + names[0])
        pairs = [half_add(grads[n], a, self.core, name=f"pair_sum_{n}") for n, a in zip(names, from_sibling)]
        result, from_chips = call(ScatterX(pairs))
        for n, t, b in zip(names, pairs, from_chips):
            self.mine[n] = chip_sum(t, b, self.chip, self.core, name=f"chip_sum_{n}")
        return result


def _usable(name, buf, d):
    if name == "w_in":
        cols = N_CHIPS * buf.shape[-1]
        full = buf.reshape(N_CHIPS, d, -1).transpose(1, 0, 2).reshape(d, cols)
        return jnp.pad(full, ((0, 0), (0, -cols % LANES)))
    if name in ("w_out", "w_q", "w_so"):
        return buf.reshape(-1, buf.shape[-1])
    return buf.reshape(N_CHIPS, -1, buf.shape[-1])


def _by_rows(t):
    return _halves(t.reshape(N_CHIPS, -1, t.shape[-1]))


def _ffn_fwd(h, nw, w_gu, w_dn, tag, host=None):
    f = rmsnorm_fwd(h, nw, name=f"{tag}_norm")
    gu = matmul(f, View(w_gu, "cols"), name=f"{tag}_gate_up", out_dtype=BF16, host=host)
    gu, got = gu if host is not None else (gu, None)
    act = swiglu_fwd(gu, name=f"{tag}_act")
    return matmul(act, View(w_dn, "rows"), name=f"{tag}_down", residual=h), (f, gu, act), got


def _ffn_bwd(dh, h, nw, w_gu, w_dn, saved, tag, ex, dn_name, carried):
    f, gu, act = saved
    dact = matmul(dh, View(w_dn, "rows"), name=f"{tag}_dact", tb=True, out_dtype=BF16)
    dw_dn = matmul(act, dh, name=f"{tag}_dw_down", ta=True, out_dtype=BF16)
    dgu = swiglu_bwd(gu, dact, name=f"{tag}_dgu")

    def dw_gate_up(host):
        return matmul(f, dgu, name=f"{tag}_dw_gate_up", ta=True, out_dtype=BF16, out_chips=N_CHIPS, host=host)

    dw_gu = ex.reduce(carried, dw_gate_up) if carried else dw_gate_up(None)
    df = ex.reduce({dn_name: _by_rows(dw_dn)}, lambda host: matmul(dgu, View(w_gu, "cols"), name=f"{tag}_df", tb=True, host=host))
    dh_in, dnw = rmsnorm_bwd(df, h, nw, dh, name=f"{tag}_dnorm")
    return dh_in, dnw, dw_gu, dw_dn


def local_step(x, target, p, ex):
    d = x.shape[1]
    hosted = ex.placed is not None
    w = {} if hosted else {n: _usable(n, p[n], d) for n in ("w_in", "w_out", "w_kv", "w_q", "w_so", "w_gu0", "w_gu1", "w_dn0", "w_dn1")}

    def take(names, got):
        for n, buf in zip(names, got or []):
            w[n] = _usable(n, buf, d)

    def carry(call, names):
        host = ex.gather(names)
        if host is None:
            return call(None)
        res, got = call(host)
        take(names, got)
        return res

    if hosted:
        take(["w_in", "w_out"], run_exchange(ex.gather(["w_in", "w_out"]), name="gather_gla"))
    a0 = rmsnorm_fwd(x, p["an0"], name="l0_attn_norm")
    proj = carry(lambda host: matmul(a0, w["w_in"], name="gla_in", host=host), ["w_dn0"])
    g = gate_fwd(proj, p["wg"], p["bg"], name="gla_gate")
    o, states = carry(lambda host: gla_fwd(proj, g, name="gla_scan", host=host), ["w_gu0"])
    og = gnorm_fwd(o, proj, p["gw"], name="gla_outnorm")
    h1 = matmul(og, w["w_out"], name="gla_out", residual=x)
    sb_names = ["w_kv", "w_q", "w_so"]
    h2, ffn0, got = _ffn_fwd(h1, p["fn0"], w["w_gu0"], w["w_dn0"], "ffn0", host=ex.gather(sb_names))
    take(sb_names, got)
    w_kv = View(w["w_kv"], "cols")
    kvn = rmsnorm_fwd(h2, p["kvn"], name="kv_norm")
    kv = matmul(kvn, w_kv, name="sb_kv", out_dtype=BF16, out_chips=2)
    a1 = rmsnorm_fwd(h2, p["an1"], name="l1_attn_norm")
    q2 = matmul(a1, w["w_q"], name="sb_q", out_dtype=BF16)
    o2, car = carry(lambda host: sb_fwd(q2, kv, name="sb_attn", host=host), ["w_gu1", "w_dn1"])
    h3 = matmul(o2, w["w_so"], name="sb_out", residual=h2)
    h4, ffn1, _ = _ffn_fwd(h3, p["fn1"], w["w_gu1"], w["w_dn1"], "ffn1")
    dh4, d_fin, loss_row = final_loss(h4, p["finn"], target, name="final_loss")

    dh3, d_fn1, dw_gu1, dw_dn1 = _ffn_bwd(dh4, h3, p["fn1"], w["w_gu1"], w["w_dn1"], ffn1, "ffn1", ex, "w_dn1", {})
    do2 = matmul(dh3, w["w_so"], name="sb_do", tb=True, out_dtype=BF16)
    dw_so = matmul(o2, dh3, name="sb_dw_out", ta=True, out_dtype=BF16)
    dq2, dkv = ex.reduce({"w_gu1": _halves(dw_gu1)}, lambda host: sb_bwd(q2, kv, do2, car, name="sb_attn_bwd", host=host))
    dkv = View(dkv, "cols")
    dw_q = matmul(a1, dq2, name="sb_dw_q", ta=True, out_dtype=BF16)
    da1 = matmul(dq2, w["w_q"], name="sb_da", tb=True)
    dh2, d_an1 = rmsnorm_bwd(da1, h2, p["an1"], dh3, name="l1_attn_dnorm")
    dw_kv = matmul(kvn, dkv, name="sb_dw_kv", ta=True, out_dtype=BF16, out_chips=N_CHIPS)
    dkvn = matmul(dkv, w_kv, name="sb_dkvn", tb=True)
    dh2, d_kvn = rmsnorm_bwd(dkvn, h2, p["kvn"], dh2, name="kv_dnorm")
    sb_grads = {"w_kv": _halves(dw_kv), "w_q": _by_rows(dw_q), "w_so": _by_rows(dw_so)}
    dh1, d_fn0, dw_gu0, dw_dn0 = _ffn_bwd(dh2, h1, p["fn0"], w["w_gu0"], w["w_dn0"], ffn0, "ffn0", ex, "w_dn0", sb_grads)
    dog = matmul(dh1, w["w_out"], name="gla_dog", tb=True, out_dtype=BF16)
    dw_out = matmul(og, dh1, name="gla_dw_out", ta=True, out_dtype=BF16)
    do, dr, d_gw = gnorm_bwd(dog, o, proj, p["gw"], name="gla_outnorm_bwd")
    dq, dk, dv, dg = ex.reduce({"w_gu0": _halves(dw_gu0)}, lambda host: gla_bwd(proj, g, states, do, name="gla_scan_bwd", host=host))
    dgl, d_wg, d_bg = gate_bwd(dg, proj, p["wg"], p["bg"], name="gla_gate_bwd")
    dproj = jnp.concatenate([dq, dk, dv, dr, dgl], axis=1)
    dw_in = matmul(a0, dproj, name="gla_dw_in", ta=True, out_dtype=BF16)
    in_w = p["in_w"]
    gla_grads = {"w_in": _halves(dw_in[:, :in_w].reshape(d, N_CHIPS, -1).transpose(1, 0, 2)), "w_out": _by_rows(dw_out)}
    da0 = ex.reduce(gla_grads, lambda host: matmul(dproj, w["w_in"], name="gla_da", tb=True, host=host))
    dx, d_an0 = rmsnorm_bwd(da0, x, p["an0"], dh1, name="l0_attn_dnorm")

    small = dict(an0=d_an0, an1=d_an1, fn0=d_fn0, fn1=d_fn1, kvn=d_kvn, finn=d_fin, wg=d_wg, bg=d_bg, gw=d_gw)
    big = {}
    if not hosted:
        big = dict(gla_grads, **sb_grads, w_gu0=_halves(dw_gu0), w_gu1=_halves(dw_gu1), w_dn0=_by_rows(dw_dn0), w_dn1=_by_rows(dw_dn1))
    return loss_row, dx, small, big


def _pack_rows(parts):
    cat = jnp.concatenate([t.reshape(-1) for t in parts])
    rows = -(-cat.size // (LANES * SUBLANES)) * SUBLANES
    return jnp.pad(cat, (0, rows * LANES - cat.size)).reshape(rows, LANES)


def _segments(flat, sizes):
    out, off = [], 0
    for n in sizes:
        out.append(flat[..., off:off + n])
        off += n
    return out


def kernel(x, attn_norm_w, ffn_norm_w, gla_w_in, gla_w_gate_up, gla_b_gate, gla_gnorm_w, gla_w_out, kv_norm_w, sb_w_kv, sb_w_q, sb_w_out, ffn_w_gate_up, ffn_w_down, final_norm_w, loss_target, m_attn_norm_w, m_ffn_norm_w, m_gla_w_in, m_gla_w_gate_up, m_gla_b_gate, m_gla_gnorm_w, m_gla_w_out, m_kv_norm_w, m_sb_w_kv, m_sb_w_q, m_sb_w_out, m_ffn_w_gate_up, m_ffn_w_down, m_final_norm_w, v_attn_norm_w, v_ffn_norm_w, v_gla_w_in, v_gla_w_gate_up, v_gla_b_gate, v_gla_gnorm_w, v_gla_w_out, v_kv_norm_w, v_sb_w_kv, v_sb_w_q, v_sb_w_out, v_ffn_w_gate_up, v_ffn_w_down, v_final_norm_w):
    xi, yi, ci = _coords()
    core = ci.astype(jnp.int32).reshape(1)
    chip1 = (2 * xi + yi).astype(jnp.int32)
    chip = chip1.reshape(1)
    rank = gla_w_gate_up.shape[1]

    gu4, dn4 = _halves(ffn_w_gate_up), _halves(ffn_w_down)
    shards = dict(w_in=(_halves(gla_w_in[0]), None), w_out=(_halves(gla_w_out[0]), None), w_kv=(_halves(sb_w_kv), None),
                  w_q=(_halves(sb_w_q[0]), None), w_so=(_halves(sb_w_out[0]), None),
                  w_gu0=(gu4, 0), w_gu1=(gu4, 1), w_dn0=(dn4, 0), w_dn1=(dn4, 1))
    placed = {n: place_shard(a, chip, name=f"place_{n}", layer=l) for n, (a, l) in shards.items()}
    small_w = [gla_w_gate_up, gla_b_gate, gla_gnorm_w]
    small_all = allgather_all(_pack_rows(small_w), name="gather_gate_weights")
    wg, bg, gw = _segments(small_all[::2].reshape(N_CHIPS, -1), [a.size for a in small_w])
    wg = wg.reshape(N_CHIPS, rank, -1).transpose(1, 0, 2).reshape(rank, -1)
    p = dict(an0=attn_norm_w[0], an1=attn_norm_w[1], fn0=ffn_norm_w[0], fn1=ffn_norm_w[1], kvn=kv_norm_w, finn=final_norm_w,
             wg=jnp.pad(wg, ((0, LANES - rank), (0, 0))).astype(BF16), bg=bg.reshape(1, -1), gw=gw.reshape(1, -1),
             in_w=N_CHIPS * gla_w_in.shape[-1])

    ex = Exchanges(placed, chip, core)
    loss_row, dx, g, _ = local_step(x[0], loss_target[0], p, ex)
    loss = lax.psum(loss_row[0, 0], ("x", "y", "c"))

    tags = list(ex.mine)
    tot = dict(zip(tags, sibling_gather([ex.mine[n] for n in tags], name="grads_from_sibling")))

    vecs = [jnp.concatenate([g["an0"], g["an1"]]), jnp.concatenate([g["fn0"], g["fn1"]]), g["kvn"], g["finn"],
            g["wg"][:rank], g["bg"], g["gw"]]
    gathered_vecs = allgather_all(_pack_rows(vecs), name="gather_small_grads")
    d_an, d_fn, d_kvn, d_fin, d_wg, d_bg, d_gw = _segments(
        sum_blocks(gathered_vecs, name="sum_small_grads").reshape(-1), [t.size for t in vecs])

    def shard(t, like):
        return lax.dynamic_index_in_dim(t.reshape(-1, N_CHIPS, like.shape[-1]), chip1, axis=1, keepdims=False).reshape(like.shape)

    weights = dict(
        attn_norm_w=(attn_norm_w, m_attn_norm_w, v_attn_norm_w), ffn_norm_w=(ffn_norm_w, m_ffn_norm_w, v_ffn_norm_w),
        gla_w_in=(gla_w_in, m_gla_w_in, v_gla_w_in), gla_w_gate_up=(gla_w_gate_up, m_gla_w_gate_up, v_gla_w_gate_up),
        gla_b_gate=(gla_b_gate, m_gla_b_gate, v_gla_b_gate), gla_gnorm_w=(gla_gnorm_w, m_gla_gnorm_w, v_gla_gnorm_w),
        gla_w_out=(gla_w_out, m_gla_w_out, v_gla_w_out), kv_norm_w=(kv_norm_w, m_kv_norm_w, v_kv_norm_w),
        sb_w_kv=(sb_w_kv, m_sb_w_kv, v_sb_w_kv), sb_w_q=(sb_w_q, m_sb_w_q, v_sb_w_q), sb_w_out=(sb_w_out, m_sb_w_out, v_sb_w_out),
        ffn_w_gate_up=(ffn_w_gate_up, m_ffn_w_gate_up, v_ffn_w_gate_up), ffn_w_down=(ffn_w_down, m_ffn_w_down, v_ffn_w_down),
        final_norm_w=(final_norm_w, m_final_norm_w, v_final_norm_w))
    grads = dict(
        attn_norm_w=d_an.reshape(attn_norm_w.shape), ffn_norm_w=d_fn.reshape(ffn_norm_w.shape),
        gla_w_in=tot["w_in"].reshape(gla_w_in.shape), gla_w_gate_up=shard(d_wg, gla_w_gate_up),
        gla_b_gate=shard(d_bg, gla_b_gate), gla_gnorm_w=shard(d_gw, gla_gnorm_w),
        gla_w_out=tot["w_out"].reshape(gla_w_out.shape), kv_norm_w=d_kvn.reshape(kv_norm_w.shape),
        sb_w_kv=tot["w_kv"].reshape(sb_w_kv.shape), sb_w_q=tot["w_q"].reshape(sb_w_q.shape),
        sb_w_out=tot["w_so"].reshape(sb_w_out.shape),
        ffn_w_gate_up=jnp.stack([tot["w_gu0"], tot["w_gu1"]]).reshape(ffn_w_gate_up.shape),
        ffn_w_down=jnp.stack([tot["w_dn0"], tot["w_dn1"]]).reshape(ffn_w_down.shape),
        final_norm_w=d_fin.reshape(final_norm_w.shape))
    names = list(weights)
    stepped = [adamw(weights[n][0], grads[n], weights[n][1], weights[n][2], name=f"adamw_{n}") for n in names]
    return (loss, dx.reshape(x.shape), *[grads[n] for n in names], *[t[0] for t in stepped], *[t[1] for t in stepped],
            *[t[2] for t in stepped])
```

```python
import functools

import jax
import jax.numpy as jnp
from jax import lax
from jax.experimental import pallas as pl
from jax.experimental.pallas import tpu as pltpu

F32 = jnp.float32
BF16 = jnp.bfloat16
SDS = jax.ShapeDtypeStruct
MESH = pl.DeviceIdType.MESH

EPS = 1e-6
GLA_HEADS = 4
GLA_GATE_RANK = 16
GLA_GATE_TAU = 16.0
GLA_CHUNK = 128
SB_HEADS = 16
SB_TQ = 512
SB_TK = 128
SB_GROUP = 2
SB_DEAD = -110.0
SB_SKIPPED = -1e30
ADAM_LR = 0.001
ADAM_B1 = 0.9
ADAM_B2 = 0.999
ADAM_EPS = 1e-08
ADAM_WD = 0.01
ADAM_STEP = 10

LANES = 128
SUBLANES = 8
N_CHIPS = 4
N_DEV = 8
VMEM_LIMIT = 56 * 1024 * 1024


def _tile(dim, target, unit=LANES):
    if dim <= target:
        return dim
    t = (target // unit) * unit
    while t >= unit:
        if dim % t == 0:
            return t
        t -= unit
    raise ValueError(f"no tile for {dim}")


def _cp(*sem):
    return pltpu.CompilerParams(dimension_semantics=sem, vmem_limit_bytes=VMEM_LIMIT)


def _sigmoid(x):
    return 1.0 / (1.0 + jnp.exp(-x))


def _dot(a, b, ca=1, cb=0):
    return lax.dot_general(a, b, (((ca,), (cb,)), ((), ())), preferred_element_type=F32)


def _split_dot(tri, x):
    hi = x.astype(BF16)
    lo = (x - hi.astype(F32)).astype(BF16)
    return _dot(tri, hi) + _dot(tri, lo)


def _hilo(x):
    hi = x.astype(BF16)
    return hi, (x - hi.astype(F32)).astype(BF16)


def _dot3(a, b, ca=1, cb=0):
    return _dot(a[0], b[0], ca, cb) + _dot(a[0], b[1], ca, cb) + _dot(a[1], b[0], ca, cb)


def _tri(n, kind):
    r = lax.broadcasted_iota(jnp.int32, (n, n), 0)
    c = lax.broadcasted_iota(jnp.int32, (n, n), 1)
    m = {"le": c <= r, "ge": c >= r, "lt": c < r, "gt": c > r}[kind]
    return jnp.where(m, 1.0, 0.0).astype(BF16)


ANY = pl.BlockSpec(memory_space=pl.ANY)


def _coords():
    return lax.axis_index("x"), lax.axis_index("y"), lax.axis_index("c")


def _other_chips(x, y):
    return [(1 - x, y), (x, 1 - y), (1 - x, 1 - y)]


def _remote(src, dst, send_sem, recv_sem, dev):
    return pltpu.make_async_remote_copy(src_ref=src, dst_ref=dst, send_sem=send_sem, recv_sem=recv_sem,
                                        device_id=dev, device_id_type=MESH)


def _dma_sems(n):
    return pltpu.SemaphoreType.DMA((n,))


class GatherX:
    def __init__(self, bufs):
        self.ins, self.ios, self.outs, self.n_sems = [], list(bufs), [], 6 * len(bufs)

    def _copy(self, ios, send_sems, recv_sems, i, k, chip, half, dev):
        blk = ios[i].at[chip, half]
        return _remote(blk, blk, send_sems.at[6 * i + k], recv_sems.at[6 * i + k], dev)

    def _first(self, ios, send_sems, recv_sems):
        x, y, c = _coords()
        return [self._copy(ios, send_sems, recv_sems, i, k, 2 * x + y, c, (cx, cy, c))
                for i in range(len(ios)) for k, (cx, cy) in enumerate(_other_chips(x, y))]

    def start(self, ins, ios, outs, send_sems, recv_sems):
        for cp in self._first(ios, send_sems, recv_sems):
            cp.start()

    def finish(self, ins, ios, outs, send_sems, recv_sems):
        x, y, c = _coords()
        chips = _other_chips(x, y)
        copy = functools.partial(self._copy, ios, send_sems, recv_sems)
        passed = []
        for i in range(len(ios)):
            for k, (cx, cy) in enumerate(chips):
                copy(i, k, 2 * cx + cy, c, (x, y, c)).wait_recv()
                passed.append(copy(i, 3 + k, 2 * cx + cy, c, (x, y, 1 - c)))
                passed[-1].start()
        for i in range(len(ios)):
            for k, (cx, cy) in enumerate(chips):
                copy(i, 3 + k, 2 * cx + cy, 1 - c, (x, y, c)).wait_recv()
        for cp in self._first(ios, send_sems, recv_sems) + passed:
            cp.wait_send()


class ScatterX:
    def __init__(self, ps):
        self.ins, self.ios, self.n_sems = list(ps), [], 3 * len(ps)
        self.outs = [SDS((N_CHIPS - 1,) + p.shape[1:], p.dtype) for p in ps]

    def _copies(self, ins, outs, send_sems, recv_sems):
        x, y, c = _coords()
        return [_remote(ins[i].at[2 * cx + cy], outs[i].at[k], send_sems.at[3 * i + k], recv_sems.at[3 * i + k], (cx, cy, c))
                for i in range(len(ins)) for k, (cx, cy) in enumerate(_other_chips(x, y))]

    def start(self, ins, ios, outs, send_sems, recv_sems):
        for cp in self._copies(ins, outs, send_sems, recv_sems):
            cp.start()

    def finish(self, ins, ios, outs, send_sems, recv_sems):
        for cp in self._copies(ins, outs, send_sems, recv_sems):
            cp.wait()


def _exchange_operands(host):
    x_in = host.ins + host.ios
    x_out = [SDS(a.shape, a.dtype) for a in host.ios] + host.outs
    return x_in, x_out


def run_exchange(host, *, name):
    x_in, x_out = _exchange_operands(host)
    n_ins, n_ios = len(host.ins), len(host.ios)

    def body(*refs):
        xin, xout = refs[:len(x_in)], refs[len(x_in):len(x_in) + len(x_out)]
        send_sems, recv_sems = refs[len(x_in) + len(x_out):]
        x_refs = (xin[:n_ins], xout[:n_ios], xout[n_ios:])
        host.start(*x_refs, send_sems, recv_sems)
        host.finish(*x_refs, send_sems, recv_sems)

    return list(pl.pallas_call(
        body, name=name, in_specs=[ANY] * len(x_in), out_specs=[ANY] * len(x_out), out_shape=x_out,
        input_output_aliases={n_ins + i: i for i in range(n_ios)},
        scratch_shapes=[_dma_sems(host.n_sems), _dma_sems(host.n_sems)])(*x_in))


def _hosted_call(body, host, *, name, grid, in_specs, out_specs, out_shape, scratch_shapes, sem, args):
    if host is None:
        return list(pl.pallas_call(body, name=name, grid=grid, in_specs=in_specs, out_specs=out_specs, out_shape=out_shape,
                                   scratch_shapes=scratch_shapes, compiler_params=_cp(*sem))(*args))
    x_in, x_out = _exchange_operands(host)
    n_in, n_out, n_scr, n_ins, n_ios = len(in_specs), len(out_specs), len(scratch_shapes), len(host.ins), len(host.ios)

    def hosted(*refs):
        ins, xin = refs[:n_in], refs[n_in:n_in + len(x_in)]
        o0 = n_in + len(x_in)
        outs, xout = refs[o0:o0 + n_out], refs[o0 + n_out:o0 + n_out + len(x_out)]
        s0 = o0 + n_out + len(x_out)
        scr, (send_sems, recv_sems) = refs[s0:s0 + n_scr], refs[s0 + n_scr:]
        ids = [pl.program_id(ax) for ax in range(len(grid))]
        first = functools.reduce(jnp.logical_and, [i == 0 for i in ids])
        last = functools.reduce(jnp.logical_and, [i == n - 1 for i, n in zip(ids, grid)])
        x_refs = (xin[:n_ins], xout[:n_ios], xout[n_ios:])

        @pl.when(first)
        def _():
            host.start(*x_refs, send_sems, recv_sems)

        body(*ins, *outs, *scr)

        @pl.when(last)
        def _():
            host.finish(*x_refs, send_sems, recv_sems)

    res = pl.pallas_call(
        hosted, name=name, grid=grid, in_specs=list(in_specs) + [ANY] * len(x_in), out_specs=list(out_specs) + [ANY] * len(x_out),
        out_shape=list(out_shape) + x_out, scratch_shapes=list(scratch_shapes) + [_dma_sems(host.n_sems), _dma_sems(host.n_sems)],
        input_output_aliases={n_in + n_ins + i: n_out + i for i in range(n_ios)},
        compiler_params=_cp(*(("arbitrary",) * len(grid))))(*args, *x_in)
    return list(res[:n_out]), list(res[n_out:])


def _div(i, n):
    return i if n == 1 else lax.div(i, n)


def _rem(i, n):
    return 0 if n == 1 else lax.rem(i, n)


class View:
    def __init__(self, arr, kind="plain", lead=(), g0=0, ng=None):
        self.arr, self.kind, self.lead, self.g0 = arr, kind, tuple(lead), g0
        self.ng = (arr.shape[0] - g0) if ng is None else ng
        r, c = arr.shape[-2:]
        self.runit, self.cunit = r, c
        self.shape = {"plain": (r, c), "cols": (r, self.ng * c), "rows": (self.ng * r, c)}[kind]

    def spec(self, br, bc, rfn, cfn):
        if self.kind == "plain":
            return pl.BlockSpec((br, bc), lambda *g: (rfn(*g), cfn(*g)))
        none = (None,) * (1 + len(self.lead))
        if self.kind == "cols":
            per = self.cunit // bc
            return pl.BlockSpec(none + (br, bc), lambda *g: (self.g0 + _div(cfn(*g), per), *self.lead, rfn(*g), _rem(cfn(*g), per)))
        per = self.runit // br
        return pl.BlockSpec(none + (br, bc), lambda *g: (self.g0 + _div(rfn(*g), per), *self.lead, _rem(rfn(*g), per), cfn(*g)))


def _as_view(a):
    return a if isinstance(a, View) else View(a)


def matmul(a, b, *, name, ta=False, tb=False, out_dtype=F32, residual=None, out_chips=None, host=None, tm=1408, tn=1408, tk=2816):
    a, b = _as_view(a), _as_view(b)
    (k, m) = a.shape if ta else a.shape[::-1]
    (n, kb) = b.shape if tb else b.shape[::-1]
    assert k == kb, (a.shape, b.shape, ta, tb)
    m_unit = a.cunit if ta else a.runit
    ka_unit = a.runit if ta else a.cunit
    n_unit = b.runit if tb else b.cunit
    kb_unit = b.cunit if tb else b.runit
    if out_chips is not None:
        n_unit = min(n_unit, n // out_chips)
    tm, tn = _tile(min(m, m_unit), tm), _tile(min(n, n_unit), tn)
    tk = _tile(min(k, ka_unit, kb_unit), tk)
    assert ka_unit % tk == 0 and kb_unit % tk == 0, (ka_unit, kb_unit, tk)
    nk = k // tk
    ca, cb = (0 if ta else 1), (1 if tb else 0)

    def body(a_ref, b_ref, *refs):
        r_ref = refs[0] if residual is not None else None
        o_ref = refs[-1] if nk == 1 else refs[-2]

        def finish(r):
            if residual is not None:
                r = r + r_ref[...]
            o_ref[...] = r.astype(out_dtype)

        part = _dot(a_ref[...].astype(BF16), b_ref[...].astype(BF16), ca, cb)
        if nk == 1:
            finish(part)
            return
        acc = refs[-1]
        kk = pl.program_id(2)

        @pl.when(kk == 0)
        def _():
            acc[...] = part

        @pl.when(kk > 0)
        def _():
            acc[...] += part

        @pl.when(kk == nk - 1)
        def _():
            finish(acc[...])

    gi, gj, gk = (lambda i, j, kk: i), (lambda i, j, kk: j), (lambda i, j, kk: kk)
    a_spec = a.spec(tk, tm, gk, gi) if ta else a.spec(tm, tk, gi, gk)
    b_spec = b.spec(tn, tk, gj, gk) if tb else b.spec(tk, tn, gk, gj)
    if out_chips is None:
        out = View(SDS((m, n), out_dtype))
    else:
        out = View(SDS((out_chips, m, n // out_chips), out_dtype), "cols")
    o_spec = out.spec(tm, tn, gi, gj)
    in_specs, args = [a_spec, b_spec], [a.arr, b.arr]
    if residual is not None:
        in_specs.append(pl.BlockSpec((tm, tn), lambda i, j, kk: (i, j)))
        args.append(residual)
    res = _hosted_call(
        body, host, name=name, grid=(m // tm, n // tn, nk), in_specs=in_specs, out_specs=[o_spec], out_shape=[out.arr],
        scratch_shapes=[] if nk == 1 else [pltpu.VMEM((tm, tn), F32)], sem=("parallel", "parallel", "arbitrary"), args=args)
    return res[0] if host is None else (res[0][0], res[1])


def rmsnorm_fwd(x, w, *, name, tr=256):
    s, d = x.shape

    def body(x_ref, w_ref, o_ref):
        xf = x_ref[...]
        r = lax.rsqrt(jnp.mean(xf * xf, axis=-1, keepdims=True) + EPS)
        o_ref[...] = (xf * r * w_ref[...]).astype(BF16)

    row = pl.BlockSpec((tr, d), lambda i: (i, 0))
    return pl.pallas_call(
        body, name=name, grid=(s // tr,), in_specs=[row, pl.BlockSpec((1, d), lambda i: (0, 0))], out_specs=row,
        out_shape=SDS((s, d), BF16), compiler_params=_cp("parallel"))(x, w.reshape(1, d))


def rmsnorm_bwd(dy, x, w, dres, *, name, tr=256):
    s, d = x.shape

    def body(dy_ref, x_ref, w_ref, dres_ref, dx_ref, dw_ref):
        i = pl.program_id(0)
        xf = x_ref[...]
        r = lax.rsqrt(jnp.mean(xf * xf, axis=-1, keepdims=True) + EPS)
        xh = xf * r
        dyf = dy_ref[...].astype(F32)
        dxh = dyf * w_ref[...]
        dx_ref[...] = dres_ref[...] + r * (dxh - xh * jnp.mean(dxh * xh, axis=-1, keepdims=True))
        part = jnp.sum(dyf * xh, axis=0, keepdims=True)

        @pl.when(i == 0)
        def _():
            dw_ref[...] = part

        @pl.when(i > 0)
        def _():
            dw_ref[...] += part

    row = pl.BlockSpec((tr, d), lambda i: (i, 0))
    vec = pl.BlockSpec((1, d), lambda i: (0, 0))
    return pl.pallas_call(
        body, name=name, grid=(s // tr,), in_specs=[row, row, vec, row], out_specs=[row, vec],
        out_shape=[SDS((s, d), F32), SDS((1, d), F32)], compiler_params=_cp("arbitrary"))(dy, x, w.reshape(1, d), dres)


def final_loss(h, w, target, *, name, tr=256):
    s, d = h.shape

    def body(h_ref, w_ref, t_ref, dh_ref, dw_ref, loss_ref):
        i = pl.program_id(0)
        xf = h_ref[...]
        r = lax.rsqrt(jnp.mean(xf * xf, axis=-1, keepdims=True) + EPS)
        xh = xf * r
        err = xh * w_ref[...] - t_ref[...]
        lpart = 0.5 * jnp.sum(jnp.sum(err * err, axis=-1, keepdims=True) * (1.0 / d), axis=0, keepdims=True)
        dy = err * (1.0 / d)
        dxh = dy * w_ref[...]
        dh_ref[...] = r * (dxh - xh * jnp.mean(dxh * xh, axis=-1, keepdims=True))
        part = jnp.sum(dy * xh, axis=0, keepdims=True)
        lrow = jnp.broadcast_to(lpart, (1, LANES))

        @pl.when(i == 0)
        def _():
            dw_ref[...] = part
            loss_ref[...] = lrow

        @pl.when(i > 0)
        def _():
            dw_ref[...] += part
            loss_ref[...] += lrow

    row = pl.BlockSpec((tr, d), lambda i: (i, 0))
    vec = pl.BlockSpec((1, d), lambda i: (0, 0))
    return pl.pallas_call(
        body, name=name, grid=(s // tr,), in_specs=[row, vec, row],
        out_specs=[row, vec, pl.BlockSpec((1, LANES), lambda i: (0, 0))],
        out_shape=[SDS((s, d), F32), SDS((1, d), F32), SDS((1, LANES), F32)],
        compiler_params=_cp("arbitrary"))(h, w.reshape(1, d), target)


def swiglu_fwd(gu, *, name, tr=512, tc=1408):
    s, f2 = gu.shape
    f = f2 // 2
    tc = _tile(f, tc)
    nf = f // tc

    def body(g_ref, u_ref, o_ref):
        g = g_ref[...].astype(F32)
        o_ref[...] = (g * _sigmoid(g) * u_ref[...].astype(F32)).astype(BF16)

    return pl.pallas_call(
        body, name=name, grid=(s // tr, nf),
        in_specs=[pl.BlockSpec((tr, tc), lambda i, j: (i, j)), pl.BlockSpec((tr, tc), lambda i, j: (i, j + nf))],
        out_specs=pl.BlockSpec((tr, tc), lambda i, j: (i, j)), out_shape=SDS((s, f), BF16),
        compiler_params=_cp("parallel", "parallel"))(gu, gu)


def swiglu_bwd(gu, dact, *, name, tr=256):
    s, f2 = gu.shape
    f = f2 // 2

    def body(gu_ref, d_ref, o_ref):
        g = gu_ref[:, :f].astype(F32)
        d = d_ref[...].astype(F32)
        sg = _sigmoid(g)
        o_ref[:, :f] = (d * gu_ref[:, f:].astype(F32) * sg * (1.0 + g * (1.0 - sg))).astype(BF16)
        o_ref[:, f:] = (d * g * sg).astype(BF16)

    return pl.pallas_call(
        body, name=name, grid=(s // tr,),
        in_specs=[pl.BlockSpec((tr, f2), lambda i: (i, 0)), pl.BlockSpec((tr, f), lambda i: (i, 0))],
        out_specs=pl.BlockSpec((tr, f2), lambda i: (i, 0)), out_shape=SDS((s, f2), BF16),
        compiler_params=_cp("parallel"))(gu, dact)


def _gate_z(gl_ref, w_ref, b_ref):
    glb = gl_ref[...].astype(BF16)
    return glb, _dot(glb, w_ref[...]) + b_ref[...]


def gate_fwd(proj, wg, bg, *, name, tr=512):
    s, inw = proj.shape
    qk = wg.shape[1]
    glc = inw // LANES - 1

    def body(gl_ref, w_ref, b_ref, g_ref):
        _, z = _gate_z(gl_ref, w_ref, b_ref)
        g_ref[...] = (jnp.minimum(z, 0.0) - jnp.log(1.0 + jnp.exp(-jnp.abs(z)))) * (1.0 / GLA_GATE_TAU)

    return pl.pallas_call(
        body, name=name, grid=(s // tr,),
        in_specs=[pl.BlockSpec((tr, LANES), lambda i: (i, glc)), pl.BlockSpec((LANES, qk), lambda i: (0, 0)),
                  pl.BlockSpec((1, qk), lambda i: (0, 0))],
        out_specs=pl.BlockSpec((tr, qk), lambda i: (i, 0)), out_shape=SDS((s, qk), F32),
        compiler_params=_cp("parallel"))(proj, wg, bg)


def gate_bwd(dg, proj, wg, bg, *, name, tr=512):
    s, inw = proj.shape
    qk = wg.shape[1]
    glc = inw // LANES - 1

    def body(dg_ref, gl_ref, w_ref, b_ref, dgl_ref, dw_ref, db_ref):
        i = pl.program_id(0)
        glb, z = _gate_z(gl_ref, w_ref, b_ref)
        dz = dg_ref[...] * (1.0 / (1.0 + jnp.exp(z))) * (1.0 / GLA_GATE_TAU)
        dzb = dz.astype(BF16)
        dgl_ref[...] = _dot(dzb, w_ref[...], 1, 1).astype(BF16)
        pw = _dot(glb, dzb, 0, 0)
        pb = jnp.sum(dz, axis=0, keepdims=True)

        @pl.when(i == 0)
        def _():
            dw_ref[...] = pw
            db_ref[...] = pb

        @pl.when(i > 0)
        def _():
            dw_ref[...] += pw
            db_ref[...] += pb

    return pl.pallas_call(
        body, name=name, grid=(s // tr,),
        in_specs=[pl.BlockSpec((tr, qk), lambda i: (i, 0)), pl.BlockSpec((tr, LANES), lambda i: (i, glc)),
                  pl.BlockSpec((LANES, qk), lambda i: (0, 0)), pl.BlockSpec((1, qk), lambda i: (0, 0))],
        out_specs=[pl.BlockSpec((tr, LANES), lambda i: (i, 0)), pl.BlockSpec((LANES, qk), lambda i: (0, 0)),
                   pl.BlockSpec((1, qk), lambda i: (0, 0))],
        out_shape=[SDS((s, LANES), BF16), SDS((LANES, qk), F32), SDS((1, qk), F32)],
        compiler_params=_cp("arbitrary"))(dg, proj, wg, bg)


def _gla_chunk_terms(q_ref, k_ref, g_ref, c, scale):
    q = q_ref[...] * scale
    k = k_ref[...]
    gg = g_ref[...]
    b = _split_dot(_tri(c, "le"), gg)
    row = lax.broadcasted_iota(jnp.int32, gg.shape, 0)
    bm = jnp.sum(jnp.where(row < c // 2, gg, 0.0), axis=0, keepdims=True)
    bl = jnp.sum(gg, axis=0, keepdims=True)
    eb, em, emi, el = jnp.exp(b), jnp.exp(b - bm), jnp.exp(bm - b), jnp.exp(bl - b)
    return q, k, bl, eb, em, emi, el


def _causal(a):
    r = lax.broadcasted_iota(jnp.int32, a.shape, 0)
    c = lax.broadcasted_iota(jnp.int32, a.shape, 1)
    return jnp.where(r >= c, a, 0.0)


def gla_fwd(proj, g, *, name, host=None, c=GLA_CHUNK):
    s = proj.shape[0]
    qk = g.shape[1]
    dk, dv = qk // GLA_HEADS, 2 * qk // GLA_HEADS
    nc = s // c
    scale = dk ** -0.5
    kq = qk // dk

    def body(q_ref, k_ref, v_ref, g_ref, o_ref, st_ref, state):
        @pl.when(pl.program_id(1) == 0)
        def _():
            state[...] = jnp.zeros_like(state)

        q, k, bl, eb, em, emi, el = _gla_chunk_terms(q_ref, k_ref, g_ref, c, scale)
        v2 = _hilo(v_ref[...])
        st = state[...]
        st_ref[...] = st
        a = _causal(_dot3(_hilo(q * em), _hilo(k * emi), 1, 1))
        o_ref[...] = _dot3(_hilo(q * eb), _hilo(st), 1, 1) + _dot3(_hilo(a), v2)
        state[...] = st * jnp.exp(bl) + _dot3(v2, _hilo(k * el), 0, 0)

    return _hosted_call(
        body, host, name=name, grid=(GLA_HEADS, nc),
        in_specs=[pl.BlockSpec((c, dk), lambda h, i: (i, h)), pl.BlockSpec((c, dk), lambda h, i: (i, kq + h)),
                  pl.BlockSpec((c, dv), lambda h, i: (i, kq + h)), pl.BlockSpec((c, dk), lambda h, i: (i, h))],
        out_specs=[pl.BlockSpec((c, dv), lambda h, i: (i, h)),
                   pl.BlockSpec((None, None, dv, dk), lambda h, i: (h, i, 0, 0))],
        out_shape=[SDS((s, 2 * qk), F32), SDS((GLA_HEADS, nc, dv, dk), F32)],
        scratch_shapes=[pltpu.VMEM((dv, dk), F32)], sem=("parallel", "arbitrary"), args=(proj, proj, proj, g))


def gla_bwd(proj, g, states, do, *, name, host=None, c=GLA_CHUNK):
    s = proj.shape[0]
    qk = g.shape[1]
    dk, dv = qk // GLA_HEADS, 2 * qk // GLA_HEADS
    nc = s // c
    scale = dk ** -0.5
    kq = qk // dk

    def body(q_ref, k_ref, v_ref, g_ref, do_ref, st_ref, dq_ref, dk_ref, dv_ref, dg_ref, dstate, dgc):
        @pl.when(pl.program_id(1) == 0)
        def _():
            dstate[...] = jnp.zeros_like(dstate)
            dgc[...] = jnp.zeros_like(dgc)

        q, k, bl, eb, em, emi, el = _gla_chunk_terms(q_ref, k_ref, g_ref, c, scale)
        v2, do2 = _hilo(v_ref[...]), _hilo(do_ref[...])
        qe, qm, km, kd = _hilo(q * eb), _hilo(q * em), _hilo(k * emi), _hilo(k * el)
        ds = dstate[...]
        ds2 = _hilo(ds)
        a = _hilo(_causal(_dot3(qm, km, 1, 1)))
        dv_ref[...] = (_dot3(a, do2, 0, 0) + _dot3(kd, ds2, 1, 1)).astype(BF16)
        da = _hilo(_causal(_dot3(do2, v2, 1, 1)))
        dq = _dot3(da, km) * em + _dot3(do2, _hilo(st_ref[...])) * eb
        dkk = _dot3(da, qm, 0, 0) * emi + _dot3(v2, ds2) * el
        dstate[...] = ds * jnp.exp(bl) + _dot3(do2, qe, 0, 0)
        db = q * dq - k * dkk
        dg_ref[...] = _split_dot(_tri(c, "ge"), db) + dgc[...]
        dgc[...] += jnp.sum(db, axis=0, keepdims=True)
        dq_ref[...] = (dq * scale).astype(BF16)
        dk_ref[...] = dkk.astype(BF16)

    rev = lambda i: nc - 1 - i
    qspec = pl.BlockSpec((c, dk), lambda h, i: (rev(i), h))
    vspec = pl.BlockSpec((c, dv), lambda h, i: (rev(i), h))
    return _hosted_call(
        body, host, name=name, grid=(GLA_HEADS, nc),
        in_specs=[qspec, pl.BlockSpec((c, dk), lambda h, i: (rev(i), kq + h)),
                  pl.BlockSpec((c, dv), lambda h, i: (rev(i), kq + h)), qspec, vspec,
                  pl.BlockSpec((None, None, dv, dk), lambda h, i: (h, rev(i), 0, 0))],
        out_specs=[qspec, qspec, vspec, qspec],
        out_shape=[SDS((s, qk), BF16), SDS((s, qk), BF16), SDS((s, 2 * qk), BF16), SDS((s, qk), F32)],
        scratch_shapes=[pltpu.VMEM((dv, dk), F32), pltpu.VMEM((1, dk), F32)], sem=("parallel", "arbitrary"),
        args=(proj, proj, proj, g, do, states))


def gnorm_fwd(o, proj, gw, *, name, tr=512):
    s, v = o.shape
    dv = v // GLA_HEADS
    roff = 2 * GLA_HEADS

    def body(o_ref, r_ref, w_ref, y_ref):
        of = o_ref[...]
        rs = lax.rsqrt(jnp.mean(of * of, axis=-1, keepdims=True) + EPS)
        r = r_ref[...]
        y_ref[...] = (of * rs * w_ref[...] * (r * _sigmoid(r))).astype(BF16)

    blk = pl.BlockSpec((tr, dv), lambda i, h: (i, h))
    return pl.pallas_call(
        body, name=name, grid=(s // tr, GLA_HEADS),
        in_specs=[blk, pl.BlockSpec((tr, dv), lambda i, h: (i, roff + h)), pl.BlockSpec((1, dv), lambda i, h: (0, 0))],
        out_specs=blk, out_shape=SDS((s, v), BF16), compiler_params=_cp("parallel", "parallel"))(o, proj, gw)


def gnorm_bwd(dy, o, proj, gw, *, name, tr=512):
    s, v = o.shape
    dv = v // GLA_HEADS
    roff = 2 * GLA_HEADS

    def body(dy_ref, o_ref, r_ref, w_ref, do_ref, dr_ref, dw_ref):
        first = jnp.logical_and(pl.program_id(0) == 0, pl.program_id(1) == 0)
        of = o_ref[...]
        rs = lax.rsqrt(jnp.mean(of * of, axis=-1, keepdims=True) + EPS)
        n = of * rs
        r = r_ref[...]
        sg = _sigmoid(r)
        dyf = dy_ref[...].astype(F32)
        dn_w = dyf * (r * sg)
        dr_ref[...] = (dyf * n * w_ref[...] * sg * (1.0 + r * (1.0 - sg))).astype(BF16)
        dn = dn_w * w_ref[...]
        do_ref[...] = rs * (dn - n * jnp.mean(dn * n, axis=-1, keepdims=True))
        part = jnp.sum(dn_w * n, axis=0, keepdims=True)

        @pl.when(first)
        def _():
            dw_ref[...] = part

        @pl.when(jnp.logical_not(first))
        def _():
            dw_ref[...] += part

    blk = pl.BlockSpec((tr, dv), lambda i, h: (i, h))
    vec = pl.BlockSpec((1, dv), lambda i, h: (0, 0))
    return pl.pallas_call(
        body, name=name, grid=(s // tr, GLA_HEADS),
        in_specs=[blk, blk, pl.BlockSpec((tr, dv), lambda i, h: (i, roff + h)), vec],
        out_specs=[blk, blk, vec], out_shape=[SDS((s, v), F32), SDS((s, v), BF16), SDS((1, dv), F32)],
        compiler_params=_cp("arbitrary", "arbitrary"))(dy, o, proj, gw)


def _sb_block(kblk, q, ks, q0, scale, carry, masked):
    tk, tq = kblk.shape[0], q.shape[0]
    z = _dot(kblk, q, 1, 1) * scale
    sp = jnp.maximum(z, 0.0) + jnp.log(1.0 + jnp.exp(-jnp.abs(z)))
    mask = None
    lf = -sp
    if masked:
        kpos = ks + lax.broadcasted_iota(jnp.int32, (tk, tq), 0)
        qpos = q0 + lax.broadcasted_iota(jnp.int32, (tk, tq), 1)
        mask = kpos < qpos
        lf = jnp.where(mask, lf, 0.0)
    later = _split_dot(_tri(tk, "gt"), lf)
    a = jnp.exp(z - sp + later + carry)
    if masked:
        a = jnp.where(mask, a, 0.0)
    return z, sp, mask, lf, a


def sb_fwd(q, kv, *, name, host=None, tq=SB_TQ, tk=SB_TK, group=SB_GROUP):
    s, w = q.shape
    hd = w // SB_HEADS
    nq, nkb = s // tq, s // tk
    scale = hd ** -0.5
    per = tq // tk
    assert per % group == 0

    def body(q_ref, k_ref, v_ref, o_ref, car_ref, o_acc):
        qi = pl.program_id(1)
        qb = q_ref[...]
        q0 = qi * tq
        car_ref[...] = jnp.full(car_ref.shape, SB_SKIPPED, F32)
        o_acc[...] = jnp.zeros_like(o_acc)

        def blocks(first, carry, skip=None):
            for t in reversed(range(group)):
                kj = first + t
                ks = pl.multiple_of(kj * tk, tk)
                car_ref[pl.ds(kj, 1), :] = carry
                lo = 0 if skip is None else (skip + t) * tk
                _, _, _, lf, a = _sb_block(k_ref[pl.ds(ks, tk), :], qb[lo:, :], ks, q0 + lo, scale, carry[:, lo:], skip is not None)
                o_acc[lo:, :] += _dot(a.astype(BF16), v_ref[pl.ds(ks, tk), :], 0, 0)
                add = jnp.sum(lf, axis=0, keepdims=True)
                carry = carry + (add if lo == 0 else jnp.concatenate([jnp.zeros((1, lo), F32), add], axis=1))
            return carry

        carry = jnp.zeros((1, tq), F32)
        for gidx in reversed(range(per // group)):
            carry = blocks(qi * per + gidx * group, carry, gidx * group)
        n_in = qi * (per // group)
        lax.while_loop(lambda st: jnp.logical_and(st[0] < n_in, jnp.max(st[1]) > SB_DEAD),
                       lambda st: (st[0] + 1, blocks((n_in - 1 - st[0]) * group, st[1])), (jnp.int32(0), carry))
        o_ref[...] = o_acc[...].astype(BF16)

    qspec = pl.BlockSpec((tq, hd), lambda h, i: (i, h))
    return _hosted_call(
        body, host, name=name, grid=(SB_HEADS, nq),
        in_specs=[qspec, pl.BlockSpec((None, s, hd), lambda h, i: (0, 0, h)), pl.BlockSpec((None, s, hd), lambda h, i: (1, 0, h))],
        out_specs=[qspec, pl.BlockSpec((None, None, nkb, tq), lambda h, i: (h, i, 0, 0))],
        out_shape=[SDS((s, w), BF16), SDS((SB_HEADS, nq, nkb, tq), F32)],
        scratch_shapes=[pltpu.VMEM((tq, hd), F32)], sem=("parallel", "parallel"), args=(q, kv, kv))


def sb_bwd(q, kv, do, car, *, name, host=None, tq=SB_TQ, tk=SB_TK, group=SB_GROUP):
    s, w = q.shape
    hd = w // SB_HEADS
    nq, nkb = s // tq, s // tk
    scale = hd ** -0.5
    per = tq // tk
    assert per % group == 0

    def body(q_ref, k_ref, v_ref, do_ref, car_ref, dq_ref, dkv_ref, dq_acc, dk_acc, dv_acc):
        qi = pl.program_id(1)
        qb = q_ref[...]
        dob = do_ref[...]
        q0 = qi * tq
        dq_acc[...] = jnp.zeros_like(dq_acc)

        @pl.when(qi == 0)
        def _():
            dk_acc[...] = jnp.zeros_like(dk_acc)
            dv_acc[...] = jnp.zeros_like(dv_acc)

        def blocks(first, pcar, skip=None):
            for t in range(group):
                kj = first + t
                ks = pl.multiple_of(kj * tk, tk)
                kblk = k_ref[pl.ds(ks, tk), :]
                lo = 0 if skip is None else (skip + t) * tk
                masked = skip is not None
                qs, dos = qb[lo:, :], dob[lo:, :]
                z, sp, mask, _, a = _sb_block(kblk, qs, ks, q0 + lo, scale, car_ref[pl.ds(kj, 1), :][:, lo:], masked)
                p = a * _dot(v_ref[pl.ds(ks, tk), :], dos, 1, 1)
                before = _split_dot(_tri(tk, "lt"), p)
                sg = jnp.exp(z - sp)
                dz = p * (1.0 - sg) - (pcar[:, lo:] + before) * sg
                if masked:
                    dz = jnp.where(mask, dz, 0.0)
                dz = (dz * scale).astype(BF16)
                dk_acc[pl.ds(ks, tk), :] += _dot(dz, qs)
                dv_acc[pl.ds(ks, tk), :] += _dot(a.astype(BF16), dos)
                dq_acc[lo:, :] += _dot(dz, kblk, 0, 0)
                add = jnp.sum(p, axis=0, keepdims=True)
                pcar = pcar + (add if lo == 0 else jnp.concatenate([jnp.zeros((1, lo), F32), add], axis=1))
            return pcar

        n_in = qi * (per // group)

        def reached(g):
            return (jnp.max(car_ref[pl.ds(g * group + group - 1, 1), :]) > SB_DEAD).astype(jnp.int32)

        start = n_in - lax.fori_loop(0, n_in, lambda g, n: n + reached(g), jnp.int32(0))
        pcar = lax.fori_loop(start, n_in, lambda i, c: blocks(i * group, c), jnp.zeros((1, tq), F32))
        for gidx in range(per // group):
            pcar = blocks(qi * per + gidx * group, pcar, gidx * group)
        dq_ref[...] = dq_acc[...].astype(BF16)

        @pl.when(qi == nq - 1)
        def _():
            dkv_ref[0] = dk_acc[...].astype(BF16)
            dkv_ref[1] = dv_acc[...].astype(BF16)

    qspec = pl.BlockSpec((tq, hd), lambda h, i: (i, h))
    return _hosted_call(
        body, host, name=name, grid=(SB_HEADS, nq),
        in_specs=[qspec, pl.BlockSpec((None, s, hd), lambda h, i: (0, 0, h)), pl.BlockSpec((None, s, hd), lambda h, i: (1, 0, h)),
                  qspec, pl.BlockSpec((None, None, nkb, tq), lambda h, i: (h, i, 0, 0))],
        out_specs=[qspec, pl.BlockSpec((2, s, hd), lambda h, i: (0, 0, h))],
        out_shape=[SDS((s, w), BF16), SDS((2, s, w), BF16)],
        scratch_shapes=[pltpu.VMEM((tq, hd), F32), pltpu.VMEM((s, hd), F32), pltpu.VMEM((s, hd), F32)],
        sem=("parallel", "arbitrary"), args=(q, kv, kv, do, car))


def adamw(w, g, m, v, *, name):
    shape = w.shape
    c = shape[-1]
    r = w.size // c
    tr = _tile(r, max(8, (3 * LANES * 1024) // c), unit=8) if r >= 8 else r

    def body(w_ref, g_ref, m_ref, v_ref, d_ref, nm_ref, nv_ref):
        gf = g_ref[...]
        mn = ADAM_B1 * m_ref[...] + (1.0 - ADAM_B1) * gf
        vn = ADAM_B2 * v_ref[...] + (1.0 - ADAM_B2) * (gf * gf)
        m_hat = mn / (1.0 - ADAM_B1 ** ADAM_STEP)
        v_hat = vn / (1.0 - ADAM_B2 ** ADAM_STEP)
        d_ref[...] = -ADAM_LR * (m_hat / (jnp.sqrt(v_hat) + ADAM_EPS) + ADAM_WD * w_ref[...])
        nm_ref[...] = mn
        nv_ref[...] = vn

    blk = pl.BlockSpec((tr, c), lambda i: (i, 0))
    outs = pl.pallas_call(
        body, name=name, grid=(r // tr,), in_specs=[blk] * 4, out_specs=[blk] * 3,
        out_shape=[SDS((r, c), F32)] * 3, compiler_params=_cp("parallel"))(
            *(t.reshape(r, c) for t in (w, g, m, v)))
    return tuple(o.reshape(shape) for o in outs)


def _row_tile(r, c):
    return _tile(r, max(16, (4 * LANES * 1024) // c), unit=16)


def _halves(a):
    return a.reshape(a.shape[:-2] + (2, a.shape[-2] // 2, a.shape[-1]))


def place_shard(w, chip, *, name, layer=None):
    r, c = w.shape[-2:]
    tr = _row_tile(r, c)

    def body(c_ref, w_ref, o_ref):
        o_ref[...] = w_ref[...].astype(BF16)

    if layer is None:
        w_spec = pl.BlockSpec((None, tr, c), lambda h, i, c_ref: (h, i, 0))
    else:
        w_spec = pl.BlockSpec((None, None, tr, c), lambda h, i, c_ref: (layer, h, i, 0))
    return pl.pallas_call(
        body, name=name, out_shape=SDS((N_CHIPS, 2, r, c), BF16),
        grid_spec=pltpu.PrefetchScalarGridSpec(
            num_scalar_prefetch=1, grid=(2, r // tr), in_specs=[w_spec],
            out_specs=pl.BlockSpec((None, None, tr, c), lambda h, i, c_ref: (c_ref[0], h, i, 0))),
        compiler_params=_cp("parallel", "parallel"))(chip, w)


def sibling_exchange(gs, *, name):
    n = len(gs)

    def body(*refs):
        g_refs, a_refs = refs[:n], refs[n:2 * n]
        send_sems, recv_sems = refs[2 * n:]
        x, y, c = _coords()
        cps = [_remote(g_refs[i].at[:, 1 - c], a_refs[i], send_sems.at[i], recv_sems.at[i], (x, y, 1 - c)) for i in range(n)]
        for cp in cps:
            cp.start()
        for cp in cps:
            cp.wait()

    return pl.pallas_call(
        body, name=name, in_specs=[ANY] * n, out_specs=[ANY] * n,
        out_shape=[SDS(g.shape[:1] + g.shape[2:], g.dtype) for g in gs],
        scratch_shapes=[_dma_sems(n), _dma_sems(n)])(*gs)


def half_add(g, a, core, *, name):
    n, _, r, c = g.shape
    tr = _row_tile(r, c)

    def body(c_ref, g_ref, a_ref, o_ref):
        o_ref[...] = (g_ref[...].astype(F32) + a_ref[...].astype(F32)).astype(o_ref.dtype)

    blk = pl.BlockSpec((None, tr, c), lambda s, i, c_ref: (s, i, 0))
    return pl.pallas_call(
        body, name=name, out_shape=SDS((n, r, c), g.dtype),
        grid_spec=pltpu.PrefetchScalarGridSpec(
            num_scalar_prefetch=1, grid=(n, r // tr),
            in_specs=[pl.BlockSpec((None, None, tr, c), lambda s, i, c_ref: (s, c_ref[0], i, 0)), blk], out_specs=blk),
        compiler_params=_cp("parallel", "parallel"))(core, g, a)


def chip_sum(p, b, chip, core, *, name, layer=None, into=None):
    _, r, c = p.shape
    tr = _row_tile(r, c)

    def body(chip_ref, core_ref, p_ref, b_ref, *refs):
        t = p_ref[...].astype(F32)
        for k in range(N_CHIPS - 1):
            t = t + b_ref[k].astype(F32)
        refs[-1][...] = t

    in_specs = [pl.BlockSpec((None, tr, c), lambda i, chip_ref, core_ref: (chip_ref[0], i, 0)),
                pl.BlockSpec((N_CHIPS - 1, tr, c), lambda i, chip_ref, core_ref: (0, i, 0))]
    if layer is None:
        shape, o_spec = (2, r, c), pl.BlockSpec((None, tr, c), lambda i, chip_ref, core_ref: (core_ref[0], i, 0))
    else:
        shape = (2, 2, r, c)
        o_spec = pl.BlockSpec((None, None, tr, c), lambda i, chip_ref, core_ref: (layer, core_ref[0], i, 0))
    args, aliases = (chip, core, p, b), {}
    if into is not None:
        in_specs, args, aliases = in_specs + [ANY], args + (into,), {4: 0}
    return pl.pallas_call(
        body, name=name, out_shape=SDS(shape, F32),
        grid_spec=pltpu.PrefetchScalarGridSpec(num_scalar_prefetch=2, grid=(r // tr,), in_specs=in_specs, out_specs=o_spec),
        input_output_aliases=aliases, compiler_params=_cp("parallel"))(*args)


def sibling_gather(ts, *, name):
    n = len(ts)

    def body(*refs):
        out_refs = refs[n:2 * n]
        send_sems, recv_sems = refs[2 * n:]
        x, y, c = _coords()

        def half(i, which):
            return out_refs[i].at[which] if len(ts[i].shape) == 3 else out_refs[i].at[:, which]

        cps = [_remote(half(i, c), half(i, c), send_sems.at[i], recv_sems.at[i], (x, y, 1 - c)) for i in range(n)]
        for cp in cps:
            cp.start()
        for i in range(n):
            _remote(half(i, c), half(i, 1 - c), send_sems.at[i], recv_sems.at[i], (x, y, 1 - c)).wait_recv()
        for cp in cps:
            cp.wait_send()

    return pl.pallas_call(
        body, name=name, in_specs=[ANY] * n, out_specs=[ANY] * n, out_shape=[SDS(t.shape, t.dtype) for t in ts],
        input_output_aliases={i: i for i in range(n)}, scratch_shapes=[_dma_sems(n), _dma_sems(n)])(*ts)


def allgather_all(sm, *, name):
    r, w = sm.shape

    def body(s_ref, out_ref, send_sems, recv_sems, local_sem):
        x, y, c = _coords()
        me = 4 * x + 2 * y + c

        def peer(rel):
            flip = lambda v, bit: 1 - v if bit else v
            return flip(x, rel & 4), flip(y, rel & 2), flip(c, rel & 1)

        loc = pltpu.make_async_copy(s_ref, out_ref.at[me], local_sem.at[0])
        loc.start()
        cps = [_remote(s_ref, out_ref.at[me], send_sems.at[rel - 1], recv_sems.at[rel - 1], peer(rel)) for rel in range(1, N_DEV)]
        for cp in cps:
            cp.start()
        for rel in range(1, N_DEV):
            px, py, pc = peer(rel)
            _remote(s_ref, out_ref.at[4 * px + 2 * py + pc], send_sems.at[rel - 1], recv_sems.at[rel - 1], (px, py, pc)).wait_recv()
        for cp in cps:
            cp.wait_send()
        loc.wait()

    return pl.pallas_call(
        body, name=name, in_specs=[ANY], out_specs=ANY, out_shape=SDS((N_DEV, r, w), sm.dtype),
        scratch_shapes=[_dma_sems(N_DEV - 1), _dma_sems(N_DEV - 1), _dma_sems(1)])(sm)


def sum_blocks(a, *, name):
    n, r, w = a.shape

    def body(a_ref, o_ref):
        t = a_ref[0]
        for k in range(1, n):
            t = t + a_ref[k]
        o_ref[...] = t

    return pl.pallas_call(body, name=name, out_shape=SDS((r, w), F32))(a)


class Exchanges:
    def __init__(self, placed=None, chip=None, core=None):
        self.placed, self.chip, self.core = placed, chip, core
        self.mine = {}

    def gather(self, names):
        return None if self.placed is None else GatherX([self.placed[n] for n in names])

    def reduce(self, grads, call):
        if self.placed is None:
            return call(None)
        names = list(grads)
        from_sibling = sibling_exchange([grads[n] for n in names], name="grads_to_sibling_" + names[0])
        pairs = [half_add(grads[n], a, self.core, name=f"pair_sum_{n}") for n, a in zip(names, from_sibling)]
        result, from_chips = call(ScatterX(pairs))
        for n, t, b in zip(names, pairs, from_chips):
            base, layer = (n[:-1], int(n[-1])) if n[:-1] in ("w_gu", "w_dn") else (n, None)
            self.mine[base] = chip_sum(t, b, self.chip, self.core, name=f"chip_sum_{n}", layer=layer, into=self.mine.get(base))
        return result


def _usable(name, buf, d):
    if name == "w_in":
        cols = N_CHIPS * buf.shape[-1]
        full = buf.reshape(N_CHIPS, d, -1).transpose(1, 0, 2).reshape(d, cols)
        return jnp.pad(full, ((0, 0), (0, -cols % LANES)))
    if name in ("w_out", "w_q", "w_so"):
        return buf.reshape(-1, buf.shape[-1])
    return buf.reshape(N_CHIPS, -1, buf.shape[-1])


def _by_rows(t):
    return _halves(t.reshape(N_CHIPS, -1, t.shape[-1]))


def _ffn_fwd(h, nw, w_gu, w_dn, tag, host=None):
    f = rmsnorm_fwd(h, nw, name=f"{tag}_norm")
    gu = matmul(f, View(w_gu, "cols"), name=f"{tag}_gate_up", out_dtype=BF16, host=host)
    gu, got = gu if host is not None else (gu, None)
    act = swiglu_fwd(gu, name=f"{tag}_act")
    return matmul(act, View(w_dn, "rows"), name=f"{tag}_down", residual=h), (f, gu, act), got


def _ffn_bwd(dh, h, nw, w_gu, w_dn, saved, tag, ex, dn_name, carried):
    f, gu, act = saved
    dact = matmul(dh, View(w_dn, "rows"), name=f"{tag}_dact", tb=True, out_dtype=BF16)
    dw_dn = matmul(act, dh, name=f"{tag}_dw_down", ta=True, out_dtype=BF16)
    dgu = swiglu_bwd(gu, dact, name=f"{tag}_dgu")

    def dw_gate_up(host):
        return matmul(f, dgu, name=f"{tag}_dw_gate_up", ta=True, out_dtype=BF16, out_chips=N_CHIPS, host=host)

    dw_gu = ex.reduce(carried, dw_gate_up) if carried else dw_gate_up(None)
    df = ex.reduce({dn_name: _by_rows(dw_dn)}, lambda host: matmul(dgu, View(w_gu, "cols"), name=f"{tag}_df", tb=True, host=host))
    dh_in, dnw = rmsnorm_bwd(df, h, nw, dh, name=f"{tag}_dnorm")
    return dh_in, dnw, dw_gu, dw_dn


def local_step(x, target, p, ex):
    d = x.shape[1]
    hosted = ex.placed is not None
    w = {} if hosted else {n: _usable(n, p[n], d) for n in ("w_in", "w_out", "w_kv", "w_q", "w_so", "w_gu0", "w_gu1", "w_dn0", "w_dn1")}

    def take(names, got):
        for n, buf in zip(names, got or []):
            w[n] = _usable(n, buf, d)

    def carry(call, names):
        host = ex.gather(names)
        if host is None:
            return call(None)
        res, got = call(host)
        take(names, got)
        return res

    if hosted:
        take(["w_in"], run_exchange(ex.gather(["w_in"]), name="gather_gla_in"))
    a0 = rmsnorm_fwd(x, p["an0"], name="l0_attn_norm")
    proj = carry(lambda host: matmul(a0, w["w_in"], name="gla_in", host=host), ["w_out", "w_dn0"])
    g = gate_fwd(proj, p["wg"], p["bg"], name="gla_gate")
    o, states = carry(lambda host: gla_fwd(proj, g, name="gla_scan", host=host), ["w_gu0"])
    og = gnorm_fwd(o, proj, p["gw"], name="gla_outnorm")
    h1 = matmul(og, w["w_out"], name="gla_out", residual=x)
    sb_names = ["w_kv", "w_q", "w_so"]
    h2, ffn0, got = _ffn_fwd(h1, p["fn0"], w["w_gu0"], w["w_dn0"], "ffn0", host=ex.gather(sb_names))
    take(sb_names, got)
    w_kv = View(w["w_kv"], "cols")
    kvn = rmsnorm_fwd(h2, p["kvn"], name="kv_norm")
    kv = matmul(kvn, w_kv, name="sb_kv", out_dtype=BF16, out_chips=2)
    a1 = rmsnorm_fwd(h2, p["an1"], name="l1_attn_norm")
    q2 = matmul(a1, w["w_q"], name="sb_q", out_dtype=BF16)
    o2, car = carry(lambda host: sb_fwd(q2, kv, name="sb_attn", host=host), ["w_gu1", "w_dn1"])
    h3 = matmul(o2, w["w_so"], name="sb_out", residual=h2)
    h4, ffn1, _ = _ffn_fwd(h3, p["fn1"], w["w_gu1"], w["w_dn1"], "ffn1")
    dh4, d_fin, loss_row = final_loss(h4, p["finn"], target, name="final_loss")

    dh3, d_fn1, dw_gu1, dw_dn1 = _ffn_bwd(dh4, h3, p["fn1"], w["w_gu1"], w["w_dn1"], ffn1, "ffn1", ex, "w_dn1", {})
    do2 = matmul(dh3, w["w_so"], name="sb_do", tb=True, out_dtype=BF16)
    dw_so = matmul(o2, dh3, name="sb_dw_out", ta=True, out_dtype=BF16)
    dq2, dkv = ex.reduce({"w_gu1": _halves(dw_gu1)}, lambda host: sb_bwd(q2, kv, do2, car, name="sb_attn_bwd", host=host))
    dkv = View(dkv, "cols")
    dw_q = matmul(a1, dq2, name="sb_dw_q", ta=True, out_dtype=BF16)
    da1 = matmul(dq2, w["w_q"], name="sb_da", tb=True)
    dh2, d_an1 = rmsnorm_bwd(da1, h2, p["an1"], dh3, name="l1_attn_dnorm")
    dw_kv = matmul(kvn, dkv, name="sb_dw_kv", ta=True, out_dtype=BF16, out_chips=N_CHIPS)
    dkvn = matmul(dkv, w_kv, name="sb_dkvn", tb=True)
    dh2, d_kvn = rmsnorm_bwd(dkvn, h2, p["kvn"], dh2, name="kv_dnorm")
    sb_grads = {"w_kv": _halves(dw_kv), "w_q": _by_rows(dw_q), "w_so": _by_rows(dw_so)}
    dh1, d_fn0, dw_gu0, dw_dn0 = _ffn_bwd(dh2, h1, p["fn0"], w["w_gu0"], w["w_dn0"], ffn0, "ffn0", ex, "w_dn0", sb_grads)
    dog = matmul(dh1, w["w_out"], name="gla_dog", tb=True, out_dtype=BF16)
    dw_out = matmul(og, dh1, name="gla_dw_out", ta=True, out_dtype=BF16)
    do, dr, d_gw = gnorm_bwd(dog, o, proj, p["gw"], name="gla_outnorm_bwd")
    dq, dk, dv, dg = ex.reduce({"w_gu0": _halves(dw_gu0)}, lambda host: gla_bwd(proj, g, states, do, name="gla_scan_bwd", host=host))
    dgl, d_wg, d_bg = gate_bwd(dg, proj, p["wg"], p["bg"], name="gla_gate_bwd")
    dproj = jnp.concatenate([dq, dk, dv, dr, dgl], axis=1)
    dw_in = matmul(a0, dproj, name="gla_dw_in", ta=True, out_dtype=BF16)
    in_w = p["in_w"]
    gla_grads = {"w_in": _halves(dw_in[:, :in_w].reshape(d, N_CHIPS, -1).transpose(1, 0, 2)), "w_out": _by_rows(dw_out)}
    da0 = ex.reduce(gla_grads, lambda host: matmul(dproj, w["w_in"], name="gla_da", tb=True, host=host))
    dx, d_an0 = rmsnorm_bwd(da0, x, p["an0"], dh1, name="l0_attn_dnorm")

    small = dict(an0=d_an0, an1=d_an1, fn0=d_fn0, fn1=d_fn1, kvn=d_kvn, finn=d_fin, wg=d_wg, bg=d_bg, gw=d_gw)
    big = {}
    if not hosted:
        big = dict(gla_grads, **sb_grads, w_gu0=_halves(dw_gu0), w_gu1=_halves(dw_gu1), w_dn0=_by_rows(dw_dn0), w_dn1=_by_rows(dw_dn1))
    return loss_row, dx, small, big


def _pack_rows(parts):
    cat = jnp.concatenate([t.reshape(-1) for t in parts])
    rows = -(-cat.size // (LANES * SUBLANES)) * SUBLANES
    return jnp.pad(cat, (0, rows * LANES - cat.size)).reshape(rows, LANES)


def _segments(flat, sizes):
    out, off = [], 0
    for n in sizes:
        out.append(flat[..., off:off + n])
        off += n
    return out


def kernel(x, attn_norm_w, ffn_norm_w, gla_w_in, gla_w_gate_up, gla_b_gate, gla_gnorm_w, gla_w_out, kv_norm_w, sb_w_kv, sb_w_q, sb_w_out, ffn_w_gate_up, ffn_w_down, final_norm_w, loss_target, m_attn_norm_w, m_ffn_norm_w, m_gla_w_in, m_gla_w_gate_up, m_gla_b_gate, m_gla_gnorm_w, m_gla_w_out, m_kv_norm_w, m_sb_w_kv, m_sb_w_q, m_sb_w_out, m_ffn_w_gate_up, m_ffn_w_down, m_final_norm_w, v_attn_norm_w, v_ffn_norm_w, v_gla_w_in, v_gla_w_gate_up, v_gla_b_gate, v_gla_gnorm_w, v_gla_w_out, v_kv_norm_w, v_sb_w_kv, v_sb_w_q, v_sb_w_out, v_ffn_w_gate_up, v_ffn_w_down, v_final_norm_w):
    xi, yi, ci = _coords()
    core = ci.astype(jnp.int32).reshape(1)
    chip1 = (2 * xi + yi).astype(jnp.int32)
    chip = chip1.reshape(1)
    rank = gla_w_gate_up.shape[1]

    gu4, dn4 = _halves(ffn_w_gate_up), _halves(ffn_w_down)
    shards = dict(w_in=(_halves(gla_w_in[0]), None), w_out=(_halves(gla_w_out[0]), None), w_kv=(_halves(sb_w_kv), None),
                  w_q=(_halves(sb_w_q[0]), None), w_so=(_halves(sb_w_out[0]), None),
                  w_gu0=(gu4, 0), w_gu1=(gu4, 1), w_dn0=(dn4, 0), w_dn1=(dn4, 1))
    placed = {n: place_shard(a, chip, name=f"place_{n}", layer=l) for n, (a, l) in shards.items()}
    small_w = [gla_w_gate_up, gla_b_gate, gla_gnorm_w]
    small_all = allgather_all(_pack_rows(small_w), name="gather_gate_weights")
    wg, bg, gw = _segments(small_all[::2].reshape(N_CHIPS, -1), [a.size for a in small_w])
    wg = wg.reshape(N_CHIPS, rank, -1).transpose(1, 0, 2).reshape(rank, -1)
    p = dict(an0=attn_norm_w[0], an1=attn_norm_w[1], fn0=ffn_norm_w[0], fn1=ffn_norm_w[1], kvn=kv_norm_w, finn=final_norm_w,
             wg=jnp.pad(wg, ((0, LANES - rank), (0, 0))).astype(BF16), bg=bg.reshape(1, -1), gw=gw.reshape(1, -1),
             in_w=N_CHIPS * gla_w_in.shape[-1])

    ex = Exchanges(placed, chip, core)
    loss_row, dx, g, _ = local_step(x[0], loss_target[0], p, ex)
    loss = lax.psum(loss_row[0, 0], ("x", "y", "c"))

    tags = list(ex.mine)
    tot = dict(zip(tags, sibling_gather([ex.mine[n] for n in tags], name="grads_from_sibling")))

    vecs = [jnp.concatenate([g["an0"], g["an1"]]), jnp.concatenate([g["fn0"], g["fn1"]]), g["kvn"], g["finn"],
            g["wg"][:rank], g["bg"], g["gw"]]
    gathered_vecs = allgather_all(_pack_rows(vecs), name="gather_small_grads")
    d_an, d_fn, d_kvn, d_fin, d_wg, d_bg, d_gw = _segments(
        sum_blocks(gathered_vecs, name="sum_small_grads").reshape(-1), [t.size for t in vecs])

    def shard(t, like):
        return lax.dynamic_index_in_dim(t.reshape(-1, N_CHIPS, like.shape[-1]), chip1, axis=1, keepdims=False).reshape(like.shape)

    weights = dict(
        attn_norm_w=(attn_norm_w, m_attn_norm_w, v_attn_norm_w), ffn_norm_w=(ffn_norm_w, m_ffn_norm_w, v_ffn_norm_w),
        gla_w_in=(gla_w_in, m_gla_w_in, v_gla_w_in), gla_w_gate_up=(gla_w_gate_up, m_gla_w_gate_up, v_gla_w_gate_up),
        gla_b_gate=(gla_b_gate, m_gla_b_gate, v_gla_b_gate), gla_gnorm_w=(gla_gnorm_w, m_gla_gnorm_w, v_gla_gnorm_w),
        gla_w_out=(gla_w_out, m_gla_w_out, v_gla_w_out), kv_norm_w=(kv_norm_w, m_kv_norm_w, v_kv_norm_w),
        sb_w_kv=(sb_w_kv, m_sb_w_kv, v_sb_w_kv), sb_w_q=(sb_w_q, m_sb_w_q, v_sb_w_q), sb_w_out=(sb_w_out, m_sb_w_out, v_sb_w_out),
        ffn_w_gate_up=(ffn_w_gate_up, m_ffn_w_gate_up, v_ffn_w_gate_up), ffn_w_down=(ffn_w_down, m_ffn_w_down, v_ffn_w_down),
        final_norm_w=(final_norm_w, m_final_norm_w, v_final_norm_w))
    grads = dict(
        attn_norm_w=d_an.reshape(attn_norm_w.shape), ffn_norm_w=d_fn.reshape(ffn_norm_w.shape),
        gla_w_in=tot["w_in"].reshape(gla_w_in.shape), gla_w_gate_up=shard(d_wg, gla_w_gate_up),
        gla_b_gate=shard(d_bg, gla_b_gate), gla_gnorm_w=shard(d_gw, gla_gnorm_w),
        gla_w_out=tot["w_out"].reshape(gla_w_out.shape), kv_norm_w=d_kvn.reshape(kv_norm_w.shape),
        sb_w_kv=tot["w_kv"].reshape(sb_w_kv.shape), sb_w_q=tot["w_q"].reshape(sb_w_q.shape),
        sb_w_out=tot["w_so"].reshape(sb_w_out.shape),
        ffn_w_gate_up=tot["w_gu"].reshape(ffn_w_gate_up.shape), ffn_w_down=tot["w_dn"].reshape(ffn_w_down.shape),
        final_norm_w=d_fin.reshape(final_norm_w.shape))
    names = list(weights)
    stepped = [adamw(weights[n][0], grads[n], weights[n][1], weights[n][2], name=f"adamw_{n}") for n in names]
    return (loss, dx.reshape(x.shape), *[grads[n] for n in names], *[t[0] for t in stepped], *[t[1] for t in stepped],
            *[t[2] for t in stepped])
```

```python
import functools

import jax
import jax.numpy as jnp
from jax import lax
from jax.experimental import pallas as pl
from jax.experimental.pallas import tpu as pltpu

F32 = jnp.float32
BF16 = jnp.bfloat16
SDS = jax.ShapeDtypeStruct
MESH = pl.DeviceIdType.MESH

EPS = 1e-6
GLA_HEADS = 4
GLA_GATE_RANK = 16
GLA_GATE_TAU = 16.0
GLA_CHUNK = 128
SB_HEADS = 16
SB_TQ = 512
SB_TK = 128
SB_GROUP = 2
SB_DEAD = -110.0
SB_SKIPPED = -1e30
ADAM_LR = 0.001
ADAM_B1 = 0.9
ADAM_B2 = 0.999
ADAM_EPS = 1e-08
ADAM_WD = 0.01
ADAM_STEP = 10

LANES = 128
SUBLANES = 8
N_CHIPS = 4
N_DEV = 8
VMEM_LIMIT = 56 * 1024 * 1024


def _tile(dim, target, unit=LANES):
    if dim <= target:
        return dim
    t = (target // unit) * unit
    while t >= unit:
        if dim % t == 0:
            return t
        t -= unit
    raise ValueError(f"no tile for {dim}")


def _cp(*sem):
    return pltpu.CompilerParams(dimension_semantics=sem, vmem_limit_bytes=VMEM_LIMIT)


def _sigmoid(x):
    return 1.0 / (1.0 + jnp.exp(-x))


def _dot(a, b, ca=1, cb=0):
    return lax.dot_general(a, b, (((ca,), (cb,)), ((), ())), preferred_element_type=F32)


def _split_dot(tri, x):
    hi = x.astype(BF16)
    lo = (x - hi.astype(F32)).astype(BF16)
    return _dot(tri, hi) + _dot(tri, lo)


def _hilo(x):
    hi = x.astype(BF16)
    return hi, (x - hi.astype(F32)).astype(BF16)


def _dot3(a, b, ca=1, cb=0):
    return _dot(a[0], b[0], ca, cb) + _dot(a[0], b[1], ca, cb) + _dot(a[1], b[0], ca, cb)


def _tri(n, kind):
    r = lax.broadcasted_iota(jnp.int32, (n, n), 0)
    c = lax.broadcasted_iota(jnp.int32, (n, n), 1)
    m = {"le": c <= r, "ge": c >= r, "lt": c < r, "gt": c > r}[kind]
    return jnp.where(m, 1.0, 0.0).astype(BF16)


ANY = pl.BlockSpec(memory_space=pl.ANY)


def _coords():
    return lax.axis_index("x"), lax.axis_index("y"), lax.axis_index("c")


def _other_chips(x, y):
    return [(1 - x, y), (x, 1 - y), (1 - x, 1 - y)]


def _remote(src, dst, send_sem, recv_sem, dev):
    return pltpu.make_async_remote_copy(src_ref=src, dst_ref=dst, send_sem=send_sem, recv_sem=recv_sem,
                                        device_id=dev, device_id_type=MESH)


def _dma_sems(n):
    return pltpu.SemaphoreType.DMA((n,))


class GatherX:
    def __init__(self, bufs):
        self.ins, self.ios, self.outs, self.n_sems = [], list(bufs), [], 6 * len(bufs)

    def _copy(self, ios, send_sems, recv_sems, i, k, chip, half, dev):
        blk = ios[i].at[chip, half]
        return _remote(blk, blk, send_sems.at[6 * i + k], recv_sems.at[6 * i + k], dev)

    def _first(self, ios, send_sems, recv_sems):
        x, y, c = _coords()
        return [self._copy(ios, send_sems, recv_sems, i, k, 2 * x + y, c, (cx, cy, c))
                for i in range(len(ios)) for k, (cx, cy) in enumerate(_other_chips(x, y))]

    def start(self, ins, ios, outs, send_sems, recv_sems):
        for cp in self._first(ios, send_sems, recv_sems):
            cp.start()

    def finish(self, ins, ios, outs, send_sems, recv_sems):
        x, y, c = _coords()
        chips = _other_chips(x, y)
        copy = functools.partial(self._copy, ios, send_sems, recv_sems)
        passed = []
        for i in range(len(ios)):
            for k, (cx, cy) in enumerate(chips):
                copy(i, k, 2 * cx + cy, c, (x, y, c)).wait_recv()
                passed.append(copy(i, 3 + k, 2 * cx + cy, c, (x, y, 1 - c)))
                passed[-1].start()
        for i in range(len(ios)):
            for k, (cx, cy) in enumerate(chips):
                copy(i, 3 + k, 2 * cx + cy, 1 - c, (x, y, c)).wait_recv()
        for cp in self._first(ios, send_sems, recv_sems) + passed:
            cp.wait_send()


class ScatterX:
    def __init__(self, ps):
        self.ins, self.ios, self.n_sems = list(ps), [], 3 * len(ps)
        self.outs = [SDS((N_CHIPS - 1,) + p.shape[1:], p.dtype) for p in ps]

    def _copies(self, ins, outs, send_sems, recv_sems):
        x, y, c = _coords()
        return [_remote(ins[i].at[2 * cx + cy], outs[i].at[k], send_sems.at[3 * i + k], recv_sems.at[3 * i + k], (cx, cy, c))
                for i in range(len(ins)) for k, (cx, cy) in enumerate(_other_chips(x, y))]

    def start(self, ins, ios, outs, send_sems, recv_sems):
        for cp in self._copies(ins, outs, send_sems, recv_sems):
            cp.start()

    def finish(self, ins, ios, outs, send_sems, recv_sems):
        for cp in self._copies(ins, outs, send_sems, recv_sems):
            cp.wait()


def _exchange_operands(host):
    x_in = host.ins + host.ios
    x_out = [SDS(a.shape, a.dtype) for a in host.ios] + host.outs
    return x_in, x_out


def run_exchange(host, *, name):
    x_in, x_out = _exchange_operands(host)
    n_ins, n_ios = len(host.ins), len(host.ios)

    def body(*refs):
        xin, xout = refs[:len(x_in)], refs[len(x_in):len(x_in) + len(x_out)]
        send_sems, recv_sems = refs[len(x_in) + len(x_out):]
        x_refs = (xin[:n_ins], xout[:n_ios], xout[n_ios:])
        host.start(*x_refs, send_sems, recv_sems)
        host.finish(*x_refs, send_sems, recv_sems)

    return list(pl.pallas_call(
        body, name=name, in_specs=[ANY] * len(x_in), out_specs=[ANY] * len(x_out), out_shape=x_out,
        input_output_aliases={n_ins + i: i for i in range(n_ios)},
        scratch_shapes=[_dma_sems(host.n_sems), _dma_sems(host.n_sems)])(*x_in))


def _hosted_call(body, host, *, name, grid, in_specs, out_specs, out_shape, scratch_shapes, sem, args):
    if host is None:
        return list(pl.pallas_call(body, name=name, grid=grid, in_specs=in_specs, out_specs=out_specs, out_shape=out_shape,
                                   scratch_shapes=scratch_shapes, compiler_params=_cp(*sem))(*args))
    x_in, x_out = _exchange_operands(host)
    n_in, n_out, n_scr, n_ins, n_ios = len(in_specs), len(out_specs), len(scratch_shapes), len(host.ins), len(host.ios)

    def hosted(*refs):
        ins, xin = refs[:n_in], refs[n_in:n_in + len(x_in)]
        o0 = n_in + len(x_in)
        outs, xout = refs[o0:o0 + n_out], refs[o0 + n_out:o0 + n_out + len(x_out)]
        s0 = o0 + n_out + len(x_out)
        scr, (send_sems, recv_sems) = refs[s0:s0 + n_scr], refs[s0 + n_scr:]
        ids = [pl.program_id(ax) for ax in range(len(grid))]
        first = functools.reduce(jnp.logical_and, [i == 0 for i in ids])
        last = functools.reduce(jnp.logical_and, [i == n - 1 for i, n in zip(ids, grid)])
        x_refs = (xin[:n_ins], xout[:n_ios], xout[n_ios:])

        @pl.when(first)
        def _():
            host.start(*x_refs, send_sems, recv_sems)

        body(*ins, *outs, *scr)

        @pl.when(last)
        def _():
            host.finish(*x_refs, send_sems, recv_sems)

    res = pl.pallas_call(
        hosted, name=name, grid=grid, in_specs=list(in_specs) + [ANY] * len(x_in), out_specs=list(out_specs) + [ANY] * len(x_out),
        out_shape=list(out_shape) + x_out, scratch_shapes=list(scratch_shapes) + [_dma_sems(host.n_sems), _dma_sems(host.n_sems)],
        input_output_aliases={n_in + n_ins + i: n_out + i for i in range(n_ios)},
        compiler_params=_cp(*(("arbitrary",) * len(grid))))(*args, *x_in)
    return list(res[:n_out]), list(res[n_out:])


def _div(i, n):
    return i if n == 1 else lax.div(i, n)


def _rem(i, n):
    return 0 if n == 1 else lax.rem(i, n)


class View:
    def __init__(self, arr, kind="plain", lead=(), g0=0, ng=None):
        self.arr, self.kind, self.lead, self.g0 = arr, kind, tuple(lead), g0
        self.ng = (arr.shape[0] - g0) if ng is None else ng
        r, c = arr.shape[-2:]
        self.runit, self.cunit = r, c
        self.shape = {"plain": (r, c), "cols": (r, self.ng * c), "rows": (self.ng * r, c)}[kind]

    def spec(self, br, bc, rfn, cfn):
        if self.kind == "plain":
            return pl.BlockSpec((br, bc), lambda *g: (rfn(*g), cfn(*g)))
        none = (None,) * (1 + len(self.lead))
        if self.kind == "cols":
            per = self.cunit // bc
            return pl.BlockSpec(none + (br, bc), lambda *g: (self.g0 + _div(cfn(*g), per), *self.lead, rfn(*g), _rem(cfn(*g), per)))
        per = self.runit // br
        return pl.BlockSpec(none + (br, bc), lambda *g: (self.g0 + _div(rfn(*g), per), *self.lead, _rem(rfn(*g), per), cfn(*g)))


def _as_view(a):
    return a if isinstance(a, View) else View(a)


def matmul(a, b, *, name, ta=False, tb=False, out_dtype=F32, residual=None, out_chips=None, host=None, dswiglu=None,
           tm=1408, tn=1408, tk=2816):
    a, b = _as_view(a), _as_view(b)
    (k, m) = a.shape if ta else a.shape[::-1]
    (n, kb) = b.shape if tb else b.shape[::-1]
    assert k == kb, (a.shape, b.shape, ta, tb)
    m_unit = a.cunit if ta else a.runit
    ka_unit = a.runit if ta else a.cunit
    n_unit = b.runit if tb else b.cunit
    kb_unit = b.cunit if tb else b.runit
    if out_chips is not None:
        n_unit = min(n_unit, n // out_chips)
    tm, tn = _tile(min(m, m_unit), tm), _tile(min(n, n_unit), tn)
    tk = _tile(min(k, ka_unit, kb_unit), tk)
    assert ka_unit % tk == 0 and kb_unit % tk == 0, (ka_unit, kb_unit, tk)
    nk = k // tk
    ca, cb = (0 if ta else 1), (1 if tb else 0)

    def body(a_ref, b_ref, *refs):
        r_ref = refs[0] if residual is not None else None
        o_ref = refs[-1] if nk == 1 else refs[-2]

        def finish(r):
            if residual is not None:
                r = r + r_ref[...]
            if dswiglu is None:
                o_ref[...] = r.astype(out_dtype)
                return
            g = refs[0][...].astype(F32)
            sg = _sigmoid(g)
            o_ref[0] = (r * refs[1][...].astype(F32) * sg * (1.0 + g * (1.0 - sg))).astype(BF16)
            o_ref[1] = (r * g * sg).astype(BF16)

        part = _dot(a_ref[...].astype(BF16), b_ref[...].astype(BF16), ca, cb)
        if nk == 1:
            finish(part)
            return
        acc = refs[-1]
        kk = pl.program_id(2)

        @pl.when(kk == 0)
        def _():
            acc[...] = part

        @pl.when(kk > 0)
        def _():
            acc[...] += part

        @pl.when(kk == nk - 1)
        def _():
            finish(acc[...])

    gi, gj, gk = (lambda i, j, kk: i), (lambda i, j, kk: j), (lambda i, j, kk: kk)
    a_spec = a.spec(tk, tm, gk, gi) if ta else a.spec(tm, tk, gi, gk)
    b_spec = b.spec(tn, tk, gj, gk) if tb else b.spec(tk, tn, gk, gj)
    if out_chips is None:
        out = View(SDS((m, n), out_dtype))
    else:
        out = View(SDS((out_chips, m, n // out_chips), out_dtype), "cols")
    o_spec = out.spec(tm, tn, gi, gj)
    in_specs, args = [a_spec, b_spec], [a.arr, b.arr]
    if residual is not None:
        in_specs.append(pl.BlockSpec((tm, tn), lambda i, j, kk: (i, j)))
        args.append(residual)
    if dswiglu is not None:
        assert residual is None and out_chips is None and dswiglu.shape == (m, 2 * n)
        in_specs += [pl.BlockSpec((tm, tn), lambda i, j, kk: (i, j)), pl.BlockSpec((tm, tn), lambda i, j, kk: (i, j + n // tn))]
        args += [dswiglu, dswiglu]
        out = View(SDS((2, m, n), BF16))
        o_spec = pl.BlockSpec((2, tm, tn), lambda i, j, kk: (0, i, j))
    res = _hosted_call(
        body, host, name=name, grid=(m // tm, n // tn, nk), in_specs=in_specs, out_specs=[o_spec], out_shape=[out.arr],
        scratch_shapes=[] if nk == 1 else [pltpu.VMEM((tm, tn), F32)], sem=("parallel", "parallel", "arbitrary"), args=args)
    return res[0] if host is None else (res[0][0], res[1])


def rmsnorm_fwd(x, w, *, name, tr=512):
    s, d = x.shape

    def body(x_ref, w_ref, o_ref):
        xf = x_ref[...]
        r = lax.rsqrt(jnp.mean(xf * xf, axis=-1, keepdims=True) + EPS)
        o_ref[...] = (xf * r * w_ref[...]).astype(BF16)

    row = pl.BlockSpec((tr, d), lambda i: (i, 0))
    return pl.pallas_call(
        body, name=name, grid=(s // tr,), in_specs=[row, pl.BlockSpec((1, d), lambda i: (0, 0))], out_specs=row,
        out_shape=SDS((s, d), BF16), compiler_params=_cp("parallel"))(x, w.reshape(1, d))


def rmsnorm_bwd(dy, x, w, dres, *, name, tr=512):
    s, d = x.shape

    def body(dy_ref, x_ref, w_ref, dres_ref, dx_ref, dw_ref):
        i = pl.program_id(0)
        xf = x_ref[...]
        r = lax.rsqrt(jnp.mean(xf * xf, axis=-1, keepdims=True) + EPS)
        xh = xf * r
        dyf = dy_ref[...].astype(F32)
        dxh = dyf * w_ref[...]
        dx_ref[...] = dres_ref[...] + r * (dxh - xh * jnp.mean(dxh * xh, axis=-1, keepdims=True))
        part = jnp.sum(dyf * xh, axis=0, keepdims=True)

        @pl.when(i == 0)
        def _():
            dw_ref[...] = part

        @pl.when(i > 0)
        def _():
            dw_ref[...] += part

    row = pl.BlockSpec((tr, d), lambda i: (i, 0))
    vec = pl.BlockSpec((1, d), lambda i: (0, 0))
    return pl.pallas_call(
        body, name=name, grid=(s // tr,), in_specs=[row, row, vec, row], out_specs=[row, vec],
        out_shape=[SDS((s, d), F32), SDS((1, d), F32)], compiler_params=_cp("arbitrary"))(dy, x, w.reshape(1, d), dres)


def final_loss(h, w, target, *, name, tr=512):
    s, d = h.shape

    def body(h_ref, w_ref, t_ref, dh_ref, dw_ref, loss_ref):
        i = pl.program_id(0)
        xf = h_ref[...]
        r = lax.rsqrt(jnp.mean(xf * xf, axis=-1, keepdims=True) + EPS)
        xh = xf * r
        err = xh * w_ref[...] - t_ref[...]
        lpart = 0.5 * jnp.sum(jnp.sum(err * err, axis=-1, keepdims=True) * (1.0 / d), axis=0, keepdims=True)
        dy = err * (1.0 / d)
        dxh = dy * w_ref[...]
        dh_ref[...] = r * (dxh - xh * jnp.mean(dxh * xh, axis=-1, keepdims=True))
        part = jnp.sum(dy * xh, axis=0, keepdims=True)
        lrow = jnp.broadcast_to(lpart, (1, LANES))

        @pl.when(i == 0)
        def _():
            dw_ref[...] = part
            loss_ref[...] = lrow

        @pl.when(i > 0)
        def _():
            dw_ref[...] += part
            loss_ref[...] += lrow

    row = pl.BlockSpec((tr, d), lambda i: (i, 0))
    vec = pl.BlockSpec((1, d), lambda i: (0, 0))
    return pl.pallas_call(
        body, name=name, grid=(s // tr,), in_specs=[row, vec, row],
        out_specs=[row, vec, pl.BlockSpec((1, LANES), lambda i: (0, 0))],
        out_shape=[SDS((s, d), F32), SDS((1, d), F32), SDS((1, LANES), F32)],
        compiler_params=_cp("arbitrary"))(h, w.reshape(1, d), target)


def swiglu_fwd(gu, *, name, tr=512, tc=1408):
    s, f2 = gu.shape
    f = f2 // 2
    tc = _tile(f, tc)
    nf = f // tc

    def body(g_ref, u_ref, o_ref):
        g = g_ref[...].astype(F32)
        o_ref[...] = (g * _sigmoid(g) * u_ref[...].astype(F32)).astype(BF16)

    return pl.pallas_call(
        body, name=name, grid=(s // tr, nf),
        in_specs=[pl.BlockSpec((tr, tc), lambda i, j: (i, j)), pl.BlockSpec((tr, tc), lambda i, j: (i, j + nf))],
        out_specs=pl.BlockSpec((tr, tc), lambda i, j: (i, j)), out_shape=SDS((s, f), BF16),
        compiler_params=_cp("parallel", "parallel"))(gu, gu)


def _gate_z(gl_ref, w_ref, b_ref):
    glb = gl_ref[...].astype(BF16)
    return glb, _dot(glb, w_ref[...]) + b_ref[...]


def gate_fwd(proj, wg, bg, *, name, tr=512):
    s, inw = proj.shape
    qk = wg.shape[1]
    glc = inw // LANES - 1

    def body(gl_ref, w_ref, b_ref, g_ref):
        _, z = _gate_z(gl_ref, w_ref, b_ref)
        g_ref[...] = (jnp.minimum(z, 0.0) - jnp.log(1.0 + jnp.exp(-jnp.abs(z)))) * (1.0 / GLA_GATE_TAU)

    return pl.pallas_call(
        body, name=name, grid=(s // tr,),
        in_specs=[pl.BlockSpec((tr, LANES), lambda i: (i, glc)), pl.BlockSpec((LANES, qk), lambda i: (0, 0)),
                  pl.BlockSpec((1, qk), lambda i: (0, 0))],
        out_specs=pl.BlockSpec((tr, qk), lambda i: (i, 0)), out_shape=SDS((s, qk), F32),
        compiler_params=_cp("parallel"))(proj, wg, bg)


def gate_bwd(dg, proj, wg, bg, *, name, tr=512):
    s, inw = proj.shape
    qk = wg.shape[1]
    glc = inw // LANES - 1

    def body(dg_ref, gl_ref, w_ref, b_ref, dgl_ref, dw_ref, db_ref):
        i = pl.program_id(0)
        glb, z = _gate_z(gl_ref, w_ref, b_ref)
        dz = dg_ref[...] * (1.0 / (1.0 + jnp.exp(z))) * (1.0 / GLA_GATE_TAU)
        dzb = dz.astype(BF16)
        dgl_ref[...] = _dot(dzb, w_ref[...], 1, 1).astype(BF16)
        pw = _dot(glb, dzb, 0, 0)
        pb = jnp.sum(dz, axis=0, keepdims=True)

        @pl.when(i == 0)
        def _():
            dw_ref[...] = pw
            db_ref[...] = pb

        @pl.when(i > 0)
        def _():
            dw_ref[...] += pw
            db_ref[...] += pb

    return pl.pallas_call(
        body, name=name, grid=(s // tr,),
        in_specs=[pl.BlockSpec((tr, qk), lambda i: (i, 0)), pl.BlockSpec((tr, LANES), lambda i: (i, glc)),
                  pl.BlockSpec((LANES, qk), lambda i: (0, 0)), pl.BlockSpec((1, qk), lambda i: (0, 0))],
        out_specs=[pl.BlockSpec((tr, LANES), lambda i: (i, 0)), pl.BlockSpec((LANES, qk), lambda i: (0, 0)),
                   pl.BlockSpec((1, qk), lambda i: (0, 0))],
        out_shape=[SDS((s, LANES), BF16), SDS((LANES, qk), F32), SDS((1, qk), F32)],
        compiler_params=_cp("arbitrary"))(dg, proj, wg, bg)


def _gla_chunk_terms(q_ref, k_ref, g_ref, c, scale):
    q = q_ref[...] * scale
    k = k_ref[...]
    gg = g_ref[...]
    b = _split_dot(_tri(c, "le"), gg)
    row = lax.broadcasted_iota(jnp.int32, gg.shape, 0)
    bm = jnp.sum(jnp.where(row < c // 2, gg, 0.0), axis=0, keepdims=True)
    bl = jnp.sum(gg, axis=0, keepdims=True)
    eb, em, emi, el = jnp.exp(b), jnp.exp(b - bm), jnp.exp(bm - b), jnp.exp(bl - b)
    return q, k, bl, eb, em, emi, el


def _causal(a):
    r = lax.broadcasted_iota(jnp.int32, a.shape, 0)
    c = lax.broadcasted_iota(jnp.int32, a.shape, 1)
    return jnp.where(r >= c, a, 0.0)


def gla_fwd(proj, g, *, name, host=None, c=GLA_CHUNK):
    s = proj.shape[0]
    qk = g.shape[1]
    dk, dv = qk // GLA_HEADS, 2 * qk // GLA_HEADS
    nc = s // c
    scale = dk ** -0.5
    kq = qk // dk

    def body(q_ref, k_ref, v_ref, g_ref, o_ref, st_ref, state):
        @pl.when(pl.program_id(1) == 0)
        def _():
            state[...] = jnp.zeros_like(state)

        q, k, bl, eb, em, emi, el = _gla_chunk_terms(q_ref, k_ref, g_ref, c, scale)
        v2 = _hilo(v_ref[...])
        st = state[...]
        st_ref[...] = st
        a = _causal(_dot3(_hilo(q * em), _hilo(k * emi), 1, 1))
        o_ref[...] = _dot3(_hilo(q * eb), _hilo(st), 1, 1) + _dot3(_hilo(a), v2)
        state[...] = st * jnp.exp(bl) + _dot3(v2, _hilo(k * el), 0, 0)

    return _hosted_call(
        body, host, name=name, grid=(GLA_HEADS, nc),
        in_specs=[pl.BlockSpec((c, dk), lambda h, i: (i, h)), pl.BlockSpec((c, dk), lambda h, i: (i, kq + h)),
                  pl.BlockSpec((c, dv), lambda h, i: (i, kq + h)), pl.BlockSpec((c, dk), lambda h, i: (i, h))],
        out_specs=[pl.BlockSpec((c, dv), lambda h, i: (i, h)),
                   pl.BlockSpec((None, None, dv, dk), lambda h, i: (h, i, 0, 0))],
        out_shape=[SDS((s, 2 * qk), F32), SDS((GLA_HEADS, nc, dv, dk), F32)],
        scratch_shapes=[pltpu.VMEM((dv, dk), F32)], sem=("parallel", "arbitrary"), args=(proj, proj, proj, g))


def gla_bwd(proj, g, states, do, *, name, host=None, c=GLA_CHUNK):
    s = proj.shape[0]
    qk = g.shape[1]
    dk, dv = qk // GLA_HEADS, 2 * qk // GLA_HEADS
    nc = s // c
    scale = dk ** -0.5
    kq = qk // dk

    def body(q_ref, k_ref, v_ref, g_ref, do_ref, st_ref, dq_ref, dk_ref, dv_ref, dg_ref, dstate, dgc):
        @pl.when(pl.program_id(1) == 0)
        def _():
            dstate[...] = jnp.zeros_like(dstate)
            dgc[...] = jnp.zeros_like(dgc)

        q, k, bl, eb, em, emi, el = _gla_chunk_terms(q_ref, k_ref, g_ref, c, scale)
        v2, do2 = _hilo(v_ref[...]), _hilo(do_ref[...])
        qe, qm, km, kd = _hilo(q * eb), _hilo(q * em), _hilo(k * emi), _hilo(k * el)
        ds = dstate[...]
        ds2 = _hilo(ds)
        a = _hilo(_causal(_dot3(qm, km, 1, 1)))
        dv_ref[...] = (_dot3(a, do2, 0, 0) + _dot3(kd, ds2, 1, 1)).astype(BF16)
        da = _hilo(_causal(_dot3(do2, v2, 1, 1)))
        dq = _dot3(da, km) * em + _dot3(do2, _hilo(st_ref[...])) * eb
        dkk = _dot3(da, qm, 0, 0) * emi + _dot3(v2, ds2) * el
        dstate[...] = ds * jnp.exp(bl) + _dot3(do2, qe, 0, 0)
        db = q * dq - k * dkk
        dg_ref[...] = _split_dot(_tri(c, "ge"), db) + dgc[...]
        dgc[...] += jnp.sum(db, axis=0, keepdims=True)
        dq_ref[...] = (dq * scale).astype(BF16)
        dk_ref[...] = dkk.astype(BF16)

    rev = lambda i: nc - 1 - i
    qspec = pl.BlockSpec((c, dk), lambda h, i: (rev(i), h))
    vspec = pl.BlockSpec((c, dv), lambda h, i: (rev(i), h))
    return _hosted_call(
        body, host, name=name, grid=(GLA_HEADS, nc),
        in_specs=[qspec, pl.BlockSpec((c, dk), lambda h, i: (rev(i), kq + h)),
                  pl.BlockSpec((c, dv), lambda h, i: (rev(i), kq + h)), qspec, vspec,
                  pl.BlockSpec((None, None, dv, dk), lambda h, i: (h, rev(i), 0, 0))],
        out_specs=[qspec, qspec, vspec, qspec],
        out_shape=[SDS((s, qk), BF16), SDS((s, qk), BF16), SDS((s, 2 * qk), BF16), SDS((s, qk), F32)],
        scratch_shapes=[pltpu.VMEM((dv, dk), F32), pltpu.VMEM((1, dk), F32)], sem=("parallel", "arbitrary"),
        args=(proj, proj, proj, g, do, states))


def gnorm_fwd(o, proj, gw, *, name, tr=512):
    s, v = o.shape
    dv = v // GLA_HEADS
    roff = 2 * GLA_HEADS

    def body(o_ref, r_ref, w_ref, y_ref):
        of = o_ref[...]
        rs = lax.rsqrt(jnp.mean(of * of, axis=-1, keepdims=True) + EPS)
        r = r_ref[...]
        y_ref[...] = (of * rs * w_ref[...] * (r * _sigmoid(r))).astype(BF16)

    blk = pl.BlockSpec((tr, dv), lambda i, h: (i, h))
    return pl.pallas_call(
        body, name=name, grid=(s // tr, GLA_HEADS),
        in_specs=[blk, pl.BlockSpec((tr, dv), lambda i, h: (i, roff + h)), pl.BlockSpec((1, dv), lambda i, h: (0, 0))],
        out_specs=blk, out_shape=SDS((s, v), BF16), compiler_params=_cp("parallel", "parallel"))(o, proj, gw)


def gnorm_bwd(dy, o, proj, gw, *, name, tr=512):
    s, v = o.shape
    dv = v // GLA_HEADS
    roff = 2 * GLA_HEADS

    def body(dy_ref, o_ref, r_ref, w_ref, do_ref, dr_ref, dw_ref):
        first = jnp.logical_and(pl.program_id(0) == 0, pl.program_id(1) == 0)
        of = o_ref[...]
        rs = lax.rsqrt(jnp.mean(of * of, axis=-1, keepdims=True) + EPS)
        n = of * rs
        r = r_ref[...]
        sg = _sigmoid(r)
        dyf = dy_ref[...].astype(F32)
        dn_w = dyf * (r * sg)
        dr_ref[...] = (dyf * n * w_ref[...] * sg * (1.0 + r * (1.0 - sg))).astype(BF16)
        dn = dn_w * w_ref[...]
        do_ref[...] = rs * (dn - n * jnp.mean(dn * n, axis=-1, keepdims=True))
        part = jnp.sum(dn_w * n, axis=0, keepdims=True)

        @pl.when(first)
        def _():
            dw_ref[...] = part

        @pl.when(jnp.logical_not(first))
        def _():
            dw_ref[...] += part

    blk = pl.BlockSpec((tr, dv), lambda i, h: (i, h))
    vec = pl.BlockSpec((1, dv), lambda i, h: (0, 0))
    return pl.pallas_call(
        body, name=name, grid=(s // tr, GLA_HEADS),
        in_specs=[blk, blk, pl.BlockSpec((tr, dv), lambda i, h: (i, roff + h)), vec],
        out_specs=[blk, blk, vec], out_shape=[SDS((s, v), F32), SDS((s, v), BF16), SDS((1, dv), F32)],
        compiler_params=_cp("arbitrary", "arbitrary"))(dy, o, proj, gw)


def _sb_block(kblk, q, ks, q0, scale, carry, masked):
    tk, tq = kblk.shape[0], q.shape[0]
    z = _dot(kblk, q, 1, 1) * scale
    sp = jnp.maximum(z, 0.0) + jnp.log(1.0 + jnp.exp(-jnp.abs(z)))
    mask = None
    lf = -sp
    if masked:
        kpos = ks + lax.broadcasted_iota(jnp.int32, (tk, tq), 0)
        qpos = q0 + lax.broadcasted_iota(jnp.int32, (tk, tq), 1)
        mask = kpos < qpos
        lf = jnp.where(mask, lf, 0.0)
    later = _split_dot(_tri(tk, "gt"), lf)
    a = jnp.exp(z - sp + later + carry)
    if masked:
        a = jnp.where(mask, a, 0.0)
    return z, sp, mask, lf, a


def sb_fwd(q, kv, *, name, host=None, tq=SB_TQ, tk=SB_TK, group=SB_GROUP):
    s, w = q.shape
    hd = w // SB_HEADS
    nq, nkb = s // tq, s // tk
    scale = hd ** -0.5
    per = tq // tk
    assert per % group == 0

    def body(q_ref, k_ref, v_ref, o_ref, car_ref, o_acc):
        qi = pl.program_id(1)
        qb = q_ref[...]
        q0 = qi * tq
        car_ref[...] = jnp.full(car_ref.shape, SB_SKIPPED, F32)
        o_acc[...] = jnp.zeros_like(o_acc)

        def blocks(first, carry, skip=None):
            for t in reversed(range(group)):
                kj = first + t
                ks = pl.multiple_of(kj * tk, tk)
                car_ref[pl.ds(kj, 1), :] = carry
                lo = 0 if skip is None else (skip + t) * tk
                _, _, _, lf, a = _sb_block(k_ref[pl.ds(ks, tk), :], qb[lo:, :], ks, q0 + lo, scale, carry[:, lo:], skip is not None)
                o_acc[lo:, :] += _dot(a.astype(BF16), v_ref[pl.ds(ks, tk), :], 0, 0)
                add = jnp.sum(lf, axis=0, keepdims=True)
                carry = carry + (add if lo == 0 else jnp.concatenate([jnp.zeros((1, lo), F32), add], axis=1))
            return carry

        carry = jnp.zeros((1, tq), F32)
        for gidx in reversed(range(per // group)):
            carry = blocks(qi * per + gidx * group, carry, gidx * group)
        n_in = qi * (per // group)
        lax.while_loop(lambda st: jnp.logical_and(st[0] < n_in, jnp.max(st[1]) > SB_DEAD),
                       lambda st: (st[0] + 1, blocks((n_in - 1 - st[0]) * group, st[1])), (jnp.int32(0), carry))
        o_ref[...] = o_acc[...].astype(BF16)

    qspec = pl.BlockSpec((tq, hd), lambda h, i: (i, h))
    return _hosted_call(
        body, host, name=name, grid=(SB_HEADS, nq),
        in_specs=[qspec, pl.BlockSpec((None, s, hd), lambda h, i: (0, 0, h)), pl.BlockSpec((None, s, hd), lambda h, i: (1, 0, h))],
        out_specs=[qspec, pl.BlockSpec((None, None, nkb, tq), lambda h, i: (h, i, 0, 0))],
        out_shape=[SDS((s, w), BF16), SDS((SB_HEADS, nq, nkb, tq), F32)],
        scratch_shapes=[pltpu.VMEM((tq, hd), F32)], sem=("parallel", "parallel"), args=(q, kv, kv))


def sb_bwd(q, kv, do, car, *, name, host=None, tq=SB_TQ, tk=SB_TK, group=SB_GROUP):
    s, w = q.shape
    hd = w // SB_HEADS
    nq, nkb = s // tq, s // tk
    scale = hd ** -0.5
    per = tq // tk
    assert per % group == 0

    def body(q_ref, k_ref, v_ref, do_ref, car_ref, dq_ref, dkv_ref, dq_acc, dk_acc, dv_acc):
        qi = pl.program_id(1)
        qb = q_ref[...]
        dob = do_ref[...]
        q0 = qi * tq
        dq_acc[...] = jnp.zeros_like(dq_acc)

        @pl.when(qi == 0)
        def _():
            dk_acc[...] = jnp.zeros_like(dk_acc)
            dv_acc[...] = jnp.zeros_like(dv_acc)

        def blocks(first, pcar, skip=None):
            for t in range(group):
                kj = first + t
                ks = pl.multiple_of(kj * tk, tk)
                kblk = k_ref[pl.ds(ks, tk), :]
                lo = 0 if skip is None else (skip + t) * tk
                masked = skip is not None
                qs, dos = qb[lo:, :], dob[lo:, :]
                z, sp, mask, _, a = _sb_block(kblk, qs, ks, q0 + lo, scale, car_ref[pl.ds(kj, 1), :][:, lo:], masked)
                p = a * _dot(v_ref[pl.ds(ks, tk), :], dos, 1, 1)
                before = _split_dot(_tri(tk, "lt"), p)
                sg = jnp.exp(z - sp)
                dz = p * (1.0 - sg) - (pcar[:, lo:] + before) * sg
                if masked:
                    dz = jnp.where(mask, dz, 0.0)
                dz = (dz * scale).astype(BF16)
                dk_acc[pl.ds(ks, tk), :] += _dot(dz, qs)
                dv_acc[pl.ds(ks, tk), :] += _dot(a.astype(BF16), dos)
                dq_acc[lo:, :] += _dot(dz, kblk, 0, 0)
                add = jnp.sum(p, axis=0, keepdims=True)
                pcar = pcar + (add if lo == 0 else jnp.concatenate([jnp.zeros((1, lo), F32), add], axis=1))
            return pcar

        n_in = qi * (per // group)

        def reached(g):
            return (jnp.max(car_ref[pl.ds(g * group + group - 1, 1), :]) > SB_DEAD).astype(jnp.int32)

        start = n_in - lax.fori_loop(0, n_in, lambda g, n: n + reached(g), jnp.int32(0))
        pcar = lax.fori_loop(start, n_in, lambda i, c: blocks(i * group, c), jnp.zeros((1, tq), F32))
        for gidx in range(per // group):
            pcar = blocks(qi * per + gidx * group, pcar, gidx * group)
        dq_ref[...] = dq_acc[...].astype(BF16)

        @pl.when(qi == nq - 1)
        def _():
            dkv_ref[0] = dk_acc[...].astype(BF16)
            dkv_ref[1] = dv_acc[...].astype(BF16)

    qspec = pl.BlockSpec((tq, hd), lambda h, i: (i, h))
    return _hosted_call(
        body, host, name=name, grid=(SB_HEADS, nq),
        in_specs=[qspec, pl.BlockSpec((None, s, hd), lambda h, i: (0, 0, h)), pl.BlockSpec((None, s, hd), lambda h, i: (1, 0, h)),
                  qspec, pl.BlockSpec((None, None, nkb, tq), lambda h, i: (h, i, 0, 0))],
        out_specs=[qspec, pl.BlockSpec((2, s, hd), lambda h, i: (0, 0, h))],
        out_shape=[SDS((s, w), BF16), SDS((2, s, w), BF16)],
        scratch_shapes=[pltpu.VMEM((tq, hd), F32), pltpu.VMEM((s, hd), F32), pltpu.VMEM((s, hd), F32)],
        sem=("parallel", "arbitrary"), args=(q, kv, kv, do, car))


def adamw(w, g, m, v, *, name):
    shape = w.shape
    c = shape[-1]
    r = w.size // c
    tr = _tile(r, max(8, (3 * LANES * 1024) // c), unit=8) if r >= 8 else r

    def body(w_ref, g_ref, m_ref, v_ref, d_ref, nm_ref, nv_ref):
        gf = g_ref[...]
        mn = ADAM_B1 * m_ref[...] + (1.0 - ADAM_B1) * gf
        vn = ADAM_B2 * v_ref[...] + (1.0 - ADAM_B2) * (gf * gf)
        m_hat = mn / (1.0 - ADAM_B1 ** ADAM_STEP)
        v_hat = vn / (1.0 - ADAM_B2 ** ADAM_STEP)
        d_ref[...] = -ADAM_LR * (m_hat / (jnp.sqrt(v_hat) + ADAM_EPS) + ADAM_WD * w_ref[...])
        nm_ref[...] = mn
        nv_ref[...] = vn

    blk = pl.BlockSpec((tr, c), lambda i: (i, 0))
    outs = pl.pallas_call(
        body, name=name, grid=(r // tr,), in_specs=[blk] * 4, out_specs=[blk] * 3,
        out_shape=[SDS((r, c), F32)] * 3, compiler_params=_cp("parallel"))(
            *(t.reshape(r, c) for t in (w, g, m, v)))
    return tuple(o.reshape(shape) for o in outs)


def _row_tile(r, c):
    return _tile(r, max(16, (4 * LANES * 1024) // c), unit=16)


def _halves(a):
    return a.reshape(a.shape[:-2] + (2, a.shape[-2] // 2, a.shape[-1]))


def place_shard(w, chip, *, name, layer=None):
    r, c = w.shape[-2:]
    tr = _row_tile(r, c)

    def body(c_ref, w_ref, o_ref):
        o_ref[...] = w_ref[...].astype(BF16)

    if layer is None:
        w_spec = pl.BlockSpec((None, tr, c), lambda h, i, c_ref: (h, i, 0))
    else:
        w_spec = pl.BlockSpec((None, None, tr, c), lambda h, i, c_ref: (layer, h, i, 0))
    return pl.pallas_call(
        body, name=name, out_shape=SDS((N_CHIPS, 2, r, c), BF16),
        grid_spec=pltpu.PrefetchScalarGridSpec(
            num_scalar_prefetch=1, grid=(2, r // tr), in_specs=[w_spec],
            out_specs=pl.BlockSpec((None, None, tr, c), lambda h, i, c_ref: (c_ref[0], h, i, 0))),
        compiler_params=_cp("parallel", "parallel"))(chip, w)


def sibling_exchange(gs, *, name):
    n = len(gs)

    def body(*refs):
        g_refs, a_refs = refs[:n], refs[n:2 * n]
        send_sems, recv_sems = refs[2 * n:]
        x, y, c = _coords()
        cps = [_remote(g_refs[i].at[:, 1 - c], a_refs[i], send_sems.at[i], recv_sems.at[i], (x, y, 1 - c)) for i in range(n)]
        for cp in cps:
            cp.start()
        for cp in cps:
            cp.wait()

    return pl.pallas_call(
        body, name=name, in_specs=[ANY] * n, out_specs=[ANY] * n,
        out_shape=[SDS(g.shape[:1] + g.shape[2:], g.dtype) for g in gs],
        scratch_shapes=[_dma_sems(n), _dma_sems(n)])(*gs)


def half_add(g, a, core, *, name):
    n, _, r, c = g.shape
    tr = _row_tile(r, c)

    def body(c_ref, g_ref, a_ref, o_ref):
        o_ref[...] = (g_ref[...].astype(F32) + a_ref[...].astype(F32)).astype(o_ref.dtype)

    blk = pl.BlockSpec((None, tr, c), lambda s, i, c_ref: (s, i, 0))
    return pl.pallas_call(
        body, name=name, out_shape=SDS((n, r, c), g.dtype),
        grid_spec=pltpu.PrefetchScalarGridSpec(
            num_scalar_prefetch=1, grid=(n, r // tr),
            in_specs=[pl.BlockSpec((None, None, tr, c), lambda s, i, c_ref: (s, c_ref[0], i, 0)), blk], out_specs=blk),
        compiler_params=_cp("parallel", "parallel"))(core, g, a)


def chip_sum(p, b, chip, core, *, name, layer=None, into=None):
    _, r, c = p.shape
    tr = _row_tile(r, c)

    def body(chip_ref, core_ref, p_ref, b_ref, *refs):
        t = p_ref[...].astype(F32)
        for k in range(N_CHIPS - 1):
            t = t + b_ref[k].astype(F32)
        refs[-1][...] = t

    in_specs = [pl.BlockSpec((None, tr, c), lambda i, chip_ref, core_ref: (chip_ref[0], i, 0)),
                pl.BlockSpec((N_CHIPS - 1, tr, c), lambda i, chip_ref, core_ref: (0, i, 0))]
    if layer is None:
        shape, o_spec = (2, r, c), pl.BlockSpec((None, tr, c), lambda i, chip_ref, core_ref: (core_ref[0], i, 0))
    else:
        shape = (2, 2, r, c)
        o_spec = pl.BlockSpec((None, None, tr, c), lambda i, chip_ref, core_ref: (layer, core_ref[0], i, 0))
    args, aliases = (chip, core, p, b), {}
    if into is not None:
        in_specs, args, aliases = in_specs + [ANY], args + (into,), {4: 0}
    return pl.pallas_call(
        body, name=name, out_shape=SDS(shape, F32),
        grid_spec=pltpu.PrefetchScalarGridSpec(num_scalar_prefetch=2, grid=(r // tr,), in_specs=in_specs, out_specs=o_spec),
        input_output_aliases=aliases, compiler_params=_cp("parallel"))(*args)


def sibling_gather(ts, *, name):
    n = len(ts)

    def body(*refs):
        out_refs = refs[n:2 * n]
        send_sems, recv_sems = refs[2 * n:]
        x, y, c = _coords()

        def half(i, which):
            return out_refs[i].at[which] if len(ts[i].shape) == 3 else out_refs[i].at[:, which]

        cps = [_remote(half(i, c), half(i, c), send_sems.at[i], recv_sems.at[i], (x, y, 1 - c)) for i in range(n)]
        for cp in cps:
            cp.start()
        for i in range(n):
            _remote(half(i, c), half(i, 1 - c), send_sems.at[i], recv_sems.at[i], (x, y, 1 - c)).wait_recv()
        for cp in cps:
            cp.wait_send()

    return pl.pallas_call(
        body, name=name, in_specs=[ANY] * n, out_specs=[ANY] * n, out_shape=[SDS(t.shape, t.dtype) for t in ts],
        input_output_aliases={i: i for i in range(n)}, scratch_shapes=[_dma_sems(n), _dma_sems(n)])(*ts)


def allgather_all(sm, *, name):
    r, w = sm.shape

    def body(s_ref, out_ref, send_sems, recv_sems, local_sem):
        x, y, c = _coords()
        me = 4 * x + 2 * y + c

        def peer(rel):
            flip = lambda v, bit: 1 - v if bit else v
            return flip(x, rel & 4), flip(y, rel & 2), flip(c, rel & 1)

        loc = pltpu.make_async_copy(s_ref, out_ref.at[me], local_sem.at[0])
        loc.start()
        cps = [_remote(s_ref, out_ref.at[me], send_sems.at[rel - 1], recv_sems.at[rel - 1], peer(rel)) for rel in range(1, N_DEV)]
        for cp in cps:
            cp.start()
        for rel in range(1, N_DEV):
            px, py, pc = peer(rel)
            _remote(s_ref, out_ref.at[4 * px + 2 * py + pc], send_sems.at[rel - 1], recv_sems.at[rel - 1], (px, py, pc)).wait_recv()
        for cp in cps:
            cp.wait_send()
        loc.wait()

    return pl.pallas_call(
        body, name=name, in_specs=[ANY], out_specs=ANY, out_shape=SDS((N_DEV, r, w), sm.dtype),
        scratch_shapes=[_dma_sems(N_DEV - 1), _dma_sems(N_DEV - 1), _dma_sems(1)])(sm)


def sum_blocks(a, *, name):
    n, r, w = a.shape

    def body(a_ref, o_ref):
        t = a_ref[0]
        for k in range(1, n):
            t = t + a_ref[k]
        o_ref[...] = t

    return pl.pallas_call(body, name=name, out_shape=SDS((r, w), F32))(a)


class Exchanges:
    def __init__(self, placed=None, chip=None, core=None):
        self.placed, self.chip, self.core = placed, chip, core
        self.mine = {}

    def gather(self, names):
        return None if self.placed is None else GatherX([self.placed[n] for n in names])

    def reduce(self, grads, call):
        if self.placed is None:
            return call(None)
        names = list(grads)
        from_sibling = sibling_exchange([grads[n] for n in names], name="grads_to_sibling_" + names[0])
        pairs = [half_add(grads[n], a, self.core, name=f"pair_sum_{n}") for n, a in zip(names, from_sibling)]
        result, from_chips = call(ScatterX(pairs))
        for n, t, b in zip(names, pairs, from_chips):
            base, layer = (n[:-1], int(n[-1])) if n[:-1] in ("w_gu", "w_dn") else (n, None)
            self.mine[base] = chip_sum(t, b, self.chip, self.core, name=f"chip_sum_{n}", layer=layer, into=self.mine.get(base))
        return result


def _usable(name, buf, d):
    if name == "w_in":
        cols = N_CHIPS * buf.shape[-1]
        full = buf.reshape(N_CHIPS, d, -1).transpose(1, 0, 2).reshape(d, cols)
        return jnp.pad(full, ((0, 0), (0, -cols % LANES)))
    if name in ("w_out", "w_q", "w_so"):
        return buf.reshape(-1, buf.shape[-1])
    return buf.reshape(N_CHIPS, -1, buf.shape[-1])


def _by_rows(t):
    return _halves(t.reshape(N_CHIPS, -1, t.shape[-1]))


def _ffn_fwd(h, nw, w_gu, w_dn, tag, host=None):
    f = rmsnorm_fwd(h, nw, name=f"{tag}_norm")
    gu = matmul(f, View(w_gu, "cols"), name=f"{tag}_gate_up", out_dtype=BF16, host=host)
    gu, got = gu if host is not None else (gu, None)
    act = swiglu_fwd(gu, name=f"{tag}_act")
    return matmul(act, View(w_dn, "rows"), name=f"{tag}_down", residual=h), (f, gu, act), got


def _ffn_bwd(dh, h, nw, w_gu, w_dn, saved, tag, ex, dn_name, carried):
    f, gu, act = saved
    dgu = View(matmul(dh, View(w_dn, "rows"), name=f"{tag}_dgu", tb=True, dswiglu=gu), "cols")
    dw_dn = matmul(act, dh, name=f"{tag}_dw_down", ta=True, out_dtype=BF16)

    def dw_gate_up(host):
        return matmul(f, dgu, name=f"{tag}_dw_gate_up", ta=True, out_dtype=BF16, out_chips=N_CHIPS, host=host)

    dw_gu = ex.reduce(carried, dw_gate_up) if carried else dw_gate_up(None)
    df = ex.reduce({dn_name: _by_rows(dw_dn)}, lambda host: matmul(dgu, View(w_gu, "cols"), name=f"{tag}_df", tb=True, host=host))
    dh_in, dnw = rmsnorm_bwd(df, h, nw, dh, name=f"{tag}_dnorm")
    return dh_in, dnw, dw_gu, dw_dn


def local_step(x, target, p, ex):
    d = x.shape[1]
    hosted = ex.placed is not None
    w = {} if hosted else {n: _usable(n, p[n], d) for n in ("w_in", "w_out", "w_kv", "w_q", "w_so", "w_gu0", "w_gu1", "w_dn0", "w_dn1")}

    def take(names, got):
        for n, buf in zip(names, got or []):
            w[n] = _usable(n, buf, d)

    def carry(call, names):
        host = ex.gather(names)
        if host is None:
            return call(None)
        res, got = call(host)
        take(names, got)
        return res

    if hosted:
        take(["w_in"], run_exchange(ex.gather(["w_in"]), name="gather_gla_in"))
    a0 = rmsnorm_fwd(x, p["an0"], name="l0_attn_norm")
    proj = carry(lambda host: matmul(a0, w["w_in"], name="gla_in", host=host), ["w_out", "w_dn0"])
    g = gate_fwd(proj, p["wg"], p["bg"], name="gla_gate")
    o, states = carry(lambda host: gla_fwd(proj, g, name="gla_scan", host=host), ["w_gu0"])
    og = gnorm_fwd(o, proj, p["gw"], name="gla_outnorm")
    h1 = matmul(og, w["w_out"], name="gla_out", residual=x)
    sb_names = ["w_kv", "w_q", "w_so"]
    h2, ffn0, got = _ffn_fwd(h1, p["fn0"], w["w_gu0"], w["w_dn0"], "ffn0", host=ex.gather(sb_names))
    take(sb_names, got)
    w_kv = View(w["w_kv"], "cols")
    kvn = rmsnorm_fwd(h2, p["kvn"], name="kv_norm")
    kv = matmul(kvn, w_kv, name="sb_kv", out_dtype=BF16, out_chips=2)
    a1 = rmsnorm_fwd(h2, p["an1"], name="l1_attn_norm")
    q2 = matmul(a1, w["w_q"], name="sb_q", out_dtype=BF16)
    o2, car = carry(lambda host: sb_fwd(q2, kv, name="sb_attn", host=host), ["w_gu1", "w_dn1"])
    h3 = matmul(o2, w["w_so"], name="sb_out", residual=h2)
    h4, ffn1, _ = _ffn_fwd(h3, p["fn1"], w["w_gu1"], w["w_dn1"], "ffn1")
    dh4, d_fin, loss_row = final_loss(h4, p["finn"], target, name="final_loss")

    dh3, d_fn1, dw_gu1, dw_dn1 = _ffn_bwd(dh4, h3, p["fn1"], w["w_gu1"], w["w_dn1"], ffn1, "ffn1", ex, "w_dn1", {})
    do2 = matmul(dh3, w["w_so"], name="sb_do", tb=True, out_dtype=BF16)
    dw_so = matmul(o2, dh3, name="sb_dw_out", ta=True, out_dtype=BF16)
    dq2, dkv = ex.reduce({"w_gu1": _halves(dw_gu1)}, lambda host: sb_bwd(q2, kv, do2, car, name="sb_attn_bwd", host=host))
    dkv = View(dkv, "cols")
    dw_q = matmul(a1, dq2, name="sb_dw_q", ta=True, out_dtype=BF16)
    da1 = matmul(dq2, w["w_q"], name="sb_da", tb=True)
    dh2, d_an1 = rmsnorm_bwd(da1, h2, p["an1"], dh3, name="l1_attn_dnorm")
    dw_kv = matmul(kvn, dkv, name="sb_dw_kv", ta=True, out_dtype=BF16, out_chips=N_CHIPS)
    dkvn = matmul(dkv, w_kv, name="sb_dkvn", tb=True)
    dh2, d_kvn = rmsnorm_bwd(dkvn, h2, p["kvn"], dh2, name="kv_dnorm")
    sb_grads = {"w_kv": _halves(dw_kv), "w_q": _by_rows(dw_q), "w_so": _by_rows(dw_so)}
    dh1, d_fn0, dw_gu0, dw_dn0 = _ffn_bwd(dh2, h1, p["fn0"], w["w_gu0"], w["w_dn0"], ffn0, "ffn0", ex, "w_dn0", sb_grads)
    dog = matmul(dh1, w["w_out"], name="gla_dog", tb=True, out_dtype=BF16)
    dw_out = matmul(og, dh1, name="gla_dw_out", ta=True, out_dtype=BF16)
    do, dr, d_gw = gnorm_bwd(dog, o, proj, p["gw"], name="gla_outnorm_bwd")
    dq, dk, dv, dg = ex.reduce({"w_gu0": _halves(dw_gu0)}, lambda host: gla_bwd(proj, g, states, do, name="gla_scan_bwd", host=host))
    dgl, d_wg, d_bg = gate_bwd(dg, proj, p["wg"], p["bg"], name="gla_gate_bwd")
    dproj = jnp.concatenate([dq, dk, dv, dr, dgl], axis=1)
    dw_in = matmul(a0, dproj, name="gla_dw_in", ta=True, out_dtype=BF16)
    in_w = p["in_w"]
    gla_grads = {"w_in": _halves(dw_in[:, :in_w].reshape(d, N_CHIPS, -1).transpose(1, 0, 2)), "w_out": _by_rows(dw_out)}
    da0 = ex.reduce(gla_grads, lambda host: matmul(dproj, w["w_in"], name="gla_da", tb=True, host=host))
    dx, d_an0 = rmsnorm_bwd(da0, x, p["an0"], dh1, name="l0_attn_dnorm")

    small = dict(an0=d_an0, an1=d_an1, fn0=d_fn0, fn1=d_fn1, kvn=d_kvn, finn=d_fin, wg=d_wg, bg=d_bg, gw=d_gw)
    big = {}
    if not hosted:
        big = dict(gla_grads, **sb_grads, w_gu0=_halves(dw_gu0), w_gu1=_halves(dw_gu1), w_dn0=_by_rows(dw_dn0), w_dn1=_by_rows(dw_dn1))
    return loss_row, dx, small, big


def _pack_rows(parts):
    cat = jnp.concatenate([t.reshape(-1) for t in parts])
    rows = -(-cat.size // (LANES * SUBLANES)) * SUBLANES
    return jnp.pad(cat, (0, rows * LANES - cat.size)).reshape(rows, LANES)


def _segments(flat, sizes):
    out, off = [], 0
    for n in sizes:
        out.append(flat[..., off:off + n])
        off += n
    return out


def kernel(x, attn_norm_w, ffn_norm_w, gla_w_in, gla_w_gate_up, gla_b_gate, gla_gnorm_w, gla_w_out, kv_norm_w, sb_w_kv, sb_w_q, sb_w_out, ffn_w_gate_up, ffn_w_down, final_norm_w, loss_target, m_attn_norm_w, m_ffn_norm_w, m_gla_w_in, m_gla_w_gate_up, m_gla_b_gate, m_gla_gnorm_w, m_gla_w_out, m_kv_norm_w, m_sb_w_kv, m_sb_w_q, m_sb_w_out, m_ffn_w_gate_up, m_ffn_w_down, m_final_norm_w, v_attn_norm_w, v_ffn_norm_w, v_gla_w_in, v_gla_w_gate_up, v_gla_b_gate, v_gla_gnorm_w, v_gla_w_out, v_kv_norm_w, v_sb_w_kv, v_sb_w_q, v_sb_w_out, v_ffn_w_gate_up, v_ffn_w_down, v_final_norm_w):
    xi, yi, ci = _coords()
    core = ci.astype(jnp.int32).reshape(1)
    chip1 = (2 * xi + yi).astype(jnp.int32)
    chip = chip1.reshape(1)
    rank = gla_w_gate_up.shape[1]

    gu4, dn4 = _halves(ffn_w_gate_up), _halves(ffn_w_down)
    shards = dict(w_in=(_halves(gla_w_in[0]), None), w_out=(_halves(gla_w_out[0]), None), w_kv=(_halves(sb_w_kv), None),
                  w_q=(_halves(sb_w_q[0]), None), w_so=(_halves(sb_w_out[0]), None),
                  w_gu0=(gu4, 0), w_gu1=(gu4, 1), w_dn0=(dn4, 0), w_dn1=(dn4, 1))
    placed = {n: place_shard(a, chip, name=f"place_{n}", layer=l) for n, (a, l) in shards.items()}
    small_w = [gla_w_gate_up, gla_b_gate, gla_gnorm_w]
    small_all = allgather_all(_pack_rows(small_w), name="gather_gate_weights")
    wg, bg, gw = _segments(small_all[::2].reshape(N_CHIPS, -1), [a.size for a in small_w])
    wg = wg.reshape(N_CHIPS, rank, -1).transpose(1, 0, 2).reshape(rank, -1)
    p = dict(an0=attn_norm_w[0], an1=attn_norm_w[1], fn0=ffn_norm_w[0], fn1=ffn_norm_w[1], kvn=kv_norm_w, finn=final_norm_w,
             wg=jnp.pad(wg, ((0, LANES - rank), (0, 0))).astype(BF16), bg=bg.reshape(1, -1), gw=gw.reshape(1, -1),
             in_w=N_CHIPS * gla_w_in.shape[-1])

    ex = Exchanges(placed, chip, core)
    loss_row, dx, g, _ = local_step(x[0], loss_target[0], p, ex)
    loss = lax.psum(loss_row[0, 0], ("x", "y", "c"))

    tags = list(ex.mine)
    tot = dict(zip(tags, sibling_gather([ex.mine[n] for n in tags], name="grads_from_sibling")))

    vecs = [jnp.concatenate([g["an0"], g["an1"]]), jnp.concatenate([g["fn0"], g["fn1"]]), g["kvn"], g["finn"],
            g["wg"][:rank], g["bg"], g["gw"]]
    gathered_vecs = allgather_all(_pack_rows(vecs), name="gather_small_grads")
    d_an, d_fn, d_kvn, d_fin, d_wg, d_bg, d_gw = _segments(
        sum_blocks(gathered_vecs, name="sum_small_grads").reshape(-1), [t.size for t in vecs])

    def shard(t, like):
        return lax.dynamic_index_in_dim(t.reshape(-1, N_CHIPS, like.shape[-1]), chip1, axis=1, keepdims=False).reshape(like.shape)

    weights = dict(
        attn_norm_w=(attn_norm_w, m_attn_norm_w, v_attn_norm_w), ffn_norm_w=(ffn_norm_w, m_ffn_norm_w, v_ffn_norm_w),
        gla_w_in=(gla_w_in, m_gla_w_in, v_gla_w_in), gla_w_gate_up=(gla_w_gate_up, m_gla_w_gate_up, v_gla_w_gate_up),
        gla_b_gate=(gla_b_gate, m_gla_b_gate, v_gla_b_gate), gla_gnorm_w=(gla_gnorm_w, m_gla_gnorm_w, v_gla_gnorm_w),
        gla_w_out=(gla_w_out, m_gla_w_out, v_gla_w_out), kv_norm_w=(kv_norm_w, m_kv_norm_w, v_kv_norm_w),
        sb_w_kv=(sb_w_kv, m_sb_w_kv, v_sb_w_kv), sb_w_q=(sb_w_q, m_sb_w_q, v_sb_w_q), sb_w_out=(sb_w_out, m_sb_w_out, v_sb_w_out),
        ffn_w_gate_up=(ffn_w_gate_up, m_ffn_w_gate_up, v_ffn_w_gate_up), ffn_w_down=(ffn_w_down, m_ffn_w_down, v_ffn_w_down),
        final_norm_w=(final_norm_w, m_final_norm_w, v_final_norm_w))
    grads = dict(
        attn_norm_w=d_an.reshape(attn_norm_w.shape), ffn_norm_w=d_fn.reshape(ffn_norm_w.shape),
        gla_w_in=tot["w_in"].reshape(gla_w_in.shape), gla_w_gate_up=shard(d_wg, gla_w_gate_up),
        gla_b_gate=shard(d_bg, gla_b_gate), gla_gnorm_w=shard(d_gw, gla_gnorm_w),
        gla_w_out=tot["w_out"].reshape(gla_w_out.shape), kv_norm_w=d_kvn.reshape(kv_norm_w.shape),
        sb_w_kv=tot["w_kv"].reshape(sb_w_kv.shape), sb_w_q=tot["w_q"].reshape(sb_w_q.shape),
        sb_w_out=tot["w_so"].reshape(sb_w_out.shape),
        ffn_w_gate_up=tot["w_gu"].reshape(ffn_w_gate_up.shape), ffn_w_down=tot["w_dn"].reshape(ffn_w_down.shape),
        final_norm_w=d_fin.reshape(final_norm_w.shape))
    names = list(weights)
    stepped = [adamw(weights[n][0], grads[n], weights[n][1], weights[n][2], name=f"adamw_{n}") for n in names]
    return (loss, dx.reshape(x.shape), *[grads[n] for n in names], *[t[0] for t in stepped], *[t[1] for t in stepped],
            *[t[2] for t in stepped])
```

```python
import functools

import jax
import jax.numpy as jnp
from jax import lax
from jax.experimental import pallas as pl
from jax.experimental.pallas import tpu as pltpu

F32 = jnp.float32
BF16 = jnp.bfloat16
SDS = jax.ShapeDtypeStruct
MESH = pl.DeviceIdType.MESH

EPS = 1e-6
GLA_HEADS = 4
GLA_GATE_RANK = 16
GLA_GATE_TAU = 16.0
GLA_CHUNK = 128
SB_HEADS = 16
SB_TQ = 512
SB_TK = 128
SB_GROUP = 2
SB_DEAD = -110.0
SB_SKIPPED = -1e30
ADAM_LR = 0.001
ADAM_B1 = 0.9
ADAM_B2 = 0.999
ADAM_EPS = 1e-08
ADAM_WD = 0.01
ADAM_STEP = 10

LANES = 128
SUBLANES = 8
N_CHIPS = 4
N_DEV = 8
VMEM_LIMIT = 56 * 1024 * 1024


def _tile(dim, target, unit=LANES):
    if dim <= target:
        return dim
    t = (target // unit) * unit
    while t >= unit:
        if dim % t == 0:
            return t
        t -= unit
    raise ValueError(f"no tile for {dim}")


def _cp(*sem):
    return pltpu.CompilerParams(dimension_semantics=sem, vmem_limit_bytes=VMEM_LIMIT)


def _sigmoid(x):
    return 1.0 / (1.0 + jnp.exp(-x))


def _dot(a, b, ca=1, cb=0):
    return lax.dot_general(a, b, (((ca,), (cb,)), ((), ())), preferred_element_type=F32)


def _split_dot(tri, x):
    hi = x.astype(BF16)
    lo = (x - hi.astype(F32)).astype(BF16)
    return _dot(tri, hi) + _dot(tri, lo)


def _hilo(x):
    hi = x.astype(BF16)
    return hi, (x - hi.astype(F32)).astype(BF16)


def _dot3(a, b, ca=1, cb=0):
    return _dot(a[0], b[0], ca, cb) + _dot(a[0], b[1], ca, cb) + _dot(a[1], b[0], ca, cb)


def _tri(n, kind):
    r = lax.broadcasted_iota(jnp.int32, (n, n), 0)
    c = lax.broadcasted_iota(jnp.int32, (n, n), 1)
    m = {"le": c <= r, "ge": c >= r, "lt": c < r, "gt": c > r}[kind]
    return jnp.where(m, 1.0, 0.0).astype(BF16)


ANY = pl.BlockSpec(memory_space=pl.ANY)


def _coords():
    return lax.axis_index("x"), lax.axis_index("y"), lax.axis_index("c")


def _other_chips(x, y):
    return [(1 - x, y), (x, 1 - y), (1 - x, 1 - y)]


def _remote(src, dst, send_sem, recv_sem, dev):
    return pltpu.make_async_remote_copy(src_ref=src, dst_ref=dst, send_sem=send_sem, recv_sem=recv_sem,
                                        device_id=dev, device_id_type=MESH)


def _dma_sems(n):
    return pltpu.SemaphoreType.DMA((n,))


class GatherX:
    def __init__(self, bufs):
        self.ins, self.ios, self.outs, self.n_sems = [], list(bufs), [], 6 * len(bufs)

    def _copy(self, ios, send_sems, recv_sems, i, k, chip, half, dev):
        blk = ios[i].at[chip, half]
        return _remote(blk, blk, send_sems.at[6 * i + k], recv_sems.at[6 * i + k], dev)

    def _first(self, ios, send_sems, recv_sems):
        x, y, c = _coords()
        return [self._copy(ios, send_sems, recv_sems, i, k, 2 * x + y, c, (cx, cy, c))
                for i in range(len(ios)) for k, (cx, cy) in enumerate(_other_chips(x, y))]

    def start(self, ins, ios, outs, send_sems, recv_sems):
        for cp in self._first(ios, send_sems, recv_sems):
            cp.start()

    def finish(self, ins, ios, outs, send_sems, recv_sems):
        x, y, c = _coords()
        chips = _other_chips(x, y)
        copy = functools.partial(self._copy, ios, send_sems, recv_sems)
        passed = []
        for i in range(len(ios)):
            for k, (cx, cy) in enumerate(chips):
                copy(i, k, 2 * cx + cy, c, (x, y, c)).wait_recv()
                passed.append(copy(i, 3 + k, 2 * cx + cy, c, (x, y, 1 - c)))
                passed[-1].start()
        for i in range(len(ios)):
            for k, (cx, cy) in enumerate(chips):
                copy(i, 3 + k, 2 * cx + cy, 1 - c, (x, y, c)).wait_recv()
        for cp in self._first(ios, send_sems, recv_sems) + passed:
            cp.wait_send()


class ScatterX:
    def __init__(self, ps):
        self.ins, self.ios, self.n_sems = list(ps), [], 3 * len(ps)
        self.outs = [SDS((N_CHIPS - 1,) + p.shape[1:], p.dtype) for p in ps]

    def _copies(self, ins, outs, send_sems, recv_sems):
        x, y, c = _coords()
        return [_remote(ins[i].at[2 * cx + cy], outs[i].at[k], send_sems.at[3 * i + k], recv_sems.at[3 * i + k], (cx, cy, c))
                for i in range(len(ins)) for k, (cx, cy) in enumerate(_other_chips(x, y))]

    def start(self, ins, ios, outs, send_sems, recv_sems):
        for cp in self._copies(ins, outs, send_sems, recv_sems):
            cp.start()

    def finish(self, ins, ios, outs, send_sems, recv_sems):
        for cp in self._copies(ins, outs, send_sems, recv_sems):
            cp.wait()


def _exchange_operands(host):
    x_in = host.ins + host.ios
    x_out = [SDS(a.shape, a.dtype) for a in host.ios] + host.outs
    return x_in, x_out


def run_exchange(host, *, name):
    x_in, x_out = _exchange_operands(host)
    n_ins, n_ios = len(host.ins), len(host.ios)

    def body(*refs):
        xin, xout = refs[:len(x_in)], refs[len(x_in):len(x_in) + len(x_out)]
        send_sems, recv_sems = refs[len(x_in) + len(x_out):]
        x_refs = (xin[:n_ins], xout[:n_ios], xout[n_ios:])
        host.start(*x_refs, send_sems, recv_sems)
        host.finish(*x_refs, send_sems, recv_sems)

    return list(pl.pallas_call(
        body, name=name, in_specs=[ANY] * len(x_in), out_specs=[ANY] * len(x_out), out_shape=x_out,
        input_output_aliases={n_ins + i: i for i in range(n_ios)},
        scratch_shapes=[_dma_sems(host.n_sems), _dma_sems(host.n_sems)])(*x_in))


def _hosted_call(body, host, *, name, grid, in_specs, out_specs, out_shape, scratch_shapes, sem, args):
    if host is None:
        return list(pl.pallas_call(body, name=name, grid=grid, in_specs=in_specs, out_specs=out_specs, out_shape=out_shape,
                                   scratch_shapes=scratch_shapes, compiler_params=_cp(*sem))(*args))
    x_in, x_out = _exchange_operands(host)
    n_in, n_out, n_scr, n_ins, n_ios = len(in_specs), len(out_specs), len(scratch_shapes), len(host.ins), len(host.ios)

    def hosted(*refs):
        ins, xin = refs[:n_in], refs[n_in:n_in + len(x_in)]
        o0 = n_in + len(x_in)
        outs, xout = refs[o0:o0 + n_out], refs[o0 + n_out:o0 + n_out + len(x_out)]
        s0 = o0 + n_out + len(x_out)
        scr, (send_sems, recv_sems) = refs[s0:s0 + n_scr], refs[s0 + n_scr:]
        ids = [pl.program_id(ax) for ax in range(len(grid))]
        first = functools.reduce(jnp.logical_and, [i == 0 for i in ids])
        last = functools.reduce(jnp.logical_and, [i == n - 1 for i, n in zip(ids, grid)])
        x_refs = (xin[:n_ins], xout[:n_ios], xout[n_ios:])

        @pl.when(first)
        def _():
            host.start(*x_refs, send_sems, recv_sems)

        body(*ins, *outs, *scr)

        @pl.when(last)
        def _():
            host.finish(*x_refs, send_sems, recv_sems)

    res = pl.pallas_call(
        hosted, name=name, grid=grid, in_specs=list(in_specs) + [ANY] * len(x_in), out_specs=list(out_specs) + [ANY] * len(x_out),
        out_shape=list(out_shape) + x_out, scratch_shapes=list(scratch_shapes) + [_dma_sems(host.n_sems), _dma_sems(host.n_sems)],
        input_output_aliases={n_in + n_ins + i: n_out + i for i in range(n_ios)},
        compiler_params=_cp(*(("arbitrary",) * len(grid))))(*args, *x_in)
    return list(res[:n_out]), list(res[n_out:])


def _div(i, n):
    return i if n == 1 else lax.div(i, n)


def _rem(i, n):
    return 0 if n == 1 else lax.rem(i, n)


class View:
    def __init__(self, arr, kind="plain", lead=(), g0=0, ng=None):
        self.arr, self.kind, self.lead, self.g0 = arr, kind, tuple(lead), g0
        self.ng = (arr.shape[0] - g0) if ng is None else ng
        r, c = arr.shape[-2:]
        self.runit, self.cunit = r, c
        self.shape = {"plain": (r, c), "cols": (r, self.ng * c), "rows": (self.ng * r, c)}[kind]

    def spec(self, br, bc, rfn, cfn):
        if self.kind == "plain":
            return pl.BlockSpec((br, bc), lambda *g: (rfn(*g), cfn(*g)))
        none = (None,) * (1 + len(self.lead))
        if self.kind == "cols":
            per = self.cunit // bc
            return pl.BlockSpec(none + (br, bc), lambda *g: (self.g0 + _div(cfn(*g), per), *self.lead, rfn(*g), _rem(cfn(*g), per)))
        per = self.runit // br
        return pl.BlockSpec(none + (br, bc), lambda *g: (self.g0 + _div(rfn(*g), per), *self.lead, _rem(rfn(*g), per), cfn(*g)))


def _as_view(a):
    return a if isinstance(a, View) else View(a)


def matmul(a, b, *, name, ta=False, tb=False, out_dtype=F32, residual=None, out_chips=None, host=None, dswiglu=None,
           tm=1408, tn=1408, tk=2816):
    a, b = _as_view(a), _as_view(b)
    (k, m) = a.shape if ta else a.shape[::-1]
    (n, kb) = b.shape if tb else b.shape[::-1]
    assert k == kb, (a.shape, b.shape, ta, tb)
    m_unit = a.cunit if ta else a.runit
    ka_unit = a.runit if ta else a.cunit
    n_unit = b.runit if tb else b.cunit
    kb_unit = b.cunit if tb else b.runit
    if out_chips is not None:
        n_unit = min(n_unit, n // out_chips)
    tm, tn = _tile(min(m, m_unit), tm), _tile(min(n, n_unit), tn)
    tk = _tile(min(k, ka_unit, kb_unit), tk)
    assert ka_unit % tk == 0 and kb_unit % tk == 0, (ka_unit, kb_unit, tk)
    nk = k // tk
    ca, cb = (0 if ta else 1), (1 if tb else 0)

    def body(a_ref, b_ref, *refs):
        r_ref = refs[0] if residual is not None else None
        o_ref = refs[-1] if nk == 1 else refs[-2]

        def finish(r):
            if residual is not None:
                r = r + r_ref[...]
            if dswiglu is None:
                o_ref[...] = r.astype(out_dtype)
                return
            g = refs[0][...].astype(F32)
            sg = _sigmoid(g)
            o_ref[0] = (r * refs[1][...].astype(F32) * sg * (1.0 + g * (1.0 - sg))).astype(BF16)
            o_ref[1] = (r * g * sg).astype(BF16)

        part = _dot(a_ref[...].astype(BF16), b_ref[...].astype(BF16), ca, cb)
        if nk == 1:
            finish(part)
            return
        acc = refs[-1]
        kk = pl.program_id(2)

        @pl.when(kk == 0)
        def _():
            acc[...] = part

        @pl.when(kk > 0)
        def _():
            acc[...] += part

        @pl.when(kk == nk - 1)
        def _():
            finish(acc[...])

    gi, gj, gk = (lambda i, j, kk: i), (lambda i, j, kk: j), (lambda i, j, kk: kk)
    a_spec = a.spec(tk, tm, gk, gi) if ta else a.spec(tm, tk, gi, gk)
    b_spec = b.spec(tn, tk, gj, gk) if tb else b.spec(tk, tn, gk, gj)
    if out_chips is None:
        out = View(SDS((m, n), out_dtype))
    else:
        out = View(SDS((out_chips, m, n // out_chips), out_dtype), "cols")
    o_spec = out.spec(tm, tn, gi, gj)
    in_specs, args = [a_spec, b_spec], [a.arr, b.arr]
    if residual is not None:
        in_specs.append(pl.BlockSpec((tm, tn), lambda i, j, kk: (i, j)))
        args.append(residual)
    if dswiglu is not None:
        assert residual is None and out_chips is None and dswiglu.shape == (2, m, n)
        in_specs += [pl.BlockSpec((None, tm, tn), lambda i, j, kk: (0, i, j)), pl.BlockSpec((None, tm, tn), lambda i, j, kk: (1, i, j))]
        args += [dswiglu, dswiglu]
        out = View(SDS((2, m, n), BF16))
        o_spec = pl.BlockSpec((2, tm, tn), lambda i, j, kk: (0, i, j))
    res = _hosted_call(
        body, host, name=name, grid=(m // tm, n // tn, nk), in_specs=in_specs, out_specs=[o_spec], out_shape=[out.arr],
        scratch_shapes=[] if nk == 1 else [pltpu.VMEM((tm, tn), F32)], sem=("parallel", "parallel", "arbitrary"), args=args)
    return res[0] if host is None else (res[0][0], res[1])


def gate_up_swiglu(f, w_gu, *, name, host=None, tm=512, tn=1408):
    m, k = f.shape
    n = w_gu.shape[1] // 2
    assert w_gu.shape[0] == k and w_gu.runit == k
    tm, tn = _tile(m, tm), _tile(min(n, w_gu.cunit), tn)
    nf = n // tn

    def body(a_ref, bg_ref, bu_ref, gu_ref, act_ref):
        a = a_ref[...]
        pg, pu = _dot(a, bg_ref[...]), _dot(a, bu_ref[...])
        gu_ref[0] = pg.astype(BF16)
        gu_ref[1] = pu.astype(BF16)
        act_ref[...] = (pg * _sigmoid(pg) * pu).astype(BF16)

    rows, cols, zero = (lambda j, i: i), (lambda j, i: j), (lambda j, i: 0)
    res = _hosted_call(
        body, host, name=name, grid=(nf, m // tm),
        in_specs=[pl.BlockSpec((tm, k), lambda j, i: (i, 0)), w_gu.spec(k, tn, zero, cols), w_gu.spec(k, tn, zero, lambda j, i: j + nf)],
        out_specs=[pl.BlockSpec((2, tm, tn), lambda j, i: (0, i, j)), pl.BlockSpec((tm, tn), lambda j, i: (i, j))],
        out_shape=[SDS((2, m, n), BF16), SDS((m, n), BF16)], scratch_shapes=[], sem=("parallel", "parallel"),
        args=(f, w_gu.arr, w_gu.arr))
    return (res[0], res[1], None) if host is None else (res[0][0], res[0][1], res[1])


def rmsnorm_fwd(x, w, *, name, tr=512):
    s, d = x.shape

    def body(x_ref, w_ref, o_ref):
        xf = x_ref[...]
        r = lax.rsqrt(jnp.mean(xf * xf, axis=-1, keepdims=True) + EPS)
        o_ref[...] = (xf * r * w_ref[...]).astype(BF16)

    row = pl.BlockSpec((tr, d), lambda i: (i, 0))
    return pl.pallas_call(
        body, name=name, grid=(s // tr,), in_specs=[row, pl.BlockSpec((1, d), lambda i: (0, 0))], out_specs=row,
        out_shape=SDS((s, d), BF16), compiler_params=_cp("parallel"))(x, w.reshape(1, d))


def rmsnorm_bwd(dy, x, w, dres, *, name, tr=512):
    s, d = x.shape

    def body(dy_ref, x_ref, w_ref, dres_ref, dx_ref, dw_ref):
        i = pl.program_id(0)
        xf = x_ref[...]
        r = lax.rsqrt(jnp.mean(xf * xf, axis=-1, keepdims=True) + EPS)
        xh = xf * r
        dyf = dy_ref[...].astype(F32)
        dxh = dyf * w_ref[...]
        dx_ref[...] = dres_ref[...] + r * (dxh - xh * jnp.mean(dxh * xh, axis=-1, keepdims=True))
        part = jnp.sum(dyf * xh, axis=0, keepdims=True)

        @pl.when(i == 0)
        def _():
            dw_ref[...] = part

        @pl.when(i > 0)
        def _():
            dw_ref[...] += part

    row = pl.BlockSpec((tr, d), lambda i: (i, 0))
    vec = pl.BlockSpec((1, d), lambda i: (0, 0))
    return pl.pallas_call(
        body, name=name, grid=(s // tr,), in_specs=[row, row, vec, row], out_specs=[row, vec],
        out_shape=[SDS((s, d), F32), SDS((1, d), F32)], compiler_params=_cp("arbitrary"))(dy, x, w.reshape(1, d), dres)


def final_loss(h, w, target, *, name, tr=512):
    s, d = h.shape

    def body(h_ref, w_ref, t_ref, dh_ref, dw_ref, loss_ref):
        i = pl.program_id(0)
        xf = h_ref[...]
        r = lax.rsqrt(jnp.mean(xf * xf, axis=-1, keepdims=True) + EPS)
        xh = xf * r
        err = xh * w_ref[...] - t_ref[...]
        lpart = 0.5 * jnp.sum(jnp.sum(err * err, axis=-1, keepdims=True) * (1.0 / d), axis=0, keepdims=True)
        dy = err * (1.0 / d)
        dxh = dy * w_ref[...]
        dh_ref[...] = r * (dxh - xh * jnp.mean(dxh * xh, axis=-1, keepdims=True))
        part = jnp.sum(dy * xh, axis=0, keepdims=True)
        lrow = jnp.broadcast_to(lpart, (1, LANES))

        @pl.when(i == 0)
        def _():
            dw_ref[...] = part
            loss_ref[...] = lrow

        @pl.when(i > 0)
        def _():
            dw_ref[...] += part
            loss_ref[...] += lrow

    row = pl.BlockSpec((tr, d), lambda i: (i, 0))
    vec = pl.BlockSpec((1, d), lambda i: (0, 0))
    return pl.pallas_call(
        body, name=name, grid=(s // tr,), in_specs=[row, vec, row],
        out_specs=[row, vec, pl.BlockSpec((1, LANES), lambda i: (0, 0))],
        out_shape=[SDS((s, d), F32), SDS((1, d), F32), SDS((1, LANES), F32)],
        compiler_params=_cp("arbitrary"))(h, w.reshape(1, d), target)


def _gate_z(gl_ref, w_ref, b_ref):
    glb = gl_ref[...].astype(BF16)
    return glb, _dot(glb, w_ref[...]) + b_ref[...]


def gate_fwd(proj, wg, bg, *, name, tr=512):
    s, inw = proj.shape
    qk = wg.shape[1]
    glc = inw // LANES - 1

    def body(gl_ref, w_ref, b_ref, g_ref):
        _, z = _gate_z(gl_ref, w_ref, b_ref)
        g_ref[...] = (jnp.minimum(z, 0.0) - jnp.log(1.0 + jnp.exp(-jnp.abs(z)))) * (1.0 / GLA_GATE_TAU)

    return pl.pallas_call(
        body, name=name, grid=(s // tr,),
        in_specs=[pl.BlockSpec((tr, LANES), lambda i: (i, glc)), pl.BlockSpec((LANES, qk), lambda i: (0, 0)),
                  pl.BlockSpec((1, qk), lambda i: (0, 0))],
        out_specs=pl.BlockSpec((tr, qk), lambda i: (i, 0)), out_shape=SDS((s, qk), F32),
        compiler_params=_cp("parallel"))(proj, wg, bg)


def gate_bwd(dg, proj, wg, bg, *, name, tr=512):
    s, inw = proj.shape
    qk = wg.shape[1]
    glc = inw // LANES - 1

    def body(dg_ref, gl_ref, w_ref, b_ref, dgl_ref, dw_ref, db_ref):
        i = pl.program_id(0)
        glb, z = _gate_z(gl_ref, w_ref, b_ref)
        dz = dg_ref[...] * (1.0 / (1.0 + jnp.exp(z))) * (1.0 / GLA_GATE_TAU)
        dzb = dz.astype(BF16)
        dgl_ref[...] = _dot(dzb, w_ref[...], 1, 1).astype(BF16)
        pw = _dot(glb, dzb, 0, 0)
        pb = jnp.sum(dz, axis=0, keepdims=True)

        @pl.when(i == 0)
        def _():
            dw_ref[...] = pw
            db_ref[...] = pb

        @pl.when(i > 0)
        def _():
            dw_ref[...] += pw
            db_ref[...] += pb

    return pl.pallas_call(
        body, name=name, grid=(s // tr,),
        in_specs=[pl.BlockSpec((tr, qk), lambda i: (i, 0)), pl.BlockSpec((tr, LANES), lambda i: (i, glc)),
                  pl.BlockSpec((LANES, qk), lambda i: (0, 0)), pl.BlockSpec((1, qk), lambda i: (0, 0))],
        out_specs=[pl.BlockSpec((tr, LANES), lambda i: (i, 0)), pl.BlockSpec((LANES, qk), lambda i: (0, 0)),
                   pl.BlockSpec((1, qk), lambda i: (0, 0))],
        out_shape=[SDS((s, LANES), BF16), SDS((LANES, qk), F32), SDS((1, qk), F32)],
        compiler_params=_cp("arbitrary"))(dg, proj, wg, bg)


def _gla_chunk_terms(q_ref, k_ref, g_ref, c, scale):
    q = q_ref[...] * scale
    k = k_ref[...]
    gg = g_ref[...]
    b = _split_dot(_tri(c, "le"), gg)
    row = lax.broadcasted_iota(jnp.int32, gg.shape, 0)
    bm = jnp.sum(jnp.where(row < c // 2, gg, 0.0), axis=0, keepdims=True)
    bl = jnp.sum(gg, axis=0, keepdims=True)
    eb, em, emi, el = jnp.exp(b), jnp.exp(b - bm), jnp.exp(bm - b), jnp.exp(bl - b)
    return q, k, bl, eb, em, emi, el


def _causal(a):
    r = lax.broadcasted_iota(jnp.int32, a.shape, 0)
    c = lax.broadcasted_iota(jnp.int32, a.shape, 1)
    return jnp.where(r >= c, a, 0.0)


def gla_fwd(proj, g, *, name, host=None, c=GLA_CHUNK):
    s = proj.shape[0]
    qk = g.shape[1]
    dk, dv = qk // GLA_HEADS, 2 * qk // GLA_HEADS
    nc = s // c
    scale = dk ** -0.5
    kq = qk // dk

    def body(q_ref, k_ref, v_ref, g_ref, o_ref, st_ref, state):
        @pl.when(pl.program_id(1) == 0)
        def _():
            state[...] = jnp.zeros_like(state)

        q, k, bl, eb, em, emi, el = _gla_chunk_terms(q_ref, k_ref, g_ref, c, scale)
        v2 = _hilo(v_ref[...])
        st = state[...]
        st_ref[...] = st
        a = _causal(_dot3(_hilo(q * em), _hilo(k * emi), 1, 1))
        o_ref[...] = _dot3(_hilo(q * eb), _hilo(st), 1, 1) + _dot3(_hilo(a), v2)
        state[...] = st * jnp.exp(bl) + _dot3(v2, _hilo(k * el), 0, 0)

    return _hosted_call(
        body, host, name=name, grid=(GLA_HEADS, nc),
        in_specs=[pl.BlockSpec((c, dk), lambda h, i: (i, h)), pl.BlockSpec((c, dk), lambda h, i: (i, kq + h)),
                  pl.BlockSpec((c, dv), lambda h, i: (i, kq + h)), pl.BlockSpec((c, dk), lambda h, i: (i, h))],
        out_specs=[pl.BlockSpec((c, dv), lambda h, i: (i, h)),
                   pl.BlockSpec((None, None, dv, dk), lambda h, i: (h, i, 0, 0))],
        out_shape=[SDS((s, 2 * qk), F32), SDS((GLA_HEADS, nc, dv, dk), F32)],
        scratch_shapes=[pltpu.VMEM((dv, dk), F32)], sem=("parallel", "arbitrary"), args=(proj, proj, proj, g))


def gla_bwd(proj, g, states, do, *, name, host=None, c=GLA_CHUNK):
    s = proj.shape[0]
    qk = g.shape[1]
    dk, dv = qk // GLA_HEADS, 2 * qk // GLA_HEADS
    nc = s // c
    scale = dk ** -0.5
    kq = qk // dk

    def body(q_ref, k_ref, v_ref, g_ref, do_ref, st_ref, dq_ref, dk_ref, dv_ref, dg_ref, dstate, dgc):
        @pl.when(pl.program_id(1) == 0)
        def _():
            dstate[...] = jnp.zeros_like(dstate)
            dgc[...] = jnp.zeros_like(dgc)

        q, k, bl, eb, em, emi, el = _gla_chunk_terms(q_ref, k_ref, g_ref, c, scale)
        v2, do2 = _hilo(v_ref[...]), _hilo(do_ref[...])
        qe, qm, km, kd = _hilo(q * eb), _hilo(q * em), _hilo(k * emi), _hilo(k * el)
        ds = dstate[...]
        ds2 = _hilo(ds)
        a = _hilo(_causal(_dot3(qm, km, 1, 1)))
        dv_ref[...] = (_dot3(a, do2, 0, 0) + _dot3(kd, ds2, 1, 1)).astype(BF16)
        da = _hilo(_causal(_dot3(do2, v2, 1, 1)))
        dq = _dot3(da, km) * em + _dot3(do2, _hilo(st_ref[...])) * eb
        dkk = _dot3(da, qm, 0, 0) * emi + _dot3(v2, ds2) * el
        dstate[...] = ds * jnp.exp(bl) + _dot3(do2, qe, 0, 0)
        db = q * dq - k * dkk
        dg_ref[...] = _split_dot(_tri(c, "ge"), db) + dgc[...]
        dgc[...] += jnp.sum(db, axis=0, keepdims=True)
        dq_ref[...] = (dq * scale).astype(BF16)
        dk_ref[...] = dkk.astype(BF16)

    rev = lambda i: nc - 1 - i
    qspec = pl.BlockSpec((c, dk), lambda h, i: (rev(i), h))
    vspec = pl.BlockSpec((c, dv), lambda h, i: (rev(i), h))
    return _hosted_call(
        body, host, name=name, grid=(GLA_HEADS, nc),
        in_specs=[qspec, pl.BlockSpec((c, dk), lambda h, i: (rev(i), kq + h)),
                  pl.BlockSpec((c, dv), lambda h, i: (rev(i), kq + h)), qspec, vspec,
                  pl.BlockSpec((None, None, dv, dk), lambda h, i: (h, rev(i), 0, 0))],
        out_specs=[qspec, qspec, vspec, qspec],
        out_shape=[SDS((s, qk), BF16), SDS((s, qk), BF16), SDS((s, 2 * qk), BF16), SDS((s, qk), F32)],
        scratch_shapes=[pltpu.VMEM((dv, dk), F32), pltpu.VMEM((1, dk), F32)], sem=("parallel", "arbitrary"),
        args=(proj, proj, proj, g, do, states))


def gnorm_fwd(o, proj, gw, *, name, tr=512):
    s, v = o.shape
    dv = v // GLA_HEADS
    roff = 2 * GLA_HEADS

    def body(o_ref, r_ref, w_ref, y_ref):
        of = o_ref[...]
        rs = lax.rsqrt(jnp.mean(of * of, axis=-1, keepdims=True) + EPS)
        r = r_ref[...]
        y_ref[...] = (of * rs * w_ref[...] * (r * _sigmoid(r))).astype(BF16)

    blk = pl.BlockSpec((tr, dv), lambda i, h: (i, h))
    return pl.pallas_call(
        body, name=name, grid=(s // tr, GLA_HEADS),
        in_specs=[blk, pl.BlockSpec((tr, dv), lambda i, h: (i, roff + h)), pl.BlockSpec((1, dv), lambda i, h: (0, 0))],
        out_specs=blk, out_shape=SDS((s, v), BF16), compiler_params=_cp("parallel", "parallel"))(o, proj, gw)


def gnorm_bwd(dy, o, proj, gw, *, name, tr=512):
    s, v = o.shape
    dv = v // GLA_HEADS
    roff = 2 * GLA_HEADS

    def body(dy_ref, o_ref, r_ref, w_ref, do_ref, dr_ref, dw_ref):
        first = jnp.logical_and(pl.program_id(0) == 0, pl.program_id(1) == 0)
        of = o_ref[...]
        rs = lax.rsqrt(jnp.mean(of * of, axis=-1, keepdims=True) + EPS)
        n = of * rs
        r = r_ref[...]
        sg = _sigmoid(r)
        dyf = dy_ref[...].astype(F32)
        dn_w = dyf * (r * sg)
        dr_ref[...] = (dyf * n * w_ref[...] * sg * (1.0 + r * (1.0 - sg))).astype(BF16)
        dn = dn_w * w_ref[...]
        do_ref[...] = rs * (dn - n * jnp.mean(dn * n, axis=-1, keepdims=True))
        part = jnp.sum(dn_w * n, axis=0, keepdims=True)

        @pl.when(first)
        def _():
            dw_ref[...] = part

        @pl.when(jnp.logical_not(first))
        def _():
            dw_ref[...] += part

    blk = pl.BlockSpec((tr, dv), lambda i, h: (i, h))
    vec = pl.BlockSpec((1, dv), lambda i, h: (0, 0))
    return pl.pallas_call(
        body, name=name, grid=(s // tr, GLA_HEADS),
        in_specs=[blk, blk, pl.BlockSpec((tr, dv), lambda i, h: (i, roff + h)), vec],
        out_specs=[blk, blk, vec], out_shape=[SDS((s, v), F32), SDS((s, v), BF16), SDS((1, dv), F32)],
        compiler_params=_cp("arbitrary", "arbitrary"))(dy, o, proj, gw)


def _sb_block(kblk, q, ks, q0, scale, carry, masked):
    tk, tq = kblk.shape[0], q.shape[0]
    z = _dot(kblk, q, 1, 1) * scale
    sp = jnp.maximum(z, 0.0) + jnp.log(1.0 + jnp.exp(-jnp.abs(z)))
    mask = None
    lf = -sp
    if masked:
        kpos = ks + lax.broadcasted_iota(jnp.int32, (tk, tq), 0)
        qpos = q0 + lax.broadcasted_iota(jnp.int32, (tk, tq), 1)
        mask = kpos < qpos
        lf = jnp.where(mask, lf, 0.0)
    later = _split_dot(_tri(tk, "gt"), lf)
    a = jnp.exp(z - sp + later + carry)
    if masked:
        a = jnp.where(mask, a, 0.0)
    return z, sp, mask, lf, a


def sb_fwd(q, kv, *, name, host=None, tq=SB_TQ, tk=SB_TK, group=SB_GROUP):
    s, w = q.shape
    hd = w // SB_HEADS
    nq, nkb = s // tq, s // tk
    scale = hd ** -0.5
    per = tq // tk
    assert per % group == 0

    def body(q_ref, k_ref, v_ref, o_ref, car_ref, o_acc):
        qi = pl.program_id(1)
        qb = q_ref[...]
        q0 = qi * tq
        car_ref[...] = jnp.full(car_ref.shape, SB_SKIPPED, F32)
        o_acc[...] = jnp.zeros_like(o_acc)

        def blocks(first, carry, skip=None):
            for t in reversed(range(group)):
                kj = first + t
                ks = pl.multiple_of(kj * tk, tk)
                car_ref[pl.ds(kj, 1), :] = carry
                lo = 0 if skip is None else (skip + t) * tk
                _, _, _, lf, a = _sb_block(k_ref[pl.ds(ks, tk), :], qb[lo:, :], ks, q0 + lo, scale, carry[:, lo:], skip is not None)
                o_acc[lo:, :] += _dot(a.astype(BF16), v_ref[pl.ds(ks, tk), :], 0, 0)
                add = jnp.sum(lf, axis=0, keepdims=True)
                carry = carry + (add if lo == 0 else jnp.concatenate([jnp.zeros((1, lo), F32), add], axis=1))
            return carry

        carry = jnp.zeros((1, tq), F32)
        for gidx in reversed(range(per // group)):
            carry = blocks(qi * per + gidx * group, carry, gidx * group)
        n_in = qi * (per // group)
        lax.while_loop(lambda st: jnp.logical_and(st[0] < n_in, jnp.max(st[1]) > SB_DEAD),
                       lambda st: (st[0] + 1, blocks((n_in - 1 - st[0]) * group, st[1])), (jnp.int32(0), carry))
        o_ref[...] = o_acc[...].astype(BF16)

    qspec = pl.BlockSpec((tq, hd), lambda h, i: (i, h))
    return _hosted_call(
        body, host, name=name, grid=(SB_HEADS, nq),
        in_specs=[qspec, pl.BlockSpec((None, s, hd), lambda h, i: (0, 0, h)), pl.BlockSpec((None, s, hd), lambda h, i: (1, 0, h))],
        out_specs=[qspec, pl.BlockSpec((None, None, nkb, tq), lambda h, i: (h, i, 0, 0))],
        out_shape=[SDS((s, w), BF16), SDS((SB_HEADS, nq, nkb, tq), F32)],
        scratch_shapes=[pltpu.VMEM((tq, hd), F32)], sem=("parallel", "parallel"), args=(q, kv, kv))


def sb_bwd(q, kv, do, car, *, name, host=None, tq=SB_TQ, tk=SB_TK, group=SB_GROUP):
    s, w = q.shape
    hd = w // SB_HEADS
    nq, nkb = s // tq, s // tk
    scale = hd ** -0.5
    per = tq // tk
    assert per % group == 0

    def body(q_ref, k_ref, v_ref, do_ref, car_ref, dq_ref, dkv_ref, dq_acc, dk_acc, dv_acc):
        qi = pl.program_id(1)
        qb = q_ref[...]
        dob = do_ref[...]
        q0 = qi * tq
        dq_acc[...] = jnp.zeros_like(dq_acc)

        @pl.when(qi == 0)
        def _():
            dk_acc[...] = jnp.zeros_like(dk_acc)
            dv_acc[...] = jnp.zeros_like(dv_acc)

        def blocks(first, pcar, skip=None):
            for t in range(group):
                kj = first + t
                ks = pl.multiple_of(kj * tk, tk)
                kblk = k_ref[pl.ds(ks, tk), :]
                lo = 0 if skip is None else (skip + t) * tk
                masked = skip is not None
                qs, dos = qb[lo:, :], dob[lo:, :]
                z, sp, mask, _, a = _sb_block(kblk, qs, ks, q0 + lo, scale, car_ref[pl.ds(kj, 1), :][:, lo:], masked)
                p = a * _dot(v_ref[pl.ds(ks, tk), :], dos, 1, 1)
                before = _split_dot(_tri(tk, "lt"), p)
                sg = jnp.exp(z - sp)
                dz = p * (1.0 - sg) - (pcar[:, lo:] + before) * sg
                if masked:
                    dz = jnp.where(mask, dz, 0.0)
                dz = (dz * scale).astype(BF16)
                dk_acc[pl.ds(ks, tk), :] += _dot(dz, qs)
                dv_acc[pl.ds(ks, tk), :] += _dot(a.astype(BF16), dos)
                dq_acc[lo:, :] += _dot(dz, kblk, 0, 0)
                add = jnp.sum(p, axis=0, keepdims=True)
                pcar = pcar + (add if lo == 0 else jnp.concatenate([jnp.zeros((1, lo), F32), add], axis=1))
            return pcar

        n_in = qi * (per // group)

        def reached(g):
            return (jnp.max(car_ref[pl.ds(g * group + group - 1, 1), :]) > SB_DEAD).astype(jnp.int32)

        start = n_in - lax.fori_loop(0, n_in, lambda g, n: n + reached(g), jnp.int32(0))
        pcar = lax.fori_loop(start, n_in, lambda i, c: blocks(i * group, c), jnp.zeros((1, tq), F32))
        for gidx in range(per // group):
            pcar = blocks(qi * per + gidx * group, pcar, gidx * group)
        dq_ref[...] = dq_acc[...].astype(BF16)

        @pl.when(qi == nq - 1)
        def _():
            dkv_ref[0] = dk_acc[...].astype(BF16)
            dkv_ref[1] = dv_acc[...].astype(BF16)

    qspec = pl.BlockSpec((tq, hd), lambda h, i: (i, h))
    return _hosted_call(
        body, host, name=name, grid=(SB_HEADS, nq),
        in_specs=[qspec, pl.BlockSpec((None, s, hd), lambda h, i: (0, 0, h)), pl.BlockSpec((None, s, hd), lambda h, i: (1, 0, h)),
                  qspec, pl.BlockSpec((None, None, nkb, tq), lambda h, i: (h, i, 0, 0))],
        out_specs=[qspec, pl.BlockSpec((2, s, hd), lambda h, i: (0, 0, h))],
        out_shape=[SDS((s, w), BF16), SDS((2, s, w), BF16)],
        scratch_shapes=[pltpu.VMEM((tq, hd), F32), pltpu.VMEM((s, hd), F32), pltpu.VMEM((s, hd), F32)],
        sem=("parallel", "arbitrary"), args=(q, kv, kv, do, car))


def adamw(w, g, m, v, *, name):
    shape = w.shape
    c = shape[-1]
    r = w.size // c
    tr = _tile(r, max(8, (3 * LANES * 1024) // c), unit=8) if r >= 8 else r

    def body(w_ref, g_ref, m_ref, v_ref, d_ref, nm_ref, nv_ref):
        gf = g_ref[...]
        mn = ADAM_B1 * m_ref[...] + (1.0 - ADAM_B1) * gf
        vn = ADAM_B2 * v_ref[...] + (1.0 - ADAM_B2) * (gf * gf)
        m_hat = mn / (1.0 - ADAM_B1 ** ADAM_STEP)
        v_hat = vn / (1.0 - ADAM_B2 ** ADAM_STEP)
        d_ref[...] = -ADAM_LR * (m_hat / (jnp.sqrt(v_hat) + ADAM_EPS) + ADAM_WD * w_ref[...])
        nm_ref[...] = mn
        nv_ref[...] = vn

    blk = pl.BlockSpec((tr, c), lambda i: (i, 0))
    outs = pl.pallas_call(
        body, name=name, grid=(r // tr,), in_specs=[blk] * 4, out_specs=[blk] * 3,
        out_shape=[SDS((r, c), F32)] * 3, compiler_params=_cp("parallel"))(
            *(t.reshape(r, c) for t in (w, g, m, v)))
    return tuple(o.reshape(shape) for o in outs)


def _row_tile(r, c):
    return _tile(r, max(16, (4 * LANES * 1024) // c), unit=16)


def _halves(a):
    return a.reshape(a.shape[:-2] + (2, a.shape[-2] // 2, a.shape[-1]))


def place_shard(w, chip, *, name, layer=None):
    r, c = w.shape[-2:]
    tr = _row_tile(r, c)

    def body(c_ref, w_ref, o_ref):
        o_ref[...] = w_ref[...].astype(BF16)

    if layer is None:
        w_spec = pl.BlockSpec((None, tr, c), lambda h, i, c_ref: (h, i, 0))
    else:
        w_spec = pl.BlockSpec((None, None, tr, c), lambda h, i, c_ref: (layer, h, i, 0))
    return pl.pallas_call(
        body, name=name, out_shape=SDS((N_CHIPS, 2, r, c), BF16),
        grid_spec=pltpu.PrefetchScalarGridSpec(
            num_scalar_prefetch=1, grid=(2, r // tr), in_specs=[w_spec],
            out_specs=pl.BlockSpec((None, None, tr, c), lambda h, i, c_ref: (c_ref[0], h, i, 0))),
        compiler_params=_cp("parallel", "parallel"))(chip, w)


def sibling_exchange(gs, *, name):
    n = len(gs)

    def body(*refs):
        g_refs, a_refs = refs[:n], refs[n:2 * n]
        send_sems, recv_sems = refs[2 * n:]
        x, y, c = _coords()
        cps = [_remote(g_refs[i].at[:, 1 - c], a_refs[i], send_sems.at[i], recv_sems.at[i], (x, y, 1 - c)) for i in range(n)]
        for cp in cps:
            cp.start()
        for cp in cps:
            cp.wait()

    return pl.pallas_call(
        body, name=name, in_specs=[ANY] * n, out_specs=[ANY] * n,
        out_shape=[SDS(g.shape[:1] + g.shape[2:], g.dtype) for g in gs],
        scratch_shapes=[_dma_sems(n), _dma_sems(n)])(*gs)


def half_add(g, a, core, *, name):
    n, _, r, c = g.shape
    tr = _row_tile(r, c)

    def body(c_ref, g_ref, a_ref, o_ref):
        o_ref[...] = (g_ref[...].astype(F32) + a_ref[...].astype(F32)).astype(o_ref.dtype)

    blk = pl.BlockSpec((None, tr, c), lambda s, i, c_ref: (s, i, 0))
    return pl.pallas_call(
        body, name=name, out_shape=SDS((n, r, c), g.dtype),
        grid_spec=pltpu.PrefetchScalarGridSpec(
            num_scalar_prefetch=1, grid=(n, r // tr),
            in_specs=[pl.BlockSpec((None, None, tr, c), lambda s, i, c_ref: (s, c_ref[0], i, 0)), blk], out_specs=blk),
        compiler_params=_cp("parallel", "parallel"))(core, g, a)


def chip_sum(p, b, chip, core, *, name, layer=None, into=None):
    _, r, c = p.shape
    tr = _row_tile(r, c)

    def body(chip_ref, core_ref, p_ref, b_ref, *refs):
        t = p_ref[...].astype(F32)
        for k in range(N_CHIPS - 1):
            t = t + b_ref[k].astype(F32)
        refs[-1][...] = t

    in_specs = [pl.BlockSpec((None, tr, c), lambda i, chip_ref, core_ref: (chip_ref[0], i, 0)),
                pl.BlockSpec((N_CHIPS - 1, tr, c), lambda i, chip_ref, core_ref: (0, i, 0))]
    if layer is None:
        shape, o_spec = (2, r, c), pl.BlockSpec((None, tr, c), lambda i, chip_ref, core_ref: (core_ref[0], i, 0))
    else:
        shape = (2, 2, r, c)
        o_spec = pl.BlockSpec((None, None, tr, c), lambda i, chip_ref, core_ref: (layer, core_ref[0], i, 0))
    args, aliases = (chip, core, p, b), {}
    if into is not None:
        in_specs, args, aliases = in_specs + [ANY], args + (into,), {4: 0}
    return pl.pallas_call(
        body, name=name, out_shape=SDS(shape, F32),
        grid_spec=pltpu.PrefetchScalarGridSpec(num_scalar_prefetch=2, grid=(r // tr,), in_specs=in_specs, out_specs=o_spec),
        input_output_aliases=aliases, compiler_params=_cp("parallel"))(*args)


def sibling_gather(ts, *, name):
    n = len(ts)

    def body(*refs):
        out_refs = refs[n:2 * n]
        send_sems, recv_sems = refs[2 * n:]
        x, y, c = _coords()

        def half(i, which):
            return out_refs[i].at[which] if len(ts[i].shape) == 3 else out_refs[i].at[:, which]

        cps = [_remote(half(i, c), half(i, c), send_sems.at[i], recv_sems.at[i], (x, y, 1 - c)) for i in range(n)]
        for cp in cps:
            cp.start()
        for i in range(n):
            _remote(half(i, c), half(i, 1 - c), send_sems.at[i], recv_sems.at[i], (x, y, 1 - c)).wait_recv()
        for cp in cps:
            cp.wait_send()

    return pl.pallas_call(
        body, name=name, in_specs=[ANY] * n, out_specs=[ANY] * n, out_shape=[SDS(t.shape, t.dtype) for t in ts],
        input_output_aliases={i: i for i in range(n)}, scratch_shapes=[_dma_sems(n), _dma_sems(n)])(*ts)


def allgather_all(sm, *, name):
    r, w = sm.shape

    def body(s_ref, out_ref, send_sems, recv_sems, local_sem):
        x, y, c = _coords()
        me = 4 * x + 2 * y + c

        def peer(rel):
            flip = lambda v, bit: 1 - v if bit else v
            return flip(x, rel & 4), flip(y, rel & 2), flip(c, rel & 1)

        loc = pltpu.make_async_copy(s_ref, out_ref.at[me], local_sem.at[0])
        loc.start()
        cps = [_remote(s_ref, out_ref.at[me], send_sems.at[rel - 1], recv_sems.at[rel - 1], peer(rel)) for rel in range(1, N_DEV)]
        for cp in cps:
            cp.start()
        for rel in range(1, N_DEV):
            px, py, pc = peer(rel)
            _remote(s_ref, out_ref.at[4 * px + 2 * py + pc], send_sems.at[rel - 1], recv_sems.at[rel - 1], (px, py, pc)).wait_recv()
        for cp in cps:
            cp.wait_send()
        loc.wait()

    return pl.pallas_call(
        body, name=name, in_specs=[ANY], out_specs=ANY, out_shape=SDS((N_DEV, r, w), sm.dtype),
        scratch_shapes=[_dma_sems(N_DEV - 1), _dma_sems(N_DEV - 1), _dma_sems(1)])(sm)


def sum_blocks(a, *, name):
    n, r, w = a.shape

    def body(a_ref, o_ref):
        t = a_ref[0]
        for k in range(1, n):
            t = t + a_ref[k]
        o_ref[...] = t

    return pl.pallas_call(body, name=name, out_shape=SDS((r, w), F32))(a)


class Exchanges:
    def __init__(self, placed=None, chip=None, core=None):
        self.placed, self.chip, self.core = placed, chip, core
        self.mine = {}

    def gather(self, names):
        return None if self.placed is None else GatherX([self.placed[n] for n in names])

    def reduce(self, grads, call):
        if self.placed is None:
            return call(None)
        names = list(grads)
        from_sibling = sibling_exchange([grads[n] for n in names], name="grads_to_sibling_" + names[0])
        pairs = [half_add(grads[n], a, self.core, name=f"pair_sum_{n}") for n, a in zip(names, from_sibling)]
        result, from_chips = call(ScatterX(pairs))
        for n, t, b in zip(names, pairs, from_chips):
            base, layer = (n[:-1], int(n[-1])) if n[:-1] in ("w_gu", "w_dn") else (n, None)
            self.mine[base] = chip_sum(t, b, self.chip, self.core, name=f"chip_sum_{n}", layer=layer, into=self.mine.get(base))
        return result


def _usable(name, buf, d):
    if name == "w_in":
        cols = N_CHIPS * buf.shape[-1]
        full = buf.reshape(N_CHIPS, d, -1).transpose(1, 0, 2).reshape(d, cols)
        return jnp.pad(full, ((0, 0), (0, -cols % LANES)))
    if name in ("w_out", "w_q", "w_so"):
        return buf.reshape(-1, buf.shape[-1])
    return buf.reshape(N_CHIPS, -1, buf.shape[-1])


def _by_rows(t):
    return _halves(t.reshape(N_CHIPS, -1, t.shape[-1]))


def _ffn_fwd(h, nw, w_gu, w_dn, tag, host=None):
    f = rmsnorm_fwd(h, nw, name=f"{tag}_norm")
    gu, act, got = gate_up_swiglu(f, View(w_gu, "cols"), name=f"{tag}_gate_up", host=host)
    return matmul(act, View(w_dn, "rows"), name=f"{tag}_down", residual=h), (f, gu, act), got


def _ffn_bwd(dh, h, nw, w_gu, w_dn, saved, tag, ex, dn_name, carried):
    f, gu, act = saved
    dgu = View(matmul(dh, View(w_dn, "rows"), name=f"{tag}_dgu", tb=True, dswiglu=gu), "cols")
    dw_dn = matmul(act, dh, name=f"{tag}_dw_down", ta=True, out_dtype=BF16)

    def dw_gate_up(host):
        return matmul(f, dgu, name=f"{tag}_dw_gate_up", ta=True, out_dtype=BF16, out_chips=N_CHIPS, host=host)

    dw_gu = ex.reduce(carried, dw_gate_up) if carried else dw_gate_up(None)
    df = ex.reduce({dn_name: _by_rows(dw_dn)}, lambda host: matmul(dgu, View(w_gu, "cols"), name=f"{tag}_df", tb=True, host=host))
    dh_in, dnw = rmsnorm_bwd(df, h, nw, dh, name=f"{tag}_dnorm")
    return dh_in, dnw, dw_gu, dw_dn


def local_step(x, target, p, ex):
    d = x.shape[1]
    hosted = ex.placed is not None
    w = {} if hosted else {n: _usable(n, p[n], d) for n in ("w_in", "w_out", "w_kv", "w_q", "w_so", "w_gu0", "w_gu1", "w_dn0", "w_dn1")}

    def take(names, got):
        for n, buf in zip(names, got or []):
            w[n] = _usable(n, buf, d)

    def carry(call, names):
        host = ex.gather(names)
        if host is None:
            return call(None)
        res, got = call(host)
        take(names, got)
        return res

    if hosted:
        take(["w_in"], run_exchange(ex.gather(["w_in"]), name="gather_gla_in"))
    a0 = rmsnorm_fwd(x, p["an0"], name="l0_attn_norm")
    proj = carry(lambda host: matmul(a0, w["w_in"], name="gla_in", host=host), ["w_out", "w_dn0"])
    g = gate_fwd(proj, p["wg"], p["bg"], name="gla_gate")
    o, states = carry(lambda host: gla_fwd(proj, g, name="gla_scan", host=host), ["w_gu0"])
    og = gnorm_fwd(o, proj, p["gw"], name="gla_outnorm")
    h1 = matmul(og, w["w_out"], name="gla_out", residual=x)
    sb_names = ["w_kv", "w_q", "w_so"]
    h2, ffn0, got = _ffn_fwd(h1, p["fn0"], w["w_gu0"], w["w_dn0"], "ffn0", host=ex.gather(sb_names))
    take(sb_names, got)
    w_kv = View(w["w_kv"], "cols")
    kvn = rmsnorm_fwd(h2, p["kvn"], name="kv_norm")
    kv = matmul(kvn, w_kv, name="sb_kv", out_dtype=BF16, out_chips=2)
    a1 = rmsnorm_fwd(h2, p["an1"], name="l1_attn_norm")
    q2 = matmul(a1, w["w_q"], name="sb_q", out_dtype=BF16)
    o2, car = carry(lambda host: sb_fwd(q2, kv, name="sb_attn", host=host), ["w_gu1", "w_dn1"])
    h3 = matmul(o2, w["w_so"], name="sb_out", residual=h2)
    h4, ffn1, _ = _ffn_fwd(h3, p["fn1"], w["w_gu1"], w["w_dn1"], "ffn1")
    dh4, d_fin, loss_row = final_loss(h4, p["finn"], target, name="final_loss")

    dh3, d_fn1, dw_gu1, dw_dn1 = _ffn_bwd(dh4, h3, p["fn1"], w["w_gu1"], w["w_dn1"], ffn1, "ffn1", ex, "w_dn1", {})
    do2 = matmul(dh3, w["w_so"], name="sb_do", tb=True, out_dtype=BF16)
    dw_so = matmul(o2, dh3, name="sb_dw_out", ta=True, out_dtype=BF16)
    dq2, dkv = ex.reduce({"w_gu1": _halves(dw_gu1)}, lambda host: sb_bwd(q2, kv, do2, car, name="sb_attn_bwd", host=host))
    dkv = View(dkv, "cols")
    dw_q = matmul(a1, dq2, name="sb_dw_q", ta=True, out_dtype=BF16)
    da1 = matmul(dq2, w["w_q"], name="sb_da", tb=True)
    dh2, d_an1 = rmsnorm_bwd(da1, h2, p["an1"], dh3, name="l1_attn_dnorm")
    dw_kv = matmul(kvn, dkv, name="sb_dw_kv", ta=True, out_dtype=BF16, out_chips=N_CHIPS)
    dkvn = matmul(dkv, w_kv, name="sb_dkvn", tb=True)
    dh2, d_kvn = rmsnorm_bwd(dkvn, h2, p["kvn"], dh2, name="kv_dnorm")
    sb_grads = {"w_kv": _halves(dw_kv), "w_q": _by_rows(dw_q), "w_so": _by_rows(dw_so)}
    dh1, d_fn0, dw_gu0, dw_dn0 = _ffn_bwd(dh2, h1, p["fn0"], w["w_gu0"], w["w_dn0"], ffn0, "ffn0", ex, "w_dn0", sb_grads)
    dog = matmul(dh1, w["w_out"], name="gla_dog", tb=True, out_dtype=BF16)
    dw_out = matmul(og, dh1, name="gla_dw_out", ta=True, out_dtype=BF16)
    do, dr, d_gw = gnorm_bwd(dog, o, proj, p["gw"], name="gla_outnorm_bwd")
    dq, dk, dv, dg = ex.reduce({"w_gu0": _halves(dw_gu0)}, lambda host: gla_bwd(proj, g, states, do, name="gla_scan_bwd", host=host))
    dgl, d_wg, d_bg = gate_bwd(dg, proj, p["wg"], p["bg"], name="gla_gate_bwd")
    dproj = jnp.concatenate([dq, dk, dv, dr, dgl], axis=1)
    dw_in = matmul(a0, dproj, name="gla_dw_in", ta=True, out_dtype=BF16)
    in_w = p["in_w"]
    gla_grads = {"w_in": _halves(dw_in[:, :in_w].reshape(d, N_CHIPS, -1).transpose(1, 0, 2)), "w_out": _by_rows(dw_out)}
    da0 = ex.reduce(gla_grads, lambda host: matmul(dproj, w["w_in"], name="gla_da", tb=True, host=host))
    dx, d_an0 = rmsnorm_bwd(da0, x, p["an0"], dh1, name="l0_attn_dnorm")

    small = dict(an0=d_an0, an1=d_an1, fn0=d_fn0, fn1=d_fn1, kvn=d_kvn, finn=d_fin, wg=d_wg, bg=d_bg, gw=d_gw)
    big = {}
    if not hosted:
        big = dict(gla_grads, **sb_grads, w_gu0=_halves(dw_gu0), w_gu1=_halves(dw_gu1), w_dn0=_by_rows(dw_dn0), w_dn1=_by_rows(dw_dn1))
    return loss_row, dx, small, big


def _pack_rows(parts):
    cat = jnp.concatenate([t.reshape(-1) for t in parts])
    rows = -(-cat.size // (LANES * SUBLANES)) * SUBLANES
    return jnp.pad(cat, (0, rows * LANES - cat.size)).reshape(rows, LANES)


def _segments(flat, sizes):
    out, off = [], 0
    for n in sizes:
        out.append(flat[..., off:off + n])
        off += n
    return out


def kernel(x, attn_norm_w, ffn_norm_w, gla_w_in, gla_w_gate_up, gla_b_gate, gla_gnorm_w, gla_w_out, kv_norm_w, sb_w_kv, sb_w_q, sb_w_out, ffn_w_gate_up, ffn_w_down, final_norm_w, loss_target, m_attn_norm_w, m_ffn_norm_w, m_gla_w_in, m_gla_w_gate_up, m_gla_b_gate, m_gla_gnorm_w, m_gla_w_out, m_kv_norm_w, m_sb_w_kv, m_sb_w_q, m_sb_w_out, m_ffn_w_gate_up, m_ffn_w_down, m_final_norm_w, v_attn_norm_w, v_ffn_norm_w, v_gla_w_in, v_gla_w_gate_up, v_gla_b_gate, v_gla_gnorm_w, v_gla_w_out, v_kv_norm_w, v_sb_w_kv, v_sb_w_q, v_sb_w_out, v_ffn_w_gate_up, v_ffn_w_down, v_final_norm_w):
    xi, yi, ci = _coords()
    core = ci.astype(jnp.int32).reshape(1)
    chip1 = (2 * xi + yi).astype(jnp.int32)
    chip = chip1.reshape(1)
    rank = gla_w_gate_up.shape[1]

    gu4, dn4 = _halves(ffn_w_gate_up), _halves(ffn_w_down)
    shards = dict(w_in=(_halves(gla_w_in[0]), None), w_out=(_halves(gla_w_out[0]), None), w_kv=(_halves(sb_w_kv), None),
                  w_q=(_halves(sb_w_q[0]), None), w_so=(_halves(sb_w_out[0]), None),
                  w_gu0=(gu4, 0), w_gu1=(gu4, 1), w_dn0=(dn4, 0), w_dn1=(dn4, 1))
    placed = {n: place_shard(a, chip, name=f"place_{n}", layer=l) for n, (a, l) in shards.items()}
    small_w = [gla_w_gate_up, gla_b_gate, gla_gnorm_w]
    small_all = allgather_all(_pack_rows(small_w), name="gather_gate_weights")
    wg, bg, gw = _segments(small_all[::2].reshape(N_CHIPS, -1), [a.size for a in small_w])
    wg = wg.reshape(N_CHIPS, rank, -1).transpose(1, 0, 2).reshape(rank, -1)
    p = dict(an0=attn_norm_w[0], an1=attn_norm_w[1], fn0=ffn_norm_w[0], fn1=ffn_norm_w[1], kvn=kv_norm_w, finn=final_norm_w,
             wg=jnp.pad(wg, ((0, LANES - rank), (0, 0))).astype(BF16), bg=bg.reshape(1, -1), gw=gw.reshape(1, -1),
             in_w=N_CHIPS * gla_w_in.shape[-1])

    ex = Exchanges(placed, chip, core)
    loss_row, dx, g, _ = local_step(x[0], loss_target[0], p, ex)
    loss = lax.psum(loss_row[0, 0], ("x", "y", "c"))

    tags = list(ex.mine)
    tot = dict(zip(tags, sibling_gather([ex.mine[n] for n in tags], name="grads_from_sibling")))

    vecs = [jnp.concatenate([g["an0"], g["an1"]]), jnp.concatenate([g["fn0"], g["fn1"]]), g["kvn"], g["finn"],
            g["wg"][:rank], g["bg"], g["gw"]]
    gathered_vecs = allgather_all(_pack_rows(vecs), name="gather_small_grads")
    d_an, d_fn, d_kvn, d_fin, d_wg, d_bg, d_gw = _segments(
        sum_blocks(gathered_vecs, name="sum_small_grads").reshape(-1), [t.size for t in vecs])

    def shard(t, like):
        return lax.dynamic_index_in_dim(t.reshape(-1, N_CHIPS, like.shape[-1]), chip1, axis=1, keepdims=False).reshape(like.shape)

    weights = dict(
        attn_norm_w=(attn_norm_w, m_attn_norm_w, v_attn_norm_w), ffn_norm_w=(ffn_norm_w, m_ffn_norm_w, v_ffn_norm_w),
        gla_w_in=(gla_w_in, m_gla_w_in, v_gla_w_in), gla_w_gate_up=(gla_w_gate_up, m_gla_w_gate_up, v_gla_w_gate_up),
        gla_b_gate=(gla_b_gate, m_gla_b_gate, v_gla_b_gate), gla_gnorm_w=(gla_gnorm_w, m_gla_gnorm_w, v_gla_gnorm_w),
        gla_w_out=(gla_w_out, m_gla_w_out, v_gla_w_out), kv_norm_w=(kv_norm_w, m_kv_norm_w, v_kv_norm_w),
        sb_w_kv=(sb_w_kv, m_sb_w_kv, v_sb_w_kv), sb_w_q=(sb_w_q, m_sb_w_q, v_sb_w_q), sb_w_out=(sb_w_out, m_sb_w_out, v_sb_w_out),
        ffn_w_gate_up=(ffn_w_gate_up, m_ffn_w_gate_up, v_ffn_w_gate_up), ffn_w_down=(ffn_w_down, m_ffn_w_down, v_ffn_w_down),
        final_norm_w=(final_norm_w, m_final_norm_w, v_final_norm_w))
    grads = dict(
        attn_norm_w=d_an.reshape(attn_norm_w.shape), ffn_norm_w=d_fn.reshape(ffn_norm_w.shape),
        gla_w_in=tot["w_in"].reshape(gla_w_in.shape), gla_w_gate_up=shard(d_wg, gla_w_gate_up),
        gla_b_gate=shard(d_bg, gla_b_gate), gla_gnorm_w=shard(d_gw, gla_gnorm_w),
        gla_w_out=tot["w_out"].reshape(gla_w_out.shape), kv_norm_w=d_kvn.reshape(kv_norm_w.shape),
        sb_w_kv=tot["w_kv"].reshape(sb_w_kv.shape), sb_w_q=tot["w_q"].reshape(sb_w_q.shape),
        sb_w_out=tot["w_so"].reshape(sb_w_out.shape),
        ffn_w_gate_up=tot["w_gu"].reshape(ffn_w_gate_up.shape), ffn_w_down=tot["w_dn"].reshape(ffn_w_down.shape),
        final_norm_w=d_fin.reshape(final_norm_w.shape))
    names = list(weights)
    stepped = [adamw(weights[n][0], grads[n], weights[n][1], weights[n][2], name=f"adamw_{n}") for n in names]
    return (loss, dx.reshape(x.shape), *[grads[n] for n in names], *[t[0] for t in stepped], *[t[1] for t in stepped],
            *[t[2] for t in stepped])
```

```python
import functools

import jax
import jax.numpy as jnp
from jax import lax
from jax.experimental import pallas as pl
from jax.experimental.pallas import tpu as pltpu

F32 = jnp.float32
BF16 = jnp.bfloat16
SDS = jax.ShapeDtypeStruct
MESH = pl.DeviceIdType.MESH

EPS = 1e-6
GLA_HEADS = 4
GLA_GATE_RANK = 16
GLA_GATE_TAU = 16.0
GLA_CHUNK = 128
SB_HEADS = 16
SB_TQ = 512
SB_TK = 128
SB_GROUP = 2
SB_DEAD = -110.0
SB_SKIPPED = -1e30
ADAM_LR = 0.001
ADAM_B1 = 0.9
ADAM_B2 = 0.999
ADAM_EPS = 1e-08
ADAM_WD = 0.01
ADAM_STEP = 10

LANES = 128
SUBLANES = 8
N_CHIPS = 4
N_DEV = 8
VMEM_LIMIT = 56 * 1024 * 1024
SEQ_TILE = 4096


def _tile(dim, target, unit=LANES):
    if dim <= target:
        return dim
    t = (target // unit) * unit
    while t >= unit:
        if dim % t == 0:
            return t
        t -= unit
    raise ValueError(f"no tile for {dim}")


def _cp(*sem):
    return pltpu.CompilerParams(dimension_semantics=sem, vmem_limit_bytes=VMEM_LIMIT)


def _sigmoid(x):
    return 1.0 / (1.0 + jnp.exp(-x))


def _dot(a, b, ca=1, cb=0):
    return lax.dot_general(a, b, (((ca,), (cb,)), ((), ())), preferred_element_type=F32)


def _split_dot(tri, x):
    hi = x.astype(BF16)
    lo = (x - hi.astype(F32)).astype(BF16)
    return _dot(tri, hi) + _dot(tri, lo)


def _hilo(x):
    hi = x.astype(BF16)
    return hi, (x - hi.astype(F32)).astype(BF16)


def _dot3(a, b, ca=1, cb=0):
    return _dot(a[0], b[0], ca, cb) + _dot(a[0], b[1], ca, cb) + _dot(a[1], b[0], ca, cb)


def _tri(n, kind):
    r = lax.broadcasted_iota(jnp.int32, (n, n), 0)
    c = lax.broadcasted_iota(jnp.int32, (n, n), 1)
    m = {"le": c <= r, "ge": c >= r, "lt": c < r, "gt": c > r}[kind]
    return jnp.where(m, 1.0, 0.0).astype(BF16)


ANY = pl.BlockSpec(memory_space=pl.ANY)


def _coords():
    return lax.axis_index("x"), lax.axis_index("y"), lax.axis_index("c")


def _other_chips(x, y):
    return [(1 - x, y), (x, 1 - y), (1 - x, 1 - y)]


def _remote(src, dst, send_sem, recv_sem, dev):
    return pltpu.make_async_remote_copy(src_ref=src, dst_ref=dst, send_sem=send_sem, recv_sem=recv_sem,
                                        device_id=dev, device_id_type=MESH)


def _dma_sems(n):
    return pltpu.SemaphoreType.DMA((n,))


class GatherX:
    def __init__(self, bufs):
        self.ins, self.ios, self.outs, self.n_sems = [], list(bufs), [], 6 * len(bufs)

    def _copy(self, ios, send_sems, recv_sems, i, k, chip, half, dev):
        blk = ios[i].at[chip, half]
        return _remote(blk, blk, send_sems.at[6 * i + k], recv_sems.at[6 * i + k], dev)

    def _first(self, ios, send_sems, recv_sems):
        x, y, c = _coords()
        return [self._copy(ios, send_sems, recv_sems, i, k, 2 * x + y, c, (cx, cy, c))
                for i in range(len(ios)) for k, (cx, cy) in enumerate(_other_chips(x, y))]

    def start(self, ins, ios, outs, send_sems, recv_sems):
        for cp in self._first(ios, send_sems, recv_sems):
            cp.start()

    def finish(self, ins, ios, outs, send_sems, recv_sems):
        x, y, c = _coords()
        chips = _other_chips(x, y)
        copy = functools.partial(self._copy, ios, send_sems, recv_sems)
        passed = []
        for i in range(len(ios)):
            for k, (cx, cy) in enumerate(chips):
                copy(i, k, 2 * cx + cy, c, (x, y, c)).wait_recv()
                passed.append(copy(i, 3 + k, 2 * cx + cy, c, (x, y, 1 - c)))
                passed[-1].start()
        for i in range(len(ios)):
            for k, (cx, cy) in enumerate(chips):
                copy(i, 3 + k, 2 * cx + cy, 1 - c, (x, y, c)).wait_recv()
        for cp in self._first(ios, send_sems, recv_sems) + passed:
            cp.wait_send()


class ScatterX:
    def __init__(self, ps):
        self.ins, self.ios, self.n_sems = list(ps), [], 3 * len(ps)
        self.outs = [SDS((N_CHIPS - 1,) + p.shape[1:], p.dtype) for p in ps]

    def _copies(self, ins, outs, send_sems, recv_sems):
        x, y, c = _coords()
        return [_remote(ins[i].at[2 * cx + cy], outs[i].at[k], send_sems.at[3 * i + k], recv_sems.at[3 * i + k], (cx, cy, c))
                for i in range(len(ins)) for k, (cx, cy) in enumerate(_other_chips(x, y))]

    def start(self, ins, ios, outs, send_sems, recv_sems):
        for cp in self._copies(ins, outs, send_sems, recv_sems):
            cp.start()

    def finish(self, ins, ios, outs, send_sems, recv_sems):
        for cp in self._copies(ins, outs, send_sems, recv_sems):
            cp.wait()


def _exchange_operands(host):
    x_in = host.ins + host.ios
    x_out = [SDS(a.shape, a.dtype) for a in host.ios] + host.outs
    return x_in, x_out


def run_exchange(host, *, name):
    x_in, x_out = _exchange_operands(host)
    n_ins, n_ios = len(host.ins), len(host.ios)

    def body(*refs):
        xin, xout = refs[:len(x_in)], refs[len(x_in):len(x_in) + len(x_out)]
        send_sems, recv_sems = refs[len(x_in) + len(x_out):]
        x_refs = (xin[:n_ins], xout[:n_ios], xout[n_ios:])
        host.start(*x_refs, send_sems, recv_sems)
        host.finish(*x_refs, send_sems, recv_sems)

    return list(pl.pallas_call(
        body, name=name, in_specs=[ANY] * len(x_in), out_specs=[ANY] * len(x_out), out_shape=x_out,
        input_output_aliases={n_ins + i: i for i in range(n_ios)},
        scratch_shapes=[_dma_sems(host.n_sems), _dma_sems(host.n_sems)])(*x_in))


def _hosted_call(body, host, *, name, grid, in_specs, out_specs, out_shape, scratch_shapes, sem, args):
    if host is None:
        return list(pl.pallas_call(body, name=name, grid=grid, in_specs=in_specs, out_specs=out_specs, out_shape=out_shape,
                                   scratch_shapes=scratch_shapes, compiler_params=_cp(*sem))(*args))
    x_in, x_out = _exchange_operands(host)
    n_in, n_out, n_scr, n_ins, n_ios = len(in_specs), len(out_specs), len(scratch_shapes), len(host.ins), len(host.ios)

    def hosted(*refs):
        ins, xin = refs[:n_in], refs[n_in:n_in + len(x_in)]
        o0 = n_in + len(x_in)
        outs, xout = refs[o0:o0 + n_out], refs[o0 + n_out:o0 + n_out + len(x_out)]
        s0 = o0 + n_out + len(x_out)
        scr, (send_sems, recv_sems) = refs[s0:s0 + n_scr], refs[s0 + n_scr:]
        ids = [pl.program_id(ax) for ax in range(len(grid))]
        first = functools.reduce(jnp.logical_and, [i == 0 for i in ids])
        last = functools.reduce(jnp.logical_and, [i == n - 1 for i, n in zip(ids, grid)])
        x_refs = (xin[:n_ins], xout[:n_ios], xout[n_ios:])

        @pl.when(first)
        def _():
            host.start(*x_refs, send_sems, recv_sems)

        body(*ins, *outs, *scr)

        @pl.when(last)
        def _():
            host.finish(*x_refs, send_sems, recv_sems)

    res = pl.pallas_call(
        hosted, name=name, grid=grid, in_specs=list(in_specs) + [ANY] * len(x_in), out_specs=list(out_specs) + [ANY] * len(x_out),
        out_shape=list(out_shape) + x_out, scratch_shapes=list(scratch_shapes) + [_dma_sems(host.n_sems), _dma_sems(host.n_sems)],
        input_output_aliases={n_in + n_ins + i: n_out + i for i in range(n_ios)},
        compiler_params=_cp(*(("arbitrary",) * len(grid))))(*args, *x_in)
    return list(res[:n_out]), list(res[n_out:])


def _div(i, n):
    return i if n == 1 else lax.div(i, n)


def _rem(i, n):
    return 0 if n == 1 else lax.rem(i, n)


class View:
    def __init__(self, arr, kind="plain", lead=(), g0=0, ng=None):
        self.arr, self.kind, self.lead, self.g0 = arr, kind, tuple(lead), g0
        self.ng = (arr.shape[0] - g0) if ng is None else ng
        r, c = arr.shape[-2:]
        self.runit, self.cunit = r, c
        self.shape = {"plain": (r, c), "cols": (r, self.ng * c), "rows": (self.ng * r, c)}[kind]

    def spec(self, br, bc, rfn, cfn):
        if self.kind == "plain":
            return pl.BlockSpec((br, bc), lambda *g: (rfn(*g), cfn(*g)))
        none = (None,) * (1 + len(self.lead))
        if self.kind == "cols":
            per = self.cunit // bc
            return pl.BlockSpec(none + (br, bc), lambda *g: (self.g0 + _div(cfn(*g), per), *self.lead, rfn(*g), _rem(cfn(*g), per)))
        per = self.runit // br
        return pl.BlockSpec(none + (br, bc), lambda *g: (self.g0 + _div(rfn(*g), per), *self.lead, _rem(rfn(*g), per), cfn(*g)))


def _as_view(a):
    return a if isinstance(a, View) else View(a)


def matmul(a, b, *, name, ta=False, tb=False, out_dtype=F32, residual=None, out_chips=None, host=None, dswiglu=None,
           tm=1408, tn=1408, tk=2816):
    a, b = _as_view(a), _as_view(b)
    (k, m) = a.shape if ta else a.shape[::-1]
    (n, kb) = b.shape if tb else b.shape[::-1]
    assert k == kb, (a.shape, b.shape, ta, tb)
    m_unit = a.cunit if ta else a.runit
    ka_unit = a.runit if ta else a.cunit
    n_unit = b.runit if tb else b.cunit
    kb_unit = b.cunit if tb else b.runit
    if out_chips is not None:
        n_unit = min(n_unit, n // out_chips)
    tm, tn = _tile(min(m, m_unit), tm), _tile(min(n, n_unit), tn)
    tk = _tile(min(k, ka_unit, kb_unit), tk)
    assert ka_unit % tk == 0 and kb_unit % tk == 0, (ka_unit, kb_unit, tk)
    nk = k // tk
    ca, cb = (0 if ta else 1), (1 if tb else 0)

    def body(a_ref, b_ref, *refs):
        r_ref = refs[0] if residual is not None else None
        o_ref = refs[-1] if nk == 1 else refs[-2]

        def finish(r):
            if residual is not None:
                r = r + r_ref[...]
            if dswiglu is None:
                o_ref[...] = r.astype(out_dtype)
                return
            g = refs[0][...].astype(F32)
            sg = _sigmoid(g)
            o_ref[0] = (r * refs[1][...].astype(F32) * sg * (1.0 + g * (1.0 - sg))).astype(BF16)
            o_ref[1] = (r * g * sg).astype(BF16)

        part = _dot(a_ref[...].astype(BF16), b_ref[...].astype(BF16), ca, cb)
        if nk == 1:
            finish(part)
            return
        acc = refs[-1]
        kk = pl.program_id(2)

        @pl.when(kk == 0)
        def _():
            acc[...] = part

        @pl.when(kk > 0)
        def _():
            acc[...] += part

        @pl.when(kk == nk - 1)
        def _():
            finish(acc[...])

    gi, gj, gk = (lambda i, j, kk: i), (lambda i, j, kk: j), (lambda i, j, kk: kk)
    a_spec = a.spec(tk, tm, gk, gi) if ta else a.spec(tm, tk, gi, gk)
    b_spec = b.spec(tn, tk, gj, gk) if tb else b.spec(tk, tn, gk, gj)
    if out_chips is None:
        out = View(SDS((m, n), out_dtype))
    else:
        out = View(SDS((out_chips, m, n // out_chips), out_dtype), "cols")
    o_spec = out.spec(tm, tn, gi, gj)
    in_specs, args = [a_spec, b_spec], [a.arr, b.arr]
    if residual is not None:
        in_specs.append(pl.BlockSpec((tm, tn), lambda i, j, kk: (i, j)))
        args.append(residual)
    if dswiglu is not None:
        assert residual is None and out_chips is None and dswiglu.shape == (2, m, n)
        in_specs += [pl.BlockSpec((None, tm, tn), lambda i, j, kk: (0, i, j)), pl.BlockSpec((None, tm, tn), lambda i, j, kk: (1, i, j))]
        args += [dswiglu, dswiglu]
        out = View(SDS((2, m, n), BF16))
        o_spec = pl.BlockSpec((2, tm, tn), lambda i, j, kk: (0, i, j))
    res = _hosted_call(
        body, host, name=name, grid=(m // tm, n // tn, nk), in_specs=in_specs, out_specs=[o_spec], out_shape=[out.arr],
        scratch_shapes=[] if nk == 1 else [pltpu.VMEM((tm, tn), F32)], sem=("parallel", "parallel", "arbitrary"), args=args)
    return res[0] if host is None else (res[0][0], res[1])


def gate_up_swiglu(f, w_gu, *, name, host=None, tm=512, tn=1408):
    m, k = f.shape
    n = w_gu.shape[1] // 2
    assert w_gu.shape[0] == k and w_gu.runit == k
    tm, tn = _tile(m, tm), _tile(min(n, w_gu.cunit), tn)
    nf = n // tn

    def body(a_ref, bg_ref, bu_ref, gu_ref, act_ref):
        a = a_ref[...]
        pg, pu = _dot(a, bg_ref[...]), _dot(a, bu_ref[...])
        gu_ref[0] = pg.astype(BF16)
        gu_ref[1] = pu.astype(BF16)
        act_ref[...] = (pg * _sigmoid(pg) * pu).astype(BF16)

    rows, cols, zero = (lambda j, i: i), (lambda j, i: j), (lambda j, i: 0)
    res = _hosted_call(
        body, host, name=name, grid=(nf, m // tm),
        in_specs=[pl.BlockSpec((tm, k), lambda j, i: (i, 0)), w_gu.spec(k, tn, zero, cols), w_gu.spec(k, tn, zero, lambda j, i: j + nf)],
        out_specs=[pl.BlockSpec((2, tm, tn), lambda j, i: (0, i, j)), pl.BlockSpec((tm, tn), lambda j, i: (i, j))],
        out_shape=[SDS((2, m, n), BF16), SDS((m, n), BF16)], scratch_shapes=[], sem=("parallel", "parallel"),
        args=(f, w_gu.arr, w_gu.arr))
    return (res[0], res[1], None) if host is None else (res[0][0], res[0][1], res[1])


def rmsnorm_fwd(x, w, *, name, tr=512):
    s, d = x.shape

    def body(x_ref, w_ref, o_ref):
        xf = x_ref[...]
        r = lax.rsqrt(jnp.mean(xf * xf, axis=-1, keepdims=True) + EPS)
        o_ref[...] = (xf * r * w_ref[...]).astype(BF16)

    row = pl.BlockSpec((tr, d), lambda i: (i, 0))
    return pl.pallas_call(
        body, name=name, grid=(s // tr,), in_specs=[row, pl.BlockSpec((1, d), lambda i: (0, 0))], out_specs=row,
        out_shape=SDS((s, d), BF16), compiler_params=_cp("parallel"))(x, w.reshape(1, d))


def rmsnorm_bwd(dy, x, w, dres, *, name, tr=512):
    s, d = x.shape

    def body(dy_ref, x_ref, w_ref, dres_ref, dx_ref, dw_ref):
        i = pl.program_id(0)
        xf = x_ref[...]
        r = lax.rsqrt(jnp.mean(xf * xf, axis=-1, keepdims=True) + EPS)
        xh = xf * r
        dyf = dy_ref[...].astype(F32)
        dxh = dyf * w_ref[...]
        dx_ref[...] = dres_ref[...] + r * (dxh - xh * jnp.mean(dxh * xh, axis=-1, keepdims=True))
        part = jnp.sum(dyf * xh, axis=0, keepdims=True)

        @pl.when(i == 0)
        def _():
            dw_ref[...] = part

        @pl.when(i > 0)
        def _():
            dw_ref[...] += part

    row = pl.BlockSpec((tr, d), lambda i: (i, 0))
    vec = pl.BlockSpec((1, d), lambda i: (0, 0))
    return pl.pallas_call(
        body, name=name, grid=(s // tr,), in_specs=[row, row, vec, row], out_specs=[row, vec],
        out_shape=[SDS((s, d), F32), SDS((1, d), F32)], compiler_params=_cp("arbitrary"))(dy, x, w.reshape(1, d), dres)


def final_loss(h, w, target, *, name, tr=512):
    s, d = h.shape

    def body(h_ref, w_ref, t_ref, dh_ref, dw_ref, loss_ref):
        i = pl.program_id(0)
        xf = h_ref[...]
        r = lax.rsqrt(jnp.mean(xf * xf, axis=-1, keepdims=True) + EPS)
        xh = xf * r
        err = xh * w_ref[...] - t_ref[...]
        lpart = 0.5 * jnp.sum(jnp.sum(err * err, axis=-1, keepdims=True) * (1.0 / d), axis=0, keepdims=True)
        dy = err * (1.0 / d)
        dxh = dy * w_ref[...]
        dh_ref[...] = r * (dxh - xh * jnp.mean(dxh * xh, axis=-1, keepdims=True))
        part = jnp.sum(dy * xh, axis=0, keepdims=True)
        lrow = jnp.broadcast_to(lpart, (1, LANES))

        @pl.when(i == 0)
        def _():
            dw_ref[...] = part
            loss_ref[...] = lrow

        @pl.when(i > 0)
        def _():
            dw_ref[...] += part
            loss_ref[...] += lrow

    row = pl.BlockSpec((tr, d), lambda i: (i, 0))
    vec = pl.BlockSpec((1, d), lambda i: (0, 0))
    return pl.pallas_call(
        body, name=name, grid=(s // tr,), in_specs=[row, vec, row],
        out_specs=[row, vec, pl.BlockSpec((1, LANES), lambda i: (0, 0))],
        out_shape=[SDS((s, d), F32), SDS((1, d), F32), SDS((1, LANES), F32)],
        compiler_params=_cp("arbitrary"))(h, w.reshape(1, d), target)


def _gate_z(gl_ref, w_ref, b_ref):
    glb = gl_ref[...].astype(BF16)
    return glb, _dot(glb, w_ref[...]) + b_ref[...]


def gate_fwd(proj, wg, bg, *, name, tr=512):
    s, inw = proj.shape
    qk = wg.shape[1]
    glc = inw // LANES - 1

    def body(gl_ref, w_ref, b_ref, g_ref):
        _, z = _gate_z(gl_ref, w_ref, b_ref)
        g_ref[...] = (jnp.minimum(z, 0.0) - jnp.log(1.0 + jnp.exp(-jnp.abs(z)))) * (1.0 / GLA_GATE_TAU)

    return pl.pallas_call(
        body, name=name, grid=(s // tr,),
        in_specs=[pl.BlockSpec((tr, LANES), lambda i: (i, glc)), pl.BlockSpec((LANES, qk), lambda i: (0, 0)),
                  pl.BlockSpec((1, qk), lambda i: (0, 0))],
        out_specs=pl.BlockSpec((tr, qk), lambda i: (i, 0)), out_shape=SDS((s, qk), F32),
        compiler_params=_cp("parallel"))(proj, wg, bg)


def gate_bwd(dg, proj, wg, bg, *, name, tr=512):
    s, inw = proj.shape
    qk = wg.shape[1]
    glc = inw // LANES - 1

    def body(dg_ref, gl_ref, w_ref, b_ref, dgl_ref, dw_ref, db_ref):
        i = pl.program_id(0)
        glb, z = _gate_z(gl_ref, w_ref, b_ref)
        dz = dg_ref[...] * (1.0 / (1.0 + jnp.exp(z))) * (1.0 / GLA_GATE_TAU)
        dzb = dz.astype(BF16)
        dgl_ref[...] = _dot(dzb, w_ref[...], 1, 1).astype(BF16)
        pw = _dot(glb, dzb, 0, 0)
        pb = jnp.sum(dz, axis=0, keepdims=True)

        @pl.when(i == 0)
        def _():
            dw_ref[...] = pw
            db_ref[...] = pb

        @pl.when(i > 0)
        def _():
            dw_ref[...] += pw
            db_ref[...] += pb

    return pl.pallas_call(
        body, name=name, grid=(s // tr,),
        in_specs=[pl.BlockSpec((tr, qk), lambda i: (i, 0)), pl.BlockSpec((tr, LANES), lambda i: (i, glc)),
                  pl.BlockSpec((LANES, qk), lambda i: (0, 0)), pl.BlockSpec((1, qk), lambda i: (0, 0))],
        out_specs=[pl.BlockSpec((tr, LANES), lambda i: (i, 0)), pl.BlockSpec((LANES, qk), lambda i: (0, 0)),
                   pl.BlockSpec((1, qk), lambda i: (0, 0))],
        out_shape=[SDS((s, LANES), BF16), SDS((LANES, qk), F32), SDS((1, qk), F32)],
        compiler_params=_cp("arbitrary"))(dg, proj, wg, bg)


def _gla_chunk_terms(q_ref, k_ref, g_ref, c, scale):
    q = q_ref[...] * scale
    k = k_ref[...]
    gg = g_ref[...]
    b = _split_dot(_tri(c, "le"), gg)
    row = lax.broadcasted_iota(jnp.int32, gg.shape, 0)
    bm = jnp.sum(jnp.where(row < c // 2, gg, 0.0), axis=0, keepdims=True)
    bl = jnp.sum(gg, axis=0, keepdims=True)
    eb, em, emi, el = jnp.exp(b), jnp.exp(b - bm), jnp.exp(bm - b), jnp.exp(bl - b)
    return q, k, bl, eb, em, emi, el


def _causal(a):
    r = lax.broadcasted_iota(jnp.int32, a.shape, 0)
    c = lax.broadcasted_iota(jnp.int32, a.shape, 1)
    return jnp.where(r >= c, a, 0.0)


def gla_fwd(proj, g, *, name, host=None, c=GLA_CHUNK):
    s = proj.shape[0]
    qk = g.shape[1]
    dk, dv = qk // GLA_HEADS, 2 * qk // GLA_HEADS
    nc = s // c
    scale = dk ** -0.5
    kq = qk // dk

    def body(q_ref, k_ref, v_ref, g_ref, o_ref, st_ref, state):
        @pl.when(pl.program_id(1) == 0)
        def _():
            state[...] = jnp.zeros_like(state)

        q, k, bl, eb, em, emi, el = _gla_chunk_terms(q_ref, k_ref, g_ref, c, scale)
        v2 = _hilo(v_ref[...])
        st = state[...]
        st_ref[...] = st
        a = _causal(_dot3(_hilo(q * em), _hilo(k * emi), 1, 1))
        o_ref[...] = _dot3(_hilo(q * eb), _hilo(st), 1, 1) + _dot3(_hilo(a), v2)
        state[...] = st * jnp.exp(bl) + _dot3(v2, _hilo(k * el), 0, 0)

    return _hosted_call(
        body, host, name=name, grid=(GLA_HEADS, nc),
        in_specs=[pl.BlockSpec((c, dk), lambda h, i: (i, h)), pl.BlockSpec((c, dk), lambda h, i: (i, kq + h)),
                  pl.BlockSpec((c, dv), lambda h, i: (i, kq + h)), pl.BlockSpec((c, dk), lambda h, i: (i, h))],
        out_specs=[pl.BlockSpec((c, dv), lambda h, i: (i, h)),
                   pl.BlockSpec((None, None, dv, dk), lambda h, i: (h, i, 0, 0))],
        out_shape=[SDS((s, 2 * qk), F32), SDS((GLA_HEADS, nc, dv, dk), F32)],
        scratch_shapes=[pltpu.VMEM((dv, dk), F32)], sem=("parallel", "arbitrary"), args=(proj, proj, proj, g))


def gla_bwd(proj, g, states, do, *, name, host=None, c=GLA_CHUNK):
    s = proj.shape[0]
    qk = g.shape[1]
    dk, dv = qk // GLA_HEADS, 2 * qk // GLA_HEADS
    nc = s // c
    scale = dk ** -0.5
    kq = qk // dk

    def body(q_ref, k_ref, v_ref, g_ref, do_ref, st_ref, dq_ref, dk_ref, dv_ref, dg_ref, dstate, dgc):
        @pl.when(pl.program_id(1) == 0)
        def _():
            dstate[...] = jnp.zeros_like(dstate)
            dgc[...] = jnp.zeros_like(dgc)

        q, k, bl, eb, em, emi, el = _gla_chunk_terms(q_ref, k_ref, g_ref, c, scale)
        v2, do2 = _hilo(v_ref[...]), _hilo(do_ref[...])
        qe, qm, km, kd = _hilo(q * eb), _hilo(q * em), _hilo(k * emi), _hilo(k * el)
        ds = dstate[...]
        ds2 = _hilo(ds)
        a = _hilo(_causal(_dot3(qm, km, 1, 1)))
        dv_ref[...] = (_dot3(a, do2, 0, 0) + _dot3(kd, ds2, 1, 1)).astype(BF16)
        da = _hilo(_causal(_dot3(do2, v2, 1, 1)))
        dq = _dot3(da, km) * em + _dot3(do2, _hilo(st_ref[...])) * eb
        dkk = _dot3(da, qm, 0, 0) * emi + _dot3(v2, ds2) * el
        dstate[...] = ds * jnp.exp(bl) + _dot3(do2, qe, 0, 0)
        db = q * dq - k * dkk
        dg_ref[...] = _split_dot(_tri(c, "ge"), db) + dgc[...]
        dgc[...] += jnp.sum(db, axis=0, keepdims=True)
        dq_ref[...] = (dq * scale).astype(BF16)
        dk_ref[...] = dkk.astype(BF16)

    rev = lambda i: nc - 1 - i
    qspec = pl.BlockSpec((c, dk), lambda h, i: (rev(i), h))
    vspec = pl.BlockSpec((c, dv), lambda h, i: (rev(i), h))
    return _hosted_call(
        body, host, name=name, grid=(GLA_HEADS, nc),
        in_specs=[qspec, pl.BlockSpec((c, dk), lambda h, i: (rev(i), kq + h)),
                  pl.BlockSpec((c, dv), lambda h, i: (rev(i), kq + h)), qspec, vspec,
                  pl.BlockSpec((None, None, dv, dk), lambda h, i: (h, rev(i), 0, 0))],
        out_specs=[qspec, qspec, vspec, qspec],
        out_shape=[SDS((s, qk), BF16), SDS((s, qk), BF16), SDS((s, 2 * qk), BF16), SDS((s, qk), F32)],
        scratch_shapes=[pltpu.VMEM((dv, dk), F32), pltpu.VMEM((1, dk), F32)], sem=("parallel", "arbitrary"),
        args=(proj, proj, proj, g, do, states))


def gnorm_fwd(o, proj, gw, *, name, tr=512):
    s, v = o.shape
    dv = v // GLA_HEADS
    roff = 2 * GLA_HEADS

    def body(o_ref, r_ref, w_ref, y_ref):
        of = o_ref[...]
        rs = lax.rsqrt(jnp.mean(of * of, axis=-1, keepdims=True) + EPS)
        r = r_ref[...]
        y_ref[...] = (of * rs * w_ref[...] * (r * _sigmoid(r))).astype(BF16)

    blk = pl.BlockSpec((tr, dv), lambda i, h: (i, h))
    return pl.pallas_call(
        body, name=name, grid=(s // tr, GLA_HEADS),
        in_specs=[blk, pl.BlockSpec((tr, dv), lambda i, h: (i, roff + h)), pl.BlockSpec((1, dv), lambda i, h: (0, 0))],
        out_specs=blk, out_shape=SDS((s, v), BF16), compiler_params=_cp("parallel", "parallel"))(o, proj, gw)


def gnorm_bwd(dy, o, proj, gw, *, name, tr=512):
    s, v = o.shape
    dv = v // GLA_HEADS
    roff = 2 * GLA_HEADS

    def body(dy_ref, o_ref, r_ref, w_ref, do_ref, dr_ref, dw_ref):
        first = jnp.logical_and(pl.program_id(0) == 0, pl.program_id(1) == 0)
        of = o_ref[...]
        rs = lax.rsqrt(jnp.mean(of * of, axis=-1, keepdims=True) + EPS)
        n = of * rs
        r = r_ref[...]
        sg = _sigmoid(r)
        dyf = dy_ref[...].astype(F32)
        dn_w = dyf * (r * sg)
        dr_ref[...] = (dyf * n * w_ref[...] * sg * (1.0 + r * (1.0 - sg))).astype(BF16)
        dn = dn_w * w_ref[...]
        do_ref[...] = rs * (dn - n * jnp.mean(dn * n, axis=-1, keepdims=True))
        part = jnp.sum(dn_w * n, axis=0, keepdims=True)

        @pl.when(first)
        def _():
            dw_ref[...] = part

        @pl.when(jnp.logical_not(first))
        def _():
            dw_ref[...] += part

    blk = pl.BlockSpec((tr, dv), lambda i, h: (i, h))
    vec = pl.BlockSpec((1, dv), lambda i, h: (0, 0))
    return pl.pallas_call(
        body, name=name, grid=(s // tr, GLA_HEADS),
        in_specs=[blk, blk, pl.BlockSpec((tr, dv), lambda i, h: (i, roff + h)), vec],
        out_specs=[blk, blk, vec], out_shape=[SDS((s, v), F32), SDS((s, v), BF16), SDS((1, dv), F32)],
        compiler_params=_cp("arbitrary", "arbitrary"))(dy, o, proj, gw)


def _sb_block(kblk, q, ks, q0, scale, carry, masked):
    tk, tq = kblk.shape[0], q.shape[0]
    z = _dot(kblk, q, 1, 1) * scale
    sp = jnp.maximum(z, 0.0) + jnp.log(1.0 + jnp.exp(-jnp.abs(z)))
    mask = None
    lf = -sp
    if masked:
        kpos = ks + lax.broadcasted_iota(jnp.int32, (tk, tq), 0)
        qpos = q0 + lax.broadcasted_iota(jnp.int32, (tk, tq), 1)
        mask = kpos < qpos
        lf = jnp.where(mask, lf, 0.0)
    later = _split_dot(_tri(tk, "gt"), lf)
    a = jnp.exp(z - sp + later + carry)
    if masked:
        a = jnp.where(mask, a, 0.0)
    return z, sp, mask, lf, a


def sb_fwd(q, kv, *, name, host=None, tq=SB_TQ, tk=SB_TK, group=SB_GROUP):
    s, w = q.shape
    hd = w // SB_HEADS
    nq, nkb = s // tq, s // tk
    scale = hd ** -0.5
    per = tq // tk
    assert per % group == 0

    def body(q_ref, k_ref, v_ref, o_ref, car_ref, o_acc):
        qi = pl.program_id(1)
        qb = q_ref[...]
        q0 = qi * tq
        car_ref[...] = jnp.full(car_ref.shape, SB_SKIPPED, F32)
        o_acc[...] = jnp.zeros_like(o_acc)

        def blocks(first, carry, skip=None):
            for t in reversed(range(group)):
                kj = first + t
                ks = pl.multiple_of(kj * tk, tk)
                car_ref[pl.ds(kj, 1), :] = carry
                lo = 0 if skip is None else (skip + t) * tk
                _, _, _, lf, a = _sb_block(k_ref[pl.ds(ks, tk), :], qb[lo:, :], ks, q0 + lo, scale, carry[:, lo:], skip is not None)
                o_acc[lo:, :] += _dot(a.astype(BF16), v_ref[pl.ds(ks, tk), :], 0, 0)
                add = jnp.sum(lf, axis=0, keepdims=True)
                carry = carry + (add if lo == 0 else jnp.concatenate([jnp.zeros((1, lo), F32), add], axis=1))
            return carry

        carry = jnp.zeros((1, tq), F32)
        for gidx in reversed(range(per // group)):
            carry = blocks(qi * per + gidx * group, carry, gidx * group)
        n_in = qi * (per // group)
        lax.while_loop(lambda st: jnp.logical_and(st[0] < n_in, jnp.max(st[1]) > SB_DEAD),
                       lambda st: (st[0] + 1, blocks((n_in - 1 - st[0]) * group, st[1])), (jnp.int32(0), carry))
        o_ref[...] = o_acc[...].astype(BF16)

    qspec = pl.BlockSpec((tq, hd), lambda h, i: (i, h))
    return _hosted_call(
        body, host, name=name, grid=(SB_HEADS, nq),
        in_specs=[qspec, pl.BlockSpec((None, s, hd), lambda h, i: (0, 0, h)), pl.BlockSpec((None, s, hd), lambda h, i: (1, 0, h))],
        out_specs=[qspec, pl.BlockSpec((None, None, nkb, tq), lambda h, i: (h, i, 0, 0))],
        out_shape=[SDS((s, w), BF16), SDS((SB_HEADS, nq, nkb, tq), F32)],
        scratch_shapes=[pltpu.VMEM((tq, hd), F32)], sem=("parallel", "parallel"), args=(q, kv, kv))


def sb_bwd(q, kv, do, car, *, name, host=None, tq=SB_TQ, tk=SB_TK, group=SB_GROUP):
    s, w = q.shape
    hd = w // SB_HEADS
    nq, nkb = s // tq, s // tk
    scale = hd ** -0.5
    per = tq // tk
    assert per % group == 0

    def body(q_ref, k_ref, v_ref, do_ref, car_ref, dq_ref, dkv_ref, dq_acc, dk_acc, dv_acc):
        qi = pl.program_id(1)
        qb = q_ref[...]
        dob = do_ref[...]
        q0 = qi * tq
        dq_acc[...] = jnp.zeros_like(dq_acc)

        @pl.when(qi == 0)
        def _():
            dk_acc[...] = jnp.zeros_like(dk_acc)
            dv_acc[...] = jnp.zeros_like(dv_acc)

        def blocks(first, pcar, skip=None):
            for t in range(group):
                kj = first + t
                ks = pl.multiple_of(kj * tk, tk)
                kblk = k_ref[pl.ds(ks, tk), :]
                lo = 0 if skip is None else (skip + t) * tk
                masked = skip is not None
                qs, dos = qb[lo:, :], dob[lo:, :]
                z, sp, mask, _, a = _sb_block(kblk, qs, ks, q0 + lo, scale, car_ref[pl.ds(kj, 1), :][:, lo:], masked)
                p = a * _dot(v_ref[pl.ds(ks, tk), :], dos, 1, 1)
                before = _split_dot(_tri(tk, "lt"), p)
                sg = jnp.exp(z - sp)
                dz = p * (1.0 - sg) - (pcar[:, lo:] + before) * sg
                if masked:
                    dz = jnp.where(mask, dz, 0.0)
                dz = (dz * scale).astype(BF16)
                dk_acc[pl.ds(ks, tk), :] += _dot(dz, qs)
                dv_acc[pl.ds(ks, tk), :] += _dot(a.astype(BF16), dos)
                dq_acc[lo:, :] += _dot(dz, kblk, 0, 0)
                add = jnp.sum(p, axis=0, keepdims=True)
                pcar = pcar + (add if lo == 0 else jnp.concatenate([jnp.zeros((1, lo), F32), add], axis=1))
            return pcar

        n_in = qi * (per // group)

        def reached(g):
            return (jnp.max(car_ref[pl.ds(g * group + group - 1, 1), :]) > SB_DEAD).astype(jnp.int32)

        start = n_in - lax.fori_loop(0, n_in, lambda g, n: n + reached(g), jnp.int32(0))
        pcar = lax.fori_loop(start, n_in, lambda i, c: blocks(i * group, c), jnp.zeros((1, tq), F32))
        for gidx in range(per // group):
            pcar = blocks(qi * per + gidx * group, pcar, gidx * group)
        dq_ref[...] = dq_acc[...].astype(BF16)

        @pl.when(qi == nq - 1)
        def _():
            dkv_ref[0] = dk_acc[...].astype(BF16)
            dkv_ref[1] = dv_acc[...].astype(BF16)

    qspec = pl.BlockSpec((tq, hd), lambda h, i: (i, h))
    return _hosted_call(
        body, host, name=name, grid=(SB_HEADS, nq),
        in_specs=[qspec, pl.BlockSpec((None, s, hd), lambda h, i: (0, 0, h)), pl.BlockSpec((None, s, hd), lambda h, i: (1, 0, h)),
                  qspec, pl.BlockSpec((None, None, nkb, tq), lambda h, i: (h, i, 0, 0))],
        out_specs=[qspec, pl.BlockSpec((2, s, hd), lambda h, i: (0, 0, h))],
        out_shape=[SDS((s, w), BF16), SDS((2, s, w), BF16)],
        scratch_shapes=[pltpu.VMEM((tq, hd), F32), pltpu.VMEM((s, hd), F32), pltpu.VMEM((s, hd), F32)],
        sem=("parallel", "arbitrary"), args=(q, kv, kv, do, car))


def adamw(w, g, m, v, *, name):
    shape = w.shape
    c = shape[-1]
    r = w.size // c
    tr = _tile(r, max(8, (3 * LANES * 1024) // c), unit=8) if r >= 8 else r

    def body(w_ref, g_ref, m_ref, v_ref, d_ref, nm_ref, nv_ref):
        gf = g_ref[...]
        mn = ADAM_B1 * m_ref[...] + (1.0 - ADAM_B1) * gf
        vn = ADAM_B2 * v_ref[...] + (1.0 - ADAM_B2) * (gf * gf)
        m_hat = mn / (1.0 - ADAM_B1 ** ADAM_STEP)
        v_hat = vn / (1.0 - ADAM_B2 ** ADAM_STEP)
        d_ref[...] = -ADAM_LR * (m_hat / (jnp.sqrt(v_hat) + ADAM_EPS) + ADAM_WD * w_ref[...])
        nm_ref[...] = mn
        nv_ref[...] = vn

    blk = pl.BlockSpec((tr, c), lambda i: (i, 0))
    outs = pl.pallas_call(
        body, name=name, grid=(r // tr,), in_specs=[blk] * 4, out_specs=[blk] * 3,
        out_shape=[SDS((r, c), F32)] * 3, compiler_params=_cp("parallel"))(
            *(t.reshape(r, c) for t in (w, g, m, v)))
    return tuple(o.reshape(shape) for o in outs)


def _row_tile(r, c):
    return _tile(r, max(16, (4 * LANES * 1024) // c), unit=16)


def _halves(a):
    return a.reshape(a.shape[:-2] + (2, a.shape[-2] // 2, a.shape[-1]))


def place_shard(w, chip, *, name, layer=None):
    r, c = w.shape[-2:]
    tr = _row_tile(r, c)

    def body(c_ref, w_ref, o_ref):
        o_ref[...] = w_ref[...].astype(BF16)

    if layer is None:
        w_spec = pl.BlockSpec((None, tr, c), lambda h, i, c_ref: (h, i, 0))
    else:
        w_spec = pl.BlockSpec((None, None, tr, c), lambda h, i, c_ref: (layer, h, i, 0))
    return pl.pallas_call(
        body, name=name, out_shape=SDS((N_CHIPS, 2, r, c), BF16),
        grid_spec=pltpu.PrefetchScalarGridSpec(
            num_scalar_prefetch=1, grid=(2, r // tr), in_specs=[w_spec],
            out_specs=pl.BlockSpec((None, None, tr, c), lambda h, i, c_ref: (c_ref[0], h, i, 0))),
        compiler_params=_cp("parallel", "parallel"))(chip, w)


def sibling_exchange(gs, *, name):
    n = len(gs)

    def body(*refs):
        g_refs, a_refs = refs[:n], refs[n:2 * n]
        send_sems, recv_sems = refs[2 * n:]
        x, y, c = _coords()
        cps = [_remote(g_refs[i].at[:, 1 - c], a_refs[i], send_sems.at[i], recv_sems.at[i], (x, y, 1 - c)) for i in range(n)]
        for cp in cps:
            cp.start()
        for cp in cps:
            cp.wait()

    return pl.pallas_call(
        body, name=name, in_specs=[ANY] * n, out_specs=[ANY] * n,
        out_shape=[SDS(g.shape[:1] + g.shape[2:], g.dtype) for g in gs],
        scratch_shapes=[_dma_sems(n), _dma_sems(n)])(*gs)


def half_add(g, a, core, *, name):
    n, _, r, c = g.shape
    tr = _row_tile(r, c)

    def body(c_ref, g_ref, a_ref, o_ref):
        o_ref[...] = (g_ref[...].astype(F32) + a_ref[...].astype(F32)).astype(o_ref.dtype)

    blk = pl.BlockSpec((None, tr, c), lambda s, i, c_ref: (s, i, 0))
    return pl.pallas_call(
        body, name=name, out_shape=SDS((n, r, c), g.dtype),
        grid_spec=pltpu.PrefetchScalarGridSpec(
            num_scalar_prefetch=1, grid=(n, r // tr),
            in_specs=[pl.BlockSpec((None, None, tr, c), lambda s, i, c_ref: (s, c_ref[0], i, 0)), blk], out_specs=blk),
        compiler_params=_cp("parallel", "parallel"))(core, g, a)


def chip_sum(p, b, chip, core, *, name, layer=None, into=None):
    _, r, c = p.shape
    tr = _row_tile(r, c)

    def body(chip_ref, core_ref, p_ref, b_ref, *refs):
        t = p_ref[...].astype(F32)
        for k in range(N_CHIPS - 1):
            t = t + b_ref[k].astype(F32)
        refs[-1][...] = t

    in_specs = [pl.BlockSpec((None, tr, c), lambda i, chip_ref, core_ref: (chip_ref[0], i, 0)),
                pl.BlockSpec((N_CHIPS - 1, tr, c), lambda i, chip_ref, core_ref: (0, i, 0))]
    if layer is None:
        shape, o_spec = (2, r, c), pl.BlockSpec((None, tr, c), lambda i, chip_ref, core_ref: (core_ref[0], i, 0))
    else:
        shape = (2, 2, r, c)
        o_spec = pl.BlockSpec((None, None, tr, c), lambda i, chip_ref, core_ref: (layer, core_ref[0], i, 0))
    args, aliases = (chip, core, p, b), {}
    if into is not None:
        in_specs, args, aliases = in_specs + [ANY], args + (into,), {4: 0}
    return pl.pallas_call(
        body, name=name, out_shape=SDS(shape, F32),
        grid_spec=pltpu.PrefetchScalarGridSpec(num_scalar_prefetch=2, grid=(r // tr,), in_specs=in_specs, out_specs=o_spec),
        input_output_aliases=aliases, compiler_params=_cp("parallel"))(*args)


def sibling_gather(ts, *, name):
    n = len(ts)

    def body(*refs):
        out_refs = refs[n:2 * n]
        send_sems, recv_sems = refs[2 * n:]
        x, y, c = _coords()

        def half(i, which):
            return out_refs[i].at[which] if len(ts[i].shape) == 3 else out_refs[i].at[:, which]

        cps = [_remote(half(i, c), half(i, c), send_sems.at[i], recv_sems.at[i], (x, y, 1 - c)) for i in range(n)]
        for cp in cps:
            cp.start()
        for i in range(n):
            _remote(half(i, c), half(i, 1 - c), send_sems.at[i], recv_sems.at[i], (x, y, 1 - c)).wait_recv()
        for cp in cps:
            cp.wait_send()

    return pl.pallas_call(
        body, name=name, in_specs=[ANY] * n, out_specs=[ANY] * n, out_shape=[SDS(t.shape, t.dtype) for t in ts],
        input_output_aliases={i: i for i in range(n)}, scratch_shapes=[_dma_sems(n), _dma_sems(n)])(*ts)


def allgather_all(sm, *, name):
    r, w = sm.shape

    def body(s_ref, out_ref, send_sems, recv_sems, local_sem):
        x, y, c = _coords()
        me = 4 * x + 2 * y + c

        def peer(rel):
            flip = lambda v, bit: 1 - v if bit else v
            return flip(x, rel & 4), flip(y, rel & 2), flip(c, rel & 1)

        loc = pltpu.make_async_copy(s_ref, out_ref.at[me], local_sem.at[0])
        loc.start()
        cps = [_remote(s_ref, out_ref.at[me], send_sems.at[rel - 1], recv_sems.at[rel - 1], peer(rel)) for rel in range(1, N_DEV)]
        for cp in cps:
            cp.start()
        for rel in range(1, N_DEV):
            px, py, pc = peer(rel)
            _remote(s_ref, out_ref.at[4 * px + 2 * py + pc], send_sems.at[rel - 1], recv_sems.at[rel - 1], (px, py, pc)).wait_recv()
        for cp in cps:
            cp.wait_send()
        loc.wait()

    return pl.pallas_call(
        body, name=name, in_specs=[ANY], out_specs=ANY, out_shape=SDS((N_DEV, r, w), sm.dtype),
        scratch_shapes=[_dma_sems(N_DEV - 1), _dma_sems(N_DEV - 1), _dma_sems(1)])(sm)


def sum_blocks(a, *, name):
    n, r, w = a.shape

    def body(a_ref, o_ref):
        t = a_ref[0]
        for k in range(1, n):
            t = t + a_ref[k]
        o_ref[...] = t

    return pl.pallas_call(body, name=name, out_shape=SDS((r, w), F32))(a)


class Exchanges:
    def __init__(self, placed=None, chip=None, core=None):
        self.placed, self.chip, self.core = placed, chip, core
        self.mine = {}

    def gather(self, names):
        return None if self.placed is None else GatherX([self.placed[n] for n in names])

    def reduce(self, grads, call):
        if self.placed is None:
            return call(None)
        names = list(grads)
        from_sibling = sibling_exchange([grads[n] for n in names], name="grads_to_sibling_" + names[0])
        pairs = [half_add(grads[n], a, self.core, name=f"pair_sum_{n}") for n, a in zip(names, from_sibling)]
        result, from_chips = call(ScatterX(pairs))
        for n, t, b in zip(names, pairs, from_chips):
            base, layer = (n[:-1], int(n[-1])) if n[:-1] in ("w_gu", "w_dn") else (n, None)
            self.mine[base] = chip_sum(t, b, self.chip, self.core, name=f"chip_sum_{n}", layer=layer, into=self.mine.get(base))
        return result


def _usable(name, buf, d):
    if name == "w_in":
        cols = N_CHIPS * buf.shape[-1]
        full = buf.reshape(N_CHIPS, d, -1).transpose(1, 0, 2).reshape(d, cols)
        return jnp.pad(full, ((0, 0), (0, -cols % LANES)))
    if name in ("w_out", "w_q", "w_so"):
        return buf.reshape(-1, buf.shape[-1])
    return buf.reshape(N_CHIPS, -1, buf.shape[-1])


def _by_rows(t):
    return _halves(t.reshape(N_CHIPS, -1, t.shape[-1]))


def _ffn_fwd(h, nw, w_gu, w_dn, tag, host=None):
    f = rmsnorm_fwd(h, nw, name=f"{tag}_norm")
    gu, act, got = gate_up_swiglu(f, View(w_gu, "cols"), name=f"{tag}_gate_up", host=host)
    return matmul(act, View(w_dn, "rows"), name=f"{tag}_down", residual=h), (f, gu, act), got


def _ffn_bwd(dh, h, nw, w_gu, w_dn, saved, tag, ex, dn_name, carried):
    f, gu, act = saved
    dgu = View(matmul(dh, View(w_dn, "rows"), name=f"{tag}_dgu", tb=True, dswiglu=gu), "cols")
    dw_dn = matmul(act, dh, name=f"{tag}_dw_down", ta=True, out_dtype=BF16)

    def dw_gate_up(host):
        return matmul(f, dgu, name=f"{tag}_dw_gate_up", ta=True, out_dtype=BF16, out_chips=N_CHIPS, host=host, tk=SEQ_TILE)

    dw_gu = ex.reduce(carried, dw_gate_up) if carried else dw_gate_up(None)
    df = ex.reduce({dn_name: _by_rows(dw_dn)}, lambda host: matmul(dgu, View(w_gu, "cols"), name=f"{tag}_df", tb=True, host=host))
    dh_in, dnw = rmsnorm_bwd(df, h, nw, dh, name=f"{tag}_dnorm")
    return dh_in, dnw, dw_gu, dw_dn


def local_step(x, target, p, ex):
    d = x.shape[1]
    hosted = ex.placed is not None
    w = {} if hosted else {n: _usable(n, p[n], d) for n in ("w_in", "w_out", "w_kv", "w_q", "w_so", "w_gu0", "w_gu1", "w_dn0", "w_dn1")}

    def take(names, got):
        for n, buf in zip(names, got or []):
            w[n] = _usable(n, buf, d)

    def carry(call, names):
        host = ex.gather(names)
        if host is None:
            return call(None)
        res, got = call(host)
        take(names, got)
        return res

    if hosted:
        take(["w_in"], run_exchange(ex.gather(["w_in"]), name="gather_gla_in"))
    a0 = rmsnorm_fwd(x, p["an0"], name="l0_attn_norm")
    proj = carry(lambda host: matmul(a0, w["w_in"], name="gla_in", host=host), ["w_out", "w_dn0"])
    g = gate_fwd(proj, p["wg"], p["bg"], name="gla_gate")
    o, states = carry(lambda host: gla_fwd(proj, g, name="gla_scan", host=host), ["w_gu0"])
    og = gnorm_fwd(o, proj, p["gw"], name="gla_outnorm")
    h1 = matmul(og, w["w_out"], name="gla_out", residual=x)
    sb_names = ["w_kv", "w_q", "w_so"]
    h2, ffn0, got = _ffn_fwd(h1, p["fn0"], w["w_gu0"], w["w_dn0"], "ffn0", host=ex.gather(sb_names))
    take(sb_names, got)
    w_kv = View(w["w_kv"], "cols")
    kvn = rmsnorm_fwd(h2, p["kvn"], name="kv_norm")
    kv = matmul(kvn, w_kv, name="sb_kv", out_dtype=BF16, out_chips=2)
    a1 = rmsnorm_fwd(h2, p["an1"], name="l1_attn_norm")
    q2 = matmul(a1, w["w_q"], name="sb_q", out_dtype=BF16)
    o2, car = carry(lambda host: sb_fwd(q2, kv, name="sb_attn", host=host), ["w_gu1", "w_dn1"])
    h3 = matmul(o2, w["w_so"], name="sb_out", residual=h2)
    h4, ffn1, _ = _ffn_fwd(h3, p["fn1"], w["w_gu1"], w["w_dn1"], "ffn1")
    dh4, d_fin, loss_row = final_loss(h4, p["finn"], target, name="final_loss")

    dh3, d_fn1, dw_gu1, dw_dn1 = _ffn_bwd(dh4, h3, p["fn1"], w["w_gu1"], w["w_dn1"], ffn1, "ffn1", ex, "w_dn1", {})
    do2 = matmul(dh3, w["w_so"], name="sb_do", tb=True, out_dtype=BF16)
    dw_so = matmul(o2, dh3, name="sb_dw_out", ta=True, out_dtype=BF16)
    dq2, dkv = ex.reduce({"w_gu1": _halves(dw_gu1)}, lambda host: sb_bwd(q2, kv, do2, car, name="sb_attn_bwd", host=host))
    dkv = View(dkv, "cols")
    dw_q = matmul(a1, dq2, name="sb_dw_q", ta=True, out_dtype=BF16, tk=SEQ_TILE)
    da1 = matmul(dq2, w["w_q"], name="sb_da", tb=True)
    dh2, d_an1 = rmsnorm_bwd(da1, h2, p["an1"], dh3, name="l1_attn_dnorm")
    dw_kv = matmul(kvn, dkv, name="sb_dw_kv", ta=True, out_dtype=BF16, out_chips=N_CHIPS, tk=SEQ_TILE)
    dkvn = matmul(dkv, w_kv, name="sb_dkvn", tb=True)
    dh2, d_kvn = rmsnorm_bwd(dkvn, h2, p["kvn"], dh2, name="kv_dnorm")
    sb_grads = {"w_kv": _halves(dw_kv), "w_q": _by_rows(dw_q), "w_so": _by_rows(dw_so)}
    dh1, d_fn0, dw_gu0, dw_dn0 = _ffn_bwd(dh2, h1, p["fn0"], w["w_gu0"], w["w_dn0"], ffn0, "ffn0", ex, "w_dn0", sb_grads)
    dog = matmul(dh1, w["w_out"], name="gla_dog", tb=True, out_dtype=BF16)
    dw_out = matmul(og, dh1, name="gla_dw_out", ta=True, out_dtype=BF16)
    do, dr, d_gw = gnorm_bwd(dog, o, proj, p["gw"], name="gla_outnorm_bwd")
    dq, dk, dv, dg = ex.reduce({"w_gu0": _halves(dw_gu0)}, lambda host: gla_bwd(proj, g, states, do, name="gla_scan_bwd", host=host))
    dgl, d_wg, d_bg = gate_bwd(dg, proj, p["wg"], p["bg"], name="gla_gate_bwd")
    dproj = jnp.concatenate([dq, dk, dv, dr, dgl], axis=1)
    dw_in = matmul(a0, dproj, name="gla_dw_in", ta=True, out_dtype=BF16, tk=SEQ_TILE)
    in_w = p["in_w"]
    gla_grads = {"w_in": _halves(dw_in[:, :in_w].reshape(d, N_CHIPS, -1).transpose(1, 0, 2)), "w_out": _by_rows(dw_out)}
    da0 = ex.reduce(gla_grads, lambda host: matmul(dproj, w["w_in"], name="gla_da", tb=True, host=host))
    dx, d_an0 = rmsnorm_bwd(da0, x, p["an0"], dh1, name="l0_attn_dnorm")

    small = dict(an0=d_an0, an1=d_an1, fn0=d_fn0, fn1=d_fn1, kvn=d_kvn, finn=d_fin, wg=d_wg, bg=d_bg, gw=d_gw)
    big = {}
    if not hosted:
        big = dict(gla_grads, **sb_grads, w_gu0=_halves(dw_gu0), w_gu1=_halves(dw_gu1), w_dn0=_by_rows(dw_dn0), w_dn1=_by_rows(dw_dn1))
    return loss_row, dx, small, big


def _pack_rows(parts):
    cat = jnp.concatenate([t.reshape(-1) for t in parts])
    rows = -(-cat.size // (LANES * SUBLANES)) * SUBLANES
    return jnp.pad(cat, (0, rows * LANES - cat.size)).reshape(rows, LANES)


def _segments(flat, sizes):
    out, off = [], 0
    for n in sizes:
        out.append(flat[..., off:off + n])
        off += n
    return out


def kernel(x, attn_norm_w, ffn_norm_w, gla_w_in, gla_w_gate_up, gla_b_gate, gla_gnorm_w, gla_w_out, kv_norm_w, sb_w_kv, sb_w_q, sb_w_out, ffn_w_gate_up, ffn_w_down, final_norm_w, loss_target, m_attn_norm_w, m_ffn_norm_w, m_gla_w_in, m_gla_w_gate_up, m_gla_b_gate, m_gla_gnorm_w, m_gla_w_out, m_kv_norm_w, m_sb_w_kv, m_sb_w_q, m_sb_w_out, m_ffn_w_gate_up, m_ffn_w_down, m_final_norm_w, v_attn_norm_w, v_ffn_norm_w, v_gla_w_in, v_gla_w_gate_up, v_gla_b_gate, v_gla_gnorm_w, v_gla_w_out, v_kv_norm_w, v_sb_w_kv, v_sb_w_q, v_sb_w_out, v_ffn_w_gate_up, v_ffn_w_down, v_final_norm_w):
    xi, yi, ci = _coords()
    core = ci.astype(jnp.int32).reshape(1)
    chip1 = (2 * xi + yi).astype(jnp.int32)
    chip = chip1.reshape(1)
    rank = gla_w_gate_up.shape[1]

    gu4, dn4 = _halves(ffn_w_gate_up), _halves(ffn_w_down)
    shards = dict(w_in=(_halves(gla_w_in[0]), None), w_out=(_halves(gla_w_out[0]), None), w_kv=(_halves(sb_w_kv), None),
                  w_q=(_halves(sb_w_q[0]), None), w_so=(_halves(sb_w_out[0]), None),
                  w_gu0=(gu4, 0), w_gu1=(gu4, 1), w_dn0=(dn4, 0), w_dn1=(dn4, 1))
    placed = {n: place_shard(a, chip, name=f"place_{n}", layer=l) for n, (a, l) in shards.items()}
    small_w = [gla_w_gate_up, gla_b_gate, gla_gnorm_w]
    small_all = allgather_all(_pack_rows(small_w), name="gather_gate_weights")
    wg, bg, gw = _segments(small_all[::2].reshape(N_CHIPS, -1), [a.size for a in small_w])
    wg = wg.reshape(N_CHIPS, rank, -1).transpose(1, 0, 2).reshape(rank, -1)
    p = dict(an0=attn_norm_w[0], an1=attn_norm_w[1], fn0=ffn_norm_w[0], fn1=ffn_norm_w[1], kvn=kv_norm_w, finn=final_norm_w,
             wg=jnp.pad(wg, ((0, LANES - rank), (0, 0))).astype(BF16), bg=bg.reshape(1, -1), gw=gw.reshape(1, -1),
             in_w=N_CHIPS * gla_w_in.shape[-1])

    ex = Exchanges(placed, chip, core)
    loss_row, dx, g, _ = local_step(x[0], loss_target[0], p, ex)
    loss = lax.psum(loss_row[0, 0], ("x", "y", "c"))

    tags = list(ex.mine)
    tot = dict(zip(tags, sibling_gather([ex.mine[n] for n in tags], name="grads_from_sibling")))

    vecs = [jnp.concatenate([g["an0"], g["an1"]]), jnp.concatenate([g["fn0"], g["fn1"]]), g["kvn"], g["finn"],
            g["wg"][:rank], g["bg"], g["gw"]]
    gathered_vecs = allgather_all(_pack_rows(vecs), name="gather_small_grads")
    d_an, d_fn, d_kvn, d_fin, d_wg, d_bg, d_gw = _segments(
        sum_blocks(gathered_vecs, name="sum_small_grads").reshape(-1), [t.size for t in vecs])

    def shard(t, like):
        return lax.dynamic_index_in_dim(t.reshape(-1, N_CHIPS, like.shape[-1]), chip1, axis=1, keepdims=False).reshape(like.shape)

    weights = dict(
        attn_norm_w=(attn_norm_w, m_attn_norm_w, v_attn_norm_w), ffn_norm_w=(ffn_norm_w, m_ffn_norm_w, v_ffn_norm_w),
        gla_w_in=(gla_w_in, m_gla_w_in, v_gla_w_in), gla_w_gate_up=(gla_w_gate_up, m_gla_w_gate_up, v_gla_w_gate_up),
        gla_b_gate=(gla_b_gate, m_gla_b_gate, v_gla_b_gate), gla_gnorm_w=(gla_gnorm_w, m_gla_gnorm_w, v_gla_gnorm_w),
        gla_w_out=(gla_w_out, m_gla_w_out, v_gla_w_out), kv_norm_w=(kv_norm_w, m_kv_norm_w, v_kv_norm_w),
        sb_w_kv=(sb_w_kv, m_sb_w_kv, v_sb_w_kv), sb_w_q=(sb_w_q, m_sb_w_q, v_sb_w_q), sb_w_out=(sb_w_out, m_sb_w_out, v_sb_w_out),
        ffn_w_gate_up=(ffn_w_gate_up, m_ffn_w_gate_up, v_ffn_w_gate_up), ffn_w_down=(ffn_w_down, m_ffn_w_down, v_ffn_w_down),
        final_norm_w=(final_norm_w, m_final_norm_w, v_final_norm_w))
    grads = dict(
        attn_norm_w=d_an.reshape(attn_norm_w.shape), ffn_norm_w=d_fn.reshape(ffn_norm_w.shape),
        gla_w_in=tot["w_in"].reshape(gla_w_in.shape), gla_w_gate_up=shard(d_wg, gla_w_gate_up),
        gla_b_gate=shard(d_bg, gla_b_gate), gla_gnorm_w=shard(d_gw, gla_gnorm_w),
        gla_w_out=tot["w_out"].reshape(gla_w_out.shape), kv_norm_w=d_kvn.reshape(kv_norm_w.shape),
        sb_w_kv=tot["w_kv"].reshape(sb_w_kv.shape), sb_w_q=tot["w_q"].reshape(sb_w_q.shape),
        sb_w_out=tot["w_so"].reshape(sb_w_out.shape),
        ffn_w_gate_up=tot["w_gu"].reshape(ffn_w_gate_up.shape), ffn_w_down=tot["w_dn"].reshape(ffn_w_down.shape),
        final_norm_w=d_fin.reshape(final_norm_w.shape))
    names = list(weights)
    stepped = [adamw(weights[n][0], grads[n], weights[n][1], weights[n][2], name=f"adamw_{n}") for n in names]
    return (loss, dx.reshape(x.shape), *[grads[n] for n in names], *[t[0] for t in stepped], *[t[1] for t in stepped],
            *[t[2] for t in stepped])
```

```python
import functools

import jax
import jax.numpy as jnp
from jax import lax
from jax.experimental import pallas as pl
from jax.experimental.pallas import tpu as pltpu

F32 = jnp.float32
BF16 = jnp.bfloat16
SDS = jax.ShapeDtypeStruct
MESH = pl.DeviceIdType.MESH

EPS = 1e-6
GLA_HEADS = 4
GLA_GATE_RANK = 16
GLA_GATE_TAU = 16.0
GLA_CHUNK = 128
SB_HEADS = 16
SB_TQ = 512
SB_TK = 128
SB_GROUP = 2
SB_DEAD = -110.0
SB_SKIPPED = -1e30
ADAM_LR = 0.001
ADAM_B1 = 0.9
ADAM_B2 = 0.999
ADAM_EPS = 1e-08
ADAM_WD = 0.01
ADAM_STEP = 10

LANES = 128
SUBLANES = 8
N_CHIPS = 4
N_DEV = 8
VMEM_LIMIT = 56 * 1024 * 1024
SEQ_TILE = 4096


def _tile(dim, target, unit=LANES):
    if dim <= target:
        return dim
    t = (target // unit) * unit
    while t >= unit:
        if dim % t == 0:
            return t
        t -= unit
    raise ValueError(f"no tile for {dim}")


def _cp(*sem):
    return pltpu.CompilerParams(dimension_semantics=sem, vmem_limit_bytes=VMEM_LIMIT)


def _sigmoid(x):
    return 1.0 / (1.0 + jnp.exp(-x))


def _dot(a, b, ca=1, cb=0):
    return lax.dot_general(a, b, (((ca,), (cb,)), ((), ())), preferred_element_type=F32)


def _split_dot(tri, x):
    hi = x.astype(BF16)
    lo = (x - hi.astype(F32)).astype(BF16)
    return _dot(tri, hi) + _dot(tri, lo)


def _hilo(x):
    hi = x.astype(BF16)
    return hi, (x - hi.astype(F32)).astype(BF16)


def _dot3(a, b, ca=1, cb=0):
    return _dot(a[0], b[0], ca, cb) + _dot(a[0], b[1], ca, cb) + _dot(a[1], b[0], ca, cb)


def _tri(n, kind):
    r = lax.broadcasted_iota(jnp.int32, (n, n), 0)
    c = lax.broadcasted_iota(jnp.int32, (n, n), 1)
    m = {"le": c <= r, "ge": c >= r, "lt": c < r, "gt": c > r}[kind]
    return jnp.where(m, 1.0, 0.0).astype(BF16)


ANY = pl.BlockSpec(memory_space=pl.ANY)


def _coords():
    return lax.axis_index("x"), lax.axis_index("y"), lax.axis_index("c")


def _other_chips(x, y):
    return [(1 - x, y), (x, 1 - y), (1 - x, 1 - y)]


def _remote(src, dst, send_sem, recv_sem, dev):
    return pltpu.make_async_remote_copy(src_ref=src, dst_ref=dst, send_sem=send_sem, recv_sem=recv_sem,
                                        device_id=dev, device_id_type=MESH)


def _dma_sems(n):
    return pltpu.SemaphoreType.DMA((n,))


class GatherX:
    def __init__(self, bufs):
        self.ins, self.ios, self.outs, self.n_sems = [], list(bufs), [], 6 * len(bufs)

    def _copy(self, ios, send_sems, recv_sems, i, k, chip, half, dev):
        blk = ios[i].at[chip, half]
        return _remote(blk, blk, send_sems.at[6 * i + k], recv_sems.at[6 * i + k], dev)

    def _first(self, ios, send_sems, recv_sems):
        x, y, c = _coords()
        return [self._copy(ios, send_sems, recv_sems, i, k, 2 * x + y, c, (cx, cy, c))
                for i in range(len(ios)) for k, (cx, cy) in enumerate(_other_chips(x, y))]

    def start(self, ins, ios, outs, send_sems, recv_sems):
        for cp in self._first(ios, send_sems, recv_sems):
            cp.start()

    def finish(self, ins, ios, outs, send_sems, recv_sems):
        x, y, c = _coords()
        chips = _other_chips(x, y)
        copy = functools.partial(self._copy, ios, send_sems, recv_sems)
        passed = []
        for i in range(len(ios)):
            for k, (cx, cy) in enumerate(chips):
                copy(i, k, 2 * cx + cy, c, (x, y, c)).wait_recv()
                passed.append(copy(i, 3 + k, 2 * cx + cy, c, (x, y, 1 - c)))
                passed[-1].start()
        for i in range(len(ios)):
            for k, (cx, cy) in enumerate(chips):
                copy(i, 3 + k, 2 * cx + cy, 1 - c, (x, y, c)).wait_recv()
        for cp in self._first(ios, send_sems, recv_sems) + passed:
            cp.wait_send()


class ScatterX:
    def __init__(self, ps):
        self.ins, self.ios, self.n_sems = list(ps), [], 3 * len(ps)
        self.outs = [SDS((N_CHIPS - 1,) + p.shape[1:], p.dtype) for p in ps]

    def _copies(self, ins, outs, send_sems, recv_sems):
        x, y, c = _coords()
        return [_remote(ins[i].at[2 * cx + cy], outs[i].at[k], send_sems.at[3 * i + k], recv_sems.at[3 * i + k], (cx, cy, c))
                for i in range(len(ins)) for k, (cx, cy) in enumerate(_other_chips(x, y))]

    def start(self, ins, ios, outs, send_sems, recv_sems):
        for cp in self._copies(ins, outs, send_sems, recv_sems):
            cp.start()

    def finish(self, ins, ios, outs, send_sems, recv_sems):
        for cp in self._copies(ins, outs, send_sems, recv_sems):
            cp.wait()


def _exchange_operands(host):
    x_in = host.ins + host.ios
    x_out = [SDS(a.shape, a.dtype) for a in host.ios] + host.outs
    return x_in, x_out


def run_exchange(host, *, name):
    x_in, x_out = _exchange_operands(host)
    n_ins, n_ios = len(host.ins), len(host.ios)

    def body(*refs):
        xin, xout = refs[:len(x_in)], refs[len(x_in):len(x_in) + len(x_out)]
        send_sems, recv_sems = refs[len(x_in) + len(x_out):]
        x_refs = (xin[:n_ins], xout[:n_ios], xout[n_ios:])
        host.start(*x_refs, send_sems, recv_sems)
        host.finish(*x_refs, send_sems, recv_sems)

    return list(pl.pallas_call(
        body, name=name, in_specs=[ANY] * len(x_in), out_specs=[ANY] * len(x_out), out_shape=x_out,
        input_output_aliases={n_ins + i: i for i in range(n_ios)},
        scratch_shapes=[_dma_sems(host.n_sems), _dma_sems(host.n_sems)])(*x_in))


def _hosted_call(body, host, *, name, grid, in_specs, out_specs, out_shape, scratch_shapes, sem, args):
    if host is None:
        return list(pl.pallas_call(body, name=name, grid=grid, in_specs=in_specs, out_specs=out_specs, out_shape=out_shape,
                                   scratch_shapes=scratch_shapes, compiler_params=_cp(*sem))(*args))
    x_in, x_out = _exchange_operands(host)
    n_in, n_out, n_scr, n_ins, n_ios = len(in_specs), len(out_specs), len(scratch_shapes), len(host.ins), len(host.ios)

    def hosted(*refs):
        ins, xin = refs[:n_in], refs[n_in:n_in + len(x_in)]
        o0 = n_in + len(x_in)
        outs, xout = refs[o0:o0 + n_out], refs[o0 + n_out:o0 + n_out + len(x_out)]
        s0 = o0 + n_out + len(x_out)
        scr, (send_sems, recv_sems) = refs[s0:s0 + n_scr], refs[s0 + n_scr:]
        ids = [pl.program_id(ax) for ax in range(len(grid))]
        first = functools.reduce(jnp.logical_and, [i == 0 for i in ids])
        last = functools.reduce(jnp.logical_and, [i == n - 1 for i, n in zip(ids, grid)])
        x_refs = (xin[:n_ins], xout[:n_ios], xout[n_ios:])

        @pl.when(first)
        def _():
            host.start(*x_refs, send_sems, recv_sems)

        body(*ins, *outs, *scr)

        @pl.when(last)
        def _():
            host.finish(*x_refs, send_sems, recv_sems)

    res = pl.pallas_call(
        hosted, name=name, grid=grid, in_specs=list(in_specs) + [ANY] * len(x_in), out_specs=list(out_specs) + [ANY] * len(x_out),
        out_shape=list(out_shape) + x_out, scratch_shapes=list(scratch_shapes) + [_dma_sems(host.n_sems), _dma_sems(host.n_sems)],
        input_output_aliases={n_in + n_ins + i: n_out + i for i in range(n_ios)},
        compiler_params=_cp(*(("arbitrary",) * len(grid))))(*args, *x_in)
    return list(res[:n_out]), list(res[n_out:])


def _div(i, n):
    return i if n == 1 else lax.div(i, n)


def _rem(i, n):
    return 0 if n == 1 else lax.rem(i, n)


class View:
    def __init__(self, arr, kind="plain", lead=(), g0=0, ng=None):
        self.arr, self.kind, self.lead, self.g0 = arr, kind, tuple(lead), g0
        self.ng = (arr.shape[0] - g0) if ng is None else ng
        r, c = arr.shape[-2:]
        self.runit, self.cunit = r, c
        self.shape = {"plain": (r, c), "cols": (r, self.ng * c), "rows": (self.ng * r, c)}[kind]

    def spec(self, br, bc, rfn, cfn):
        if self.kind == "plain":
            return pl.BlockSpec((br, bc), lambda *g: (rfn(*g), cfn(*g)))
        none = (None,) * (1 + len(self.lead))
        if self.kind == "cols":
            per = self.cunit // bc
            return pl.BlockSpec(none + (br, bc), lambda *g: (self.g0 + _div(cfn(*g), per), *self.lead, rfn(*g), _rem(cfn(*g), per)))
        per = self.runit // br
        return pl.BlockSpec(none + (br, bc), lambda *g: (self.g0 + _div(rfn(*g), per), *self.lead, _rem(rfn(*g), per), cfn(*g)))


def _as_view(a):
    return a if isinstance(a, View) else View(a)


def matmul(a, b, *, name, ta=False, tb=False, out_dtype=F32, residual=None, out_chips=None, host=None, dswiglu=None,
           tm=1408, tn=1408, tk=2816):
    a, b = _as_view(a), _as_view(b)
    (k, m) = a.shape if ta else a.shape[::-1]
    (n, kb) = b.shape if tb else b.shape[::-1]
    assert k == kb, (a.shape, b.shape, ta, tb)
    m_unit = a.cunit if ta else a.runit
    ka_unit = a.runit if ta else a.cunit
    n_unit = b.runit if tb else b.cunit
    kb_unit = b.cunit if tb else b.runit
    if out_chips is not None:
        n_unit = min(n_unit, n // out_chips)
    tm, tn = _tile(min(m, m_unit), tm), _tile(min(n, n_unit), tn)
    tk = _tile(min(k, ka_unit, kb_unit), tk)
    assert ka_unit % tk == 0 and kb_unit % tk == 0, (ka_unit, kb_unit, tk)
    nk = k // tk
    ca, cb = (0 if ta else 1), (1 if tb else 0)

    def body(a_ref, b_ref, *refs):
        r_ref = refs[0] if residual is not None else None
        o_ref = refs[-1] if nk == 1 else refs[-2]

        def finish(r):
            if residual is not None:
                r = r + r_ref[...]
            if dswiglu is None:
                o_ref[...] = r.astype(out_dtype)
                return
            g = refs[0][...].astype(F32)
            sg = _sigmoid(g)
            o_ref[0] = (r * refs[1][...].astype(F32) * sg * (1.0 + g * (1.0 - sg))).astype(BF16)
            o_ref[1] = (r * g * sg).astype(BF16)

        part = _dot(a_ref[...].astype(BF16), b_ref[...].astype(BF16), ca, cb)
        if nk == 1:
            finish(part)
            return
        acc = refs[-1]
        kk = pl.program_id(2)

        @pl.when(kk == 0)
        def _():
            acc[...] = part

        @pl.when(kk > 0)
        def _():
            acc[...] += part

        @pl.when(kk == nk - 1)
        def _():
            finish(acc[...])

    gi, gj, gk = (lambda i, j, kk: i), (lambda i, j, kk: j), (lambda i, j, kk: kk)
    a_spec = a.spec(tk, tm, gk, gi) if ta else a.spec(tm, tk, gi, gk)
    b_spec = b.spec(tn, tk, gj, gk) if tb else b.spec(tk, tn, gk, gj)
    if out_chips is None:
        out = View(SDS((m, n), out_dtype))
    else:
        out = View(SDS((out_chips, m, n // out_chips), out_dtype), "cols")
    o_spec = out.spec(tm, tn, gi, gj)
    in_specs, args = [a_spec, b_spec], [a.arr, b.arr]
    if residual is not None:
        in_specs.append(pl.BlockSpec((tm, tn), lambda i, j, kk: (i, j)))
        args.append(residual)
    if dswiglu is not None:
        assert residual is None and out_chips is None and dswiglu.shape == (2, m, n)
        in_specs += [pl.BlockSpec((None, tm, tn), lambda i, j, kk: (0, i, j)), pl.BlockSpec((None, tm, tn), lambda i, j, kk: (1, i, j))]
        args += [dswiglu, dswiglu]
        out = View(SDS((2, m, n), BF16))
        o_spec = pl.BlockSpec((2, tm, tn), lambda i, j, kk: (0, i, j))
    res = _hosted_call(
        body, host, name=name, grid=(m // tm, n // tn, nk), in_specs=in_specs, out_specs=[o_spec], out_shape=[out.arr],
        scratch_shapes=[] if nk == 1 else [pltpu.VMEM((tm, tn), F32)], sem=("parallel", "parallel", "arbitrary"), args=args)
    return res[0] if host is None else (res[0][0], res[1])


def gate_up_swiglu(f, w_gu, *, name, host=None, tm=512, tn=1408):
    m, k = f.shape
    n = w_gu.shape[1] // 2
    assert w_gu.shape[0] == k and w_gu.runit == k
    tm, tn = _tile(m, tm), _tile(min(n, w_gu.cunit), tn)
    nf = n // tn

    def body(a_ref, bg_ref, bu_ref, gu_ref, act_ref):
        a = a_ref[...]
        pg, pu = _dot(a, bg_ref[...]), _dot(a, bu_ref[...])
        gu_ref[0] = pg.astype(BF16)
        gu_ref[1] = pu.astype(BF16)
        act_ref[...] = (pg * _sigmoid(pg) * pu).astype(BF16)

    rows, cols, zero = (lambda j, i: i), (lambda j, i: j), (lambda j, i: 0)
    res = _hosted_call(
        body, host, name=name, grid=(nf, m // tm),
        in_specs=[pl.BlockSpec((tm, k), lambda j, i: (i, 0)), w_gu.spec(k, tn, zero, cols), w_gu.spec(k, tn, zero, lambda j, i: j + nf)],
        out_specs=[pl.BlockSpec((2, tm, tn), lambda j, i: (0, i, j)), pl.BlockSpec((tm, tn), lambda j, i: (i, j))],
        out_shape=[SDS((2, m, n), BF16), SDS((m, n), BF16)], scratch_shapes=[], sem=("parallel", "parallel"),
        args=(f, w_gu.arr, w_gu.arr))
    return (res[0], res[1], None) if host is None else (res[0][0], res[0][1], res[1])


def rmsnorm_fwd(x, w, *, name, tr=512):
    s, d = x.shape

    def body(x_ref, w_ref, o_ref):
        xf = x_ref[...]
        r = lax.rsqrt(jnp.mean(xf * xf, axis=-1, keepdims=True) + EPS)
        o_ref[...] = (xf * r * w_ref[...]).astype(BF16)

    row = pl.BlockSpec((tr, d), lambda i: (i, 0))
    return pl.pallas_call(
        body, name=name, grid=(s // tr,), in_specs=[row, pl.BlockSpec((1, d), lambda i: (0, 0))], out_specs=row,
        out_shape=SDS((s, d), BF16), compiler_params=_cp("parallel"))(x, w.reshape(1, d))


def rmsnorm_bwd(dy, x, w, dres, *, name, tr=512):
    s, d = x.shape

    def body(dy_ref, x_ref, w_ref, dres_ref, dx_ref, dx16_ref, dw_ref):
        i = pl.program_id(0)
        xf = x_ref[...]
        r = lax.rsqrt(jnp.mean(xf * xf, axis=-1, keepdims=True) + EPS)
        xh = xf * r
        dyf = dy_ref[...].astype(F32)
        dxh = dyf * w_ref[...]
        dx = dres_ref[...] + r * (dxh - xh * jnp.mean(dxh * xh, axis=-1, keepdims=True))
        dx_ref[...] = dx
        dx16_ref[...] = dx.astype(BF16)
        part = jnp.sum(dyf * xh, axis=0, keepdims=True)

        @pl.when(i == 0)
        def _():
            dw_ref[...] = part

        @pl.when(i > 0)
        def _():
            dw_ref[...] += part

    row = pl.BlockSpec((tr, d), lambda i: (i, 0))
    vec = pl.BlockSpec((1, d), lambda i: (0, 0))
    return pl.pallas_call(
        body, name=name, grid=(s // tr,), in_specs=[row, row, vec, row], out_specs=[row, row, vec],
        out_shape=[SDS((s, d), F32), SDS((s, d), BF16), SDS((1, d), F32)],
        compiler_params=_cp("arbitrary"))(dy, x, w.reshape(1, d), dres)


def final_loss(h, w, target, *, name, tr=512):
    s, d = h.shape

    def body(h_ref, w_ref, t_ref, dh_ref, dh16_ref, dw_ref, loss_ref):
        i = pl.program_id(0)
        xf = h_ref[...]
        r = lax.rsqrt(jnp.mean(xf * xf, axis=-1, keepdims=True) + EPS)
        xh = xf * r
        err = xh * w_ref[...] - t_ref[...]
        lpart = 0.5 * jnp.sum(jnp.sum(err * err, axis=-1, keepdims=True) * (1.0 / d), axis=0, keepdims=True)
        dy = err * (1.0 / d)
        dxh = dy * w_ref[...]
        dh = r * (dxh - xh * jnp.mean(dxh * xh, axis=-1, keepdims=True))
        dh_ref[...] = dh
        dh16_ref[...] = dh.astype(BF16)
        part = jnp.sum(dy * xh, axis=0, keepdims=True)
        lrow = jnp.broadcast_to(lpart, (1, LANES))

        @pl.when(i == 0)
        def _():
            dw_ref[...] = part
            loss_ref[...] = lrow

        @pl.when(i > 0)
        def _():
            dw_ref[...] += part
            loss_ref[...] += lrow

    row = pl.BlockSpec((tr, d), lambda i: (i, 0))
    vec = pl.BlockSpec((1, d), lambda i: (0, 0))
    return pl.pallas_call(
        body, name=name, grid=(s // tr,), in_specs=[row, vec, row],
        out_specs=[row, row, vec, pl.BlockSpec((1, LANES), lambda i: (0, 0))],
        out_shape=[SDS((s, d), F32), SDS((s, d), BF16), SDS((1, d), F32), SDS((1, LANES), F32)],
        compiler_params=_cp("arbitrary"))(h, w.reshape(1, d), target)


def _gate_z(gl_ref, w_ref, b_ref):
    glb = gl_ref[...].astype(BF16)
    return glb, _dot(glb, w_ref[...]) + b_ref[...]


def gate_fwd(proj, wg, bg, *, name, tr=512):
    s, inw = proj.shape
    qk = wg.shape[1]
    glc = inw // LANES - 1

    def body(gl_ref, w_ref, b_ref, g_ref):
        _, z = _gate_z(gl_ref, w_ref, b_ref)
        g_ref[...] = (jnp.minimum(z, 0.0) - jnp.log(1.0 + jnp.exp(-jnp.abs(z)))) * (1.0 / GLA_GATE_TAU)

    return pl.pallas_call(
        body, name=name, grid=(s // tr,),
        in_specs=[pl.BlockSpec((tr, LANES), lambda i: (i, glc)), pl.BlockSpec((LANES, qk), lambda i: (0, 0)),
                  pl.BlockSpec((1, qk), lambda i: (0, 0))],
        out_specs=pl.BlockSpec((tr, qk), lambda i: (i, 0)), out_shape=SDS((s, qk), F32),
        compiler_params=_cp("parallel"))(proj, wg, bg)


def gate_bwd(dg, proj, wg, bg, *, name, tr=512):
    s, inw = proj.shape
    qk = wg.shape[1]
    glc = inw // LANES - 1

    def body(dg_ref, gl_ref, w_ref, b_ref, dgl_ref, dw_ref, db_ref):
        i = pl.program_id(0)
        glb, z = _gate_z(gl_ref, w_ref, b_ref)
        dz = dg_ref[...] * (1.0 / (1.0 + jnp.exp(z))) * (1.0 / GLA_GATE_TAU)
        dzb = dz.astype(BF16)
        dgl_ref[...] = _dot(dzb, w_ref[...], 1, 1).astype(BF16)
        pw = _dot(glb, dzb, 0, 0)
        pb = jnp.sum(dz, axis=0, keepdims=True)

        @pl.when(i == 0)
        def _():
            dw_ref[...] = pw
            db_ref[...] = pb

        @pl.when(i > 0)
        def _():
            dw_ref[...] += pw
            db_ref[...] += pb

    return pl.pallas_call(
        body, name=name, grid=(s // tr,),
        in_specs=[pl.BlockSpec((tr, qk), lambda i: (i, 0)), pl.BlockSpec((tr, LANES), lambda i: (i, glc)),
                  pl.BlockSpec((LANES, qk), lambda i: (0, 0)), pl.BlockSpec((1, qk), lambda i: (0, 0))],
        out_specs=[pl.BlockSpec((tr, LANES), lambda i: (i, 0)), pl.BlockSpec((LANES, qk), lambda i: (0, 0)),
                   pl.BlockSpec((1, qk), lambda i: (0, 0))],
        out_shape=[SDS((s, LANES), BF16), SDS((LANES, qk), F32), SDS((1, qk), F32)],
        compiler_params=_cp("arbitrary"))(dg, proj, wg, bg)


def _gla_chunk_terms(q_ref, k_ref, g_ref, c, scale):
    q = q_ref[...] * scale
    k = k_ref[...]
    gg = g_ref[...]
    b = _split_dot(_tri(c, "le"), gg)
    row = lax.broadcasted_iota(jnp.int32, gg.shape, 0)
    bm = jnp.sum(jnp.where(row < c // 2, gg, 0.0), axis=0, keepdims=True)
    bl = jnp.sum(gg, axis=0, keepdims=True)
    eb, em, emi, el = jnp.exp(b), jnp.exp(b - bm), jnp.exp(bm - b), jnp.exp(bl - b)
    return q, k, bl, eb, em, emi, el


def _causal(a):
    r = lax.broadcasted_iota(jnp.int32, a.shape, 0)
    c = lax.broadcasted_iota(jnp.int32, a.shape, 1)
    return jnp.where(r >= c, a, 0.0)


def gla_fwd(proj, g, *, name, host=None, c=GLA_CHUNK):
    s = proj.shape[0]
    qk = g.shape[1]
    dk, dv = qk // GLA_HEADS, 2 * qk // GLA_HEADS
    nc = s // c
    scale = dk ** -0.5
    kq = qk // dk

    def body(q_ref, k_ref, v_ref, g_ref, o_ref, st_ref, state):
        @pl.when(pl.program_id(1) == 0)
        def _():
            state[...] = jnp.zeros_like(state)

        q, k, bl, eb, em, emi, el = _gla_chunk_terms(q_ref, k_ref, g_ref, c, scale)
        v2 = _hilo(v_ref[...])
        st = state[...]
        st_ref[...] = st
        a = _causal(_dot3(_hilo(q * em), _hilo(k * emi), 1, 1))
        o_ref[...] = _dot3(_hilo(q * eb), _hilo(st), 1, 1) + _dot3(_hilo(a), v2)
        state[...] = st * jnp.exp(bl) + _dot3(v2, _hilo(k * el), 0, 0)

    return _hosted_call(
        body, host, name=name, grid=(GLA_HEADS, nc),
        in_specs=[pl.BlockSpec((c, dk), lambda h, i: (i, h)), pl.BlockSpec((c, dk), lambda h, i: (i, kq + h)),
                  pl.BlockSpec((c, dv), lambda h, i: (i, kq + h)), pl.BlockSpec((c, dk), lambda h, i: (i, h))],
        out_specs=[pl.BlockSpec((c, dv), lambda h, i: (i, h)),
                   pl.BlockSpec((None, None, dv, dk), lambda h, i: (h, i, 0, 0))],
        out_shape=[SDS((s, 2 * qk), F32), SDS((GLA_HEADS, nc, dv, dk), F32)],
        scratch_shapes=[pltpu.VMEM((dv, dk), F32)], sem=("parallel", "arbitrary"), args=(proj, proj, proj, g))


def gla_bwd(proj, g, states, do, *, name, host=None, c=GLA_CHUNK):
    s = proj.shape[0]
    qk = g.shape[1]
    dk, dv = qk // GLA_HEADS, 2 * qk // GLA_HEADS
    nc = s // c
    scale = dk ** -0.5
    kq = qk // dk

    def body(q_ref, k_ref, v_ref, g_ref, do_ref, st_ref, dq_ref, dk_ref, dv_ref, dg_ref, dstate, dgc):
        @pl.when(pl.program_id(1) == 0)
        def _():
            dstate[...] = jnp.zeros_like(dstate)
            dgc[...] = jnp.zeros_like(dgc)

        q, k, bl, eb, em, emi, el = _gla_chunk_terms(q_ref, k_ref, g_ref, c, scale)
        v2, do2 = _hilo(v_ref[...]), _hilo(do_ref[...])
        qe, qm, km, kd = _hilo(q * eb), _hilo(q * em), _hilo(k * emi), _hilo(k * el)
        ds = dstate[...]
        ds2 = _hilo(ds)
        a = _hilo(_causal(_dot3(qm, km, 1, 1)))
        dv_ref[...] = (_dot3(a, do2, 0, 0) + _dot3(kd, ds2, 1, 1)).astype(BF16)
        da = _hilo(_causal(_dot3(do2, v2, 1, 1)))
        dq = _dot3(da, km) * em + _dot3(do2, _hilo(st_ref[...])) * eb
        dkk = _dot3(da, qm, 0, 0) * emi + _dot3(v2, ds2) * el
        dstate[...] = ds * jnp.exp(bl) + _dot3(do2, qe, 0, 0)
        db = q * dq - k * dkk
        dg_ref[...] = _split_dot(_tri(c, "ge"), db) + dgc[...]
        dgc[...] += jnp.sum(db, axis=0, keepdims=True)
        dq_ref[...] = (dq * scale).astype(BF16)
        dk_ref[...] = dkk.astype(BF16)

    rev = lambda i: nc - 1 - i
    qspec = pl.BlockSpec((c, dk), lambda h, i: (rev(i), h))
    vspec = pl.BlockSpec((c, dv), lambda h, i: (rev(i), h))
    return _hosted_call(
        body, host, name=name, grid=(GLA_HEADS, nc),
        in_specs=[qspec, pl.BlockSpec((c, dk), lambda h, i: (rev(i), kq + h)),
                  pl.BlockSpec((c, dv), lambda h, i: (rev(i), kq + h)), qspec, vspec,
                  pl.BlockSpec((None, None, dv, dk), lambda h, i: (h, rev(i), 0, 0))],
        out_specs=[qspec, qspec, vspec, qspec],
        out_shape=[SDS((s, qk), BF16), SDS((s, qk), BF16), SDS((s, 2 * qk), BF16), SDS((s, qk), F32)],
        scratch_shapes=[pltpu.VMEM((dv, dk), F32), pltpu.VMEM((1, dk), F32)], sem=("parallel", "arbitrary"),
        args=(proj, proj, proj, g, do, states))


def gnorm_fwd(o, proj, gw, *, name, tr=512):
    s, v = o.shape
    dv = v // GLA_HEADS
    roff = 2 * GLA_HEADS

    def body(o_ref, r_ref, w_ref, y_ref):
        of = o_ref[...]
        rs = lax.rsqrt(jnp.mean(of * of, axis=-1, keepdims=True) + EPS)
        r = r_ref[...]
        y_ref[...] = (of * rs * w_ref[...] * (r * _sigmoid(r))).astype(BF16)

    blk = pl.BlockSpec((tr, dv), lambda i, h: (i, h))
    return pl.pallas_call(
        body, name=name, grid=(s // tr, GLA_HEADS),
        in_specs=[blk, pl.BlockSpec((tr, dv), lambda i, h: (i, roff + h)), pl.BlockSpec((1, dv), lambda i, h: (0, 0))],
        out_specs=blk, out_shape=SDS((s, v), BF16), compiler_params=_cp("parallel", "parallel"))(o, proj, gw)


def gnorm_bwd(dy, o, proj, gw, *, name, tr=512):
    s, v = o.shape
    dv = v // GLA_HEADS
    roff = 2 * GLA_HEADS

    def body(dy_ref, o_ref, r_ref, w_ref, do_ref, dr_ref, dw_ref):
        first = jnp.logical_and(pl.program_id(0) == 0, pl.program_id(1) == 0)
        of = o_ref[...]
        rs = lax.rsqrt(jnp.mean(of * of, axis=-1, keepdims=True) + EPS)
        n = of * rs
        r = r_ref[...]
        sg = _sigmoid(r)
        dyf = dy_ref[...].astype(F32)
        dn_w = dyf * (r * sg)
        dr_ref[...] = (dyf * n * w_ref[...] * sg * (1.0 + r * (1.0 - sg))).astype(BF16)
        dn = dn_w * w_ref[...]
        do_ref[...] = rs * (dn - n * jnp.mean(dn * n, axis=-1, keepdims=True))
        part = jnp.sum(dn_w * n, axis=0, keepdims=True)

        @pl.when(first)
        def _():
            dw_ref[...] = part

        @pl.when(jnp.logical_not(first))
        def _():
            dw_ref[...] += part

    blk = pl.BlockSpec((tr, dv), lambda i, h: (i, h))
    vec = pl.BlockSpec((1, dv), lambda i, h: (0, 0))
    return pl.pallas_call(
        body, name=name, grid=(s // tr, GLA_HEADS),
        in_specs=[blk, blk, pl.BlockSpec((tr, dv), lambda i, h: (i, roff + h)), vec],
        out_specs=[blk, blk, vec], out_shape=[SDS((s, v), F32), SDS((s, v), BF16), SDS((1, dv), F32)],
        compiler_params=_cp("arbitrary", "arbitrary"))(dy, o, proj, gw)


def _sb_block(kblk, q, ks, q0, scale, carry, masked):
    tk, tq = kblk.shape[0], q.shape[0]
    z = _dot(kblk, q, 1, 1) * scale
    sp = jnp.maximum(z, 0.0) + jnp.log(1.0 + jnp.exp(-jnp.abs(z)))
    mask = None
    lf = -sp
    if masked:
        kpos = ks + lax.broadcasted_iota(jnp.int32, (tk, tq), 0)
        qpos = q0 + lax.broadcasted_iota(jnp.int32, (tk, tq), 1)
        mask = kpos < qpos
        lf = jnp.where(mask, lf, 0.0)
    later = _split_dot(_tri(tk, "gt"), lf)
    a = jnp.exp(z - sp + later + carry)
    if masked:
        a = jnp.where(mask, a, 0.0)
    return z, sp, mask, lf, a


def sb_fwd(q, kv, *, name, host=None, tq=SB_TQ, tk=SB_TK, group=SB_GROUP):
    s, w = q.shape
    hd = w // SB_HEADS
    nq, nkb = s // tq, s // tk
    scale = hd ** -0.5
    per = tq // tk
    assert per % group == 0

    def body(q_ref, k_ref, v_ref, o_ref, car_ref, o_acc):
        qi = pl.program_id(1)
        qb = q_ref[...]
        q0 = qi * tq
        car_ref[...] = jnp.full(car_ref.shape, SB_SKIPPED, F32)
        o_acc[...] = jnp.zeros_like(o_acc)

        def blocks(first, carry, skip=None):
            for t in reversed(range(group)):
                kj = first + t
                ks = pl.multiple_of(kj * tk, tk)
                car_ref[pl.ds(kj, 1), :] = carry
                lo = 0 if skip is None else (skip + t) * tk
                _, _, _, lf, a = _sb_block(k_ref[pl.ds(ks, tk), :], qb[lo:, :], ks, q0 + lo, scale, carry[:, lo:], skip is not None)
                o_acc[lo:, :] += _dot(a.astype(BF16), v_ref[pl.ds(ks, tk), :], 0, 0)
                add = jnp.sum(lf, axis=0, keepdims=True)
                carry = carry + (add if lo == 0 else jnp.concatenate([jnp.zeros((1, lo), F32), add], axis=1))
            return carry

        carry = jnp.zeros((1, tq), F32)
        for gidx in reversed(range(per // group)):
            carry = blocks(qi * per + gidx * group, carry, gidx * group)
        n_in = qi * (per // group)
        lax.while_loop(lambda st: jnp.logical_and(st[0] < n_in, jnp.max(st[1]) > SB_DEAD),
                       lambda st: (st[0] + 1, blocks((n_in - 1 - st[0]) * group, st[1])), (jnp.int32(0), carry))
        o_ref[...] = o_acc[...].astype(BF16)

    qspec = pl.BlockSpec((tq, hd), lambda h, i: (i, h))
    return _hosted_call(
        body, host, name=name, grid=(SB_HEADS, nq),
        in_specs=[qspec, pl.BlockSpec((None, s, hd), lambda h, i: (0, 0, h)), pl.BlockSpec((None, s, hd), lambda h, i: (1, 0, h))],
        out_specs=[qspec, pl.BlockSpec((None, None, nkb, tq), lambda h, i: (h, i, 0, 0))],
        out_shape=[SDS((s, w), BF16), SDS((SB_HEADS, nq, nkb, tq), F32)],
        scratch_shapes=[pltpu.VMEM((tq, hd), F32)], sem=("parallel", "parallel"), args=(q, kv, kv))


def sb_bwd(q, kv, do, car, *, name, host=None, tq=SB_TQ, tk=SB_TK, group=SB_GROUP):
    s, w = q.shape
    hd = w // SB_HEADS
    nq, nkb = s // tq, s // tk
    scale = hd ** -0.5
    per = tq // tk
    assert per % group == 0

    def body(q_ref, k_ref, v_ref, do_ref, car_ref, dq_ref, dkv_ref, dq_acc, dk_acc, dv_acc):
        qi = pl.program_id(1)
        qb = q_ref[...]
        dob = do_ref[...]
        q0 = qi * tq
        dq_acc[...] = jnp.zeros_like(dq_acc)

        @pl.when(qi == 0)
        def _():
            dk_acc[...] = jnp.zeros_like(dk_acc)
            dv_acc[...] = jnp.zeros_like(dv_acc)

        def blocks(first, pcar, skip=None):
            for t in range(group):
                kj = first + t
                ks = pl.multiple_of(kj * tk, tk)
                kblk = k_ref[pl.ds(ks, tk), :]
                lo = 0 if skip is None else (skip + t) * tk
                masked = skip is not None
                qs, dos = qb[lo:, :], dob[lo:, :]
                z, sp, mask, _, a = _sb_block(kblk, qs, ks, q0 + lo, scale, car_ref[pl.ds(kj, 1), :][:, lo:], masked)
                p = a * _dot(v_ref[pl.ds(ks, tk), :], dos, 1, 1)
                before = _split_dot(_tri(tk, "lt"), p)
                sg = jnp.exp(z - sp)
                dz = p * (1.0 - sg) - (pcar[:, lo:] + before) * sg
                if masked:
                    dz = jnp.where(mask, dz, 0.0)
                dz = (dz * scale).astype(BF16)
                dk_acc[pl.ds(ks, tk), :] += _dot(dz, qs)
                dv_acc[pl.ds(ks, tk), :] += _dot(a.astype(BF16), dos)
                dq_acc[lo:, :] += _dot(dz, kblk, 0, 0)
                add = jnp.sum(p, axis=0, keepdims=True)
                pcar = pcar + (add if lo == 0 else jnp.concatenate([jnp.zeros((1, lo), F32), add], axis=1))
            return pcar

        n_in = qi * (per // group)

        def reached(g):
            return (jnp.max(car_ref[pl.ds(g * group + group - 1, 1), :]) > SB_DEAD).astype(jnp.int32)

        start = n_in - lax.fori_loop(0, n_in, lambda g, n: n + reached(g), jnp.int32(0))
        pcar = lax.fori_loop(start, n_in, lambda i, c: blocks(i * group, c), jnp.zeros((1, tq), F32))
        for gidx in range(per // group):
            pcar = blocks(qi * per + gidx * group, pcar, gidx * group)
        dq_ref[...] = dq_acc[...].astype(BF16)

        @pl.when(qi == nq - 1)
        def _():
            dkv_ref[0] = dk_acc[...].astype(BF16)
            dkv_ref[1] = dv_acc[...].astype(BF16)

    qspec = pl.BlockSpec((tq, hd), lambda h, i: (i, h))
    return _hosted_call(
        body, host, name=name, grid=(SB_HEADS, nq),
        in_specs=[qspec, pl.BlockSpec((None, s, hd), lambda h, i: (0, 0, h)), pl.BlockSpec((None, s, hd), lambda h, i: (1, 0, h)),
                  qspec, pl.BlockSpec((None, None, nkb, tq), lambda h, i: (h, i, 0, 0))],
        out_specs=[qspec, pl.BlockSpec((2, s, hd), lambda h, i: (0, 0, h))],
        out_shape=[SDS((s, w), BF16), SDS((2, s, w), BF16)],
        scratch_shapes=[pltpu.VMEM((tq, hd), F32), pltpu.VMEM((s, hd), F32), pltpu.VMEM((s, hd), F32)],
        sem=("parallel", "arbitrary"), args=(q, kv, kv, do, car))


def adamw(w, g, m, v, *, name):
    shape = w.shape
    c = shape[-1]
    r = w.size // c
    tr = _tile(r, max(8, (3 * LANES * 1024) // c), unit=8) if r >= 8 else r

    def body(w_ref, g_ref, m_ref, v_ref, d_ref, nm_ref, nv_ref):
        gf = g_ref[...]
        mn = ADAM_B1 * m_ref[...] + (1.0 - ADAM_B1) * gf
        vn = ADAM_B2 * v_ref[...] + (1.0 - ADAM_B2) * (gf * gf)
        m_hat = mn / (1.0 - ADAM_B1 ** ADAM_STEP)
        v_hat = vn / (1.0 - ADAM_B2 ** ADAM_STEP)
        d_ref[...] = -ADAM_LR * (m_hat / (jnp.sqrt(v_hat) + ADAM_EPS) + ADAM_WD * w_ref[...])
        nm_ref[...] = mn
        nv_ref[...] = vn

    blk = pl.BlockSpec((tr, c), lambda i: (i, 0))
    outs = pl.pallas_call(
        body, name=name, grid=(r // tr,), in_specs=[blk] * 4, out_specs=[blk] * 3,
        out_shape=[SDS((r, c), F32)] * 3, compiler_params=_cp("parallel"))(
            *(t.reshape(r, c) for t in (w, g, m, v)))
    return tuple(o.reshape(shape) for o in outs)


def _row_tile(r, c):
    return _tile(r, max(16, (4 * LANES * 1024) // c), unit=16)


def _halves(a):
    return a.reshape(a.shape[:-2] + (2, a.shape[-2] // 2, a.shape[-1]))


def place_shard(w, chip, *, name, layer=None):
    r, c = w.shape[-2:]
    tr = _row_tile(r, c)

    def body(c_ref, w_ref, o_ref):
        o_ref[...] = w_ref[...].astype(BF16)

    if layer is None:
        w_spec = pl.BlockSpec((None, tr, c), lambda h, i, c_ref: (h, i, 0))
    else:
        w_spec = pl.BlockSpec((None, None, tr, c), lambda h, i, c_ref: (layer, h, i, 0))
    return pl.pallas_call(
        body, name=name, out_shape=SDS((N_CHIPS, 2, r, c), BF16),
        grid_spec=pltpu.PrefetchScalarGridSpec(
            num_scalar_prefetch=1, grid=(2, r // tr), in_specs=[w_spec],
            out_specs=pl.BlockSpec((None, None, tr, c), lambda h, i, c_ref: (c_ref[0], h, i, 0))),
        compiler_params=_cp("parallel", "parallel"))(chip, w)


def sibling_exchange(gs, *, name):
    n = len(gs)

    def body(*refs):
        g_refs, a_refs = refs[:n], refs[n:2 * n]
        send_sems, recv_sems = refs[2 * n:]
        x, y, c = _coords()
        cps = [_remote(g_refs[i].at[:, 1 - c], a_refs[i], send_sems.at[i], recv_sems.at[i], (x, y, 1 - c)) for i in range(n)]
        for cp in cps:
            cp.start()
        for cp in cps:
            cp.wait()

    return pl.pallas_call(
        body, name=name, in_specs=[ANY] * n, out_specs=[ANY] * n,
        out_shape=[SDS(g.shape[:1] + g.shape[2:], g.dtype) for g in gs],
        scratch_shapes=[_dma_sems(n), _dma_sems(n)])(*gs)


def half_add(g, a, core, *, name):
    n, _, r, c = g.shape
    tr = _row_tile(r, c)

    def body(c_ref, g_ref, a_ref, o_ref):
        o_ref[...] = (g_ref[...].astype(F32) + a_ref[...].astype(F32)).astype(o_ref.dtype)

    blk = pl.BlockSpec((None, tr, c), lambda s, i, c_ref: (s, i, 0))
    return pl.pallas_call(
        body, name=name, out_shape=SDS((n, r, c), g.dtype),
        grid_spec=pltpu.PrefetchScalarGridSpec(
            num_scalar_prefetch=1, grid=(n, r // tr),
            in_specs=[pl.BlockSpec((None, None, tr, c), lambda s, i, c_ref: (s, c_ref[0], i, 0)), blk], out_specs=blk),
        compiler_params=_cp("parallel", "parallel"))(core, g, a)


def chip_sum(p, b, chip, core, *, name, layer=None, into=None):
    _, r, c = p.shape
    tr = _row_tile(r, c)

    def body(chip_ref, core_ref, p_ref, b_ref, *refs):
        t = p_ref[...].astype(F32)
        for k in range(N_CHIPS - 1):
            t = t + b_ref[k].astype(F32)
        refs[-1][...] = t

    in_specs = [pl.BlockSpec((None, tr, c), lambda i, chip_ref, core_ref: (chip_ref[0], i, 0)),
                pl.BlockSpec((N_CHIPS - 1, tr, c), lambda i, chip_ref, core_ref: (0, i, 0))]
    if layer is None:
        shape, o_spec = (2, r, c), pl.BlockSpec((None, tr, c), lambda i, chip_ref, core_ref: (core_ref[0], i, 0))
    else:
        shape = (2, 2, r, c)
        o_spec = pl.BlockSpec((None, None, tr, c), lambda i, chip_ref, core_ref: (layer, core_ref[0], i, 0))
    args, aliases = (chip, core, p, b), {}
    if into is not None:
        in_specs, args, aliases = in_specs + [ANY], args + (into,), {4: 0}
    return pl.pallas_call(
        body, name=name, out_shape=SDS(shape, F32),
        grid_spec=pltpu.PrefetchScalarGridSpec(num_scalar_prefetch=2, grid=(r // tr,), in_specs=in_specs, out_specs=o_spec),
        input_output_aliases=aliases, compiler_params=_cp("parallel"))(*args)


def sibling_gather(ts, *, name):
    n = len(ts)

    def body(*refs):
        out_refs = refs[n:2 * n]
        send_sems, recv_sems = refs[2 * n:]
        x, y, c = _coords()

        def half(i, which):
            return out_refs[i].at[which] if len(ts[i].shape) == 3 else out_refs[i].at[:, which]

        cps = [_remote(half(i, c), half(i, c), send_sems.at[i], recv_sems.at[i], (x, y, 1 - c)) for i in range(n)]
        for cp in cps:
            cp.start()
        for i in range(n):
            _remote(half(i, c), half(i, 1 - c), send_sems.at[i], recv_sems.at[i], (x, y, 1 - c)).wait_recv()
        for cp in cps:
            cp.wait_send()

    return pl.pallas_call(
        body, name=name, in_specs=[ANY] * n, out_specs=[ANY] * n, out_shape=[SDS(t.shape, t.dtype) for t in ts],
        input_output_aliases={i: i for i in range(n)}, scratch_shapes=[_dma_sems(n), _dma_sems(n)])(*ts)


def allgather_all(sm, *, name):
    r, w = sm.shape

    def body(s_ref, out_ref, send_sems, recv_sems, local_sem):
        x, y, c = _coords()
        me = 4 * x + 2 * y + c

        def peer(rel):
            flip = lambda v, bit: 1 - v if bit else v
            return flip(x, rel & 4), flip(y, rel & 2), flip(c, rel & 1)

        loc = pltpu.make_async_copy(s_ref, out_ref.at[me], local_sem.at[0])
        loc.start()
        cps = [_remote(s_ref, out_ref.at[me], send_sems.at[rel - 1], recv_sems.at[rel - 1], peer(rel)) for rel in range(1, N_DEV)]
        for cp in cps:
            cp.start()
        for rel in range(1, N_DEV):
            px, py, pc = peer(rel)
            _remote(s_ref, out_ref.at[4 * px + 2 * py + pc], send_sems.at[rel - 1], recv_sems.at[rel - 1], (px, py, pc)).wait_recv()
        for cp in cps:
            cp.wait_send()
        loc.wait()

    return pl.pallas_call(
        body, name=name, in_specs=[ANY], out_specs=ANY, out_shape=SDS((N_DEV, r, w), sm.dtype),
        scratch_shapes=[_dma_sems(N_DEV - 1), _dma_sems(N_DEV - 1), _dma_sems(1)])(sm)


def sum_blocks(a, *, name):
    n, r, w = a.shape

    def body(a_ref, o_ref):
        t = a_ref[0]
        for k in range(1, n):
            t = t + a_ref[k]
        o_ref[...] = t

    return pl.pallas_call(body, name=name, out_shape=SDS((r, w), F32))(a)


class Exchanges:
    def __init__(self, placed=None, chip=None, core=None):
        self.placed, self.chip, self.core = placed, chip, core
        self.mine = {}

    def gather(self, names):
        return None if self.placed is None else GatherX([self.placed[n] for n in names])

    def reduce(self, grads, call):
        if self.placed is None:
            return call(None)
        names = list(grads)
        from_sibling = sibling_exchange([grads[n] for n in names], name="grads_to_sibling_" + names[0])
        pairs = [half_add(grads[n], a, self.core, name=f"pair_sum_{n}") for n, a in zip(names, from_sibling)]
        result, from_chips = call(ScatterX(pairs))
        for n, t, b in zip(names, pairs, from_chips):
            base, layer = (n[:-1], int(n[-1])) if n[:-1] in ("w_gu", "w_dn") else (n, None)
            self.mine[base] = chip_sum(t, b, self.chip, self.core, name=f"chip_sum_{n}", layer=layer, into=self.mine.get(base))
        return result


def _usable(name, buf, d):
    if name == "w_in":
        cols = N_CHIPS * buf.shape[-1]
        full = buf.reshape(N_CHIPS, d, -1).transpose(1, 0, 2).reshape(d, cols)
        return jnp.pad(full, ((0, 0), (0, -cols % LANES)))
    if name in ("w_out", "w_q", "w_so"):
        return buf.reshape(-1, buf.shape[-1])
    return buf.reshape(N_CHIPS, -1, buf.shape[-1])


def _by_rows(t):
    return _halves(t.reshape(N_CHIPS, -1, t.shape[-1]))


def _ffn_fwd(h, nw, w_gu, w_dn, tag, host=None):
    f = rmsnorm_fwd(h, nw, name=f"{tag}_norm")
    gu, act, got = gate_up_swiglu(f, View(w_gu, "cols"), name=f"{tag}_gate_up", host=host)
    return matmul(act, View(w_dn, "rows"), name=f"{tag}_down", residual=h), (f, gu, act), got


def _ffn_bwd(dh, h, nw, w_gu, w_dn, saved, tag, ex, dn_name, carried):
    f, gu, act = saved
    dh, dh16 = dh
    dgu = View(matmul(dh16, View(w_dn, "rows"), name=f"{tag}_dgu", tb=True, dswiglu=gu), "cols")
    dw_dn = matmul(act, dh16, name=f"{tag}_dw_down", ta=True, out_dtype=BF16)

    def dw_gate_up(host):
        return matmul(f, dgu, name=f"{tag}_dw_gate_up", ta=True, out_dtype=BF16, out_chips=N_CHIPS, host=host, tk=SEQ_TILE)

    dw_gu = ex.reduce(carried, dw_gate_up) if carried else dw_gate_up(None)
    df = ex.reduce({dn_name: _by_rows(dw_dn)}, lambda host: matmul(dgu, View(w_gu, "cols"), name=f"{tag}_df", tb=True, host=host))
    dh_in, dh_in16, dnw = rmsnorm_bwd(df, h, nw, dh, name=f"{tag}_dnorm")
    return (dh_in, dh_in16), dnw, dw_gu, dw_dn


def local_step(x, target, p, ex):
    d = x.shape[1]
    hosted = ex.placed is not None
    w = {} if hosted else {n: _usable(n, p[n], d) for n in ("w_in", "w_out", "w_kv", "w_q", "w_so", "w_gu0", "w_gu1", "w_dn0", "w_dn1")}

    def take(names, got):
        for n, buf in zip(names, got or []):
            w[n] = _usable(n, buf, d)

    def carry(call, names):
        host = ex.gather(names)
        if host is None:
            return call(None)
        res, got = call(host)
        take(names, got)
        return res

    if hosted:
        take(["w_in"], run_exchange(ex.gather(["w_in"]), name="gather_gla_in"))
    a0 = rmsnorm_fwd(x, p["an0"], name="l0_attn_norm")
    proj = carry(lambda host: matmul(a0, w["w_in"], name="gla_in", host=host), ["w_out", "w_dn0"])
    g = gate_fwd(proj, p["wg"], p["bg"], name="gla_gate")
    o, states = carry(lambda host: gla_fwd(proj, g, name="gla_scan", host=host), ["w_gu0"])
    og = gnorm_fwd(o, proj, p["gw"], name="gla_outnorm")
    h1 = matmul(og, w["w_out"], name="gla_out", residual=x)
    sb_names = ["w_kv", "w_q", "w_so"]
    h2, ffn0, got = _ffn_fwd(h1, p["fn0"], w["w_gu0"], w["w_dn0"], "ffn0", host=ex.gather(sb_names))
    take(sb_names, got)
    w_kv = View(w["w_kv"], "cols")
    kvn = rmsnorm_fwd(h2, p["kvn"], name="kv_norm")
    kv = matmul(kvn, w_kv, name="sb_kv", out_dtype=BF16, out_chips=2)
    a1 = rmsnorm_fwd(h2, p["an1"], name="l1_attn_norm")
    q2 = matmul(a1, w["w_q"], name="sb_q", out_dtype=BF16)
    o2, car = carry(lambda host: sb_fwd(q2, kv, name="sb_attn", host=host), ["w_gu1", "w_dn1"])
    h3 = matmul(o2, w["w_so"], name="sb_out", residual=h2)
    h4, ffn1, _ = _ffn_fwd(h3, p["fn1"], w["w_gu1"], w["w_dn1"], "ffn1")
    dh4, dh4_16, d_fin, loss_row = final_loss(h4, p["finn"], target, name="final_loss")

    (dh3, dh3_16), d_fn1, dw_gu1, dw_dn1 = _ffn_bwd((dh4, dh4_16), h3, p["fn1"], w["w_gu1"], w["w_dn1"], ffn1, "ffn1", ex, "w_dn1", {})
    do2 = matmul(dh3_16, w["w_so"], name="sb_do", tb=True, out_dtype=BF16)
    dw_so = matmul(o2, dh3_16, name="sb_dw_out", ta=True, out_dtype=BF16, tk=SEQ_TILE)
    dq2, dkv = ex.reduce({"w_gu1": _halves(dw_gu1)}, lambda host: sb_bwd(q2, kv, do2, car, name="sb_attn_bwd", host=host))
    dkv = View(dkv, "cols")
    dw_q = matmul(a1, dq2, name="sb_dw_q", ta=True, out_dtype=BF16, tk=SEQ_TILE)
    da1 = matmul(dq2, w["w_q"], name="sb_da", tb=True)
    dh2, _, d_an1 = rmsnorm_bwd(da1, h2, p["an1"], dh3, name="l1_attn_dnorm")
    dw_kv = matmul(kvn, dkv, name="sb_dw_kv", ta=True, out_dtype=BF16, out_chips=N_CHIPS, tk=SEQ_TILE)
    dkvn = matmul(dkv, w_kv, name="sb_dkvn", tb=True)
    dh2, dh2_16, d_kvn = rmsnorm_bwd(dkvn, h2, p["kvn"], dh2, name="kv_dnorm")
    sb_grads = {"w_kv": _halves(dw_kv), "w_q": _by_rows(dw_q), "w_so": _by_rows(dw_so)}
    (dh1, dh1_16), d_fn0, dw_gu0, dw_dn0 = _ffn_bwd((dh2, dh2_16), h1, p["fn0"], w["w_gu0"], w["w_dn0"], ffn0, "ffn0", ex, "w_dn0", sb_grads)
    dog = matmul(dh1_16, w["w_out"], name="gla_dog", tb=True, out_dtype=BF16)
    dw_out = matmul(og, dh1_16, name="gla_dw_out", ta=True, out_dtype=BF16, tk=SEQ_TILE)
    do, dr, d_gw = gnorm_bwd(dog, o, proj, p["gw"], name="gla_outnorm_bwd")
    dq, dk, dv, dg = ex.reduce({"w_gu0": _halves(dw_gu0)}, lambda host: gla_bwd(proj, g, states, do, name="gla_scan_bwd", host=host))
    dgl, d_wg, d_bg = gate_bwd(dg, proj, p["wg"], p["bg"], name="gla_gate_bwd")
    dproj = jnp.concatenate([dq, dk, dv, dr, dgl], axis=1)
    dw_in = matmul(a0, dproj, name="gla_dw_in", ta=True, out_dtype=BF16, tk=SEQ_TILE)
    in_w = p["in_w"]
    gla_grads = {"w_in": _halves(dw_in[:, :in_w].reshape(d, N_CHIPS, -1).transpose(1, 0, 2)), "w_out": _by_rows(dw_out)}
    da0 = ex.reduce(gla_grads, lambda host: matmul(dproj, w["w_in"], name="gla_da", tb=True, host=host))
    dx, _, d_an0 = rmsnorm_bwd(da0, x, p["an0"], dh1, name="l0_attn_dnorm")

    small = dict(an0=d_an0, an1=d_an1, fn0=d_fn0, fn1=d_fn1, kvn=d_kvn, finn=d_fin, wg=d_wg, bg=d_bg, gw=d_gw)
    big = {}
    if not hosted:
        big = dict(gla_grads, **sb_grads, w_gu0=_halves(dw_gu0), w_gu1=_halves(dw_gu1), w_dn0=_by_rows(dw_dn0), w_dn1=_by_rows(dw_dn1))
    return loss_row, dx, small, big


def _pack_rows(parts):
    cat = jnp.concatenate([t.reshape(-1) for t in parts])
    rows = -(-cat.size // (LANES * SUBLANES)) * SUBLANES
    return jnp.pad(cat, (0, rows * LANES - cat.size)).reshape(rows, LANES)


def _segments(flat, sizes):
    out, off = [], 0
    for n in sizes:
        out.append(flat[..., off:off + n])
        off += n
    return out


def kernel(x, attn_norm_w, ffn_norm_w, gla_w_in, gla_w_gate_up, gla_b_gate, gla_gnorm_w, gla_w_out, kv_norm_w, sb_w_kv, sb_w_q, sb_w_out, ffn_w_gate_up, ffn_w_down, final_norm_w, loss_target, m_attn_norm_w, m_ffn_norm_w, m_gla_w_in, m_gla_w_gate_up, m_gla_b_gate, m_gla_gnorm_w, m_gla_w_out, m_kv_norm_w, m_sb_w_kv, m_sb_w_q, m_sb_w_out, m_ffn_w_gate_up, m_ffn_w_down, m_final_norm_w, v_attn_norm_w, v_ffn_norm_w, v_gla_w_in, v_gla_w_gate_up, v_gla_b_gate, v_gla_gnorm_w, v_gla_w_out, v_kv_norm_w, v_sb_w_kv, v_sb_w_q, v_sb_w_out, v_ffn_w_gate_up, v_ffn_w_down, v_final_norm_w):
    xi, yi, ci = _coords()
    core = ci.astype(jnp.int32).reshape(1)
    chip1 = (2 * xi + yi).astype(jnp.int32)
    chip = chip1.reshape(1)
    rank = gla_w_gate_up.shape[1]

    gu4, dn4 = _halves(ffn_w_gate_up), _halves(ffn_w_down)
    shards = dict(w_in=(_halves(gla_w_in[0]), None), w_out=(_halves(gla_w_out[0]), None), w_kv=(_halves(sb_w_kv), None),
                  w_q=(_halves(sb_w_q[0]), None), w_so=(_halves(sb_w_out[0]), None),
                  w_gu0=(gu4, 0), w_gu1=(gu4, 1), w_dn0=(dn4, 0), w_dn1=(dn4, 1))
    placed = {n: place_shard(a, chip, name=f"place_{n}", layer=l) for n, (a, l) in shards.items()}
    small_w = [gla_w_gate_up, gla_b_gate, gla_gnorm_w]
    small_all = allgather_all(_pack_rows(small_w), name="gather_gate_weights")
    wg, bg, gw = _segments(small_all[::2].reshape(N_CHIPS, -1), [a.size for a in small_w])
    wg = wg.reshape(N_CHIPS, rank, -1).transpose(1, 0, 2).reshape(rank, -1)
    p = dict(an0=attn_norm_w[0], an1=attn_norm_w[1], fn0=ffn_norm_w[0], fn1=ffn_norm_w[1], kvn=kv_norm_w, finn=final_norm_w,
             wg=jnp.pad(wg, ((0, LANES - rank), (0, 0))).astype(BF16), bg=bg.reshape(1, -1), gw=gw.reshape(1, -1),
             in_w=N_CHIPS * gla_w_in.shape[-1])

    ex = Exchanges(placed, chip, core)
    loss_row, dx, g, _ = local_step(x[0], loss_target[0], p, ex)
    loss = lax.psum(loss_row[0, 0], ("x", "y", "c"))

    tags = list(ex.mine)
    tot = dict(zip(tags, sibling_gather([ex.mine[n] for n in tags], name="grads_from_sibling")))

    vecs = [jnp.concatenate([g["an0"], g["an1"]]), jnp.concatenate([g["fn0"], g["fn1"]]), g["kvn"], g["finn"],
            g["wg"][:rank], g["bg"], g["gw"]]
    gathered_vecs = allgather_all(_pack_rows(vecs), name="gather_small_grads")
    d_an, d_fn, d_kvn, d_fin, d_wg, d_bg, d_gw = _segments(
        sum_blocks(gathered_vecs, name="sum_small_grads").reshape(-1), [t.size for t in vecs])

    def shard(t, like):
        return lax.dynamic_index_in_dim(t.reshape(-1, N_CHIPS, like.shape[-1]), chip1, axis=1, keepdims=False).reshape(like.shape)

    weights = dict(
        attn_norm_w=(attn_norm_w, m_attn_norm_w, v_attn_norm_w), ffn_norm_w=(ffn_norm_w, m_ffn_norm_w, v_ffn_norm_w),
        gla_w_in=(gla_w_in, m_gla_w_in, v_gla_w_in), gla_w_gate_up=(gla_w_gate_up, m_gla_w_gate_up, v_gla_w_gate_up),
        gla_b_gate=(gla_b_gate, m_gla_b_gate, v_gla_b_gate), gla_gnorm_w=(gla_gnorm_w, m_gla_gnorm_w, v_gla_gnorm_w),
        gla_w_out=(gla_w_out, m_gla_w_out, v_gla_w_out), kv_norm_w=(kv_norm_w, m_kv_norm_w, v_kv_norm_w),
        sb_w_kv=(sb_w_kv, m_sb_w_kv, v_sb_w_kv), sb_w_q=(sb_w_q, m_sb_w_q, v_sb_w_q), sb_w_out=(sb_w_out, m_sb_w_out, v_sb_w_out),
        ffn_w_gate_up=(ffn_w_gate_up, m_ffn_w_gate_up, v_ffn_w_gate_up), ffn_w_down=(ffn_w_down, m_ffn_w_down, v_ffn_w_down),
        final_norm_w=(final_norm_w, m_final_norm_w, v_final_norm_w))
    grads = dict(
        attn_norm_w=d_an.reshape(attn_norm_w.shape), ffn_norm_w=d_fn.reshape(ffn_norm_w.shape),
        gla_w_in=tot["w_in"].reshape(gla_w_in.shape), gla_w_gate_up=shard(d_wg, gla_w_gate_up),
        gla_b_gate=shard(d_bg, gla_b_gate), gla_gnorm_w=shard(d_gw, gla_gnorm_w),
        gla_w_out=tot["w_out"].reshape(gla_w_out.shape), kv_norm_w=d_kvn.reshape(kv_norm_w.shape),
        sb_w_kv=tot["w_kv"].reshape(sb_w_kv.shape), sb_w_q=tot["w_q"].reshape(sb_w_q.shape),
        sb_w_out=tot["w_so"].reshape(sb_w_out.shape),
        ffn_w_gate_up=tot["w_gu"].reshape(ffn_w_gate_up.shape), ffn_w_down=tot["w_dn"].reshape(ffn_w_down.shape),
        final_norm_w=d_fin.reshape(final_norm_w.shape))
    names = list(weights)
    stepped = [adamw(weights[n][0], grads[n], weights[n][1], weights[n][2], name=f"adamw_{n}") for n in names]
    return (loss, dx.reshape(x.shape), *[grads[n] for n in names], *[t[0] for t in stepped], *[t[1] for t in stepped],
            *[t[2] for t in stepped])
```
